```python
import math
import jax, jax.numpy as jnp
from jax import lax
import numpy as np

D_MODEL = 1024
BATCH = 2
SEQ = 8192
DEPTH = 1
DEC_BATCH = 16
DEC_SEQ = 16
PAST_LEN = 2048

CHUNK = 64
D_MIX = D_MODEL
D_SSM = D_MIX // 2
D_ATT = D_MIX - D_SSM
SSM_GROUP = 16
SSM_GROUPS = D_SSM // SSM_GROUP
SSM_STATE = 64
ATT_HEADS = 4
QK_DIM = 64
V_DIM = 2 * QK_DIM
ROPE_DIM = QK_DIM // 4
ROPE_THETA = 500000.0
Q_BLOCK = 128
EPS = 1e-6
DT_MIN = 0.001
DT_MAX = 0.1
D_IN = 2 * D_SSM + 2 * ATT_HEADS * 2 * QK_DIM + ATT_HEADS * V_DIM + D_ATT
SPLITS = tuple(np.cumsum([D_SSM, D_SSM, ATT_HEADS * 2 * QK_DIM, ATT_HEADS * 2 * QK_DIM, ATT_HEADS * V_DIM])[:].tolist())

kernel_name = "hybrid_s5_diffattn_stream_step"


def rms_norm(x, g):
    xf = x.astype(jnp.float32)
    y = xf * lax.rsqrt(jnp.mean(xf * xf, axis=-1, keepdims=True) + EPS)
    return (y * g.astype(jnp.float32)).astype(x.dtype)


def rope(x, pos):
    half = ROPE_DIM // 2
    inv = ROPE_THETA ** (-jnp.arange(half, dtype=jnp.float32) * 2.0 / ROPE_DIM)
    ang = pos.astype(jnp.float32)[:, None] * inv[None, :]
    cos = jnp.cos(ang)[:, None, None, :]
    sin = jnp.sin(ang)[:, None, None, :]
    xr = x[..., :ROPE_DIM].astype(jnp.float32)
    x1, x2 = xr[..., :half], xr[..., half:]
    rot = jnp.concatenate([x1 * cos - x2 * sin, x2 * cos + x1 * sin], axis=-1)
    return jnp.concatenate([rot.astype(x.dtype), x[..., ROPE_DIM:]], axis=-1)


def diff_attend(q, k, v, mask, lam):
    s = jnp.einsum('bqhcd,bkhcd->bhcqk', q, k, preferred_element_type=jnp.float32) * (QK_DIM ** -0.5)
    s = jnp.where(mask, s, -jnp.inf)
    p = jax.nn.softmax(s, axis=-1)
    pd = p[:, :, 0] - lam * p[:, :, 1]
    return jnp.einsum('bhqk,bkhe->bqhe', pd.astype(v.dtype), v)


def prompt_attention(q, k, v, lam):
    B, S = q.shape[0], q.shape[1]
    nb = S // Q_BLOCK
    qb = q.reshape(B, nb, Q_BLOCK, ATT_HEADS, 2, QK_DIM).swapaxes(0, 1)
    starts = jnp.arange(nb, dtype=jnp.int32) * Q_BLOCK
    key_chunk = jnp.arange(S, dtype=jnp.int32) // CHUNK

    def one(args):
        qi, s0 = args
        q_chunk = (s0 + jnp.arange(Q_BLOCK, dtype=jnp.int32)) // CHUNK
        mask = key_chunk[None, :] <= q_chunk[:, None]
        return diff_attend(qi, k, v, mask, lam)

    o = lax.map(one, (qb, starts))
    return o.swapaxes(0, 1).reshape(B, S, ATT_HEADS, V_DIM)


def ssm_discretize(lam_re, lam_im, log_dt, b_re, b_im):
    lr = lam_re.astype(jnp.float32)
    li = lam_im.astype(jnp.float32)
    dt = jnp.exp(log_dt.astype(jnp.float32))[:, None]
    mag = jnp.exp(lr * dt)
    ar = mag * jnp.cos(li * dt)
    ai = mag * jnp.sin(li * dt)
    den = lr * lr + li * li
    cr = ((ar - 1.0) * lr + ai * li) / den
    ci = (ai * lr - (ar - 1.0) * li) / den
    br = b_re.astype(jnp.float32)
    bi = b_im.astype(jnp.float32)
    bbar_re = cr[:, :, None] * br - ci[:, :, None] * bi
    bbar_im = cr[:, :, None] * bi + ci[:, :, None] * br
    return ar, ai, bbar_re, bbar_im


def ssm_combine(e1, e2):
    a1r, a1i, b1r, b1i = e1
    a2r, a2i, b2r, b2i = e2
    return (a2r * a1r - a2i * a1i,
            a2r * a1i + a2i * a1r,
            a2r * b1r - a2i * b1i + b2r,
            a2r * b1i + a2i * b1r + b2i)


def s5_scan(u, h0_re, h0_im, lam_re, lam_im, log_dt, b_re, b_im, c_re, c_im, d):
    B, L, _ = u.shape
    ar, ai, bbr, bbi = ssm_discretize(lam_re, lam_im, log_dt, b_re, b_im)
    uf = u.astype(jnp.float32).reshape(B, L, SSM_GROUPS, SSM_GROUP)
    bu_re = jnp.einsum('blgc,gpc->blgp', uf, bbr)
    bu_im = jnp.einsum('blgc,gpc->blgp', uf, bbi)
    h0r = h0_re.astype(jnp.float32)
    h0i = h0_im.astype(jnp.float32)
    bu_re = bu_re.at[:, 0].add(ar * h0r - ai * h0i)
    bu_im = bu_im.at[:, 0].add(ar * h0i + ai * h0r)
    a_re = jnp.broadcast_to(ar, bu_re.shape)
    a_im = jnp.broadcast_to(ai, bu_im.shape)
    _, _, h_re, h_im = lax.associative_scan(ssm_combine, (a_re, a_im, bu_re, bu_im), axis=1)
    y = (jnp.einsum('blgp,gcp->blgc', h_re, c_re.astype(jnp.float32))
         - jnp.einsum('blgp,gcp->blgc', h_im, c_im.astype(jnp.float32))
         + uf * d.astype(jnp.float32))
    return y.reshape(B, L, D_SSM).astype(u.dtype), h_re[:, -1], h_im[:, -1]


def layer(x, past_k, past_v, h0_re, h0_im, layer_idx,
          norm_pre_g, w_in, lam_re, lam_im, log_dt, b_re, b_im, c_re, c_im, ssm_d,
          glu_w1, glu_b1, glu_w2, glu_b2, lq1, lk1, lq2, lk2, subln_g, w_out, norm_post_g):
    B, L, _ = x.shape
    past = 0 if past_k is None else past_k.shape[1]
    hn = rms_norm(x, norm_pre_g)
    proj = hn @ w_in
    u, z_s, q, k, v, z_a = jnp.split(proj, SPLITS, axis=-1)

    y_ssm, h_re, h_im = s5_scan(u, h0_re, h0_im, lam_re, lam_im, log_dt, b_re, b_im, c_re, c_im, ssm_d)
    g = jax.nn.gelu(y_ssm)
    y_glu = (g @ glu_w1 + glu_b1) * jax.nn.sigmoid(g @ glu_w2 + glu_b2)
    ssm_out = y_glu * jax.nn.silu(z_s)

    pos = past + jnp.arange(L, dtype=jnp.int32)
    q = rope(q.reshape(B, L, ATT_HEADS, 2, QK_DIM), pos)
    k = rope(k.reshape(B, L, ATT_HEADS, 2, QK_DIM), pos)
    v = v.reshape(B, L, ATT_HEADS, V_DIM)
    lambda_init = 0.8 - 0.6 * math.exp(-0.3 * layer_idx)
    lam = (jnp.exp(jnp.sum(lq1.astype(jnp.float32) * lk1.astype(jnp.float32)))
           - jnp.exp(jnp.sum(lq2.astype(jnp.float32) * lk2.astype(jnp.float32))) + lambda_init)
    if past_k is None:
        o = prompt_attention(q, k, v, lam)
    else:
        kk = jnp.concatenate([past_k, k.astype(past_k.dtype)], axis=1)
        vv = jnp.concatenate([past_v, v.astype(past_v.dtype)], axis=1)
        key_chunk = jnp.arange(past + L, dtype=jnp.int32) // CHUNK
        mask = key_chunk[None, :] <= (pos // CHUNK)[:, None]
        o = diff_attend(q, kk, vv, mask, lam)
    o = rms_norm(o, subln_g) * (1.0 - lambda_init)
    att_out = o.reshape(B, L, D_ATT) * jax.nn.silu(z_a)

    mix = jnp.concatenate([ssm_out, att_out], axis=-1) @ w_out
    y = x + rms_norm(mix, norm_post_g)
    return y, k, v, h_re, h_im


def setup_inputs(seed: int = 0) -> dict:
    key = jax.random.key(seed)
    ks = jax.random.split(key, 32)
    f32 = jnp.float32
    nrm = lambda k, s, sc: jax.random.normal(k, s, f32) * sc
    n_idx = jnp.arange(SSM_STATE, dtype=f32)
    lam_re = -0.5 + nrm(ks[0], (DEPTH, SSM_GROUPS, SSM_STATE), 0.01)
    lam_im = math.pi * n_idx[None, None, :] + nrm(ks[1], (DEPTH, SSM_GROUPS, SSM_STATE), 0.01)
    log_dt = jax.random.uniform(ks[2], (DEPTH, SSM_GROUPS), f32, math.log(DT_MIN), math.log(DT_MAX))
    return {
        "x_prompt": nrm(ks[3], (BATCH, SEQ, D_MODEL), 1.0),
        "x_sample": nrm(ks[4], (DEC_BATCH, DEC_SEQ, D_MODEL), 1.0),
        "cache_k": nrm(ks[5], (DEPTH, DEC_BATCH, PAST_LEN, ATT_HEADS, 2, QK_DIM), 1.0),
        "cache_v": nrm(ks[6], (DEPTH, DEC_BATCH, PAST_LEN, ATT_HEADS, V_DIM), 1.0),
        "state_ssm_re": nrm(ks[7], (DEPTH, DEC_BATCH, SSM_GROUPS, SSM_STATE), 0.5),
        "state_ssm_im": nrm(ks[8], (DEPTH, DEC_BATCH, SSM_GROUPS, SSM_STATE), 0.5),
        "norm_pre_g": 1.0 + nrm(ks[9], (DEPTH, D_MODEL), 0.01),
        "w_in": nrm(ks[10], (DEPTH, D_MODEL, D_IN), D_MODEL ** -0.5),
        "ssm_lambda_re": lam_re,
        "ssm_lambda_im": lam_im,
        "ssm_log_dt": log_dt,
        "ssm_b_re": nrm(ks[11], (DEPTH, SSM_GROUPS, SSM_STATE, SSM_GROUP), (2 * SSM_GROUP) ** -0.5),
        "ssm_b_im": nrm(ks[12], (DEPTH, SSM_GROUPS, SSM_STATE, SSM_GROUP), (2 * SSM_GROUP) ** -0.5),
        "ssm_c_re": nrm(ks[13], (DEPTH, SSM_GROUPS, SSM_GROUP, SSM_STATE), SSM_STATE ** -0.5),
        "ssm_c_im": nrm(ks[14], (DEPTH, SSM_GROUPS, SSM_GROUP, SSM_STATE), SSM_STATE ** -0.5),
        "ssm_d": nrm(ks[15], (DEPTH, SSM_GROUPS, SSM_GROUP), 1.0),
        "glu_w1": nrm(ks[16], (DEPTH, D_SSM, D_SSM), D_SSM ** -0.5),
        "glu_b1": nrm(ks[17], (DEPTH, D_SSM), 0.01),
        "glu_w2": nrm(ks[18], (DEPTH, D_SSM, D_SSM), D_SSM ** -0.5),
        "glu_b2": nrm(ks[19], (DEPTH, D_SSM), 0.01),
        "lambda_q1": nrm(ks[20], (DEPTH, QK_DIM), 0.1),
        "lambda_k1": nrm(ks[21], (DEPTH, QK_DIM), 0.1),
        "lambda_q2": nrm(ks[22], (DEPTH, QK_DIM), 0.1),
        "lambda_k2": nrm(ks[23], (DEPTH, QK_DIM), 0.1),
        "attn_subln_g": 1.0 + nrm(ks[24], (DEPTH, V_DIM), 0.01),
        "w_out": nrm(ks[25], (DEPTH, D_MIX, D_MODEL), D_MIX ** -0.5),
        "norm_post_g": 1.0 + nrm(ks[26], (DEPTH, D_MODEL), 0.01),
    }


def reference(x_prompt, x_sample, cache_k, cache_v, state_ssm_re, state_ssm_im,
              norm_pre_g, w_in, ssm_lambda_re, ssm_lambda_im, ssm_log_dt,
              ssm_b_re, ssm_b_im, ssm_c_re, ssm_c_im, ssm_d,
              glu_w1, glu_b1, glu_w2, glu_b2,
              lambda_q1, lambda_k1, lambda_q2, lambda_k2,
              attn_subln_g, w_out, norm_post_g):
    yp, ys = x_prompt, x_sample
    kp_l, vp_l, hrp_l, hip_l = [], [], [], []
    ks_l, vs_l, hrs_l, his_l = [], [], [], []
    Bp = x_prompt.shape[0]
    for l in range(DEPTH):
        w = (norm_pre_g[l], w_in[l], ssm_lambda_re[l], ssm_lambda_im[l], ssm_log_dt[l],
             ssm_b_re[l], ssm_b_im[l], ssm_c_re[l], ssm_c_im[l], ssm_d[l],
             glu_w1[l], glu_b1[l], glu_w2[l], glu_b2[l],
             lambda_q1[l], lambda_k1[l], lambda_q2[l], lambda_k2[l],
             attn_subln_g[l], w_out[l], norm_post_g[l])
        zeros = jnp.zeros((Bp, SSM_GROUPS, SSM_STATE), jnp.float32)
        yp, kp, vp, hrp, hip = layer(yp, None, None, zeros, zeros, l, *w)
        ys, ks_, vs_, hrs, his = layer(ys, cache_k[l], cache_v[l], state_ssm_re[l], state_ssm_im[l], l, *w)
        kp_l.append(kp); vp_l.append(vp)
        hrp_l.append(hrp.astype(x_prompt.dtype)); hip_l.append(hip.astype(x_prompt.dtype))
        ks_l.append(ks_.astype(cache_k.dtype)); vs_l.append(vs_.astype(cache_v.dtype))
        hrs_l.append(hrs.astype(state_ssm_re.dtype)); his_l.append(his.astype(state_ssm_im.dtype))
    k_prompt = jnp.stack(kp_l)
    v_prompt = jnp.stack(vp_l)
    ssm_re_prompt = jnp.stack(hrp_l)
    ssm_im_prompt = jnp.stack(hip_l)
    k_sample = jnp.stack(ks_l)
    v_sample = jnp.stack(vs_l)
    ssm_re_sample = jnp.stack(hrs_l)
    ssm_im_sample = jnp.stack(his_l)
    return (yp, ys, k_prompt, v_prompt, ssm_re_prompt, ssm_im_prompt,
            k_sample, v_sample, ssm_re_sample, ssm_im_sample)
```

```python
import functools
import math

import jax
import jax.numpy as jnp
from jax import lax
from jax.experimental import pallas as pl
from jax.experimental.pallas import tpu as pltpu

F32 = jnp.float32
BF16 = jnp.bfloat16

D_MODEL = 1024
D_SSM = 512
D_ATT = 512
SSM_GROUP = 16
SSM_GROUPS = 32
SSM_STATE = 64
N_CH = SSM_GROUPS * SSM_STATE
ATT_HEADS = 4
QK_DIM = 64
V_DIM = 128
ROPE_DIM = 16
ROPE_THETA = 500000.0
CHUNK = 64
EPS = 1e-6
D_IN = 3072
LAMBDA_INIT = 0.8 - 0.6 * math.exp(-0.3 * 0)

LANES = 128
SUBLANES = 8
N_SLAB = N_CH // LANES
ROW_PAD = 4
NEG = -1e30
VMEM_LIMIT = 56 * 1024 * 1024


def _params(sem):
    return pltpu.CompilerParams(dimension_semantics=sem, vmem_limit_bytes=VMEM_LIMIT)


def _full(shape):
    n = len(shape)
    return pl.BlockSpec(shape, lambda *_: (0,) * n)


def _prep_kernel(lr_ref, li_ref, ldt_ref, br_ref, bi_ref, ar_ref, ai_ref, bbr_ref, bbi_ref):
    lr = lr_ref[...]
    li = li_ref[...]
    dt = jnp.exp(ldt_ref[...])
    mag = jnp.exp(lr * dt)
    ar = mag * jnp.cos(li * dt)
    ai = mag * jnp.sin(li * dt)
    den = lr * lr + li * li
    cr = ((ar - 1.0) * lr + ai * li) / den
    ci = (ai * lr - (ar - 1.0) * li) / den
    ar_ref[...] = ar
    ai_ref[...] = ai
    br = br_ref[...]
    bi = bi_ref[...]
    crb = cr[:, None, :]
    cib = ci[:, None, :]
    bbr_ref[...] = crb * br - cib * bi
    bbi_ref[...] = crb * bi + cib * br


def _prep(lam_re, lam_im, log_dt, b_re, b_im):
    g, p, c = b_re.shape
    brt = jnp.swapaxes(b_re, 1, 2)
    bit = jnp.swapaxes(b_im, 1, 2)
    out_shape = (jax.ShapeDtypeStruct((g, p), F32), jax.ShapeDtypeStruct((g, p), F32),
                 jax.ShapeDtypeStruct((g, c, p), F32), jax.ShapeDtypeStruct((g, c, p), F32))
    return pl.pallas_call(_prep_kernel, out_shape=out_shape, name="s5_prep")(
        lam_re, lam_im, log_dt.reshape(g, 1), brt, bit)


def _in_proj_kernel(x_ref, g_ref, w_ref, inv_ref, u_ref, zs_ref, q_ref, kf_ref, kb_ref, vf_ref,
                    vb_ref, za_ref, *, tm, seq_len, pos0):
    i = pl.program_id(0)
    x = x_ref[...]
    ms = jnp.mean(x * x, axis=-1, keepdims=True)
    hn = (x * lax.rsqrt(ms + EPS) * g_ref[...]).astype(BF16)

    row = i * tm + lax.broadcasted_iota(jnp.int32, (tm, LANES), 0)
    pos = (pos0 + (row & (seq_len - 1))).astype(F32)
    ang = pos * inv_ref[...]
    cos = jnp.cos(ang)
    sin = jnp.sin(ang)
    lane = lax.broadcasted_iota(jnp.int32, (tm, LANES), 1) & (QK_DIM - 1)
    half = ROPE_DIM // 2
    c_m = jnp.where(lane < ROPE_DIM, cos, 1.0)
    s_lo = jnp.where(lane < half, -sin, 0.0)
    s_hi = jnp.where((lane >= half) & (lane < ROPE_DIM), sin, 0.0)

    def seg(lo, hi):
        return jnp.dot(hn, w_ref[:, lo:hi], preferred_element_type=F32)

    def rope(t):
        outs = []
        for h in range(ATT_HEADS):
            th = t[:, h * LANES:(h + 1) * LANES]
            outs.append(th * c_m + pltpu.roll(th, LANES - half, 1) * s_lo + pltpu.roll(th, half, 1) * s_hi)
        return jnp.concatenate(outs, axis=1)

    u_ref[...] = seg(0, D_SSM).astype(BF16)
    zs_ref[...] = seg(D_SSM, 2 * D_SSM).astype(BF16)
    q = rope(seg(1024, 1536))
    q_ref[...] = (q * (QK_DIM ** -0.5)).astype(BF16)
    k = rope(seg(1536, 2048))
    kf_ref[...] = k
    kb_ref[...] = k.astype(BF16)
    v = seg(2048, 2560)
    vf_ref[...] = v
    vb_ref[...] = v.astype(BF16)
    za_ref[...] = seg(2560, 3072).astype(BF16)


def _in_proj(x2d, g, w_bf, inv_lane, *, seq_len, pos0, tm):
    n = x2d.shape[0]
    assert n % tm == 0 and seq_len & (seq_len - 1) == 0
    row = lambda i: (i, 0)
    o512 = pl.BlockSpec((tm, 512), row)
    shp = lambda dt: jax.ShapeDtypeStruct((n, 512), dt)
    return pl.pallas_call(
        functools.partial(_in_proj_kernel, tm=tm, seq_len=seq_len, pos0=pos0),
        grid=(n // tm,),
        in_specs=[pl.BlockSpec((tm, D_MODEL), row), _full((1, D_MODEL)), _full((D_MODEL, D_IN)),
                  _full((1, LANES))],
        out_specs=[o512] * 8,
        out_shape=[shp(BF16), shp(BF16), shp(BF16), shp(F32), shp(BF16), shp(F32), shp(BF16), shp(BF16)],
        compiler_params=_params(("arbitrary",)),
        name="in_proj",
    )(x2d, g, w_bf, inv_lane)


def _cmul(ar, ai, br, bi):
    return ar * br - ai * bi, ar * bi + ai * br


def _s5_kernel(u_ref, zs_ref, bdb_ref, bdc_ref, d_ref, are_ref, aim_ref, w1_ref, b1_ref, w2_ref, b2_ref,
               h0r_ref, h0i_ref, out_ref, hfr_ref, hfi_ref, hbuf, hb16, car_re, car_im,
               *, R, T, chained, slab_group):
    TP = T + ROW_PAD
    tm = R * T
    half_cols = N_CH // 2

    if chained:
        @pl.when(pl.program_id(1) == 0)
        def _():
            car_re[...] = jnp.zeros_like(car_re)
            car_im[...] = jnp.zeros_like(car_im)

    for b in range(2):
        bu = jnp.dot(u_ref[:, b * 256:(b + 1) * 256], bdb_ref[b], preferred_element_type=F32)
        for part in range(2):
            for k in range(N_SLAB // 2):
                slab = part * N_SLAB + b * (N_SLAB // 2) + k
                col = part * half_cols + k * LANES
                for j in range(R):
                    hbuf[slab, j * TP:j * TP + T, :] = bu[j * T:(j + 1) * T, col:col + LANES]

    finals_re = [None] * N_SLAB
    finals_im = [None] * N_SLAB
    for g0 in range(0, N_SLAB, slab_group):
        slabs = list(range(g0, g0 + slab_group))
        a_r = [jnp.broadcast_to(are_ref[:, c * LANES:(c + 1) * LANES], (R, LANES)) for c in slabs]
        a_i = [jnp.broadcast_to(aim_ref[:, c * LANES:(c + 1) * LANES], (R, LANES)) for c in slabs]

        def body(s, carry, slabs=slabs, a_r=a_r, a_i=a_i):
            new = []
            for n, c in enumerate(slabs):
                hr, hi = carry[n]
                rows = pl.ds(s, R, stride=TP)
                pr, pi = _cmul(a_r[n], a_i[n], hr, hi)
                nr = pr + hbuf[c, rows, :]
                ni = pi + hbuf[N_SLAB + c, rows, :]
                hbuf[c, rows, :] = nr
                hbuf[N_SLAB + c, rows, :] = ni
                new.append((nr, ni))
            return tuple(new)

        z = jnp.zeros((R, LANES), F32)
        fin = lax.fori_loop(0, T, body, tuple((z, z) for _ in slabs))
        for n, c in enumerate(slabs):
            finals_re[c], finals_im[c] = fin[n]

    f_re = jnp.concatenate(finals_re, axis=1)
    f_im = jnp.concatenate(finals_im, axis=1)

    at_r, at_i = are_ref[...], aim_ref[...]
    for _ in range(T.bit_length() - 1):
        at_r, at_i = _cmul(at_r, at_i, at_r, at_i)

    if chained:
        e_r, e_i = car_re[...], car_im[...]
        rows_r, rows_i = [], []
        for j in range(R):
            rows_r.append(e_r)
            rows_i.append(e_i)
            pr, pi = _cmul(at_r, at_i, e_r, e_i)
            e_r = pr + f_re[j:j + 1]
            e_i = pi + f_im[j:j + 1]
        ent_r = jnp.concatenate(rows_r, axis=0)
        ent_i = jnp.concatenate(rows_i, axis=0)
        car_re[...] = e_r
        car_im[...] = e_i
        hfr_ref[0] = e_r
        hfi_ref[0] = e_i
    else:
        ent_r, ent_i = h0r_ref[...], h0i_ref[...]
        pr, pi = _cmul(at_r, at_i, ent_r, ent_i)
        hfr_ref[...] = pr + f_re
        hfi_ref[...] = pi + f_im

    for g0 in range(0, N_SLAB, slab_group):
        slabs = list(range(g0, g0 + slab_group))
        a_r = [jnp.broadcast_to(are_ref[:, c * LANES:(c + 1) * LANES], (R, LANES)) for c in slabs]
        a_i = [jnp.broadcast_to(aim_ref[:, c * LANES:(c + 1) * LANES], (R, LANES)) for c in slabs]

        def fix(s, carry, slabs=slabs, a_r=a_r, a_i=a_i):
            new = []
            for n, c in enumerate(slabs):
                er, ei = carry[n]
                rows = pl.ds(s, R, stride=TP)
                er, ei = _cmul(a_r[n], a_i[n], er, ei)
                hbuf[c, rows, :] = hbuf[c, rows, :] + er
                hbuf[N_SLAB + c, rows, :] = hbuf[N_SLAB + c, rows, :] + ei
                new.append((er, ei))
            return tuple(new)

        init = tuple((ent_r[:, c * LANES:(c + 1) * LANES], ent_i[:, c * LANES:(c + 1) * LANES]) for c in slabs)
        lax.fori_loop(0, T, fix, init)

    for b in range(2):
        for part in range(2):
            for k in range(N_SLAB // 2):
                slab = part * N_SLAB + b * (N_SLAB // 2) + k
                col = b * N_CH + part * half_cols + k * LANES
                for j in range(R):
                    hb16[j * T:(j + 1) * T, col:col + LANES] = hbuf[slab, j * TP:j * TP + T, :].astype(BF16)
    ys = [jnp.dot(hb16[:, b * N_CH:(b + 1) * N_CH], bdc_ref[b], preferred_element_type=F32) for b in range(2)]
    y = jnp.concatenate(ys, axis=1) + u_ref[...].astype(F32) * d_ref[...]

    gb = jax.nn.gelu(y).astype(BF16)
    y1 = jnp.dot(gb, w1_ref[...], preferred_element_type=F32) + b1_ref[...]
    y2 = jnp.dot(gb, w2_ref[...], preferred_element_type=F32) + b2_ref[...]
    out_ref[...] = (y1 * jax.nn.sigmoid(y2) * jax.nn.silu(zs_ref[...].astype(F32))).astype(BF16)


def _s5(u, zs, bdb, bdc, dvec, a_re, a_im, w1, b1, w2, b2, h0r, h0i, *, n_seq, seq_len, chained):
    n = u.shape[0]
    if chained:
        R, T = SUBLANES, 64
        tm = R * T
        tiles = seq_len // tm
        grid = (n_seq, tiles)
        row = lambda b, t: (b * tiles + t, 0)
        hf_shape = jax.ShapeDtypeStruct((n_seq, 1, N_CH), F32)
        hf_spec = pl.BlockSpec((1, 1, N_CH), lambda b, t: (b, 0, 0))
        slab_group = 8
    else:
        R, T = n_seq, seq_len
        tm = R * T
        assert tm == n
        grid = (1, 1)
        row = lambda b, t: (0, 0)
        hf_shape = jax.ShapeDtypeStruct((n_seq, N_CH), F32)
        hf_spec = _full((n_seq, N_CH))
        slab_group = 4
    assert T & (T - 1) == 0 and R % SUBLANES == 0
    blk = pl.BlockSpec((tm, D_SSM), row)
    return pl.pallas_call(
        functools.partial(_s5_kernel, R=R, T=T, chained=chained, slab_group=slab_group),
        grid=grid,
        in_specs=[blk, blk, _full(bdb.shape), _full(bdc.shape), _full((1, D_SSM)), _full((1, N_CH)),
                  _full((1, N_CH)), _full((D_SSM, D_SSM)), _full((1, D_SSM)), _full((D_SSM, D_SSM)),
                  _full((1, D_SSM)), _full(h0r.shape), _full(h0i.shape)],
        out_specs=[blk, hf_spec, hf_spec],
        out_shape=[jax.ShapeDtypeStruct((n, D_SSM), BF16), hf_shape, hf_shape],
        scratch_shapes=[pltpu.VMEM((2 * N_SLAB, R * (T + ROW_PAD), LANES), F32),
                        pltpu.VMEM((tm, 2 * N_CH), BF16),
                        pltpu.VMEM((1, N_CH), F32), pltpu.VMEM((1, N_CH), F32)],
        compiler_params=_params(("arbitrary", "arbitrary")),
        name="s5_chained" if chained else "s5_independent",
    )(u, zs, bdb, bdc, dvec, a_re, a_im, w1, b1, w2, b2, h0r, h0i)


def _lambda(lq1, lk1, lq2, lk2):
    s1 = jnp.sum(lq1[...] * lk1[...], axis=1, keepdims=True)
    s2 = jnp.sum(lq2[...] * lk2[...], axis=1, keepdims=True)
    return jnp.exp(s1) - jnp.exp(s2) + LAMBDA_INIT


def _stack_maps(q):
    lane = lax.broadcasted_iota(jnp.int32, q.shape, 1)
    zero = jnp.zeros_like(q)
    return jnp.concatenate([jnp.where(lane < QK_DIM, q, zero), jnp.where(lane >= QK_DIM, q, zero)], axis=0)


def _finish_head(acc, l, lam, g, za, t):
    inv = 1.0 / l
    o = acc[:t] * inv[:t] - lam * (acc[t:] * inv[t:])
    ms = jnp.mean(o * o, axis=-1, keepdims=True)
    on = (o * lax.rsqrt(ms + EPS) * g) * (1.0 - LAMBDA_INIT)
    return on * jax.nn.silu(za.astype(F32))


_NT = (((1,), (1,)), ((), ()))


def _attn_kernel(lq1, lk1, lq2, lk2, g_ref, q_ref, k_ref, v_ref, za_ref, o_ref, m_sc, l_sc, acc_sc, *, tq):
    qi = pl.program_id(2)
    q2 = _stack_maps(q_ref[...])
    m_sc[...] = jnp.full_like(m_sc, NEG)
    l_sc[...] = jnp.zeros_like(l_sc)
    acc_sc[...] = jnp.zeros_like(acc_sc)

    def step(kt, masked):
        start = pl.multiple_of(kt * tq, tq)
        kk = k_ref[pl.ds(start, tq), :]
        vv = v_ref[pl.ds(start, tq), :]
        s = lax.dot_general(q2, kk, _NT, preferred_element_type=F32)
        if masked:
            qc = (lax.broadcasted_iota(jnp.int32, s.shape, 0) & (tq - 1)) // CHUNK
            kc = lax.broadcasted_iota(jnp.int32, s.shape, 1) // CHUNK
            s = jnp.where(kc <= qc, s, NEG)
        m_old = m_sc[...]
        m_new = jnp.maximum(m_old, jnp.max(s, axis=1, keepdims=True))
        alpha = jnp.exp(m_old - m_new)
        p = jnp.exp(s - m_new)
        l_sc[...] = alpha * l_sc[...] + jnp.sum(p, axis=1, keepdims=True)
        acc_sc[...] = alpha * acc_sc[...] + jnp.dot(p.astype(BF16), vv, preferred_element_type=F32)
        m_sc[...] = m_new

    def body(kt, c):
        step(kt, False)
        return c

    lax.fori_loop(0, qi, body, 0)
    step(qi, True)
    lam = _lambda(lq1, lk1, lq2, lk2)
    o_ref[...] = _finish_head(acc_sc[...], l_sc[...], lam, g_ref[...], za_ref[...], tq).astype(BF16)


def _attn_prompt(lams, g, q, kb, vb, za, *, n_seq, seq_len, tq):
    n = q.shape[0]
    nq = seq_len // tq
    assert tq % CHUNK == 0 and tq & (tq - 1) == 0
    qspec = pl.BlockSpec((tq, LANES), lambda b, h, i: (b * nq + i, h))
    kvspec = pl.BlockSpec((seq_len, LANES), lambda b, h, i: (b, h))
    vec = _full((1, QK_DIM))
    return pl.pallas_call(
        functools.partial(_attn_kernel, tq=tq),
        grid=(n_seq, ATT_HEADS, nq),
        in_specs=[vec, vec, vec, vec, _full((1, V_DIM)), qspec, kvspec, kvspec, qspec],
        out_specs=qspec,
        out_shape=jax.ShapeDtypeStruct((n, D_ATT), BF16),
        scratch_shapes=[pltpu.VMEM((2 * tq, 1), F32), pltpu.VMEM((2 * tq, 1), F32),
                        pltpu.VMEM((2 * tq, V_DIM), F32)],
        compiler_params=_params(("arbitrary", "arbitrary", "arbitrary")),
        name="attn_prompt",
    )(*lams, g, q, kb, vb, za)


def _attn_sample_kernel(lq1, lk1, lq2, lk2, g_ref, q_ref, kn_ref, vn_ref, ck_ref, cv_ref, za_ref, o_ref, *, t):
    lam = _lambda(lq1, lk1, lq2, lk2)
    outs = []
    for h in range(ATT_HEADS):
        cols = slice(h * LANES, (h + 1) * LANES)
        q2 = _stack_maps(q_ref[:, cols])
        kp = ck_ref[0, :, cols].astype(BF16)
        vp = cv_ref[0, :, cols].astype(BF16)
        s_p = lax.dot_general(q2, kp, _NT, preferred_element_type=F32)
        s_n = lax.dot_general(q2, kn_ref[:, cols], _NT, preferred_element_type=F32)
        m = jnp.maximum(jnp.max(s_p, axis=1, keepdims=True), jnp.max(s_n, axis=1, keepdims=True))
        p_p = jnp.exp(s_p - m)
        p_n = jnp.exp(s_n - m)
        l = jnp.sum(p_p, axis=1, keepdims=True) + jnp.sum(p_n, axis=1, keepdims=True)
        acc = (jnp.dot(p_p.astype(BF16), vp, preferred_element_type=F32)
               + jnp.dot(p_n.astype(BF16), vn_ref[:, cols], preferred_element_type=F32))
        outs.append(_finish_head(acc, l, lam, g_ref[...], za_ref[:, cols], t))
    o_ref[...] = jnp.concatenate(outs, axis=1).astype(BF16)


def _attn_sample(lams, g, q, kb, vb, ck, cv, za, *, n_seq, t, past):
    assert past % CHUNK == 0 and t <= CHUNK
    row = pl.BlockSpec((t, D_ATT), lambda b: (b, 0))
    cache = pl.BlockSpec((1, past, D_ATT), lambda b: (b, 0, 0))
    vec = _full((1, QK_DIM))
    return pl.pallas_call(
        functools.partial(_attn_sample_kernel, t=t),
        grid=(n_seq,),
        in_specs=[vec, vec, vec, vec, _full((1, V_DIM)), row, row, row, cache, cache, row],
        out_specs=row,
        out_shape=jax.ShapeDtypeStruct((n_seq * t, D_ATT), BF16),
        compiler_params=_params(("arbitrary",)),
        name="attn_sample",
    )(*lams, g, q, kb, vb, ck, cv, za)


def _out_proj_kernel(s_ref, a_ref, w_ref, x_ref, g_ref, y_ref):
    mix = (jnp.dot(s_ref[...], w_ref[:D_SSM, :], preferred_element_type=F32)
           + jnp.dot(a_ref[...], w_ref[D_SSM:, :], preferred_element_type=F32))
    ms = jnp.mean(mix * mix, axis=-1, keepdims=True)
    y_ref[...] = x_ref[...] + mix * lax.rsqrt(ms + EPS) * g_ref[...]


def _out_proj(ssm_out, att_out, w_bf, x2d, g, *, tm):
    n = x2d.shape[0]
    row = lambda i: (i, 0)
    half = pl.BlockSpec((tm, 512), row)
    full = pl.BlockSpec((tm, D_MODEL), row)
    return pl.pallas_call(
        _out_proj_kernel,
        grid=(n // tm,),
        in_specs=[half, half, _full((D_MODEL, D_MODEL)), full, _full((1, D_MODEL))],
        out_specs=full,
        out_shape=jax.ShapeDtypeStruct((n, D_MODEL), F32),
        compiler_params=_params(("arbitrary",)),
        name="out_proj",
    )(ssm_out, att_out, w_bf, x2d, g)


def _block_diag_weights(bbr, bbi, c_re, c_im):
    eye = jnp.eye(SSM_GROUPS // 2, dtype=F32)

    def b_side(m):
        m = m.reshape(2, SSM_GROUPS // 2, SSM_GROUP, SSM_STATE)
        return jnp.einsum('bgcp,gh->bgchp', m, eye).reshape(2, 256, N_CH // 2)

    def c_side(m):
        m = m.reshape(2, SSM_GROUPS // 2, SSM_GROUP, SSM_STATE)
        return jnp.einsum('bgcp,gh->bgphc', m, eye).reshape(2, N_CH // 2, 256)

    bdb = jnp.concatenate([b_side(bbr), b_side(bbi)], axis=2).astype(BF16)
    bdc = jnp.concatenate([c_side(c_re), -c_side(c_im)], axis=1).astype(BF16)
    return bdb, bdc


def kernel(x_prompt, x_sample, cache_k, cache_v, state_ssm_re, state_ssm_im, norm_pre_g, w_in, ssm_lambda_re,
           ssm_lambda_im, ssm_log_dt, ssm_b_re, ssm_b_im, ssm_c_re, ssm_c_im, ssm_d, glu_w1, glu_b1, glu_w2,
           glu_b2, lambda_q1, lambda_k1, lambda_q2, lambda_k2, attn_subln_g, w_out, norm_post_g):
    bp, sp, _ = x_prompt.shape
    bs, ss, _ = x_sample.shape
    past = cache_k.shape[2]

    a_re, a_im, bbr, bbi = _prep(ssm_lambda_re[0], ssm_lambda_im[0], ssm_log_dt[0], ssm_b_re[0], ssm_b_im[0])
    bdb, bdc = _block_diag_weights(bbr, bbi, ssm_c_re[0], ssm_c_im[0])
    a_re = a_re.reshape(1, N_CH)
    a_im = a_im.reshape(1, N_CH)
    dvec = ssm_d[0].reshape(1, D_SSM)
    w_in_bf = w_in[0].astype(BF16)
    w_out_bf = w_out[0].astype(BF16)
    w1 = glu_w1[0].astype(BF16)
    w2 = glu_w2[0].astype(BF16)
    b1 = glu_b1[0].reshape(1, D_SSM)
    b2 = glu_b2[0].reshape(1, D_SSM)
    g_pre = norm_pre_g[0].reshape(1, D_MODEL)
    g_post = norm_post_g[0].reshape(1, D_MODEL)
    g_sub = attn_subln_g[0].reshape(1, V_DIM)
    lams = tuple(v[0].reshape(1, QK_DIM) for v in (lambda_q1, lambda_k1, lambda_q2, lambda_k2))
    inv = ROPE_THETA ** (-jnp.arange(ROPE_DIM // 2, dtype=F32) * 2.0 / ROPE_DIM)
    inv_lane = jnp.tile(inv, LANES // (ROPE_DIM // 2)).reshape(1, LANES)

    def run(x, n_seq, seq_len, pos0, chained, h0r, h0i, tm):
        x2d = x.reshape(n_seq * seq_len, D_MODEL)
        u, zs, q, kf, kb, vf, vb, za = _in_proj(x2d, g_pre, w_in_bf, inv_lane, seq_len=seq_len, pos0=pos0, tm=tm)
        ssm_out, hfr, hfi = _s5(u, zs, bdb, bdc, dvec, a_re, a_im, w1, b1, w2, b2, h0r, h0i,
                                n_seq=n_seq, seq_len=seq_len, chained=chained)
        if chained:
            att = _attn_prompt(lams, g_sub, q, kb, vb, za, n_seq=n_seq, seq_len=seq_len, tq=256)
        else:
            ck = cache_k[0].reshape(n_seq, past, D_ATT)
            cv = cache_v[0].reshape(n_seq, past, D_ATT)
            att = _attn_sample(lams, g_sub, q, kb, vb, ck, cv, za, n_seq=n_seq, t=seq_len, past=past)
        y = _out_proj(ssm_out, att, w_out_bf, x2d, g_post, tm=tm)
        return (y.reshape(n_seq, seq_len, D_MODEL),
                kf.reshape(1, n_seq, seq_len, ATT_HEADS, 2, QK_DIM),
                vf.reshape(1, n_seq, seq_len, ATT_HEADS, V_DIM),
                hfr.reshape(1, n_seq, SSM_GROUPS, SSM_STATE),
                hfi.reshape(1, n_seq, SSM_GROUPS, SSM_STATE))

    zero = jnp.zeros((1, N_CH), F32)
    yp, kp, vp, hrp, hip = run(x_prompt, bp, sp, 0, True, zero, zero, 512)
    h0r = state_ssm_re[0].reshape(bs, N_CH)
    h0i = state_ssm_im[0].reshape(bs, N_CH)
    ys, ks, vs, hrs, his = run(x_sample, bs, ss, past, False, h0r, h0i, bs * ss)
    return (yp, ys, kp, vp, hrp, hip, ks, vs, hrs, his)
```

```python
import functools
import math

import jax
import jax.numpy as jnp
from jax import lax
from jax.experimental import pallas as pl
from jax.experimental.pallas import tpu as pltpu

F32 = jnp.float32
BF16 = jnp.bfloat16

D_MODEL = 1024
D_SSM = 512
D_ATT = 512
SSM_GROUP = 16
SSM_GROUPS = 32
SSM_STATE = 64
N_CH = SSM_GROUPS * SSM_STATE
ATT_HEADS = 4
QK_DIM = 64
V_DIM = 128
ROPE_DIM = 16
ROPE_THETA = 500000.0
CHUNK = 64
EPS = 1e-6
D_IN = 3072
LAMBDA_INIT = 0.8 - 0.6 * math.exp(-0.3 * 0)

LANES = 128
SUBLANES = 8
N_SLAB = N_CH // LANES
ROW_PAD = 4
NEG = -1e30
ATT_TILE = 256
HEADS_PER_STEP = 4
LOG2E = math.log2(math.e)
Q_SCALE = QK_DIM ** -0.5 * LOG2E
VMEM_LIMIT = 56 * 1024 * 1024


def _params(sem):
    return pltpu.CompilerParams(dimension_semantics=sem, vmem_limit_bytes=VMEM_LIMIT)


def _full(shape):
    n = len(shape)
    return pl.BlockSpec(shape, lambda *_: (0,) * n)


def _prep_kernel(lr_ref, li_ref, ldt_ref, br_ref, bi_ref, ar_ref, ai_ref, bbr_ref, bbi_ref):
    lr = lr_ref[...]
    li = li_ref[...]
    dt = jnp.exp(ldt_ref[...])
    mag = jnp.exp(lr * dt)
    ar = mag * jnp.cos(li * dt)
    ai = mag * jnp.sin(li * dt)
    den = lr * lr + li * li
    cr = ((ar - 1.0) * lr + ai * li) / den
    ci = (ai * lr - (ar - 1.0) * li) / den
    ar_ref[...] = ar
    ai_ref[...] = ai
    br = br_ref[...]
    bi = bi_ref[...]
    crb = cr[:, None, :]
    cib = ci[:, None, :]
    bbr_ref[...] = crb * br - cib * bi
    bbi_ref[...] = crb * bi + cib * br


def _prep(lam_re, lam_im, log_dt, b_re, b_im):
    g, p, c = b_re.shape
    brt = jnp.swapaxes(b_re, 1, 2)
    bit = jnp.swapaxes(b_im, 1, 2)
    out_shape = (jax.ShapeDtypeStruct((g, p), F32), jax.ShapeDtypeStruct((g, p), F32),
                 jax.ShapeDtypeStruct((g, c, p), F32), jax.ShapeDtypeStruct((g, c, p), F32))
    return pl.pallas_call(_prep_kernel, out_shape=out_shape, name="s5_prep")(
        lam_re, lam_im, log_dt.reshape(g, 1), brt, bit)


def _in_proj_kernel(x_ref, g_ref, w_ref, inv_ref, u_ref, zs_ref, q_ref, kf_ref, kb_ref, vf_ref,
                    vb_ref, za_ref, *, tm, seq_len, pos0, transposed_qv):
    i = pl.program_id(0)
    x = x_ref[...]
    ms = jnp.mean(x * x, axis=-1, keepdims=True)
    hn = (x * lax.rsqrt(ms + EPS) * g_ref[...]).astype(BF16)

    row = i * tm + lax.broadcasted_iota(jnp.int32, (tm, LANES), 0)
    pos = (pos0 + (row & (seq_len - 1))).astype(F32)
    ang = pos * inv_ref[...]
    cos = jnp.cos(ang)
    sin = jnp.sin(ang)
    lane = lax.broadcasted_iota(jnp.int32, (tm, LANES), 1) & (QK_DIM - 1)
    half = ROPE_DIM // 2
    c_m = jnp.where(lane < ROPE_DIM, cos, 1.0)
    s_lo = jnp.where(lane < half, -sin, 0.0)
    s_hi = jnp.where((lane >= half) & (lane < ROPE_DIM), sin, 0.0)

    def seg(lo, hi):
        return jnp.dot(hn, w_ref[:, lo:hi], preferred_element_type=F32)

    def rope(t):
        outs = []
        for h in range(ATT_HEADS):
            th = t[:, h * LANES:(h + 1) * LANES]
            outs.append(th * c_m + pltpu.roll(th, LANES - half, 1) * s_lo + pltpu.roll(th, half, 1) * s_hi)
        return jnp.concatenate(outs, axis=1)

    def put(ref, t):
        if not transposed_qv:
            ref[...] = t.astype(BF16)
            return
        for h in range(ATT_HEADS):
            tt = t[:, h * LANES:(h + 1) * LANES].T.astype(BF16)
            for c in range(tm // ATT_TILE):
                ref[h, c] = tt[:, c * ATT_TILE:(c + 1) * ATT_TILE]

    u_ref[...] = seg(0, D_SSM).astype(BF16)
    zs_ref[...] = seg(D_SSM, 2 * D_SSM).astype(BF16)
    q = rope(seg(1024, 1536))
    put(q_ref, q * Q_SCALE)
    k = rope(seg(1536, 2048))
    kf_ref[...] = k
    kb_ref[...] = k.astype(BF16)
    v = seg(2048, 2560)
    vf_ref[...] = v
    put(vb_ref, v)
    za_ref[...] = seg(2560, 3072).astype(BF16)


def _in_proj(x2d, g, w_bf, inv_lane, *, seq_len, pos0, tm, transposed_qv):
    n = x2d.shape[0]
    assert n % tm == 0 and seq_len & (seq_len - 1) == 0
    row = lambda i: (i, 0)
    o512 = pl.BlockSpec((tm, 512), row)
    shp = lambda dt: jax.ShapeDtypeStruct((n, 512), dt)
    if transposed_qv:
        assert seq_len % tm == 0 and tm % ATT_TILE == 0
        tps = seq_len // tm
        per = tm // ATT_TILE
        t_spec = pl.BlockSpec((None, ATT_HEADS, per, LANES, ATT_TILE), lambda i: (i // tps, 0, i % tps, 0, 0))
        t_shape = jax.ShapeDtypeStruct((n // seq_len, ATT_HEADS, seq_len // ATT_TILE, LANES, ATT_TILE), BF16)
    else:
        t_spec, t_shape = o512, shp(BF16)
    return pl.pallas_call(
        functools.partial(_in_proj_kernel, tm=tm, seq_len=seq_len, pos0=pos0, transposed_qv=transposed_qv),
        grid=(n // tm,),
        in_specs=[pl.BlockSpec((tm, D_MODEL), row), _full((1, D_MODEL)), _full((D_MODEL, D_IN)),
                  _full((1, LANES))],
        out_specs=[o512, o512, t_spec, o512, o512, o512, t_spec, o512],
        out_shape=[shp(BF16), shp(BF16), t_shape, shp(F32), shp(BF16), shp(F32), t_shape, shp(BF16)],
        compiler_params=_params(("arbitrary",)),
        name="in_proj",
    )(x2d, g, w_bf, inv_lane)


def _cmul(ar, ai, br, bi):
    return ar * br - ai * bi, ar * bi + ai * br


def _s5_kernel(u_ref, zs_ref, bdb_ref, bdc_ref, d_ref, are_ref, aim_ref, w1_ref, b1_ref, w2_ref, b2_ref,
               h0r_ref, h0i_ref, out_ref, hfr_ref, hfi_ref, hbuf, hb16, car_re, car_im,
               *, R, T, chained, slab_group):
    TP = T + ROW_PAD
    tm = R * T
    half_cols = N_CH // 2

    if chained:
        @pl.when(pl.program_id(1) == 0)
        def _():
            car_re[...] = jnp.zeros_like(car_re)
            car_im[...] = jnp.zeros_like(car_im)

    for b in range(2):
        bu = jnp.dot(u_ref[:, b * 256:(b + 1) * 256], bdb_ref[b], preferred_element_type=F32)
        for part in range(2):
            for k in range(N_SLAB // 2):
                slab = part * N_SLAB + b * (N_SLAB // 2) + k
                col = part * half_cols + k * LANES
                for j in range(R):
                    hbuf[slab, j * TP:j * TP + T, :] = bu[j * T:(j + 1) * T, col:col + LANES]

    finals_re = [None] * N_SLAB
    finals_im = [None] * N_SLAB
    for g0 in range(0, N_SLAB, slab_group):
        slabs = list(range(g0, g0 + slab_group))
        a_r = [jnp.broadcast_to(are_ref[:, c * LANES:(c + 1) * LANES], (R, LANES)) for c in slabs]
        a_i = [jnp.broadcast_to(aim_ref[:, c * LANES:(c + 1) * LANES], (R, LANES)) for c in slabs]

        def body(s, carry, slabs=slabs, a_r=a_r, a_i=a_i):
            new = []
            for n, c in enumerate(slabs):
                hr, hi = carry[n]
                rows = pl.ds(s, R, stride=TP)
                pr, pi = _cmul(a_r[n], a_i[n], hr, hi)
                nr = pr + hbuf[c, rows, :]
                ni = pi + hbuf[N_SLAB + c, rows, :]
                hbuf[c, rows, :] = nr
                hbuf[N_SLAB + c, rows, :] = ni
                new.append((nr, ni))
            return tuple(new)

        z = jnp.zeros((R, LANES), F32)
        fin = lax.fori_loop(0, T, body, tuple((z, z) for _ in slabs))
        for n, c in enumerate(slabs):
            finals_re[c], finals_im[c] = fin[n]

    f_re = jnp.concatenate(finals_re, axis=1)
    f_im = jnp.concatenate(finals_im, axis=1)

    at_r, at_i = are_ref[...], aim_ref[...]
    for _ in range(T.bit_length() - 1):
        at_r, at_i = _cmul(at_r, at_i, at_r, at_i)

    if chained:
        e_r, e_i = car_re[...], car_im[...]
        rows_r, rows_i = [], []
        for j in range(R):
            rows_r.append(e_r)
            rows_i.append(e_i)
            pr, pi = _cmul(at_r, at_i, e_r, e_i)
            e_r = pr + f_re[j:j + 1]
            e_i = pi + f_im[j:j + 1]
        ent_r = jnp.concatenate(rows_r, axis=0)
        ent_i = jnp.concatenate(rows_i, axis=0)
        car_re[...] = e_r
        car_im[...] = e_i
        hfr_ref[0] = e_r
        hfi_ref[0] = e_i
    else:
        ent_r, ent_i = h0r_ref[...], h0i_ref[...]
        pr, pi = _cmul(at_r, at_i, ent_r, ent_i)
        hfr_ref[...] = pr + f_re
        hfi_ref[...] = pi + f_im

    for g0 in range(0, N_SLAB, slab_group):
        slabs = list(range(g0, g0 + slab_group))
        a_r = [jnp.broadcast_to(are_ref[:, c * LANES:(c + 1) * LANES], (R, LANES)) for c in slabs]
        a_i = [jnp.broadcast_to(aim_ref[:, c * LANES:(c + 1) * LANES], (R, LANES)) for c in slabs]

        def fix(s, carry, slabs=slabs, a_r=a_r, a_i=a_i):
            new = []
            for n, c in enumerate(slabs):
                er, ei = carry[n]
                rows = pl.ds(s, R, stride=TP)
                er, ei = _cmul(a_r[n], a_i[n], er, ei)
                hbuf[c, rows, :] = hbuf[c, rows, :] + er
                hbuf[N_SLAB + c, rows, :] = hbuf[N_SLAB + c, rows, :] + ei
                new.append((er, ei))
            return tuple(new)

        init = tuple((ent_r[:, c * LANES:(c + 1) * LANES], ent_i[:, c * LANES:(c + 1) * LANES]) for c in slabs)
        lax.fori_loop(0, T, fix, init)

    for b in range(2):
        for part in range(2):
            for k in range(N_SLAB // 2):
                slab = part * N_SLAB + b * (N_SLAB // 2) + k
                col = b * N_CH + part * half_cols + k * LANES
                for j in range(R):
                    hb16[j * T:(j + 1) * T, col:col + LANES] = hbuf[slab, j * TP:j * TP + T, :].astype(BF16)
    ys = [jnp.dot(hb16[:, b * N_CH:(b + 1) * N_CH], bdc_ref[b], preferred_element_type=F32) for b in range(2)]
    y = jnp.concatenate(ys, axis=1) + u_ref[...].astype(F32) * d_ref[...]

    gb = jax.nn.gelu(y).astype(BF16)
    y1 = jnp.dot(gb, w1_ref[...], preferred_element_type=F32) + b1_ref[...]
    y2 = jnp.dot(gb, w2_ref[...], preferred_element_type=F32) + b2_ref[...]
    out_ref[...] = (y1 * jax.nn.sigmoid(y2) * jax.nn.silu(zs_ref[...].astype(F32))).astype(BF16)


def _s5(u, zs, bdb, bdc, dvec, a_re, a_im, w1, b1, w2, b2, h0r, h0i, *, n_seq, seq_len, chained):
    n = u.shape[0]
    if chained:
        R, T = SUBLANES, 64
        tm = R * T
        tiles = seq_len // tm
        grid = (n_seq, tiles)
        row = lambda b, t: (b * tiles + t, 0)
        hf_shape = jax.ShapeDtypeStruct((n_seq, 1, N_CH), F32)
        hf_spec = pl.BlockSpec((1, 1, N_CH), lambda b, t: (b, 0, 0))
        slab_group = 8
    else:
        R, T = n_seq, seq_len
        tm = R * T
        assert tm == n
        grid = (1, 1)
        row = lambda b, t: (0, 0)
        hf_shape = jax.ShapeDtypeStruct((n_seq, N_CH), F32)
        hf_spec = _full((n_seq, N_CH))
        slab_group = 4
    assert T & (T - 1) == 0 and R % SUBLANES == 0
    blk = pl.BlockSpec((tm, D_SSM), row)
    return pl.pallas_call(
        functools.partial(_s5_kernel, R=R, T=T, chained=chained, slab_group=slab_group),
        grid=grid,
        in_specs=[blk, blk, _full(bdb.shape), _full(bdc.shape), _full((1, D_SSM)), _full((1, N_CH)),
                  _full((1, N_CH)), _full((D_SSM, D_SSM)), _full((1, D_SSM)), _full((D_SSM, D_SSM)),
                  _full((1, D_SSM)), _full(h0r.shape), _full(h0i.shape)],
        out_specs=[blk, hf_spec, hf_spec],
        out_shape=[jax.ShapeDtypeStruct((n, D_SSM), BF16), hf_shape, hf_shape],
        scratch_shapes=[pltpu.VMEM((2 * N_SLAB, R * (T + ROW_PAD), LANES), F32),
                        pltpu.VMEM((tm, 2 * N_CH), BF16),
                        pltpu.VMEM((1, N_CH), F32), pltpu.VMEM((1, N_CH), F32)],
        compiler_params=_params(("arbitrary", "arbitrary")),
        name="s5_chained" if chained else "s5_independent",
    )(u, zs, bdb, bdc, dvec, a_re, a_im, w1, b1, w2, b2, h0r, h0i)


def _lambda(lq1, lk1, lq2, lk2):
    s1 = jnp.sum(lq1[...] * lk1[...], axis=1, keepdims=True)
    s2 = jnp.sum(lq2[...] * lk2[...], axis=1, keepdims=True)
    return jnp.exp(s1) - jnp.exp(s2) + LAMBDA_INIT


def _stack_maps(q):
    lane = lax.broadcasted_iota(jnp.int32, q.shape, 1)
    zero = jnp.zeros_like(q)
    return jnp.concatenate([jnp.where(lane < QK_DIM, q, zero), jnp.where(lane >= QK_DIM, q, zero)], axis=0)


def _subln_gate(o, g, za):
    ms = jnp.mean(o * o, axis=-1, keepdims=True)
    on = (o * lax.rsqrt(ms + EPS) * g) * (1.0 - LAMBDA_INIT)
    return on * jax.nn.silu(za.astype(F32))


def _finish_head(acc, l, lam, g, za, t):
    inv = 1.0 / l
    o = acc[:t] * inv[:t] - lam * (acc[t:] * inv[t:])
    return _subln_gate(o, g, za)


_NT = (((1,), (1,)), ((), ()))


def _attn_kernel(lq1, lk1, lq2, lk2, g_ref, qt_ref, k_ref, vt_ref, za_ref, o_ref, m_sc, l_sc, acc_sc):
    tq = ATT_TILE
    qi = pl.program_id(2)
    q2t = []
    for h in range(HEADS_PER_STEP):
        qt = qt_ref[h]
        row = lax.broadcasted_iota(jnp.int32, qt.shape, 0)
        zero = jnp.zeros_like(qt)
        q2t.append(jnp.concatenate([jnp.where(row < QK_DIM, qt, zero), jnp.where(row >= QK_DIM, qt, zero)], axis=1))
    m_sc[...] = jnp.full_like(m_sc, NEG)
    l_sc[...] = jnp.zeros_like(l_sc)
    acc_sc[...] = jnp.zeros_like(acc_sc)

    def step(kt, masked):
        start = pl.multiple_of(kt * tq, tq)
        scores = [jnp.dot(k_ref[pl.ds(start, tq), h * LANES:(h + 1) * LANES], q2t[h],
                          preferred_element_type=F32) for h in range(HEADS_PER_STEP)]
        for h in range(HEADS_PER_STEP):
            s = scores[h]
            if masked:
                kc = lax.broadcasted_iota(jnp.int32, s.shape, 0) // CHUNK
                qc = (lax.broadcasted_iota(jnp.int32, s.shape, 1) & (tq - 1)) // CHUNK
                s = jnp.where(kc <= qc, s, NEG)
            m_old = m_sc[h]
            m_new = jnp.maximum(m_old, jnp.max(s, axis=0, keepdims=True))
            alpha = jnp.exp2(m_old - m_new)
            p = jnp.exp2(s - m_new)
            l_sc[h] = alpha * l_sc[h] + jnp.sum(p, axis=0, keepdims=True)
            acc_sc[h] = alpha * acc_sc[h] + jnp.dot(vt_ref[h, kt], p.astype(BF16), preferred_element_type=F32)
            m_sc[h] = m_new

    def body(kt, c):
        step(kt, False)
        return c

    lax.fori_loop(0, qi, body, 0)
    step(qi, True)
    lam = _lambda(lq1, lk1, lq2, lk2)
    for h in range(HEADS_PER_STEP):
        acc = acc_sc[h]
        inv = 1.0 / l_sc[h]
        ot = acc[:, :tq] * inv[:, :tq] - lam * (acc[:, tq:] * inv[:, tq:])
        cols = slice(h * LANES, (h + 1) * LANES)
        o_ref[:, cols] = _subln_gate(ot.T, g_ref[...], za_ref[:, cols]).astype(BF16)


def _attn_prompt(lams, g, qt, kb, vt, za, *, n_seq, seq_len):
    tq = ATT_TILE
    nq = seq_len // tq
    hps = HEADS_PER_STEP
    assert tq % CHUNK == 0 and tq & (tq - 1) == 0 and ATT_HEADS % hps == 0
    rowspec = pl.BlockSpec((tq, hps * LANES), lambda b, h, i: (b * nq + i, h))
    qtspec = pl.BlockSpec((None, hps, None, LANES, tq), lambda b, h, i: (b, h, i, 0, 0))
    kspec = pl.BlockSpec((seq_len, hps * LANES), lambda b, h, i: (b, h))
    vtspec = pl.BlockSpec((None, hps, nq, LANES, tq), lambda b, h, i: (b, h, 0, 0, 0))
    vec = _full((1, QK_DIM))
    return pl.pallas_call(
        _attn_kernel,
        grid=(n_seq, ATT_HEADS // hps, nq),
        in_specs=[vec, vec, vec, vec, _full((1, V_DIM)), qtspec, kspec, vtspec, rowspec],
        out_specs=rowspec,
        out_shape=jax.ShapeDtypeStruct((n_seq * seq_len, D_ATT), BF16),
        scratch_shapes=[pltpu.VMEM((hps, 1, 2 * tq), F32), pltpu.VMEM((hps, 1, 2 * tq), F32),
                        pltpu.VMEM((hps, V_DIM, 2 * tq), F32)],
        compiler_params=_params(("arbitrary", "arbitrary", "arbitrary")),
        name="attn_prompt",
    )(*lams, g, qt, kb, vt, za)


def _attn_sample_kernel(lq1, lk1, lq2, lk2, g_ref, q_ref, kn_ref, vn_ref, ck_ref, cv_ref, za_ref, o_ref, *, t):
    lam = _lambda(lq1, lk1, lq2, lk2)
    outs = []
    for h in range(ATT_HEADS):
        cols = slice(h * LANES, (h + 1) * LANES)
        q2 = _stack_maps(q_ref[:, cols])
        kp = ck_ref[0, :, cols].astype(BF16)
        vp = cv_ref[0, :, cols].astype(BF16)
        s_p = lax.dot_general(q2, kp, _NT, preferred_element_type=F32)
        s_n = lax.dot_general(q2, kn_ref[:, cols], _NT, preferred_element_type=F32)
        m = jnp.maximum(jnp.max(s_p, axis=1, keepdims=True), jnp.max(s_n, axis=1, keepdims=True))
        p_p = jnp.exp2(s_p - m)
        p_n = jnp.exp2(s_n - m)
        l = jnp.sum(p_p, axis=1, keepdims=True) + jnp.sum(p_n, axis=1, keepdims=True)
        acc = (jnp.dot(p_p.astype(BF16), vp, preferred_element_type=F32)
               + jnp.dot(p_n.astype(BF16), vn_ref[:, cols], preferred_element_type=F32))
        outs.append(_finish_head(acc, l, lam, g_ref[...], za_ref[:, cols], t))
    o_ref[...] = jnp.concatenate(outs, axis=1).astype(BF16)


def _attn_sample(lams, g, q, kb, vb, ck, cv, za, *, n_seq, t, past):
    assert past % CHUNK == 0 and t <= CHUNK
    row = pl.BlockSpec((t, D_ATT), lambda b: (b, 0))
    cache = pl.BlockSpec((1, past, D_ATT), lambda b: (b, 0, 0))
    vec = _full((1, QK_DIM))
    return pl.pallas_call(
        functools.partial(_attn_sample_kernel, t=t),
        grid=(n_seq,),
        in_specs=[vec, vec, vec, vec, _full((1, V_DIM)), row, row, row, cache, cache, row],
        out_specs=row,
        out_shape=jax.ShapeDtypeStruct((n_seq * t, D_ATT), BF16),
        compiler_params=_params(("arbitrary",)),
        name="attn_sample",
    )(*lams, g, q, kb, vb, ck, cv, za)


def _out_proj_kernel(s_ref, a_ref, w_ref, x_ref, g_ref, y_ref):
    mix = (jnp.dot(s_ref[...], w_ref[:D_SSM, :], preferred_element_type=F32)
           + jnp.dot(a_ref[...], w_ref[D_SSM:, :], preferred_element_type=F32))
    ms = jnp.mean(mix * mix, axis=-1, keepdims=True)
    y_ref[...] = x_ref[...] + mix * lax.rsqrt(ms + EPS) * g_ref[...]


def _out_proj(ssm_out, att_out, w_bf, x2d, g, *, tm):
    n = x2d.shape[0]
    row = lambda i: (i, 0)
    half = pl.BlockSpec((tm, 512), row)
    full = pl.BlockSpec((tm, D_MODEL), row)
    return pl.pallas_call(
        _out_proj_kernel,
        grid=(n // tm,),
        in_specs=[half, half, _full((D_MODEL, D_MODEL)), full, _full((1, D_MODEL))],
        out_specs=full,
        out_shape=jax.ShapeDtypeStruct((n, D_MODEL), F32),
        compiler_params=_params(("arbitrary",)),
        name="out_proj",
    )(ssm_out, att_out, w_bf, x2d, g)


def _block_diag_weights(bbr, bbi, c_re, c_im):
    eye = jnp.eye(SSM_GROUPS // 2, dtype=F32)

    def b_side(m):
        m = m.reshape(2, SSM_GROUPS // 2, SSM_GROUP, SSM_STATE)
        return jnp.einsum('bgcp,gh->bgchp', m, eye).reshape(2, 256, N_CH // 2)

    def c_side(m):
        m = m.reshape(2, SSM_GROUPS // 2, SSM_GROUP, SSM_STATE)
        return jnp.einsum('bgcp,gh->bgphc', m, eye).reshape(2, N_CH // 2, 256)

    bdb = jnp.concatenate([b_side(bbr), b_side(bbi)], axis=2).astype(BF16)
    bdc = jnp.concatenate([c_side(c_re), -c_side(c_im)], axis=1).astype(BF16)
    return bdb, bdc


def kernel(x_prompt, x_sample, cache_k, cache_v, state_ssm_re, state_ssm_im, norm_pre_g, w_in, ssm_lambda_re,
           ssm_lambda_im, ssm_log_dt, ssm_b_re, ssm_b_im, ssm_c_re, ssm_c_im, ssm_d, glu_w1, glu_b1, glu_w2,
           glu_b2, lambda_q1, lambda_k1, lambda_q2, lambda_k2, attn_subln_g, w_out, norm_post_g):
    bp, sp, _ = x_prompt.shape
    bs, ss, _ = x_sample.shape
    past = cache_k.shape[2]

    a_re, a_im, bbr, bbi = _prep(ssm_lambda_re[0], ssm_lambda_im[0], ssm_log_dt[0], ssm_b_re[0], ssm_b_im[0])
    bdb, bdc = _block_diag_weights(bbr, bbi, ssm_c_re[0], ssm_c_im[0])
    a_re = a_re.reshape(1, N_CH)
    a_im = a_im.reshape(1, N_CH)
    dvec = ssm_d[0].reshape(1, D_SSM)
    w_in_bf = w_in[0].astype(BF16)
    w_out_bf = w_out[0].astype(BF16)
    w1 = glu_w1[0].astype(BF16)
    w2 = glu_w2[0].astype(BF16)
    b1 = glu_b1[0].reshape(1, D_SSM)
    b2 = glu_b2[0].reshape(1, D_SSM)
    g_pre = norm_pre_g[0].reshape(1, D_MODEL)
    g_post = norm_post_g[0].reshape(1, D_MODEL)
    g_sub = attn_subln_g[0].reshape(1, V_DIM)
    lams = tuple(v[0].reshape(1, QK_DIM) for v in (lambda_q1, lambda_k1, lambda_q2, lambda_k2))
    inv = ROPE_THETA ** (-jnp.arange(ROPE_DIM // 2, dtype=F32) * 2.0 / ROPE_DIM)
    inv_lane = jnp.tile(inv, LANES // (ROPE_DIM // 2)).reshape(1, LANES)

    def run(x, n_seq, seq_len, pos0, chained, h0r, h0i, tm):
        x2d = x.reshape(n_seq * seq_len, D_MODEL)
        u, zs, q, kf, kb, vf, vb, za = _in_proj(x2d, g_pre, w_in_bf, inv_lane, seq_len=seq_len, pos0=pos0, tm=tm,
                                                transposed_qv=chained)
        ssm_out, hfr, hfi = _s5(u, zs, bdb, bdc, dvec, a_re, a_im, w1, b1, w2, b2, h0r, h0i,
                                n_seq=n_seq, seq_len=seq_len, chained=chained)
        if chained:
            att = _attn_prompt(lams, g_sub, q, kb, vb, za, n_seq=n_seq, seq_len=seq_len)
        else:
            ck = cache_k[0].reshape(n_seq, past, D_ATT)
            cv = cache_v[0].reshape(n_seq, past, D_ATT)
            att = _attn_sample(lams, g_sub, q, kb, vb, ck, cv, za, n_seq=n_seq, t=seq_len, past=past)
        y = _out_proj(ssm_out, att, w_out_bf, x2d, g_post, tm=tm)
        return (y.reshape(n_seq, seq_len, D_MODEL),
                kf.reshape(1, n_seq, seq_len, ATT_HEADS, 2, QK_DIM),
                vf.reshape(1, n_seq, seq_len, ATT_HEADS, V_DIM),
                hfr.reshape(1, n_seq, SSM_GROUPS, SSM_STATE),
                hfi.reshape(1, n_seq, SSM_GROUPS, SSM_STATE))

    zero = jnp.zeros((1, N_CH), F32)
    yp, kp, vp, hrp, hip = run(x_prompt, bp, sp, 0, True, zero, zero, 512)
    h0r = state_ssm_re[0].reshape(bs, N_CH)
    h0i = state_ssm_im[0].reshape(bs, N_CH)
    ys, ks, vs, hrs, his = run(x_sample, bs, ss, past, False, h0r, h0i, bs * ss)
    return (yp, ys, kp, vp, hrp, hip, ks, vs, hrs, his)
```

```python
import functools
import math

import jax
import jax.numpy as jnp
from jax import lax
from jax.experimental import pallas as pl
from jax.experimental.pallas import tpu as pltpu

F32 = jnp.float32
BF16 = jnp.bfloat16

D_MODEL = 1024
D_SSM = 512
D_ATT = 512
SSM_GROUP = 16
SSM_GROUPS = 32
SSM_STATE = 64
N_CH = SSM_GROUPS * SSM_STATE
ATT_HEADS = 4
QK_DIM = 64
V_DIM = 128
ROPE_DIM = 16
ROPE_THETA = 500000.0
CHUNK = 64
EPS = 1e-6
D_IN = 3072
LAMBDA_INIT = 0.8 - 0.6 * math.exp(-0.3 * 0)

LANES = 128
SUBLANES = 8
N_SLAB = N_CH // LANES
ROW_PAD = 4
NEG = -1e30
ATT_TILE = 256
HEADS_PER_STEP = 4
LOG2E = math.log2(math.e)
Q_SCALE = QK_DIM ** -0.5 * LOG2E
VMEM_LIMIT = 56 * 1024 * 1024


def _params(sem):
    return pltpu.CompilerParams(dimension_semantics=sem, vmem_limit_bytes=VMEM_LIMIT)


def _full(shape):
    n = len(shape)
    return pl.BlockSpec(shape, lambda *_: (0,) * n)


def _prep_kernel(lr_ref, li_ref, ldt_ref, br_ref, bi_ref, ar_ref, ai_ref, bbr_ref, bbi_ref):
    lr = lr_ref[...]
    li = li_ref[...]
    dt = jnp.exp(ldt_ref[...])
    mag = jnp.exp(lr * dt)
    ar = mag * jnp.cos(li * dt)
    ai = mag * jnp.sin(li * dt)
    den = lr * lr + li * li
    cr = ((ar - 1.0) * lr + ai * li) / den
    ci = (ai * lr - (ar - 1.0) * li) / den
    ar_ref[...] = ar
    ai_ref[...] = ai
    br = br_ref[...]
    bi = bi_ref[...]
    crb = cr[:, None, :]
    cib = ci[:, None, :]
    bbr_ref[...] = crb * br - cib * bi
    bbi_ref[...] = crb * bi + cib * br


def _prep(lam_re, lam_im, log_dt, b_re, b_im):
    g, p, c = b_re.shape
    brt = jnp.swapaxes(b_re, 1, 2)
    bit = jnp.swapaxes(b_im, 1, 2)
    out_shape = (jax.ShapeDtypeStruct((g, p), F32), jax.ShapeDtypeStruct((g, p), F32),
                 jax.ShapeDtypeStruct((g, c, p), F32), jax.ShapeDtypeStruct((g, c, p), F32))
    return pl.pallas_call(_prep_kernel, out_shape=out_shape, name="s5_prep")(
        lam_re, lam_im, log_dt.reshape(g, 1), brt, bit)


def _in_proj_kernel(x_ref, g_ref, w_ref, inv_ref, u_ref, zs_ref, q_ref, kf_ref, kb_ref, vf_ref,
                    vb_ref, za_ref, *, tm, seq_len, pos0, transposed_qv):
    i = pl.program_id(0)
    x = x_ref[...]
    ms = jnp.mean(x * x, axis=-1, keepdims=True)
    hn = (x * lax.rsqrt(ms + EPS) * g_ref[...]).astype(BF16)

    row = i * tm + lax.broadcasted_iota(jnp.int32, (tm, LANES), 0)
    pos = (pos0 + (row & (seq_len - 1))).astype(F32)
    ang = pos * inv_ref[...]
    cos = jnp.cos(ang)
    sin = jnp.sin(ang)
    lane = lax.broadcasted_iota(jnp.int32, (tm, LANES), 1) & (QK_DIM - 1)
    half = ROPE_DIM // 2
    c_m = jnp.where(lane < ROPE_DIM, cos, 1.0)
    s_lo = jnp.where(lane < half, -sin, 0.0)
    s_hi = jnp.where((lane >= half) & (lane < ROPE_DIM), sin, 0.0)

    def seg(lo, hi):
        return jnp.dot(hn, w_ref[:, lo:hi], preferred_element_type=F32)

    def rope(t):
        outs = []
        for h in range(ATT_HEADS):
            th = t[:, h * LANES:(h + 1) * LANES]
            outs.append(th * c_m + pltpu.roll(th, LANES - half, 1) * s_lo + pltpu.roll(th, half, 1) * s_hi)
        return jnp.concatenate(outs, axis=1)

    def put(ref, t):
        if not transposed_qv:
            ref[...] = t.astype(BF16)
            return
        for h in range(ATT_HEADS):
            tt = t[:, h * LANES:(h + 1) * LANES].T.astype(BF16)
            for c in range(tm // ATT_TILE):
                ref[h, c] = tt[:, c * ATT_TILE:(c + 1) * ATT_TILE]

    u_ref[...] = seg(0, D_SSM).astype(BF16)
    zs_ref[...] = seg(D_SSM, 2 * D_SSM).astype(BF16)
    q = rope(seg(1024, 1536))
    put(q_ref, q * Q_SCALE)
    k = rope(seg(1536, 2048))
    if transposed_qv:
        for h in range(ATT_HEADS):
            kt = k[:, h * LANES:(h + 1) * LANES].T
            kf_ref[h, 0] = kt[:QK_DIM]
            kf_ref[h, 1] = kt[QK_DIM:]
    else:
        kf_ref[...] = k
    kb_ref[...] = k.astype(BF16)
    v = seg(2048, 2560)
    for h in range(ATT_HEADS):
        vf_ref[:, h, :] = v[:, h * LANES:(h + 1) * LANES]
    put(vb_ref, v)
    za_ref[...] = seg(2560, 3072).astype(BF16)


def _in_proj(x2d, g, w_bf, inv_lane, *, seq_len, pos0, tm, transposed_qv):
    n = x2d.shape[0]
    assert n % tm == 0 and seq_len & (seq_len - 1) == 0
    row = lambda i: (i, 0)
    o512 = pl.BlockSpec((tm, 512), row)
    shp = lambda dt: jax.ShapeDtypeStruct((n, 512), dt)
    if transposed_qv:
        assert seq_len % tm == 0 and tm % ATT_TILE == 0
        tps = seq_len // tm
        per = tm // ATT_TILE
        t_spec = pl.BlockSpec((None, ATT_HEADS, per, LANES, ATT_TILE), lambda i: (i // tps, 0, i % tps, 0, 0))
        t_shape = jax.ShapeDtypeStruct((n // seq_len, ATT_HEADS, seq_len // ATT_TILE, LANES, ATT_TILE), BF16)
        kf_spec = pl.BlockSpec((None, ATT_HEADS, 2, QK_DIM, tm), lambda i: (i // tps, 0, 0, 0, i % tps))
        kf_shape = jax.ShapeDtypeStruct((n // seq_len, ATT_HEADS, 2, QK_DIM, seq_len), F32)
    else:
        t_spec, t_shape = o512, shp(BF16)
        kf_spec, kf_shape = o512, shp(F32)
    vf_spec = pl.BlockSpec((tm, ATT_HEADS, V_DIM), lambda i: (i, 0, 0))
    vf_shape = jax.ShapeDtypeStruct((n, ATT_HEADS, V_DIM), F32)
    return pl.pallas_call(
        functools.partial(_in_proj_kernel, tm=tm, seq_len=seq_len, pos0=pos0, transposed_qv=transposed_qv),
        grid=(n // tm,),
        in_specs=[pl.BlockSpec((tm, D_MODEL), row), _full((1, D_MODEL)), _full((D_MODEL, D_IN)),
                  _full((1, LANES))],
        out_specs=[o512, o512, t_spec, kf_spec, o512, vf_spec, t_spec, o512],
        out_shape=[shp(BF16), shp(BF16), t_shape, kf_shape, shp(BF16), vf_shape, t_shape, shp(BF16)],
        compiler_params=_params(("arbitrary",)),
        name="in_proj",
    )(x2d, g, w_bf, inv_lane)


def _cmul(ar, ai, br, bi):
    return ar * br - ai * bi, ar * bi + ai * br


def _s5_kernel(u_ref, zs_ref, bdb_ref, bdc_ref, d_ref, are_ref, aim_ref, w1_ref, b1_ref, w2_ref, b2_ref,
               h0r_ref, h0i_ref, out_ref, hfr_ref, hfi_ref, hbuf, hb16, car_re, car_im,
               *, R, T, chained, slab_group):
    TP = T + ROW_PAD
    tm = R * T
    half_cols = N_CH // 2

    if chained:
        @pl.when(pl.program_id(1) == 0)
        def _():
            car_re[...] = jnp.zeros_like(car_re)
            car_im[...] = jnp.zeros_like(car_im)

    for b in range(2):
        bu = jnp.dot(u_ref[:, b * 256:(b + 1) * 256], bdb_ref[b], preferred_element_type=F32)
        for part in range(2):
            for k in range(N_SLAB // 2):
                slab = part * N_SLAB + b * (N_SLAB // 2) + k
                col = part * half_cols + k * LANES
                for j in range(R):
                    hbuf[slab, j * TP:j * TP + T, :] = bu[j * T:(j + 1) * T, col:col + LANES]

    finals_re = [None] * N_SLAB
    finals_im = [None] * N_SLAB
    for g0 in range(0, N_SLAB, slab_group):
        slabs = list(range(g0, g0 + slab_group))
        a_r = [jnp.broadcast_to(are_ref[:, c * LANES:(c + 1) * LANES], (R, LANES)) for c in slabs]
        a_i = [jnp.broadcast_to(aim_ref[:, c * LANES:(c + 1) * LANES], (R, LANES)) for c in slabs]

        def body(s, carry, slabs=slabs, a_r=a_r, a_i=a_i):
            new = []
            for n, c in enumerate(slabs):
                hr, hi = carry[n]
                rows = pl.ds(s, R, stride=TP)
                pr, pi = _cmul(a_r[n], a_i[n], hr, hi)
                nr = pr + hbuf[c, rows, :]
                ni = pi + hbuf[N_SLAB + c, rows, :]
                hbuf[c, rows, :] = nr
                hbuf[N_SLAB + c, rows, :] = ni
                new.append((nr, ni))
            return tuple(new)

        z = jnp.zeros((R, LANES), F32)
        fin = lax.fori_loop(0, T, body, tuple((z, z) for _ in slabs))
        for n, c in enumerate(slabs):
            finals_re[c], finals_im[c] = fin[n]

    f_re = jnp.concatenate(finals_re, axis=1)
    f_im = jnp.concatenate(finals_im, axis=1)

    at_r, at_i = are_ref[...], aim_ref[...]
    for _ in range(T.bit_length() - 1):
        at_r, at_i = _cmul(at_r, at_i, at_r, at_i)

    if chained:
        e_r, e_i = car_re[...], car_im[...]
        rows_r, rows_i = [], []
        for j in range(R):
            rows_r.append(e_r)
            rows_i.append(e_i)
            pr, pi = _cmul(at_r, at_i, e_r, e_i)
            e_r = pr + f_re[j:j + 1]
            e_i = pi + f_im[j:j + 1]
        ent_r = jnp.concatenate(rows_r, axis=0)
        ent_i = jnp.concatenate(rows_i, axis=0)
        car_re[...] = e_r
        car_im[...] = e_i
        hfr_ref[0] = e_r
        hfi_ref[0] = e_i
    else:
        ent_r, ent_i = h0r_ref[...], h0i_ref[...]
        pr, pi = _cmul(at_r, at_i, ent_r, ent_i)
        hfr_ref[...] = pr + f_re
        hfi_ref[...] = pi + f_im

    for g0 in range(0, N_SLAB, slab_group):
        slabs = list(range(g0, g0 + slab_group))
        a_r = [jnp.broadcast_to(are_ref[:, c * LANES:(c + 1) * LANES], (R, LANES)) for c in slabs]
        a_i = [jnp.broadcast_to(aim_ref[:, c * LANES:(c + 1) * LANES], (R, LANES)) for c in slabs]

        def fix(s, carry, slabs=slabs, a_r=a_r, a_i=a_i):
            new = []
            for n, c in enumerate(slabs):
                er, ei = carry[n]
                rows = pl.ds(s, R, stride=TP)
                er, ei = _cmul(a_r[n], a_i[n], er, ei)
                hbuf[c, rows, :] = hbuf[c, rows, :] + er
                hbuf[N_SLAB + c, rows, :] = hbuf[N_SLAB + c, rows, :] + ei
                new.append((er, ei))
            return tuple(new)

        init = tuple((ent_r[:, c * LANES:(c + 1) * LANES], ent_i[:, c * LANES:(c + 1) * LANES]) for c in slabs)
        lax.fori_loop(0, T, fix, init)

    for b in range(2):
        for part in range(2):
            for k in range(N_SLAB // 2):
                slab = part * N_SLAB + b * (N_SLAB // 2) + k
                col = b * N_CH + part * half_cols + k * LANES
                for j in range(R):
                    hb16[j * T:(j + 1) * T, col:col + LANES] = hbuf[slab, j * TP:j * TP + T, :].astype(BF16)
    ys = [jnp.dot(hb16[:, b * N_CH:(b + 1) * N_CH], bdc_ref[b], preferred_element_type=F32) for b in range(2)]
    y = jnp.concatenate(ys, axis=1) + u_ref[...].astype(F32) * d_ref[...]

    gb = jax.nn.gelu(y).astype(BF16)
    y1 = jnp.dot(gb, w1_ref[...], preferred_element_type=F32) + b1_ref[...]
    y2 = jnp.dot(gb, w2_ref[...], preferred_element_type=F32) + b2_ref[...]
    out_ref[...] = (y1 * jax.nn.sigmoid(y2) * jax.nn.silu(zs_ref[...].astype(F32))).astype(BF16)


def _s5(u, zs, bdb, bdc, dvec, a_re, a_im, w1, b1, w2, b2, h0r, h0i, *, n_seq, seq_len, chained):
    n = u.shape[0]
    if chained:
        R, T = SUBLANES, 64
        tm = R * T
        tiles = seq_len // tm
        grid = (n_seq, tiles)
        row = lambda b, t: (b * tiles + t, 0)
        hf_shape = jax.ShapeDtypeStruct((n_seq, 1, N_CH), F32)
        hf_spec = pl.BlockSpec((1, 1, N_CH), lambda b, t: (b, 0, 0))
        slab_group = 8
    else:
        R, T = n_seq, seq_len
        tm = R * T
        assert tm == n
        grid = (1, 1)
        row = lambda b, t: (0, 0)
        hf_shape = jax.ShapeDtypeStruct((n_seq, N_CH), F32)
        hf_spec = _full((n_seq, N_CH))
        slab_group = 4
    assert T & (T - 1) == 0 and R % SUBLANES == 0
    blk = pl.BlockSpec((tm, D_SSM), row)
    return pl.pallas_call(
        functools.partial(_s5_kernel, R=R, T=T, chained=chained, slab_group=slab_group),
        grid=grid,
        in_specs=[blk, blk, _full(bdb.shape), _full(bdc.shape), _full((1, D_SSM)), _full((1, N_CH)),
                  _full((1, N_CH)), _full((D_SSM, D_SSM)), _full((1, D_SSM)), _full((D_SSM, D_SSM)),
                  _full((1, D_SSM)), _full(h0r.shape), _full(h0i.shape)],
        out_specs=[blk, hf_spec, hf_spec],
        out_shape=[jax.ShapeDtypeStruct((n, D_SSM), BF16), hf_shape, hf_shape],
        scratch_shapes=[pltpu.VMEM((2 * N_SLAB, R * (T + ROW_PAD), LANES), F32),
                        pltpu.VMEM((tm, 2 * N_CH), BF16),
                        pltpu.VMEM((1, N_CH), F32), pltpu.VMEM((1, N_CH), F32)],
        compiler_params=_params(("arbitrary", "arbitrary")),
        name="s5_chained" if chained else "s5_independent",
    )(u, zs, bdb, bdc, dvec, a_re, a_im, w1, b1, w2, b2, h0r, h0i)


def _lambda(lq1, lk1, lq2, lk2):
    s1 = jnp.sum(lq1[...] * lk1[...], axis=1, keepdims=True)
    s2 = jnp.sum(lq2[...] * lk2[...], axis=1, keepdims=True)
    return jnp.exp(s1) - jnp.exp(s2) + LAMBDA_INIT


def _stack_maps(q):
    lane = lax.broadcasted_iota(jnp.int32, q.shape, 1)
    zero = jnp.zeros_like(q)
    return jnp.concatenate([jnp.where(lane < QK_DIM, q, zero), jnp.where(lane >= QK_DIM, q, zero)], axis=0)


def _subln_gate(o, g, za):
    ms = jnp.mean(o * o, axis=-1, keepdims=True)
    on = (o * lax.rsqrt(ms + EPS) * g) * (1.0 - LAMBDA_INIT)
    return on * jax.nn.silu(za.astype(F32))


def _finish_head(acc, l, lam, g, za, t):
    inv = 1.0 / l
    o = acc[:t] * inv[:t] - lam * (acc[t:] * inv[t:])
    return _subln_gate(o, g, za)


_NT = (((1,), (1,)), ((), ()))


def _attn_kernel(lq1, lk1, lq2, lk2, g_ref, qt_ref, k_ref, vt_ref, za_ref, o_ref, m_sc, l_sc, acc_sc):
    tq = ATT_TILE
    qi = pl.program_id(2)
    q2t = []
    for h in range(HEADS_PER_STEP):
        qt = qt_ref[h]
        row = lax.broadcasted_iota(jnp.int32, qt.shape, 0)
        zero = jnp.zeros_like(qt)
        q2t.append(jnp.concatenate([jnp.where(row < QK_DIM, qt, zero), jnp.where(row >= QK_DIM, qt, zero)], axis=1))
    m_sc[...] = jnp.full_like(m_sc, NEG)
    l_sc[...] = jnp.zeros_like(l_sc)
    acc_sc[...] = jnp.zeros_like(acc_sc)

    def step(kt, masked):
        start = pl.multiple_of(kt * tq, tq)
        scores = [jnp.dot(k_ref[pl.ds(start, tq), h * LANES:(h + 1) * LANES], q2t[h],
                          preferred_element_type=F32) for h in range(HEADS_PER_STEP)]
        for h in range(HEADS_PER_STEP):
            s = scores[h]
            if masked:
                kc = lax.broadcasted_iota(jnp.int32, s.shape, 0) // CHUNK
                qc = (lax.broadcasted_iota(jnp.int32, s.shape, 1) & (tq - 1)) // CHUNK
                s = jnp.where(kc <= qc, s, NEG)
            m_old = m_sc[h]
            m_new = jnp.maximum(m_old, jnp.max(s, axis=0, keepdims=True))
            alpha = jnp.exp2(m_old - m_new)
            p = jnp.exp2(s - m_new)
            l_sc[h] = alpha * l_sc[h] + jnp.sum(p, axis=0, keepdims=True)
            acc_sc[h] = alpha * acc_sc[h] + jnp.dot(vt_ref[h, kt], p.astype(BF16), preferred_element_type=F32)
            m_sc[h] = m_new

    def body(kt, c):
        step(kt, False)
        return c

    lax.fori_loop(0, qi, body, 0)
    step(qi, True)
    lam = _lambda(lq1, lk1, lq2, lk2)
    for h in range(HEADS_PER_STEP):
        acc = acc_sc[h]
        inv = 1.0 / l_sc[h]
        ot = acc[:, :tq] * inv[:, :tq] - lam * (acc[:, tq:] * inv[:, tq:])
        cols = slice(h * LANES, (h + 1) * LANES)
        o_ref[:, cols] = _subln_gate(ot.T, g_ref[...], za_ref[:, cols]).astype(BF16)


def _attn_prompt(lams, g, qt, kb, vt, za, *, n_seq, seq_len):
    tq = ATT_TILE
    nq = seq_len // tq
    hps = HEADS_PER_STEP
    assert tq % CHUNK == 0 and tq & (tq - 1) == 0 and ATT_HEADS % hps == 0
    rowspec = pl.BlockSpec((tq, hps * LANES), lambda b, h, i: (b * nq + i, h))
    qtspec = pl.BlockSpec((None, hps, None, LANES, tq), lambda b, h, i: (b, h, i, 0, 0))
    kspec = pl.BlockSpec((seq_len, hps * LANES), lambda b, h, i: (b, h))
    vtspec = pl.BlockSpec((None, hps, nq, LANES, tq), lambda b, h, i: (b, h, 0, 0, 0))
    vec = _full((1, QK_DIM))
    return pl.pallas_call(
        _attn_kernel,
        grid=(n_seq, ATT_HEADS // hps, nq),
        in_specs=[vec, vec, vec, vec, _full((1, V_DIM)), qtspec, kspec, vtspec, rowspec],
        out_specs=rowspec,
        out_shape=jax.ShapeDtypeStruct((n_seq * seq_len, D_ATT), BF16),
        scratch_shapes=[pltpu.VMEM((hps, 1, 2 * tq), F32), pltpu.VMEM((hps, 1, 2 * tq), F32),
                        pltpu.VMEM((hps, V_DIM, 2 * tq), F32)],
        compiler_params=_params(("arbitrary", "arbitrary", "arbitrary")),
        name="attn_prompt",
    )(*lams, g, qt, kb, vt, za)


def _attn_sample_kernel(lq1, lk1, lq2, lk2, g_ref, q_ref, kn_ref, vn_ref, ck_ref, cv_ref, za_ref, o_ref, *, t):
    lam = _lambda(lq1, lk1, lq2, lk2)
    outs = []
    for h in range(ATT_HEADS):
        cols = slice(h * LANES, (h + 1) * LANES)
        q2 = _stack_maps(q_ref[:, cols])
        kpt = jnp.concatenate([ck_ref[h, 0], ck_ref[h, 1]], axis=0).astype(BF16)
        vp = cv_ref[:, h, :].astype(BF16)
        s_p = jnp.dot(q2, kpt, preferred_element_type=F32)
        s_n = lax.dot_general(q2, kn_ref[:, cols], _NT, preferred_element_type=F32)
        m = jnp.maximum(jnp.max(s_p, axis=1, keepdims=True), jnp.max(s_n, axis=1, keepdims=True))
        p_p = jnp.exp2(s_p - m)
        p_n = jnp.exp2(s_n - m)
        l = jnp.sum(p_p, axis=1, keepdims=True) + jnp.sum(p_n, axis=1, keepdims=True)
        acc = (jnp.dot(p_p.astype(BF16), vp, preferred_element_type=F32)
               + jnp.dot(p_n.astype(BF16), vn_ref[:, cols], preferred_element_type=F32))
        outs.append(_finish_head(acc, l, lam, g_ref[...], za_ref[:, cols], t))
    o_ref[...] = jnp.concatenate(outs, axis=1).astype(BF16)


def _attn_sample(lams, g, q, kb, vb, ck, cv, za, *, n_seq, t, past):
    assert past % CHUNK == 0 and t <= CHUNK
    row = pl.BlockSpec((t, D_ATT), lambda b: (b, 0))
    kcache = pl.BlockSpec((None, ATT_HEADS, 2, QK_DIM, past), lambda b: (b, 0, 0, 0, 0))
    vcache = pl.BlockSpec((None, past, ATT_HEADS, V_DIM), lambda b: (b, 0, 0, 0))
    vec = _full((1, QK_DIM))
    return pl.pallas_call(
        functools.partial(_attn_sample_kernel, t=t),
        grid=(n_seq,),
        in_specs=[vec, vec, vec, vec, _full((1, V_DIM)), row, row, row, kcache, vcache, row],
        out_specs=row,
        out_shape=jax.ShapeDtypeStruct((n_seq * t, D_ATT), BF16),
        compiler_params=_params(("arbitrary",)),
        name="attn_sample",
    )(*lams, g, q, kb, vb, ck, cv, za)


def _out_proj_kernel(s_ref, a_ref, w_ref, x_ref, g_ref, y_ref):
    mix = (jnp.dot(s_ref[...], w_ref[:D_SSM, :], preferred_element_type=F32)
           + jnp.dot(a_ref[...], w_ref[D_SSM:, :], preferred_element_type=F32))
    ms = jnp.mean(mix * mix, axis=-1, keepdims=True)
    y_ref[...] = x_ref[...] + mix * lax.rsqrt(ms + EPS) * g_ref[...]


def _out_proj(ssm_out, att_out, w_bf, x2d, g, *, tm):
    n = x2d.shape[0]
    row = lambda i: (i, 0)
    half = pl.BlockSpec((tm, 512), row)
    full = pl.BlockSpec((tm, D_MODEL), row)
    return pl.pallas_call(
        _out_proj_kernel,
        grid=(n // tm,),
        in_specs=[half, half, _full((D_MODEL, D_MODEL)), full, _full((1, D_MODEL))],
        out_specs=full,
        out_shape=jax.ShapeDtypeStruct((n, D_MODEL), F32),
        compiler_params=_params(("arbitrary",)),
        name="out_proj",
    )(ssm_out, att_out, w_bf, x2d, g)


def _block_diag_weights(bbr, bbi, c_re, c_im):
    eye = jnp.eye(SSM_GROUPS // 2, dtype=F32)

    def b_side(m):
        m = m.reshape(2, SSM_GROUPS // 2, SSM_GROUP, SSM_STATE)
        return jnp.einsum('bgcp,gh->bgchp', m, eye).reshape(2, 256, N_CH // 2)

    def c_side(m):
        m = m.reshape(2, SSM_GROUPS // 2, SSM_GROUP, SSM_STATE)
        return jnp.einsum('bgcp,gh->bgphc', m, eye).reshape(2, N_CH // 2, 256)

    bdb = jnp.concatenate([b_side(bbr), b_side(bbi)], axis=2).astype(BF16)
    bdc = jnp.concatenate([c_side(c_re), -c_side(c_im)], axis=1).astype(BF16)
    return bdb, bdc


def kernel(x_prompt, x_sample, cache_k, cache_v, state_ssm_re, state_ssm_im, norm_pre_g, w_in, ssm_lambda_re,
           ssm_lambda_im, ssm_log_dt, ssm_b_re, ssm_b_im, ssm_c_re, ssm_c_im, ssm_d, glu_w1, glu_b1, glu_w2,
           glu_b2, lambda_q1, lambda_k1, lambda_q2, lambda_k2, attn_subln_g, w_out, norm_post_g):
    bp, sp, _ = x_prompt.shape
    bs, ss, _ = x_sample.shape
    past = cache_k.shape[2]

    a_re, a_im, bbr, bbi = _prep(ssm_lambda_re[0], ssm_lambda_im[0], ssm_log_dt[0], ssm_b_re[0], ssm_b_im[0])
    bdb, bdc = _block_diag_weights(bbr, bbi, ssm_c_re[0], ssm_c_im[0])
    a_re = a_re.reshape(1, N_CH)
    a_im = a_im.reshape(1, N_CH)
    dvec = ssm_d[0].reshape(1, D_SSM)
    w_in_bf = w_in[0].astype(BF16)
    w_out_bf = w_out[0].astype(BF16)
    w1 = glu_w1[0].astype(BF16)
    w2 = glu_w2[0].astype(BF16)
    b1 = glu_b1[0].reshape(1, D_SSM)
    b2 = glu_b2[0].reshape(1, D_SSM)
    g_pre = norm_pre_g[0].reshape(1, D_MODEL)
    g_post = norm_post_g[0].reshape(1, D_MODEL)
    g_sub = attn_subln_g[0].reshape(1, V_DIM)
    lams = tuple(v[0].reshape(1, QK_DIM) for v in (lambda_q1, lambda_k1, lambda_q2, lambda_k2))
    inv = ROPE_THETA ** (-jnp.arange(ROPE_DIM // 2, dtype=F32) * 2.0 / ROPE_DIM)
    inv_lane = jnp.tile(inv, LANES // (ROPE_DIM // 2)).reshape(1, LANES)

    def run(x, n_seq, seq_len, pos0, chained, h0r, h0i, tm):
        x2d = x.reshape(n_seq * seq_len, D_MODEL)
        u, zs, q, kf, kb, vf, vb, za = _in_proj(x2d, g_pre, w_in_bf, inv_lane, seq_len=seq_len, pos0=pos0, tm=tm,
                                                transposed_qv=chained)
        ssm_out, hfr, hfi = _s5(u, zs, bdb, bdc, dvec, a_re, a_im, w1, b1, w2, b2, h0r, h0i,
                                n_seq=n_seq, seq_len=seq_len, chained=chained)
        if chained:
            att = _attn_prompt(lams, g_sub, q, kb, vb, za, n_seq=n_seq, seq_len=seq_len)
            k_out = jnp.transpose(kf, (0, 4, 1, 2, 3))[None]
        else:
            ck = jnp.transpose(cache_k[0], (0, 2, 3, 4, 1))
            att = _attn_sample(lams, g_sub, q, kb, vb, ck, cache_v[0], za, n_seq=n_seq, t=seq_len, past=past)
            k_out = kf.reshape(1, n_seq, seq_len, ATT_HEADS, 2, QK_DIM)
        y = _out_proj(ssm_out, att, w_out_bf, x2d, g_post, tm=tm)
        return (y.reshape(n_seq, seq_len, D_MODEL),
                k_out,
                vf.reshape(1, n_seq, seq_len, ATT_HEADS, V_DIM),
                hfr.reshape(1, n_seq, SSM_GROUPS, SSM_STATE),
                hfi.reshape(1, n_seq, SSM_GROUPS, SSM_STATE))

    zero = jnp.zeros((1, N_CH), F32)
    yp, kp, vp, hrp, hip = run(x_prompt, bp, sp, 0, True, zero, zero, 512)
    h0r = state_ssm_re[0].reshape(bs, N_CH)
    h0i = state_ssm_im[0].reshape(bs, N_CH)
    ys, ks, vs, hrs, his = run(x_sample, bs, ss, past, False, h0r, h0i, bs * ss)
    return (yp, ys, kp, vp, hrp, hip, ks, vs, hrs, his)
```

```python
import functools
import math

import jax
import jax.numpy as jnp
from jax import lax
from jax.experimental import pallas as pl
from jax.experimental.pallas import tpu as pltpu

F32 = jnp.float32
BF16 = jnp.bfloat16

D_MODEL = 1024
D_SSM = 512
D_ATT = 512
SSM_GROUP = 16
SSM_GROUPS = 32
SSM_STATE = 64
N_CH = SSM_GROUPS * SSM_STATE
ATT_HEADS = 4
QK_DIM = 64
V_DIM = 128
ROPE_DIM = 16
ROPE_THETA = 500000.0
CHUNK = 64
EPS = 1e-6
D_IN = 3072
LAMBDA_INIT = 0.8 - 0.6 * math.exp(-0.3 * 0)

LANES = 128
SUBLANES = 8
N_SLAB = N_CH // LANES
ROW_PAD = 4
NEG = -1e30
ATT_TILE = 256
HEADS_PER_STEP = 4
LOG2E = math.log2(math.e)
Q_SCALE = QK_DIM ** -0.5 * LOG2E
VMEM_LIMIT = 56 * 1024 * 1024


def _params(sem):
    return pltpu.CompilerParams(dimension_semantics=sem, vmem_limit_bytes=VMEM_LIMIT)


def _full(shape):
    n = len(shape)
    return pl.BlockSpec(shape, lambda *_: (0,) * n)


def _prep_kernel(lr_ref, li_ref, ldt_ref, br_ref, bi_ref, ar_ref, ai_ref, bbr_ref, bbi_ref):
    lr = lr_ref[...]
    li = li_ref[...]
    dt = jnp.exp(ldt_ref[...])
    mag = jnp.exp(lr * dt)
    ar = mag * jnp.cos(li * dt)
    ai = mag * jnp.sin(li * dt)
    den = lr * lr + li * li
    cr = ((ar - 1.0) * lr + ai * li) / den
    ci = (ai * lr - (ar - 1.0) * li) / den
    ar_ref[...] = ar
    ai_ref[...] = ai
    br = br_ref[...]
    bi = bi_ref[...]
    crb = cr[:, None, :]
    cib = ci[:, None, :]
    bbr_ref[...] = crb * br - cib * bi
    bbi_ref[...] = crb * bi + cib * br


def _prep(lam_re, lam_im, log_dt, b_re, b_im):
    g, p, c = b_re.shape
    brt = jnp.swapaxes(b_re, 1, 2)
    bit = jnp.swapaxes(b_im, 1, 2)
    out_shape = (jax.ShapeDtypeStruct((g, p), F32), jax.ShapeDtypeStruct((g, p), F32),
                 jax.ShapeDtypeStruct((g, c, p), F32), jax.ShapeDtypeStruct((g, c, p), F32))
    return pl.pallas_call(_prep_kernel, out_shape=out_shape, name="s5_prep")(
        lam_re, lam_im, log_dt.reshape(g, 1), brt, bit)


def _in_proj_kernel(x_ref, g_ref, w_ref, inv_ref, u_ref, zs_ref, q_ref, kf_ref, kb_ref, vf_ref,
                    vb_ref, za_ref, *, tm, seq_len, pos0, transposed_qv):
    i = pl.program_id(0)
    x = x_ref[...]
    ms = jnp.mean(x * x, axis=-1, keepdims=True)
    hn = (x * lax.rsqrt(ms + EPS) * g_ref[...]).astype(BF16)

    row = i * tm + lax.broadcasted_iota(jnp.int32, (tm, LANES), 0)
    pos = (pos0 + (row & (seq_len - 1))).astype(F32)
    ang = pos * inv_ref[...]
    cos = jnp.cos(ang)
    sin = jnp.sin(ang)
    lane = lax.broadcasted_iota(jnp.int32, (tm, LANES), 1) & (QK_DIM - 1)
    half = ROPE_DIM // 2
    c_m = jnp.where(lane < ROPE_DIM, cos, 1.0)
    s_lo = jnp.where(lane < half, -sin, 0.0)
    s_hi = jnp.where((lane >= half) & (lane < ROPE_DIM), sin, 0.0)

    def seg(lo, hi):
        return jnp.dot(hn, w_ref[:, lo:hi], preferred_element_type=F32)

    def rope(t):
        outs = []
        for h in range(ATT_HEADS):
            th = t[:, h * LANES:(h + 1) * LANES]
            outs.append(th * c_m + pltpu.roll(th, LANES - half, 1) * s_lo + pltpu.roll(th, half, 1) * s_hi)
        return jnp.concatenate(outs, axis=1)

    def put(ref, t):
        if not transposed_qv:
            ref[...] = t.astype(BF16)
            return
        for h in range(ATT_HEADS):
            tt = t[:, h * LANES:(h + 1) * LANES].T.astype(BF16)
            for c in range(tm // ATT_TILE):
                ref[h, c] = tt[:, c * ATT_TILE:(c + 1) * ATT_TILE]

    u_ref[...] = seg(0, D_SSM).astype(BF16)
    zs_ref[...] = seg(D_SSM, 2 * D_SSM).astype(BF16)
    q = rope(seg(1024, 1536))
    put(q_ref, q * Q_SCALE)
    k = rope(seg(1536, 2048))
    if transposed_qv:
        for h in range(ATT_HEADS):
            kt = k[:, h * LANES:(h + 1) * LANES].T
            kf_ref[h, 0] = kt[:QK_DIM]
            kf_ref[h, 1] = kt[QK_DIM:]
    else:
        kf_ref[...] = k
    kb_ref[...] = k.astype(BF16)
    v = seg(2048, 2560)
    for h in range(ATT_HEADS):
        vf_ref[:, h, :] = v[:, h * LANES:(h + 1) * LANES]
    put(vb_ref, v)
    za_ref[...] = seg(2560, 3072).astype(BF16)


def _in_proj(x2d, g, w_bf, inv_lane, *, seq_len, pos0, tm, transposed_qv):
    n = x2d.shape[0]
    assert n % tm == 0 and seq_len & (seq_len - 1) == 0
    row = lambda i: (i, 0)
    o512 = pl.BlockSpec((tm, 512), row)
    shp = lambda dt: jax.ShapeDtypeStruct((n, 512), dt)
    if transposed_qv:
        assert seq_len % tm == 0 and tm % ATT_TILE == 0
        tps = seq_len // tm
        per = tm // ATT_TILE
        t_spec = pl.BlockSpec((None, ATT_HEADS, per, LANES, ATT_TILE), lambda i: (i // tps, 0, i % tps, 0, 0))
        t_shape = jax.ShapeDtypeStruct((n // seq_len, ATT_HEADS, seq_len // ATT_TILE, LANES, ATT_TILE), BF16)
        kf_spec = pl.BlockSpec((None, ATT_HEADS, 2, QK_DIM, tm), lambda i: (i // tps, 0, 0, 0, i % tps))
        kf_shape = jax.ShapeDtypeStruct((n // seq_len, ATT_HEADS, 2, QK_DIM, seq_len), F32)
    else:
        t_spec, t_shape = o512, shp(BF16)
        kf_spec, kf_shape = o512, shp(F32)
    vf_spec = pl.BlockSpec((tm, ATT_HEADS, V_DIM), lambda i: (i, 0, 0))
    vf_shape = jax.ShapeDtypeStruct((n, ATT_HEADS, V_DIM), F32)
    return pl.pallas_call(
        functools.partial(_in_proj_kernel, tm=tm, seq_len=seq_len, pos0=pos0, transposed_qv=transposed_qv),
        grid=(n // tm,),
        in_specs=[pl.BlockSpec((tm, D_MODEL), row), _full((1, D_MODEL)), _full((D_MODEL, D_IN)),
                  _full((1, LANES))],
        out_specs=[o512, o512, t_spec, kf_spec, o512, vf_spec, t_spec, o512],
        out_shape=[shp(BF16), shp(BF16), t_shape, kf_shape, shp(BF16), vf_shape, t_shape, shp(BF16)],
        compiler_params=_params(("arbitrary",)),
        name="in_proj",
    )(x2d, g, w_bf, inv_lane)


def _cmul(ar, ai, br, bi):
    return ar * br - ai * bi, ar * bi + ai * br


def _s5_kernel(u_ref, zs_ref, bdb_ref, bdc_ref, d_ref, are_ref, aim_ref, w1_ref, b1_ref, w2_ref, b2_ref,
               h0r_ref, h0i_ref, out_ref, hfr_ref, hfi_ref, hbuf, hb16, car_re, car_im,
               *, R, T, chained, slab_group):
    TP = T + ROW_PAD
    tm = R * T
    half_cols = N_CH // 2

    if chained:
        @pl.when(pl.program_id(1) == 0)
        def _():
            car_re[...] = jnp.zeros_like(car_re)
            car_im[...] = jnp.zeros_like(car_im)

    for b in range(2):
        bu = jnp.dot(u_ref[:, b * 256:(b + 1) * 256], bdb_ref[b], preferred_element_type=F32)
        for part in range(2):
            for k in range(N_SLAB // 2):
                slab = part * N_SLAB + b * (N_SLAB // 2) + k
                col = part * half_cols + k * LANES
                for j in range(R):
                    hbuf[slab, j * TP:j * TP + T, :] = bu[j * T:(j + 1) * T, col:col + LANES]

    finals_re = [None] * N_SLAB
    finals_im = [None] * N_SLAB
    for g0 in range(0, N_SLAB, slab_group):
        slabs = list(range(g0, g0 + slab_group))
        a_r = [jnp.broadcast_to(are_ref[:, c * LANES:(c + 1) * LANES], (R, LANES)) for c in slabs]
        a_i = [jnp.broadcast_to(aim_ref[:, c * LANES:(c + 1) * LANES], (R, LANES)) for c in slabs]

        def body(s, carry, slabs=slabs, a_r=a_r, a_i=a_i):
            new = []
            for n, c in enumerate(slabs):
                hr, hi = carry[n]
                rows = pl.ds(s, R, stride=TP)
                pr, pi = _cmul(a_r[n], a_i[n], hr, hi)
                nr = pr + hbuf[c, rows, :]
                ni = pi + hbuf[N_SLAB + c, rows, :]
                hbuf[c, rows, :] = nr
                hbuf[N_SLAB + c, rows, :] = ni
                new.append((nr, ni))
            return tuple(new)

        z = jnp.zeros((R, LANES), F32)
        fin = lax.fori_loop(0, T, body, tuple((z, z) for _ in slabs))
        for n, c in enumerate(slabs):
            finals_re[c], finals_im[c] = fin[n]

    f_re = jnp.concatenate(finals_re, axis=1)
    f_im = jnp.concatenate(finals_im, axis=1)

    at_r, at_i = are_ref[...], aim_ref[...]
    for _ in range(T.bit_length() - 1):
        at_r, at_i = _cmul(at_r, at_i, at_r, at_i)

    if chained:
        e_r, e_i = car_re[...], car_im[...]
        rows_r, rows_i = [], []
        for j in range(R):
            rows_r.append(e_r)
            rows_i.append(e_i)
            pr, pi = _cmul(at_r, at_i, e_r, e_i)
            e_r = pr + f_re[j:j + 1]
            e_i = pi + f_im[j:j + 1]
        ent_r = jnp.concatenate(rows_r, axis=0)
        ent_i = jnp.concatenate(rows_i, axis=0)
        car_re[...] = e_r
        car_im[...] = e_i
        hfr_ref[0] = e_r
        hfi_ref[0] = e_i
    else:
        ent_r, ent_i = h0r_ref[...], h0i_ref[...]
        pr, pi = _cmul(at_r, at_i, ent_r, ent_i)
        hfr_ref[...] = pr + f_re
        hfi_ref[...] = pi + f_im

    for g0 in range(0, N_SLAB, slab_group):
        slabs = list(range(g0, g0 + slab_group))
        a_r = [jnp.broadcast_to(are_ref[:, c * LANES:(c + 1) * LANES], (R, LANES)) for c in slabs]
        a_i = [jnp.broadcast_to(aim_ref[:, c * LANES:(c + 1) * LANES], (R, LANES)) for c in slabs]

        def fix(s, carry, slabs=slabs, a_r=a_r, a_i=a_i):
            new = []
            for n, c in enumerate(slabs):
                er, ei = carry[n]
                rows = pl.ds(s, R, stride=TP)
                er, ei = _cmul(a_r[n], a_i[n], er, ei)
                hbuf[c, rows, :] = hbuf[c, rows, :] + er
                hbuf[N_SLAB + c, rows, :] = hbuf[N_SLAB + c, rows, :] + ei
                new.append((er, ei))
            return tuple(new)

        init = tuple((ent_r[:, c * LANES:(c + 1) * LANES], ent_i[:, c * LANES:(c + 1) * LANES]) for c in slabs)
        lax.fori_loop(0, T, fix, init)

    for b in range(2):
        for part in range(2):
            for k in range(N_SLAB // 2):
                slab = part * N_SLAB + b * (N_SLAB // 2) + k
                col = b * N_CH + part * half_cols + k * LANES
                for j in range(R):
                    hb16[j * T:(j + 1) * T, col:col + LANES] = hbuf[slab, j * TP:j * TP + T, :].astype(BF16)
    ys = [jnp.dot(hb16[:, b * N_CH:(b + 1) * N_CH], bdc_ref[b], preferred_element_type=F32) for b in range(2)]
    y = jnp.concatenate(ys, axis=1) + u_ref[...].astype(F32) * d_ref[...]

    gb = jax.nn.gelu(y).astype(BF16)
    y1 = jnp.dot(gb, w1_ref[...], preferred_element_type=F32) + b1_ref[...]
    y2 = jnp.dot(gb, w2_ref[...], preferred_element_type=F32) + b2_ref[...]
    out_ref[...] = (y1 * jax.nn.sigmoid(y2) * jax.nn.silu(zs_ref[...].astype(F32))).astype(BF16)


def _s5(u, zs, bdb, bdc, dvec, a_re, a_im, w1, b1, w2, b2, h0r, h0i, *, n_seq, seq_len, chained):
    n = u.shape[0]
    if chained:
        R, T = SUBLANES, 64
        tm = R * T
        tiles = seq_len // tm
        grid = (n_seq, tiles)
        row = lambda b, t: (b * tiles + t, 0)
        hf_shape = jax.ShapeDtypeStruct((n_seq, 1, N_CH), F32)
        hf_spec = pl.BlockSpec((1, 1, N_CH), lambda b, t: (b, 0, 0))
        slab_group = 8
    else:
        R, T = n_seq, seq_len
        tm = R * T
        assert tm == n
        grid = (1, 1)
        row = lambda b, t: (0, 0)
        hf_shape = jax.ShapeDtypeStruct((n_seq, N_CH), F32)
        hf_spec = _full((n_seq, N_CH))
        slab_group = 4
    assert T & (T - 1) == 0 and R % SUBLANES == 0
    blk = pl.BlockSpec((tm, D_SSM), row)
    return pl.pallas_call(
        functools.partial(_s5_kernel, R=R, T=T, chained=chained, slab_group=slab_group),
        grid=grid,
        in_specs=[blk, blk, _full(bdb.shape), _full(bdc.shape), _full((1, D_SSM)), _full((1, N_CH)),
                  _full((1, N_CH)), _full((D_SSM, D_SSM)), _full((1, D_SSM)), _full((D_SSM, D_SSM)),
                  _full((1, D_SSM)), _full(h0r.shape), _full(h0i.shape)],
        out_specs=[blk, hf_spec, hf_spec],
        out_shape=[jax.ShapeDtypeStruct((n, D_SSM), BF16), hf_shape, hf_shape],
        scratch_shapes=[pltpu.VMEM((2 * N_SLAB, R * (T + ROW_PAD), LANES), F32),
                        pltpu.VMEM((tm, 2 * N_CH), BF16),
                        pltpu.VMEM((1, N_CH), F32), pltpu.VMEM((1, N_CH), F32)],
        compiler_params=_params(("arbitrary", "arbitrary")),
        name="s5_chained" if chained else "s5_independent",
    )(u, zs, bdb, bdc, dvec, a_re, a_im, w1, b1, w2, b2, h0r, h0i)


def _lambda(lq1, lk1, lq2, lk2):
    s1 = jnp.sum(lq1[...] * lk1[...], axis=1, keepdims=True)
    s2 = jnp.sum(lq2[...] * lk2[...], axis=1, keepdims=True)
    return jnp.exp(s1) - jnp.exp(s2) + LAMBDA_INIT


def _stack_maps(q):
    lane = lax.broadcasted_iota(jnp.int32, q.shape, 1)
    zero = jnp.zeros_like(q)
    return jnp.concatenate([jnp.where(lane < QK_DIM, q, zero), jnp.where(lane >= QK_DIM, q, zero)], axis=0)


def _subln_gate(o, g, za):
    ms = jnp.mean(o * o, axis=-1, keepdims=True)
    on = (o * lax.rsqrt(ms + EPS) * g) * (1.0 - LAMBDA_INIT)
    return on * jax.nn.silu(za.astype(F32))


def _finish_head(acc, l, lam, g, za, t):
    inv = 1.0 / l
    o = acc[:t] * inv[:t] - lam * (acc[t:] * inv[t:])
    return _subln_gate(o, g, za)


_NT = (((1,), (1,)), ((), ()))


def _attn_kernel(lq1, lk1, lq2, lk2, g_ref, qt_ref, k_ref, vt_ref, za_ref, o_ref, m_sc, l_sc, acc_sc,
                 s_a, s_b, p_a, p_b, al_a, al_b):
    tq = ATT_TILE
    qi = pl.program_id(2)
    q2t = []
    for h in range(HEADS_PER_STEP):
        qt = qt_ref[h]
        row = lax.broadcasted_iota(jnp.int32, qt.shape, 0)
        zero = jnp.zeros_like(qt)
        q2t.append(jnp.concatenate([jnp.where(row < QK_DIM, qt, zero), jnp.where(row >= QK_DIM, qt, zero)], axis=1))
    m_sc[...] = jnp.full_like(m_sc, NEG)
    l_sc[...] = jnp.zeros_like(l_sc)
    acc_sc[...] = jnp.zeros_like(acc_sc)
    heads = range(HEADS_PER_STEP)

    def scores_into(s_ref, kt, h):
        start = pl.multiple_of(kt * tq, tq)
        s_ref[h] = jnp.dot(k_ref[pl.ds(start, tq), h * LANES:(h + 1) * LANES], q2t[h],
                           preferred_element_type=F32)

    def accumulate(p_ref, al_ref, kt, h):
        acc_sc[h] = al_ref[h] * acc_sc[h] + jnp.dot(vt_ref[h, kt], p_ref[h], preferred_element_type=F32)

    def softmax_into(s_ref, p_ref, al_ref, masked, h):
        s = s_ref[h]
        if masked:
            kc = lax.broadcasted_iota(jnp.int32, s.shape, 0) // CHUNK
            qc = (lax.broadcasted_iota(jnp.int32, s.shape, 1) & (tq - 1)) // CHUNK
            s = jnp.where(kc <= qc, s, NEG)
        m_old = m_sc[h]
        m_new = jnp.maximum(m_old, jnp.max(s, axis=0, keepdims=True))
        alpha = jnp.exp2(m_old - m_new)
        p = jnp.exp2(s - m_new)
        l_sc[h] = alpha * l_sc[h] + jnp.sum(p, axis=0, keepdims=True)
        p_ref[h] = p.astype(BF16)
        al_ref[h] = alpha
        m_sc[h] = m_new

    def stage(kt, s_cur, s_nxt, p_cur, p_prev, al_cur, al_prev):
        prev = jnp.maximum(kt - 1, 0)
        for h in heads:
            scores_into(s_nxt, kt + 1, h)
            softmax_into(s_cur, p_cur, al_cur, False, h)
            accumulate(p_prev, al_prev, prev, h)

    for h in heads:
        scores_into(s_a, 0, h)
    p_b[...] = jnp.zeros_like(p_b)
    al_b[...] = jnp.ones_like(al_b)

    def body(kt, c):
        even = (kt & 1) == 0

        @pl.when(even)
        def _():
            stage(kt, s_a, s_b, p_a, p_b, al_a, al_b)

        @pl.when(jnp.logical_not(even))
        def _():
            stage(kt, s_b, s_a, p_b, p_a, al_b, al_a)

        return c

    lax.fori_loop(0, qi, body, 0)

    def last(s_cur, p_cur, p_prev, al_cur, al_prev):
        prev = jnp.maximum(qi - 1, 0)
        for h in heads:
            accumulate(p_prev, al_prev, prev, h)
            softmax_into(s_cur, p_cur, al_cur, True, h)
        for h in heads:
            accumulate(p_cur, al_cur, qi, h)

    @pl.when((qi & 1) == 0)
    def _():
        last(s_a, p_a, p_b, al_a, al_b)

    @pl.when((qi & 1) == 1)
    def _():
        last(s_b, p_b, p_a, al_b, al_a)

    lam = _lambda(lq1, lk1, lq2, lk2)
    for h in range(HEADS_PER_STEP):
        acc = acc_sc[h]
        inv = 1.0 / l_sc[h]
        ot = acc[:, :tq] * inv[:, :tq] - lam * (acc[:, tq:] * inv[:, tq:])
        cols = slice(h * LANES, (h + 1) * LANES)
        o_ref[:, cols] = _subln_gate(ot.T, g_ref[...], za_ref[:, cols]).astype(BF16)


def _attn_prompt(lams, g, qt, kb, vt, za, *, n_seq, seq_len):
    tq = ATT_TILE
    nq = seq_len // tq
    hps = HEADS_PER_STEP
    assert tq % CHUNK == 0 and tq & (tq - 1) == 0 and ATT_HEADS % hps == 0
    rowspec = pl.BlockSpec((tq, hps * LANES), lambda b, h, i: (b * nq + i, h))
    qtspec = pl.BlockSpec((None, hps, None, LANES, tq), lambda b, h, i: (b, h, i, 0, 0))
    kspec = pl.BlockSpec((seq_len, hps * LANES), lambda b, h, i: (b, h))
    vtspec = pl.BlockSpec((None, hps, nq, LANES, tq), lambda b, h, i: (b, h, 0, 0, 0))
    vec = _full((1, QK_DIM))
    stat = pltpu.VMEM((hps, 1, 2 * tq), F32)
    return pl.pallas_call(
        _attn_kernel,
        grid=(n_seq, ATT_HEADS // hps, nq),
        in_specs=[vec, vec, vec, vec, _full((1, V_DIM)), qtspec, kspec, vtspec, rowspec],
        out_specs=rowspec,
        out_shape=jax.ShapeDtypeStruct((n_seq * seq_len, D_ATT), BF16),
        scratch_shapes=[stat, stat, pltpu.VMEM((hps, V_DIM, 2 * tq), F32),
                        pltpu.VMEM((hps, tq, 2 * tq), F32), pltpu.VMEM((hps, tq, 2 * tq), F32),
                        pltpu.VMEM((hps, tq, 2 * tq), BF16), pltpu.VMEM((hps, tq, 2 * tq), BF16), stat, stat],
        compiler_params=_params(("arbitrary", "arbitrary", "arbitrary")),
        name="attn_prompt",
    )(*lams, g, qt, kb, vt, za)


def _attn_sample_kernel(lq1, lk1, lq2, lk2, g_ref, q_ref, kn_ref, vn_ref, ck_ref, cv_ref, za_ref, o_ref, *, t):
    lam = _lambda(lq1, lk1, lq2, lk2)
    outs = []
    for h in range(ATT_HEADS):
        cols = slice(h * LANES, (h + 1) * LANES)
        q2 = _stack_maps(q_ref[:, cols])
        kpt = jnp.concatenate([ck_ref[h, 0], ck_ref[h, 1]], axis=0).astype(BF16)
        vp = cv_ref[:, h, :].astype(BF16)
        s_p = jnp.dot(q2, kpt, preferred_element_type=F32)
        s_n = lax.dot_general(q2, kn_ref[:, cols], _NT, preferred_element_type=F32)
        m = jnp.maximum(jnp.max(s_p, axis=1, keepdims=True), jnp.max(s_n, axis=1, keepdims=True))
        p_p = jnp.exp2(s_p - m)
        p_n = jnp.exp2(s_n - m)
        l = jnp.sum(p_p, axis=1, keepdims=True) + jnp.sum(p_n, axis=1, keepdims=True)
        acc = (jnp.dot(p_p.astype(BF16), vp, preferred_element_type=F32)
               + jnp.dot(p_n.astype(BF16), vn_ref[:, cols], preferred_element_type=F32))
        outs.append(_finish_head(acc, l, lam, g_ref[...], za_ref[:, cols], t))
    o_ref[...] = jnp.concatenate(outs, axis=1).astype(BF16)


def _attn_sample(lams, g, q, kb, vb, ck, cv, za, *, n_seq, t, past):
    assert past % CHUNK == 0 and t <= CHUNK
    row = pl.BlockSpec((t, D_ATT), lambda b: (b, 0))
    kcache = pl.BlockSpec((None, ATT_HEADS, 2, QK_DIM, past), lambda b: (b, 0, 0, 0, 0))
    vcache = pl.BlockSpec((None, past, ATT_HEADS, V_DIM), lambda b: (b, 0, 0, 0))
    vec = _full((1, QK_DIM))
    return pl.pallas_call(
        functools.partial(_attn_sample_kernel, t=t),
        grid=(n_seq,),
        in_specs=[vec, vec, vec, vec, _full((1, V_DIM)), row, row, row, kcache, vcache, row],
        out_specs=row,
        out_shape=jax.ShapeDtypeStruct((n_seq * t, D_ATT), BF16),
        compiler_params=_params(("arbitrary",)),
        name="attn_sample",
    )(*lams, g, q, kb, vb, ck, cv, za)


def _out_proj_kernel(s_ref, a_ref, w_ref, x_ref, g_ref, y_ref):
    mix = (jnp.dot(s_ref[...], w_ref[:D_SSM, :], preferred_element_type=F32)
           + jnp.dot(a_ref[...], w_ref[D_SSM:, :], preferred_element_type=F32))
    ms = jnp.mean(mix * mix, axis=-1, keepdims=True)
    y_ref[...] = x_ref[...] + mix * lax.rsqrt(ms + EPS) * g_ref[...]


def _out_proj(ssm_out, att_out, w_bf, x2d, g, *, tm):
    n = x2d.shape[0]
    row = lambda i: (i, 0)
    half = pl.BlockSpec((tm, 512), row)
    full = pl.BlockSpec((tm, D_MODEL), row)
    return pl.pallas_call(
        _out_proj_kernel,
        grid=(n // tm,),
        in_specs=[half, half, _full((D_MODEL, D_MODEL)), full, _full((1, D_MODEL))],
        out_specs=full,
        out_shape=jax.ShapeDtypeStruct((n, D_MODEL), F32),
        compiler_params=_params(("arbitrary",)),
        name="out_proj",
    )(ssm_out, att_out, w_bf, x2d, g)


def _block_diag_weights(bbr, bbi, c_re, c_im):
    eye = jnp.eye(SSM_GROUPS // 2, dtype=F32)

    def b_side(m):
        m = m.reshape(2, SSM_GROUPS // 2, SSM_GROUP, SSM_STATE)
        return jnp.einsum('bgcp,gh->bgchp', m, eye).reshape(2, 256, N_CH // 2)

    def c_side(m):
        m = m.reshape(2, SSM_GROUPS // 2, SSM_GROUP, SSM_STATE)
        return jnp.einsum('bgcp,gh->bgphc', m, eye).reshape(2, N_CH // 2, 256)

    bdb = jnp.concatenate([b_side(bbr), b_side(bbi)], axis=2).astype(BF16)
    bdc = jnp.concatenate([c_side(c_re), -c_side(c_im)], axis=1).astype(BF16)
    return bdb, bdc


def kernel(x_prompt, x_sample, cache_k, cache_v, state_ssm_re, state_ssm_im, norm_pre_g, w_in, ssm_lambda_re,
           ssm_lambda_im, ssm_log_dt, ssm_b_re, ssm_b_im, ssm_c_re, ssm_c_im, ssm_d, glu_w1, glu_b1, glu_w2,
           glu_b2, lambda_q1, lambda_k1, lambda_q2, lambda_k2, attn_subln_g, w_out, norm_post_g):
    bp, sp, _ = x_prompt.shape
    bs, ss, _ = x_sample.shape
    past = cache_k.shape[2]

    a_re, a_im, bbr, bbi = _prep(ssm_lambda_re[0], ssm_lambda_im[0], ssm_log_dt[0], ssm_b_re[0], ssm_b_im[0])
    bdb, bdc = _block_diag_weights(bbr, bbi, ssm_c_re[0], ssm_c_im[0])
    a_re = a_re.reshape(1, N_CH)
    a_im = a_im.reshape(1, N_CH)
    dvec = ssm_d[0].reshape(1, D_SSM)
    w_in_bf = w_in[0].astype(BF16)
    w_out_bf = w_out[0].astype(BF16)
    w1 = glu_w1[0].astype(BF16)
    w2 = glu_w2[0].astype(BF16)
    b1 = glu_b1[0].reshape(1, D_SSM)
    b2 = glu_b2[0].reshape(1, D_SSM)
    g_pre = norm_pre_g[0].reshape(1, D_MODEL)
    g_post = norm_post_g[0].reshape(1, D_MODEL)
    g_sub = attn_subln_g[0].reshape(1, V_DIM)
    lams = tuple(v[0].reshape(1, QK_DIM) for v in (lambda_q1, lambda_k1, lambda_q2, lambda_k2))
    inv = ROPE_THETA ** (-jnp.arange(ROPE_DIM // 2, dtype=F32) * 2.0 / ROPE_DIM)
    inv_lane = jnp.tile(inv, LANES // (ROPE_DIM // 2)).reshape(1, LANES)

    def run(x, n_seq, seq_len, pos0, chained, h0r, h0i, tm):
        x2d = x.reshape(n_seq * seq_len, D_MODEL)
        u, zs, q, kf, kb, vf, vb, za = _in_proj(x2d, g_pre, w_in_bf, inv_lane, seq_len=seq_len, pos0=pos0, tm=tm,
                                                transposed_qv=chained)
        ssm_out, hfr, hfi = _s5(u, zs, bdb, bdc, dvec, a_re, a_im, w1, b1, w2, b2, h0r, h0i,
                                n_seq=n_seq, seq_len=seq_len, chained=chained)
        if chained:
            att = _attn_prompt(lams, g_sub, q, kb, vb, za, n_seq=n_seq, seq_len=seq_len)
            k_out = jnp.transpose(kf, (0, 4, 1, 2, 3))[None]
        else:
            ck = jnp.transpose(cache_k[0], (0, 2, 3, 4, 1))
            att = _attn_sample(lams, g_sub, q, kb, vb, ck, cache_v[0], za, n_seq=n_seq, t=seq_len, past=past)
            k_out = kf.reshape(1, n_seq, seq_len, ATT_HEADS, 2, QK_DIM)
        y = _out_proj(ssm_out, att, w_out_bf, x2d, g_post, tm=tm)
        return (y.reshape(n_seq, seq_len, D_MODEL),
                k_out,
                vf.reshape(1, n_seq, seq_len, ATT_HEADS, V_DIM),
                hfr.reshape(1, n_seq, SSM_GROUPS, SSM_STATE),
                hfi.reshape(1, n_seq, SSM_GROUPS, SSM_STATE))

    zero = jnp.zeros((1, N_CH), F32)
    yp, kp, vp, hrp, hip = run(x_prompt, bp, sp, 0, True, zero, zero, 512)
    h0r = state_ssm_re[0].reshape(bs, N_CH)
    h0i = state_ssm_im[0].reshape(bs, N_CH)
    ys, ks, vs, hrs, his = run(x_sample, bs, ss, past, False, h0r, h0i, bs * ss)
    return (yp, ys, kp, vp, hrp, hip, ks, vs, hrs, his)
```

```python
import functools
import math

import jax
import jax.numpy as jnp
from jax import lax
from jax.experimental import pallas as pl
from jax.experimental.pallas import tpu as pltpu

F32 = jnp.float32
BF16 = jnp.bfloat16

D_MODEL = 1024
D_SSM = 512
D_ATT = 512
SSM_GROUP = 16
SSM_GROUPS = 32
SSM_STATE = 64
N_CH = SSM_GROUPS * SSM_STATE
ATT_HEADS = 4
QK_DIM = 64
V_DIM = 128
ROPE_DIM = 16
ROPE_THETA = 500000.0
CHUNK = 64
EPS = 1e-6
D_IN = 3072
LAMBDA_INIT = 0.8 - 0.6 * math.exp(-0.3 * 0)

LANES = 128
SUBLANES = 8
N_SLAB = N_CH // LANES
SCAN_UNROLL = 4
ROW_PAD = 4
NEG = -1e30
ATT_TILE = 256
HEADS_PER_STEP = 4
LOG2E = math.log2(math.e)
Q_SCALE = QK_DIM ** -0.5 * LOG2E
VMEM_LIMIT = 56 * 1024 * 1024


def _params(sem):
    return pltpu.CompilerParams(dimension_semantics=sem, vmem_limit_bytes=VMEM_LIMIT)


def _full(shape):
    n = len(shape)
    return pl.BlockSpec(shape, lambda *_: (0,) * n)


def _prep_kernel(lr_ref, li_ref, ldt_ref, br_ref, bi_ref, ar_ref, ai_ref, bbr_ref, bbi_ref):
    lr = lr_ref[...]
    li = li_ref[...]
    dt = jnp.exp(ldt_ref[...])
    mag = jnp.exp(lr * dt)
    ar = mag * jnp.cos(li * dt)
    ai = mag * jnp.sin(li * dt)
    den = lr * lr + li * li
    cr = ((ar - 1.0) * lr + ai * li) / den
    ci = (ai * lr - (ar - 1.0) * li) / den
    ar_ref[...] = ar
    ai_ref[...] = ai
    br = br_ref[...]
    bi = bi_ref[...]
    crb = cr[:, None, :]
    cib = ci[:, None, :]
    bbr_ref[...] = crb * br - cib * bi
    bbi_ref[...] = crb * bi + cib * br


def _prep(lam_re, lam_im, log_dt, b_re, b_im):
    g, p, c = b_re.shape
    brt = jnp.swapaxes(b_re, 1, 2)
    bit = jnp.swapaxes(b_im, 1, 2)
    out_shape = (jax.ShapeDtypeStruct((g, p), F32), jax.ShapeDtypeStruct((g, p), F32),
                 jax.ShapeDtypeStruct((g, c, p), F32), jax.ShapeDtypeStruct((g, c, p), F32))
    return pl.pallas_call(_prep_kernel, out_shape=out_shape, name="s5_prep")(
        lam_re, lam_im, log_dt.reshape(g, 1), brt, bit)


def _in_proj_kernel(x_ref, g_ref, w_ref, inv_ref, u_ref, zs_ref, q_ref, kf_ref, kb_ref, vf_ref,
                    vb_ref, za_ref, cl_sc, sl_sc, *, tm, seq_len, pos0, transposed_qv):
    i = pl.program_id(0)
    x = x_ref[...]
    ms = jnp.mean(x * x, axis=-1, keepdims=True)
    hn = (x * lax.rsqrt(ms + EPS) * g_ref[...]).astype(BF16)

    inv = inv_ref[...]

    @pl.when(i == 0)
    def _():
        off = (lax.broadcasted_iota(jnp.int32, (tm, LANES), 0) & (seq_len - 1)).astype(F32) * inv
        cl_sc[...] = jnp.cos(off)
        sl_sc[...] = jnp.sin(off)

    base = (pos0 + ((i * tm) & (seq_len - 1))).astype(F32) * jnp.broadcast_to(inv, (SUBLANES, LANES))
    cb = jnp.cos(base)[:1]
    sb = jnp.sin(base)[:1]
    cl = cl_sc[...]
    sl = sl_sc[...]
    c_m = cb * cl - sb * sl
    sin = sb * cl + cb * sl
    lane = lax.broadcasted_iota(jnp.int32, (tm, LANES), 1) & (QK_DIM - 1)
    half = ROPE_DIM // 2
    s_lo = jnp.where(lane < half, -sin, 0.0)
    s_hi = jnp.where(lane >= half, sin, 0.0)

    def seg(lo, hi):
        return jnp.dot(hn, w_ref[:, lo:hi], preferred_element_type=F32)

    def rope(t):
        outs = []
        for h in range(ATT_HEADS):
            th = t[:, h * LANES:(h + 1) * LANES]
            outs.append(th * c_m + pltpu.roll(th, LANES - half, 1) * s_lo + pltpu.roll(th, half, 1) * s_hi)
        return jnp.concatenate(outs, axis=1)

    def put(ref, t):
        if not transposed_qv:
            ref[...] = t.astype(BF16)
            return
        for h in range(ATT_HEADS):
            tt = t[:, h * LANES:(h + 1) * LANES].T.astype(BF16)
            for c in range(tm // ATT_TILE):
                ref[h, c] = tt[:, c * ATT_TILE:(c + 1) * ATT_TILE]

    u_ref[...] = seg(0, D_SSM).astype(BF16)
    zs_ref[...] = seg(D_SSM, 2 * D_SSM).astype(BF16)
    q = rope(seg(1024, 1536))
    put(q_ref, q * Q_SCALE)
    k = rope(seg(1536, 2048))
    if transposed_qv:
        for h in range(ATT_HEADS):
            kt = k[:, h * LANES:(h + 1) * LANES].T
            kf_ref[h, 0] = kt[:QK_DIM]
            kf_ref[h, 1] = kt[QK_DIM:]
    else:
        kf_ref[...] = k
    kb_ref[...] = k.astype(BF16)
    v = seg(2048, 2560)
    for h in range(ATT_HEADS):
        vf_ref[:, h, :] = v[:, h * LANES:(h + 1) * LANES]
    put(vb_ref, v)
    za_ref[...] = seg(2560, 3072).astype(BF16)


def _in_proj(x2d, g, w_bf, inv_lane, *, seq_len, pos0, tm, transposed_qv):
    n = x2d.shape[0]
    assert n % tm == 0 and seq_len & (seq_len - 1) == 0 and (tm % seq_len == 0 or seq_len % tm == 0)
    row = lambda i: (i, 0)
    o512 = pl.BlockSpec((tm, 512), row)
    shp = lambda dt: jax.ShapeDtypeStruct((n, 512), dt)
    if transposed_qv:
        assert seq_len % tm == 0 and tm % ATT_TILE == 0
        tps = seq_len // tm
        per = tm // ATT_TILE
        t_spec = pl.BlockSpec((None, ATT_HEADS, per, LANES, ATT_TILE), lambda i: (i // tps, 0, i % tps, 0, 0))
        t_shape = jax.ShapeDtypeStruct((n // seq_len, ATT_HEADS, seq_len // ATT_TILE, LANES, ATT_TILE), BF16)
        kf_spec = pl.BlockSpec((None, ATT_HEADS, 2, QK_DIM, tm), lambda i: (i // tps, 0, 0, 0, i % tps))
        kf_shape = jax.ShapeDtypeStruct((n // seq_len, ATT_HEADS, 2, QK_DIM, seq_len), F32)
    else:
        t_spec, t_shape = o512, shp(BF16)
        kf_spec, kf_shape = o512, shp(F32)
    vf_spec = pl.BlockSpec((tm, ATT_HEADS, V_DIM), lambda i: (i, 0, 0))
    vf_shape = jax.ShapeDtypeStruct((n, ATT_HEADS, V_DIM), F32)
    return pl.pallas_call(
        functools.partial(_in_proj_kernel, tm=tm, seq_len=seq_len, pos0=pos0, transposed_qv=transposed_qv),
        grid=(n // tm,),
        in_specs=[pl.BlockSpec((tm, D_MODEL), row), _full((1, D_MODEL)), _full((D_MODEL, D_IN)),
                  _full((1, LANES))],
        out_specs=[o512, o512, t_spec, kf_spec, o512, vf_spec, t_spec, o512],
        out_shape=[shp(BF16), shp(BF16), t_shape, kf_shape, shp(BF16), vf_shape, t_shape, shp(BF16)],
        scratch_shapes=[pltpu.VMEM((tm, LANES), F32), pltpu.VMEM((tm, LANES), F32)],
        compiler_params=_params(("arbitrary",)),
        name="in_proj",
    )(x2d, g, w_bf, inv_lane)


def _cmul(ar, ai, br, bi):
    return ar * br - ai * bi, ar * bi + ai * br


def _s5_kernel(u_ref, zs_ref, bdb_ref, bdc_ref, d_ref, are_ref, aim_ref, w1_ref, b1_ref, w2_ref, b2_ref,
               h0r_ref, h0i_ref, out_ref, hfr_ref, hfi_ref, hbuf, hb16, car_re, car_im,
               *, R, T, chained, slab_group):
    TP = T + ROW_PAD
    tm = R * T
    half_cols = N_CH // 2

    if chained:
        @pl.when(pl.program_id(1) == 0)
        def _():
            car_re[...] = jnp.zeros_like(car_re)
            car_im[...] = jnp.zeros_like(car_im)

    for b in range(2):
        bu = jnp.dot(u_ref[:, b * 256:(b + 1) * 256], bdb_ref[b], preferred_element_type=F32)
        for part in range(2):
            for k in range(N_SLAB // 2):
                slab = part * N_SLAB + b * (N_SLAB // 2) + k
                col = part * half_cols + k * LANES
                for j in range(R):
                    hbuf[slab, j * TP:j * TP + T, :] = bu[j * T:(j + 1) * T, col:col + LANES]

    finals_re = [None] * N_SLAB
    finals_im = [None] * N_SLAB
    for g0 in range(0, N_SLAB, slab_group):
        slabs = list(range(g0, g0 + slab_group))
        a_r = [jnp.broadcast_to(are_ref[:, c * LANES:(c + 1) * LANES], (R, LANES)) for c in slabs]
        a_i = [jnp.broadcast_to(aim_ref[:, c * LANES:(c + 1) * LANES], (R, LANES)) for c in slabs]

        def body(s, carry, slabs=slabs, a_r=a_r, a_i=a_i):
            new = []
            for n, c in enumerate(slabs):
                hr, hi = carry[n]
                rows = pl.ds(s, R, stride=TP)
                pr, pi = _cmul(a_r[n], a_i[n], hr, hi)
                nr = pr + hbuf[c, rows, :]
                ni = pi + hbuf[N_SLAB + c, rows, :]
                hbuf[c, rows, :] = nr
                hbuf[N_SLAB + c, rows, :] = ni
                new.append((nr, ni))
            return tuple(new)

        z = jnp.zeros((R, LANES), F32)
        fin = lax.fori_loop(0, T, body, tuple((z, z) for _ in slabs), unroll=SCAN_UNROLL)
        for n, c in enumerate(slabs):
            finals_re[c], finals_im[c] = fin[n]

    f_re = jnp.concatenate(finals_re, axis=1)
    f_im = jnp.concatenate(finals_im, axis=1)

    at_r, at_i = are_ref[...], aim_ref[...]
    for _ in range(T.bit_length() - 1):
        at_r, at_i = _cmul(at_r, at_i, at_r, at_i)

    if chained:
        e_r, e_i = car_re[...], car_im[...]
        rows_r, rows_i = [], []
        for j in range(R):
            rows_r.append(e_r)
            rows_i.append(e_i)
            pr, pi = _cmul(at_r, at_i, e_r, e_i)
            e_r = pr + f_re[j:j + 1]
            e_i = pi + f_im[j:j + 1]
        ent_r = jnp.concatenate(rows_r, axis=0)
        ent_i = jnp.concatenate(rows_i, axis=0)
        car_re[...] = e_r
        car_im[...] = e_i
        hfr_ref[0] = e_r
        hfi_ref[0] = e_i
    else:
        ent_r, ent_i = h0r_ref[...], h0i_ref[...]
        pr, pi = _cmul(at_r, at_i, ent_r, ent_i)
        hfr_ref[...] = pr + f_re
        hfi_ref[...] = pi + f_im

    for g0 in range(0, N_SLAB, slab_group):
        slabs = list(range(g0, g0 + slab_group))
        a_r = [jnp.broadcast_to(are_ref[:, c * LANES:(c + 1) * LANES], (R, LANES)) for c in slabs]
        a_i = [jnp.broadcast_to(aim_ref[:, c * LANES:(c + 1) * LANES], (R, LANES)) for c in slabs]

        def fix(s, carry, slabs=slabs, a_r=a_r, a_i=a_i):
            new = []
            for n, c in enumerate(slabs):
                er, ei = carry[n]
                rows = pl.ds(s, R, stride=TP)
                er, ei = _cmul(a_r[n], a_i[n], er, ei)
                hbuf[c, rows, :] = hbuf[c, rows, :] + er
                hbuf[N_SLAB + c, rows, :] = hbuf[N_SLAB + c, rows, :] + ei
                new.append((er, ei))
            return tuple(new)

        init = tuple((ent_r[:, c * LANES:(c + 1) * LANES], ent_i[:, c * LANES:(c + 1) * LANES]) for c in slabs)
        lax.fori_loop(0, T, fix, init, unroll=SCAN_UNROLL)

    for b in range(2):
        for part in range(2):
            for k in range(N_SLAB // 2):
                slab = part * N_SLAB + b * (N_SLAB // 2) + k
                col = b * N_CH + part * half_cols + k * LANES
                for j in range(R):
                    hb16[j * T:(j + 1) * T, col:col + LANES] = hbuf[slab, j * TP:j * TP + T, :].astype(BF16)
    ys = [jnp.dot(hb16[:, b * N_CH:(b + 1) * N_CH], bdc_ref[b], preferred_element_type=F32) for b in range(2)]
    y = jnp.concatenate(ys, axis=1) + u_ref[...].astype(F32) * d_ref[...]

    gb = jax.nn.gelu(y).astype(BF16)
    y1 = jnp.dot(gb, w1_ref[...], preferred_element_type=F32) + b1_ref[...]
    y2 = jnp.dot(gb, w2_ref[...], preferred_element_type=F32) + b2_ref[...]
    out_ref[...] = (y1 * jax.nn.sigmoid(y2) * jax.nn.silu(zs_ref[...].astype(F32))).astype(BF16)


def _s5(u, zs, bdb, bdc, dvec, a_re, a_im, w1, b1, w2, b2, h0r, h0i, *, n_seq, seq_len, chained):
    n = u.shape[0]
    if chained:
        R, T = SUBLANES, 64
        tm = R * T
        tiles = seq_len // tm
        grid = (n_seq, tiles)
        row = lambda b, t: (b * tiles + t, 0)
        hf_shape = jax.ShapeDtypeStruct((n_seq, 1, N_CH), F32)
        hf_spec = pl.BlockSpec((1, 1, N_CH), lambda b, t: (b, 0, 0))
        slab_group = 8
    else:
        R, T = n_seq, seq_len
        tm = R * T
        assert tm == n
        grid = (1, 1)
        row = lambda b, t: (0, 0)
        hf_shape = jax.ShapeDtypeStruct((n_seq, N_CH), F32)
        hf_spec = _full((n_seq, N_CH))
        slab_group = 4
    assert T & (T - 1) == 0 and R % SUBLANES == 0
    blk = pl.BlockSpec((tm, D_SSM), row)
    return pl.pallas_call(
        functools.partial(_s5_kernel, R=R, T=T, chained=chained, slab_group=slab_group),
        grid=grid,
        in_specs=[blk, blk, _full(bdb.shape), _full(bdc.shape), _full((1, D_SSM)), _full((1, N_CH)),
                  _full((1, N_CH)), _full((D_SSM, D_SSM)), _full((1, D_SSM)), _full((D_SSM, D_SSM)),
                  _full((1, D_SSM)), _full(h0r.shape), _full(h0i.shape)],
        out_specs=[blk, hf_spec, hf_spec],
        out_shape=[jax.ShapeDtypeStruct((n, D_SSM), BF16), hf_shape, hf_shape],
        scratch_shapes=[pltpu.VMEM((2 * N_SLAB, R * (T + ROW_PAD), LANES), F32),
                        pltpu.VMEM((tm, 2 * N_CH), BF16),
                        pltpu.VMEM((1, N_CH), F32), pltpu.VMEM((1, N_CH), F32)],
        compiler_params=_params(("arbitrary", "arbitrary")),
        name="s5_chained" if chained else "s5_independent",
    )(u, zs, bdb, bdc, dvec, a_re, a_im, w1, b1, w2, b2, h0r, h0i)


def _lambda(lq1, lk1, lq2, lk2):
    s1 = jnp.sum(lq1[...] * lk1[...], axis=1, keepdims=True)
    s2 = jnp.sum(lq2[...] * lk2[...], axis=1, keepdims=True)
    return jnp.exp(s1) - jnp.exp(s2) + LAMBDA_INIT


def _stack_maps(q):
    lane = lax.broadcasted_iota(jnp.int32, q.shape, 1)
    zero = jnp.zeros_like(q)
    return jnp.concatenate([jnp.where(lane < QK_DIM, q, zero), jnp.where(lane >= QK_DIM, q, zero)], axis=0)


def _subln_gate(o, g, za):
    ms = jnp.mean(o * o, axis=-1, keepdims=True)
    on = (o * lax.rsqrt(ms + EPS) * g) * (1.0 - LAMBDA_INIT)
    return on * jax.nn.silu(za.astype(F32))


def _finish_head(acc, l, lam, g, za, t):
    inv = 1.0 / l
    o = acc[:t] * inv[:t] - lam * (acc[t:] * inv[t:])
    return _subln_gate(o, g, za)


_NT = (((1,), (1,)), ((), ()))


def _attn_kernel(lq1, lk1, lq2, lk2, g_ref, qt_ref, k_ref, vt_ref, za_ref, o_ref, m_sc, l_sc, acc_sc,
                 s_a, s_b, p_a, p_b, al_a, al_b):
    tq = ATT_TILE
    qi = pl.program_id(2)
    q2t = []
    for h in range(HEADS_PER_STEP):
        qt = qt_ref[h]
        row = lax.broadcasted_iota(jnp.int32, qt.shape, 0)
        zero = jnp.zeros_like(qt)
        q2t.append(jnp.concatenate([jnp.where(row < QK_DIM, qt, zero), jnp.where(row >= QK_DIM, qt, zero)], axis=1))
    m_sc[...] = jnp.full_like(m_sc, NEG)
    l_sc[...] = jnp.zeros_like(l_sc)
    acc_sc[...] = jnp.zeros_like(acc_sc)
    heads = range(HEADS_PER_STEP)

    def scores_into(s_ref, kt, h):
        start = pl.multiple_of(kt * tq, tq)
        s_ref[h] = jnp.dot(k_ref[pl.ds(start, tq), h * LANES:(h + 1) * LANES], q2t[h],
                           preferred_element_type=F32)

    def accumulate(p_ref, al_ref, kt, h):
        acc_sc[h] = al_ref[h] * acc_sc[h] + jnp.dot(vt_ref[h, kt], p_ref[h], preferred_element_type=F32)

    def softmax_into(s_ref, p_ref, al_ref, masked, h):
        s = s_ref[h]
        if masked:
            kc = lax.broadcasted_iota(jnp.int32, s.shape, 0) // CHUNK
            qc = (lax.broadcasted_iota(jnp.int32, s.shape, 1) & (tq - 1)) // CHUNK
            s = jnp.where(kc <= qc, s, NEG)
        m_old = m_sc[h]
        m_new = jnp.maximum(m_old, jnp.max(s, axis=0, keepdims=True))
        alpha = jnp.exp2(m_old - m_new)
        p = jnp.exp2(s - m_new)
        l_sc[h] = alpha * l_sc[h] + jnp.sum(p, axis=0, keepdims=True)
        p_ref[h] = p.astype(BF16)
        al_ref[h] = alpha
        m_sc[h] = m_new

    def stage(kt, s_cur, s_nxt, p_cur, p_prev, al_cur, al_prev):
        prev = jnp.maximum(kt - 1, 0)
        for h in heads:
            scores_into(s_nxt, kt + 1, h)
            softmax_into(s_cur, p_cur, al_cur, False, h)
            accumulate(p_prev, al_prev, prev, h)

    for h in heads:
        scores_into(s_a, 0, h)
    p_b[...] = jnp.zeros_like(p_b)
    al_b[...] = jnp.ones_like(al_b)

    def body(kt, c):
        even = (kt & 1) == 0

        @pl.when(even)
        def _():
            stage(kt, s_a, s_b, p_a, p_b, al_a, al_b)

        @pl.when(jnp.logical_not(even))
        def _():
            stage(kt, s_b, s_a, p_b, p_a, al_b, al_a)

        return c

    lax.fori_loop(0, qi, body, 0)

    def last(s_cur, p_cur, p_prev, al_cur, al_prev):
        prev = jnp.maximum(qi - 1, 0)
        for h in heads:
            accumulate(p_prev, al_prev, prev, h)
            softmax_into(s_cur, p_cur, al_cur, True, h)
        for h in heads:
            accumulate(p_cur, al_cur, qi, h)

    @pl.when((qi & 1) == 0)
    def _():
        last(s_a, p_a, p_b, al_a, al_b)

    @pl.when((qi & 1) == 1)
    def _():
        last(s_b, p_b, p_a, al_b, al_a)

    lam = _lambda(lq1, lk1, lq2, lk2)
    for h in range(HEADS_PER_STEP):
        acc = acc_sc[h]
        inv = 1.0 / l_sc[h]
        ot = acc[:, :tq] * inv[:, :tq] - lam * (acc[:, tq:] * inv[:, tq:])
        cols = slice(h * LANES, (h + 1) * LANES)
        o_ref[:, cols] = _subln_gate(ot.T, g_ref[...], za_ref[:, cols]).astype(BF16)


def _attn_prompt(lams, g, qt, kb, vt, za, *, n_seq, seq_len):
    tq = ATT_TILE
    nq = seq_len // tq
    hps = HEADS_PER_STEP
    assert tq % CHUNK == 0 and tq & (tq - 1) == 0 and ATT_HEADS % hps == 0
    rowspec = pl.BlockSpec((tq, hps * LANES), lambda b, h, i: (b * nq + i, h))
    qtspec = pl.BlockSpec((None, hps, None, LANES, tq), lambda b, h, i: (b, h, i, 0, 0))
    kspec = pl.BlockSpec((seq_len, hps * LANES), lambda b, h, i: (b, h))
    vtspec = pl.BlockSpec((None, hps, nq, LANES, tq), lambda b, h, i: (b, h, 0, 0, 0))
    vec = _full((1, QK_DIM))
    stat = pltpu.VMEM((hps, 1, 2 * tq), F32)
    return pl.pallas_call(
        _attn_kernel,
        grid=(n_seq, ATT_HEADS // hps, nq),
        in_specs=[vec, vec, vec, vec, _full((1, V_DIM)), qtspec, kspec, vtspec, rowspec],
        out_specs=rowspec,
        out_shape=jax.ShapeDtypeStruct((n_seq * seq_len, D_ATT), BF16),
        scratch_shapes=[stat, stat, pltpu.VMEM((hps, V_DIM, 2 * tq), F32),
                        pltpu.VMEM((hps, tq, 2 * tq), F32), pltpu.VMEM((hps, tq, 2 * tq), F32),
                        pltpu.VMEM((hps, tq, 2 * tq), BF16), pltpu.VMEM((hps, tq, 2 * tq), BF16), stat, stat],
        compiler_params=_params(("arbitrary", "arbitrary", "arbitrary")),
        name="attn_prompt",
    )(*lams, g, qt, kb, vt, za)


def _attn_sample_kernel(lq1, lk1, lq2, lk2, g_ref, q_ref, kn_ref, vn_ref, ck_ref, cv_ref, za_ref, o_ref, *, t):
    lam = _lambda(lq1, lk1, lq2, lk2)
    outs = []
    for h in range(ATT_HEADS):
        cols = slice(h * LANES, (h + 1) * LANES)
        q2 = _stack_maps(q_ref[:, cols])
        kpt = jnp.concatenate([ck_ref[h, 0], ck_ref[h, 1]], axis=0).astype(BF16)
        vp = cv_ref[:, h, :].astype(BF16)
        s_p = jnp.dot(q2, kpt, preferred_element_type=F32)
        s_n = lax.dot_general(q2, kn_ref[:, cols], _NT, preferred_element_type=F32)
        m = jnp.maximum(jnp.max(s_p, axis=1, keepdims=True), jnp.max(s_n, axis=1, keepdims=True))
        p_p = jnp.exp2(s_p - m)
        p_n = jnp.exp2(s_n - m)
        l = jnp.sum(p_p, axis=1, keepdims=True) + jnp.sum(p_n, axis=1, keepdims=True)
        acc = (jnp.dot(p_p.astype(BF16), vp, preferred_element_type=F32)
               + jnp.dot(p_n.astype(BF16), vn_ref[:, cols], preferred_element_type=F32))
        outs.append(_finish_head(acc, l, lam, g_ref[...], za_ref[:, cols], t))
    o_ref[...] = jnp.concatenate(outs, axis=1).astype(BF16)


def _attn_sample(lams, g, q, kb, vb, ck, cv, za, *, n_seq, t, past):
    assert past % CHUNK == 0 and t <= CHUNK
    row = pl.BlockSpec((t, D_ATT), lambda b: (b, 0))
    kcache = pl.BlockSpec((None, ATT_HEADS, 2, QK_DIM, past), lambda b: (b, 0, 0, 0, 0))
    vcache = pl.BlockSpec((None, past, ATT_HEADS, V_DIM), lambda b: (b, 0, 0, 0))
    vec = _full((1, QK_DIM))
    return pl.pallas_call(
        functools.partial(_attn_sample_kernel, t=t),
        grid=(n_seq,),
        in_specs=[vec, vec, vec, vec, _full((1, V_DIM)), row, row, row, kcache, vcache, row],
        out_specs=row,
        out_shape=jax.ShapeDtypeStruct((n_seq * t, D_ATT), BF16),
        compiler_params=_params(("arbitrary",)),
        name="attn_sample",
    )(*lams, g, q, kb, vb, ck, cv, za)


def _out_proj_kernel(s_ref, a_ref, w_ref, x_ref, g_ref, y_ref):
    mix = (jnp.dot(s_ref[...], w_ref[:D_SSM, :], preferred_element_type=F32)
           + jnp.dot(a_ref[...], w_ref[D_SSM:, :], preferred_element_type=F32))
    ms = jnp.mean(mix * mix, axis=-1, keepdims=True)
    y_ref[...] = x_ref[...] + mix * lax.rsqrt(ms + EPS) * g_ref[...]


def _out_proj(ssm_out, att_out, w_bf, x2d, g, *, tm):
    n = x2d.shape[0]
    row = lambda i: (i, 0)
    half = pl.BlockSpec((tm, 512), row)
    full = pl.BlockSpec((tm, D_MODEL), row)
    return pl.pallas_call(
        _out_proj_kernel,
        grid=(n // tm,),
        in_specs=[half, half, _full((D_MODEL, D_MODEL)), full, _full((1, D_MODEL))],
        out_specs=full,
        out_shape=jax.ShapeDtypeStruct((n, D_MODEL), F32),
        compiler_params=_params(("arbitrary",)),
        name="out_proj",
    )(ssm_out, att_out, w_bf, x2d, g)


def _block_diag_weights(bbr, bbi, c_re, c_im):
    eye = jnp.eye(SSM_GROUPS // 2, dtype=F32)

    def b_side(m):
        m = m.reshape(2, SSM_GROUPS // 2, SSM_GROUP, SSM_STATE)
        return jnp.einsum('bgcp,gh->bgchp', m, eye).reshape(2, 256, N_CH // 2)

    def c_side(m):
        m = m.reshape(2, SSM_GROUPS // 2, SSM_GROUP, SSM_STATE)
        return jnp.einsum('bgcp,gh->bgphc', m, eye).reshape(2, N_CH // 2, 256)

    bdb = jnp.concatenate([b_side(bbr), b_side(bbi)], axis=2).astype(BF16)
    bdc = jnp.concatenate([c_side(c_re), -c_side(c_im)], axis=1).astype(BF16)
    return bdb, bdc


def kernel(x_prompt, x_sample, cache_k, cache_v, state_ssm_re, state_ssm_im, norm_pre_g, w_in, ssm_lambda_re,
           ssm_lambda_im, ssm_log_dt, ssm_b_re, ssm_b_im, ssm_c_re, ssm_c_im, ssm_d, glu_w1, glu_b1, glu_w2,
           glu_b2, lambda_q1, lambda_k1, lambda_q2, lambda_k2, attn_subln_g, w_out, norm_post_g):
    bp, sp, _ = x_prompt.shape
    bs, ss, _ = x_sample.shape
    past = cache_k.shape[2]

    a_re, a_im, bbr, bbi = _prep(ssm_lambda_re[0], ssm_lambda_im[0], ssm_log_dt[0], ssm_b_re[0], ssm_b_im[0])
    bdb, bdc = _block_diag_weights(bbr, bbi, ssm_c_re[0], ssm_c_im[0])
    a_re = a_re.reshape(1, N_CH)
    a_im = a_im.reshape(1, N_CH)
    dvec = ssm_d[0].reshape(1, D_SSM)
    w_in_bf = w_in[0].astype(BF16)
    w_out_bf = w_out[0].astype(BF16)
    w1 = glu_w1[0].astype(BF16)
    w2 = glu_w2[0].astype(BF16)
    b1 = glu_b1[0].reshape(1, D_SSM)
    b2 = glu_b2[0].reshape(1, D_SSM)
    g_pre = norm_pre_g[0].reshape(1, D_MODEL)
    g_post = norm_post_g[0].reshape(1, D_MODEL)
    g_sub = attn_subln_g[0].reshape(1, V_DIM)
    lams = tuple(v[0].reshape(1, QK_DIM) for v in (lambda_q1, lambda_k1, lambda_q2, lambda_k2))
    inv = ROPE_THETA ** (-jnp.arange(ROPE_DIM // 2, dtype=F32) * 2.0 / ROPE_DIM)
    rotary_lane = (jnp.arange(LANES) % QK_DIM) < ROPE_DIM
    inv_lane = jnp.where(rotary_lane, jnp.tile(inv, LANES // (ROPE_DIM // 2)), 0.0).reshape(1, LANES)

    def run(x, n_seq, seq_len, pos0, chained, h0r, h0i, tm):
        x2d = x.reshape(n_seq * seq_len, D_MODEL)
        u, zs, q, kf, kb, vf, vb, za = _in_proj(x2d, g_pre, w_in_bf, inv_lane, seq_len=seq_len, pos0=pos0, tm=tm,
                                                transposed_qv=chained)
        ssm_out, hfr, hfi = _s5(u, zs, bdb, bdc, dvec, a_re, a_im, w1, b1, w2, b2, h0r, h0i,
                                n_seq=n_seq, seq_len=seq_len, chained=chained)
        if chained:
            att = _attn_prompt(lams, g_sub, q, kb, vb, za, n_seq=n_seq, seq_len=seq_len)
            k_out = jnp.transpose(kf, (0, 4, 1, 2, 3))[None]
        else:
            ck = jnp.transpose(cache_k[0], (0, 2, 3, 4, 1))
            att = _attn_sample(lams, g_sub, q, kb, vb, ck, cache_v[0], za, n_seq=n_seq, t=seq_len, past=past)
            k_out = kf.reshape(1, n_seq, seq_len, ATT_HEADS, 2, QK_DIM)
        y = _out_proj(ssm_out, att, w_out_bf, x2d, g_post, tm=tm)
        return (y.reshape(n_seq, seq_len, D_MODEL),
                k_out,
                vf.reshape(1, n_seq, seq_len, ATT_HEADS, V_DIM),
                hfr.reshape(1, n_seq, SSM_GROUPS, SSM_STATE),
                hfi.reshape(1, n_seq, SSM_GROUPS, SSM_STATE))

    zero = jnp.zeros((1, N_CH), F32)
    yp, kp, vp, hrp, hip = run(x_prompt, bp, sp, 0, True, zero, zero, 512)
    h0r = state_ssm_re[0].reshape(bs, N_CH)
    h0i = state_ssm_im[0].reshape(bs, N_CH)
    ys, ks, vs, hrs, his = run(x_sample, bs, ss, past, False, h0r, h0i, bs * ss)
    return (yp, ys, kp, vp, hrp, hip, ks, vs, hrs, his)
```

```python
import functools
import math

import jax
import jax.numpy as jnp
from jax import lax
from jax.experimental import pallas as pl
from jax.experimental.pallas import tpu as pltpu

F32 = jnp.float32
BF16 = jnp.bfloat16

D_MODEL = 1024
D_SSM = 512
D_ATT = 512
SSM_GROUP = 16
SSM_GROUPS = 32
SSM_STATE = 64
N_CH = SSM_GROUPS * SSM_STATE
ATT_HEADS = 4
QK_DIM = 64
V_DIM = 128
ROPE_DIM = 16
ROPE_THETA = 500000.0
CHUNK = 64
EPS = 1e-6
D_IN = 3072
LAMBDA_INIT = 0.8 - 0.6 * math.exp(-0.3 * 0)

LANES = 128
SUBLANES = 8
N_SLAB = N_CH // LANES
SCAN_UNROLL = 4
ROW_PAD = 4
NEG = -1e30
ATT_TILE = 256
HEADS_PER_STEP = 4
LOG2E = math.log2(math.e)
Q_SCALE = QK_DIM ** -0.5 * LOG2E
VMEM_LIMIT = 56 * 1024 * 1024


def _params(sem):
    return pltpu.CompilerParams(dimension_semantics=sem, vmem_limit_bytes=VMEM_LIMIT)


def _full(shape):
    n = len(shape)
    return pl.BlockSpec(shape, lambda *_: (0,) * n)


def _prep_kernel(lr_ref, li_ref, ldt_ref, br_ref, bi_ref, ar_ref, ai_ref, bbr_ref, bbi_ref):
    lr = lr_ref[...]
    li = li_ref[...]
    dt = jnp.exp(ldt_ref[...])
    mag = jnp.exp(lr * dt)
    ar = mag * jnp.cos(li * dt)
    ai = mag * jnp.sin(li * dt)
    den = lr * lr + li * li
    cr = ((ar - 1.0) * lr + ai * li) / den
    ci = (ai * lr - (ar - 1.0) * li) / den
    ar_ref[...] = ar
    ai_ref[...] = ai
    br = br_ref[...]
    bi = bi_ref[...]
    crb = cr[:, None, :]
    cib = ci[:, None, :]
    bbr_ref[...] = crb * br - cib * bi
    bbi_ref[...] = crb * bi + cib * br


def _prep(lam_re, lam_im, log_dt, b_re, b_im):
    g, p, c = b_re.shape
    brt = jnp.swapaxes(b_re, 1, 2)
    bit = jnp.swapaxes(b_im, 1, 2)
    out_shape = (jax.ShapeDtypeStruct((g, p), F32), jax.ShapeDtypeStruct((g, p), F32),
                 jax.ShapeDtypeStruct((g, c, p), F32), jax.ShapeDtypeStruct((g, c, p), F32))
    return pl.pallas_call(_prep_kernel, out_shape=out_shape, name="s5_prep")(
        lam_re, lam_im, log_dt.reshape(g, 1), brt, bit)


def _in_proj_kernel(x_ref, g_ref, w_ref, inv_ref, u_ref, zs_ref, q_ref, kf_ref, kb_ref, vf_ref,
                    vb_ref, za_ref, cl_sc, sl_sc, *, tm, seq_len, pos0, transposed_qv):
    i = pl.program_id(0)
    x = x_ref[...]
    ms = jnp.mean(x * x, axis=-1, keepdims=True)
    hn = (x * lax.rsqrt(ms + EPS) * g_ref[...]).astype(BF16)

    inv = inv_ref[...]

    @pl.when(i == 0)
    def _():
        off = (lax.broadcasted_iota(jnp.int32, (tm, LANES), 0) & (seq_len - 1)).astype(F32) * inv
        cl_sc[...] = jnp.cos(off)
        sl_sc[...] = jnp.sin(off)

    base = (pos0 + ((i * tm) & (seq_len - 1))).astype(F32) * jnp.broadcast_to(inv, (SUBLANES, LANES))
    cb = jnp.cos(base)[:1]
    sb = jnp.sin(base)[:1]
    cl = cl_sc[...]
    sl = sl_sc[...]
    c_m = cb * cl - sb * sl
    sin = sb * cl + cb * sl
    lane = lax.broadcasted_iota(jnp.int32, (tm, LANES), 1) & (QK_DIM - 1)
    half = ROPE_DIM // 2
    s_lo = jnp.where(lane < half, -sin, 0.0)
    s_hi = jnp.where(lane >= half, sin, 0.0)

    def seg(lo, hi):
        return jnp.dot(hn, w_ref[:, lo:hi], preferred_element_type=F32)

    def rope(t):
        outs = []
        for h in range(ATT_HEADS):
            th = t[:, h * LANES:(h + 1) * LANES]
            outs.append(th * c_m + pltpu.roll(th, LANES - half, 1) * s_lo + pltpu.roll(th, half, 1) * s_hi)
        return jnp.concatenate(outs, axis=1)

    def put(ref, t):
        if not transposed_qv:
            ref[...] = t.astype(BF16)
            return
        for h in range(ATT_HEADS):
            tt = t[:, h * LANES:(h + 1) * LANES].T.astype(BF16)
            for c in range(tm // ATT_TILE):
                ref[h, c] = tt[:, c * ATT_TILE:(c + 1) * ATT_TILE]

    u_ref[...] = seg(0, D_SSM).astype(BF16)
    zs_ref[...] = seg(D_SSM, 2 * D_SSM).astype(BF16)
    q = rope(seg(1024, 1536))
    put(q_ref, q * Q_SCALE)
    k = rope(seg(1536, 2048))
    if transposed_qv:
        for h in range(ATT_HEADS):
            kt = k[:, h * LANES:(h + 1) * LANES].T
            kf_ref[h, 0] = kt[:QK_DIM]
            kf_ref[h, 1] = kt[QK_DIM:]
    else:
        kf_ref[...] = k
    kb_ref[...] = k.astype(BF16)
    v = seg(2048, 2560)
    for h in range(ATT_HEADS):
        vf_ref[:, h, :] = v[:, h * LANES:(h + 1) * LANES]
    put(vb_ref, v)
    za_ref[...] = seg(2560, 3072).astype(BF16)


def _in_proj(x2d, g, w_bf, inv_lane, *, seq_len, pos0, tm, transposed_qv):
    n = x2d.shape[0]
    assert n % tm == 0 and seq_len & (seq_len - 1) == 0 and (tm % seq_len == 0 or seq_len % tm == 0)
    row = lambda i: (i, 0)
    o512 = pl.BlockSpec((tm, 512), row)
    shp = lambda dt: jax.ShapeDtypeStruct((n, 512), dt)
    if transposed_qv:
        assert seq_len % tm == 0 and tm % ATT_TILE == 0
        tps = seq_len // tm
        per = tm // ATT_TILE
        t_spec = pl.BlockSpec((None, ATT_HEADS, per, LANES, ATT_TILE), lambda i: (i // tps, 0, i % tps, 0, 0))
        t_shape = jax.ShapeDtypeStruct((n // seq_len, ATT_HEADS, seq_len // ATT_TILE, LANES, ATT_TILE), BF16)
        kf_spec = pl.BlockSpec((None, ATT_HEADS, 2, QK_DIM, tm), lambda i: (i // tps, 0, 0, 0, i % tps))
        kf_shape = jax.ShapeDtypeStruct((n // seq_len, ATT_HEADS, 2, QK_DIM, seq_len), F32)
    else:
        t_spec, t_shape = o512, shp(BF16)
        kf_spec, kf_shape = o512, shp(F32)
    vf_spec = pl.BlockSpec((tm, ATT_HEADS, V_DIM), lambda i: (i, 0, 0))
    vf_shape = jax.ShapeDtypeStruct((n, ATT_HEADS, V_DIM), F32)
    return pl.pallas_call(
        functools.partial(_in_proj_kernel, tm=tm, seq_len=seq_len, pos0=pos0, transposed_qv=transposed_qv),
        grid=(n // tm,),
        in_specs=[pl.BlockSpec((tm, D_MODEL), row), _full((1, D_MODEL)), _full((D_MODEL, D_IN)),
                  _full((1, LANES))],
        out_specs=[o512, o512, t_spec, kf_spec, o512, vf_spec, t_spec, o512],
        out_shape=[shp(BF16), shp(BF16), t_shape, kf_shape, shp(BF16), vf_shape, t_shape, shp(BF16)],
        scratch_shapes=[pltpu.VMEM((tm, LANES), F32), pltpu.VMEM((tm, LANES), F32)],
        compiler_params=_params(("arbitrary",)),
        name="in_proj",
    )(x2d, g, w_bf, inv_lane)


def _cmul(ar, ai, br, bi):
    return ar * br - ai * bi, ar * bi + ai * br


def _s5_kernel(u_ref, zs_ref, bdb_ref, bdc_ref, d_ref, are_ref, aim_ref, w1_ref, b1_ref, w2_ref, b2_ref,
               h0r_ref, h0i_ref, out_ref, hfr_ref, hfi_ref, hbuf, hb16, car_re, car_im,
               *, R, T, chained, slab_group):
    TP = T + ROW_PAD
    tm = R * T
    half_cols = N_CH // 2

    if chained:
        @pl.when(pl.program_id(1) == 0)
        def _():
            car_re[...] = jnp.zeros_like(car_re)
            car_im[...] = jnp.zeros_like(car_im)

    for b in range(2):
        bu = jnp.dot(u_ref[:, b * 256:(b + 1) * 256], bdb_ref[b], preferred_element_type=F32)
        for part in range(2):
            for k in range(N_SLAB // 2):
                slab = part * N_SLAB + b * (N_SLAB // 2) + k
                col = part * half_cols + k * LANES
                for j in range(R):
                    hbuf[slab, j * TP:j * TP + T, :] = bu[j * T:(j + 1) * T, col:col + LANES]

    finals_re = [None] * N_SLAB
    finals_im = [None] * N_SLAB
    for g0 in range(0, N_SLAB, slab_group):
        slabs = list(range(g0, g0 + slab_group))
        a_r = [jnp.broadcast_to(are_ref[:, c * LANES:(c + 1) * LANES], (R, LANES)) for c in slabs]
        a_i = [jnp.broadcast_to(aim_ref[:, c * LANES:(c + 1) * LANES], (R, LANES)) for c in slabs]

        def body(s, carry, slabs=slabs, a_r=a_r, a_i=a_i):
            new = []
            for n, c in enumerate(slabs):
                hr, hi = carry[n]
                rows = pl.ds(s, R, stride=TP)
                pr, pi = _cmul(a_r[n], a_i[n], hr, hi)
                nr = pr + hbuf[c, rows, :]
                ni = pi + hbuf[N_SLAB + c, rows, :]
                hbuf[c, rows, :] = nr
                hbuf[N_SLAB + c, rows, :] = ni
                new.append((nr, ni))
            return tuple(new)

        z = jnp.zeros((R, LANES), F32)
        fin = lax.fori_loop(0, T, body, tuple((z, z) for _ in slabs), unroll=SCAN_UNROLL)
        for n, c in enumerate(slabs):
            finals_re[c], finals_im[c] = fin[n]

    f_re = jnp.concatenate(finals_re, axis=1)
    f_im = jnp.concatenate(finals_im, axis=1)

    at_r, at_i = are_ref[...], aim_ref[...]
    for _ in range(T.bit_length() - 1):
        at_r, at_i = _cmul(at_r, at_i, at_r, at_i)

    if chained:
        e_r, e_i = car_re[...], car_im[...]
        rows_r, rows_i = [], []
        for j in range(R):
            rows_r.append(e_r)
            rows_i.append(e_i)
            pr, pi = _cmul(at_r, at_i, e_r, e_i)
            e_r = pr + f_re[j:j + 1]
            e_i = pi + f_im[j:j + 1]
        ent_r = jnp.concatenate(rows_r, axis=0)
        ent_i = jnp.concatenate(rows_i, axis=0)
        car_re[...] = e_r
        car_im[...] = e_i
        hfr_ref[0] = e_r
        hfi_ref[0] = e_i
    else:
        ent_r, ent_i = h0r_ref[...], h0i_ref[...]
        pr, pi = _cmul(at_r, at_i, ent_r, ent_i)
        hfr_ref[...] = pr + f_re
        hfi_ref[...] = pi + f_im

    for g0 in range(0, N_SLAB, slab_group):
        slabs = list(range(g0, g0 + slab_group))
        a_r = [jnp.broadcast_to(are_ref[:, c * LANES:(c + 1) * LANES], (R, LANES)) for c in slabs]
        a_i = [jnp.broadcast_to(aim_ref[:, c * LANES:(c + 1) * LANES], (R, LANES)) for c in slabs]

        def fix(s, carry, slabs=slabs, a_r=a_r, a_i=a_i):
            new = []
            for n, c in enumerate(slabs):
                er, ei = carry[n]
                rows = pl.ds(s, R, stride=TP)
                er, ei = _cmul(a_r[n], a_i[n], er, ei)
                hbuf[c, rows, :] = hbuf[c, rows, :] + er
                hbuf[N_SLAB + c, rows, :] = hbuf[N_SLAB + c, rows, :] + ei
                new.append((er, ei))
            return tuple(new)

        init = tuple((ent_r[:, c * LANES:(c + 1) * LANES], ent_i[:, c * LANES:(c + 1) * LANES]) for c in slabs)
        lax.fori_loop(0, T, fix, init, unroll=SCAN_UNROLL)

    for b in range(2):
        for part in range(2):
            for k in range(N_SLAB // 2):
                slab = part * N_SLAB + b * (N_SLAB // 2) + k
                col = b * N_CH + part * half_cols + k * LANES
                for j in range(R):
                    hb16[j * T:(j + 1) * T, col:col + LANES] = hbuf[slab, j * TP:j * TP + T, :].astype(BF16)
    ys = [jnp.dot(hb16[:, b * N_CH:(b + 1) * N_CH], bdc_ref[b], preferred_element_type=F32) for b in range(2)]
    y = jnp.concatenate(ys, axis=1) + u_ref[...].astype(F32) * d_ref[...]

    gb = jax.nn.gelu(y).astype(BF16)
    y1 = jnp.dot(gb, w1_ref[...], preferred_element_type=F32) + b1_ref[...]
    y2 = jnp.dot(gb, w2_ref[...], preferred_element_type=F32) + b2_ref[...]
    out_ref[...] = (y1 * jax.nn.sigmoid(y2) * jax.nn.silu(zs_ref[...].astype(F32))).astype(BF16)


def _s5(u, zs, bdb, bdc, dvec, a_re, a_im, w1, b1, w2, b2, h0r, h0i, *, n_seq, seq_len, chained):
    n = u.shape[0]
    if chained:
        R, T = SUBLANES, 64
        tm = R * T
        tiles = seq_len // tm
        grid = (n_seq, tiles)
        row = lambda b, t: (b * tiles + t, 0)
        hf_shape = jax.ShapeDtypeStruct((n_seq, 1, N_CH), F32)
        hf_spec = pl.BlockSpec((1, 1, N_CH), lambda b, t: (b, 0, 0))
        slab_group = 8
    else:
        R, T = n_seq, seq_len
        tm = R * T
        assert tm == n
        grid = (1, 1)
        row = lambda b, t: (0, 0)
        hf_shape = jax.ShapeDtypeStruct((n_seq, N_CH), F32)
        hf_spec = _full((n_seq, N_CH))
        slab_group = 4
    assert T & (T - 1) == 0 and R % SUBLANES == 0
    blk = pl.BlockSpec((tm, D_SSM), row)
    return pl.pallas_call(
        functools.partial(_s5_kernel, R=R, T=T, chained=chained, slab_group=slab_group),
        grid=grid,
        in_specs=[blk, blk, _full(bdb.shape), _full(bdc.shape), _full((1, D_SSM)), _full((1, N_CH)),
                  _full((1, N_CH)), _full((D_SSM, D_SSM)), _full((1, D_SSM)), _full((D_SSM, D_SSM)),
                  _full((1, D_SSM)), _full(h0r.shape), _full(h0i.shape)],
        out_specs=[blk, hf_spec, hf_spec],
        out_shape=[jax.ShapeDtypeStruct((n, D_SSM), BF16), hf_shape, hf_shape],
        scratch_shapes=[pltpu.VMEM((2 * N_SLAB, R * (T + ROW_PAD), LANES), F32),
                        pltpu.VMEM((tm, 2 * N_CH), BF16),
                        pltpu.VMEM((1, N_CH), F32), pltpu.VMEM((1, N_CH), F32)],
        compiler_params=_params(("arbitrary", "arbitrary")),
        name="s5_chained" if chained else "s5_independent",
    )(u, zs, bdb, bdc, dvec, a_re, a_im, w1, b1, w2, b2, h0r, h0i)


def _lambda(lq1, lk1, lq2, lk2):
    s1 = jnp.sum(lq1[...] * lk1[...], axis=1, keepdims=True)
    s2 = jnp.sum(lq2[...] * lk2[...], axis=1, keepdims=True)
    return jnp.exp(s1) - jnp.exp(s2) + LAMBDA_INIT


def _stack_maps(q):
    lane = lax.broadcasted_iota(jnp.int32, q.shape, 1)
    zero = jnp.zeros_like(q)
    return jnp.concatenate([jnp.where(lane < QK_DIM, q, zero), jnp.where(lane >= QK_DIM, q, zero)], axis=0)


def _subln_gate(o, g, za):
    ms = jnp.mean(o * o, axis=-1, keepdims=True)
    on = (o * lax.rsqrt(ms + EPS) * g) * (1.0 - LAMBDA_INIT)
    return on * jax.nn.silu(za.astype(F32))


def _finish_head(acc, l, lam, g, za, t):
    inv = 1.0 / l
    o = acc[:t] * inv[:t] - lam * (acc[t:] * inv[t:])
    return _subln_gate(o, g, za)


_NT = (((1,), (1,)), ((), ()))


def _attn_kernel(lq1, lk1, lq2, lk2, g_ref, qt_ref, k_ref, vt_ref, za_ref, o_ref, m_sc, l_sc, acc_sc, s_a, s_b):
    tq = ATT_TILE
    qi = pl.program_id(2)
    q2t = []
    for h in range(HEADS_PER_STEP):
        qt = qt_ref[h]
        row = lax.broadcasted_iota(jnp.int32, qt.shape, 0)
        zero = jnp.zeros_like(qt)
        q2t.append(jnp.concatenate([jnp.where(row < QK_DIM, qt, zero), jnp.where(row >= QK_DIM, qt, zero)], axis=1))
    m_sc[...] = jnp.full_like(m_sc, NEG)
    l_sc[...] = jnp.zeros_like(l_sc)
    acc_sc[...] = jnp.zeros_like(acc_sc)
    heads = range(HEADS_PER_STEP)

    def scores_into(s_ref, kt, h):
        start = pl.multiple_of(kt * tq, tq)
        s_ref[h] = jnp.dot(k_ref[pl.ds(start, tq), h * LANES:(h + 1) * LANES], q2t[h],
                           preferred_element_type=F32)

    def softmax_accumulate(s_ref, kt, masked, h):
        s = s_ref[h]
        if masked:
            kc = lax.broadcasted_iota(jnp.int32, s.shape, 0) // CHUNK
            qc = (lax.broadcasted_iota(jnp.int32, s.shape, 1) & (tq - 1)) // CHUNK
            s = jnp.where(kc <= qc, s, NEG)
        m_old = m_sc[h]
        m_new = jnp.maximum(m_old, jnp.max(s, axis=0, keepdims=True))
        alpha = jnp.exp2(m_old - m_new)
        p = jnp.exp2(s - m_new)
        l_sc[h] = alpha * l_sc[h] + jnp.sum(p, axis=0, keepdims=True)
        acc_sc[h] = alpha * acc_sc[h] + jnp.dot(vt_ref[h, kt], p.astype(BF16), preferred_element_type=F32)
        m_sc[h] = m_new

    def stage(kt, s_cur, s_nxt):
        for h in heads:
            scores_into(s_nxt, kt + 1, h)
            softmax_accumulate(s_cur, kt, False, h)

    for h in heads:
        scores_into(s_a, 0, h)

    def body(kt, c):
        even = (kt & 1) == 0

        @pl.when(even)
        def _():
            stage(kt, s_a, s_b)

        @pl.when(jnp.logical_not(even))
        def _():
            stage(kt, s_b, s_a)

        return c

    lax.fori_loop(0, qi, body, 0)

    @pl.when((qi & 1) == 0)
    def _():
        for h in heads:
            softmax_accumulate(s_a, qi, True, h)

    @pl.when((qi & 1) == 1)
    def _():
        for h in heads:
            softmax_accumulate(s_b, qi, True, h)

    lam = _lambda(lq1, lk1, lq2, lk2)
    for h in range(HEADS_PER_STEP):
        acc = acc_sc[h]
        inv = 1.0 / l_sc[h]
        ot = acc[:, :tq] * inv[:, :tq] - lam * (acc[:, tq:] * inv[:, tq:])
        cols = slice(h * LANES, (h + 1) * LANES)
        o_ref[:, cols] = _subln_gate(ot.T, g_ref[...], za_ref[:, cols]).astype(BF16)


def _attn_prompt(lams, g, qt, kb, vt, za, *, n_seq, seq_len):
    tq = ATT_TILE
    nq = seq_len // tq
    hps = HEADS_PER_STEP
    assert tq % CHUNK == 0 and tq & (tq - 1) == 0 and ATT_HEADS % hps == 0
    rowspec = pl.BlockSpec((tq, hps * LANES), lambda b, h, i: (b * nq + i, h))
    qtspec = pl.BlockSpec((None, hps, None, LANES, tq), lambda b, h, i: (b, h, i, 0, 0))
    kspec = pl.BlockSpec((seq_len, hps * LANES), lambda b, h, i: (b, h))
    vtspec = pl.BlockSpec((None, hps, nq, LANES, tq), lambda b, h, i: (b, h, 0, 0, 0))
    vec = _full((1, QK_DIM))
    stat = pltpu.VMEM((hps, 1, 2 * tq), F32)
    return pl.pallas_call(
        _attn_kernel,
        grid=(n_seq, ATT_HEADS // hps, nq),
        in_specs=[vec, vec, vec, vec, _full((1, V_DIM)), qtspec, kspec, vtspec, rowspec],
        out_specs=rowspec,
        out_shape=jax.ShapeDtypeStruct((n_seq * seq_len, D_ATT), BF16),
        scratch_shapes=[stat, stat, pltpu.VMEM((hps, V_DIM, 2 * tq), F32),
                        pltpu.VMEM((hps, tq, 2 * tq), F32), pltpu.VMEM((hps, tq, 2 * tq), F32)],
        compiler_params=_params(("arbitrary", "arbitrary", "arbitrary")),
        name="attn_prompt",
    )(*lams, g, qt, kb, vt, za)


def _attn_sample_kernel(lq1, lk1, lq2, lk2, g_ref, q_ref, kn_ref, vn_ref, ck_ref, cv_ref, za_ref, o_ref, *, t):
    lam = _lambda(lq1, lk1, lq2, lk2)
    outs = []
    for h in range(ATT_HEADS):
        cols = slice(h * LANES, (h + 1) * LANES)
        q2 = _stack_maps(q_ref[:, cols])
        kpt = jnp.concatenate([ck_ref[h, 0], ck_ref[h, 1]], axis=0).astype(BF16)
        vp = cv_ref[:, h, :].astype(BF16)
        s_p = jnp.dot(q2, kpt, preferred_element_type=F32)
        s_n = lax.dot_general(q2, kn_ref[:, cols], _NT, preferred_element_type=F32)
        m = jnp.maximum(jnp.max(s_p, axis=1, keepdims=True), jnp.max(s_n, axis=1, keepdims=True))
        p_p = jnp.exp2(s_p - m)
        p_n = jnp.exp2(s_n - m)
        l = jnp.sum(p_p, axis=1, keepdims=True) + jnp.sum(p_n, axis=1, keepdims=True)
        acc = (jnp.dot(p_p.astype(BF16), vp, preferred_element_type=F32)
               + jnp.dot(p_n.astype(BF16), vn_ref[:, cols], preferred_element_type=F32))
        outs.append(_finish_head(acc, l, lam, g_ref[...], za_ref[:, cols], t))
    o_ref[...] = jnp.concatenate(outs, axis=1).astype(BF16)


def _attn_sample(lams, g, q, kb, vb, ck, cv, za, *, n_seq, t, past):
    assert past % CHUNK == 0 and t <= CHUNK
    row = pl.BlockSpec((t, D_ATT), lambda b: (b, 0))
    kcache = pl.BlockSpec((None, ATT_HEADS, 2, QK_DIM, past), lambda b: (b, 0, 0, 0, 0))
    vcache = pl.BlockSpec((None, past, ATT_HEADS, V_DIM), lambda b: (b, 0, 0, 0))
    vec = _full((1, QK_DIM))
    return pl.pallas_call(
        functools.partial(_attn_sample_kernel, t=t),
        grid=(n_seq,),
        in_specs=[vec, vec, vec, vec, _full((1, V_DIM)), row, row, row, kcache, vcache, row],
        out_specs=row,
        out_shape=jax.ShapeDtypeStruct((n_seq * t, D_ATT), BF16),
        compiler_params=_params(("arbitrary",)),
        name="attn_sample",
    )(*lams, g, q, kb, vb, ck, cv, za)


def _out_proj_kernel(s_ref, a_ref, w_ref, x_ref, g_ref, y_ref):
    mix = (jnp.dot(s_ref[...], w_ref[:D_SSM, :], preferred_element_type=F32)
           + jnp.dot(a_ref[...], w_ref[D_SSM:, :], preferred_element_type=F32))
    ms = jnp.mean(mix * mix, axis=-1, keepdims=True)
    y_ref[...] = x_ref[...] + mix * lax.rsqrt(ms + EPS) * g_ref[...]


def _out_proj(ssm_out, att_out, w_bf, x2d, g, *, tm):
    n = x2d.shape[0]
    row = lambda i: (i, 0)
    half = pl.BlockSpec((tm, 512), row)
    full = pl.BlockSpec((tm, D_MODEL), row)
    return pl.pallas_call(
        _out_proj_kernel,
        grid=(n // tm,),
        in_specs=[half, half, _full((D_MODEL, D_MODEL)), full, _full((1, D_MODEL))],
        out_specs=full,
        out_shape=jax.ShapeDtypeStruct((n, D_MODEL), F32),
        compiler_params=_params(("arbitrary",)),
        name="out_proj",
    )(ssm_out, att_out, w_bf, x2d, g)


def _block_diag_weights(bbr, bbi, c_re, c_im):
    eye = jnp.eye(SSM_GROUPS // 2, dtype=F32)

    def b_side(m):
        m = m.reshape(2, SSM_GROUPS // 2, SSM_GROUP, SSM_STATE)
        return jnp.einsum('bgcp,gh->bgchp', m, eye).reshape(2, 256, N_CH // 2)

    def c_side(m):
        m = m.reshape(2, SSM_GROUPS // 2, SSM_GROUP, SSM_STATE)
        return jnp.einsum('bgcp,gh->bgphc', m, eye).reshape(2, N_CH // 2, 256)

    bdb = jnp.concatenate([b_side(bbr), b_side(bbi)], axis=2).astype(BF16)
    bdc = jnp.concatenate([c_side(c_re), -c_side(c_im)], axis=1).astype(BF16)
    return bdb, bdc


def kernel(x_prompt, x_sample, cache_k, cache_v, state_ssm_re, state_ssm_im, norm_pre_g, w_in, ssm_lambda_re,
           ssm_lambda_im, ssm_log_dt, ssm_b_re, ssm_b_im, ssm_c_re, ssm_c_im, ssm_d, glu_w1, glu_b1, glu_w2,
           glu_b2, lambda_q1, lambda_k1, lambda_q2, lambda_k2, attn_subln_g, w_out, norm_post_g):
    bp, sp, _ = x_prompt.shape
    bs, ss, _ = x_sample.shape
    past = cache_k.shape[2]

    a_re, a_im, bbr, bbi = _prep(ssm_lambda_re[0], ssm_lambda_im[0], ssm_log_dt[0], ssm_b_re[0], ssm_b_im[0])
    bdb, bdc = _block_diag_weights(bbr, bbi, ssm_c_re[0], ssm_c_im[0])
    a_re = a_re.reshape(1, N_CH)
    a_im = a_im.reshape(1, N_CH)
    dvec = ssm_d[0].reshape(1, D_SSM)
    w_in_bf = w_in[0].astype(BF16)
    w_out_bf = w_out[0].astype(BF16)
    w1 = glu_w1[0].astype(BF16)
    w2 = glu_w2[0].astype(BF16)
    b1 = glu_b1[0].reshape(1, D_SSM)
    b2 = glu_b2[0].reshape(1, D_SSM)
    g_pre = norm_pre_g[0].reshape(1, D_MODEL)
    g_post = norm_post_g[0].reshape(1, D_MODEL)
    g_sub = attn_subln_g[0].reshape(1, V_DIM)
    lams = tuple(v[0].reshape(1, QK_DIM) for v in (lambda_q1, lambda_k1, lambda_q2, lambda_k2))
    inv = ROPE_THETA ** (-jnp.arange(ROPE_DIM // 2, dtype=F32) * 2.0 / ROPE_DIM)
    rotary_lane = (jnp.arange(LANES) % QK_DIM) < ROPE_DIM
    inv_lane = jnp.where(rotary_lane, jnp.tile(inv, LANES // (ROPE_DIM // 2)), 0.0).reshape(1, LANES)

    def run(x, n_seq, seq_len, pos0, chained, h0r, h0i, tm):
        x2d = x.reshape(n_seq * seq_len, D_MODEL)
        u, zs, q, kf, kb, vf, vb, za = _in_proj(x2d, g_pre, w_in_bf, inv_lane, seq_len=seq_len, pos0=pos0, tm=tm,
                                                transposed_qv=chained)
        ssm_out, hfr, hfi = _s5(u, zs, bdb, bdc, dvec, a_re, a_im, w1, b1, w2, b2, h0r, h0i,
                                n_seq=n_seq, seq_len=seq_len, chained=chained)
        if chained:
            att = _attn_prompt(lams, g_sub, q, kb, vb, za, n_seq=n_seq, seq_len=seq_len)
            k_out = jnp.transpose(kf, (0, 4, 1, 2, 3))[None]
        else:
            ck = jnp.transpose(cache_k[0], (0, 2, 3, 4, 1))
            att = _attn_sample(lams, g_sub, q, kb, vb, ck, cache_v[0], za, n_seq=n_seq, t=seq_len, past=past)
            k_out = kf.reshape(1, n_seq, seq_len, ATT_HEADS, 2, QK_DIM)
        y = _out_proj(ssm_out, att, w_out_bf, x2d, g_post, tm=tm)
        return (y.reshape(n_seq, seq_len, D_MODEL),
                k_out,
                vf.reshape(1, n_seq, seq_len, ATT_HEADS, V_DIM),
                hfr.reshape(1, n_seq, SSM_GROUPS, SSM_STATE),
                hfi.reshape(1, n_seq, SSM_GROUPS, SSM_STATE))

    zero = jnp.zeros((1, N_CH), F32)
    yp, kp, vp, hrp, hip = run(x_prompt, bp, sp, 0, True, zero, zero, 512)
    h0r = state_ssm_re[0].reshape(bs, N_CH)
    h0i = state_ssm_im[0].reshape(bs, N_CH)
    ys, ks, vs, hrs, his = run(x_sample, bs, ss, past, False, h0r, h0i, bs * ss)
    return (yp, ys, kp, vp, hrp, hip, ks, vs, hrs, his)
```

```python
import functools
import math

import jax
import jax.numpy as jnp
from jax import lax
from jax.experimental import pallas as pl
from jax.experimental.pallas import tpu as pltpu

F32 = jnp.float32
BF16 = jnp.bfloat16

D_MODEL = 1024
D_SSM = 512
D_ATT = 512
SSM_GROUP = 16
SSM_GROUPS = 32
SSM_STATE = 64
N_CH = SSM_GROUPS * SSM_STATE
ATT_HEADS = 4
QK_DIM = 64
V_DIM = 128
ROPE_DIM = 16
ROPE_THETA = 500000.0
CHUNK = 64
EPS = 1e-6
D_IN = 3072
LAMBDA_INIT = 0.8 - 0.6 * math.exp(-0.3 * 0)

LANES = 128
SUBLANES = 8
N_SLAB = N_CH // LANES
SCAN_UNROLL = 4
ROW_PAD = 4
NEG = -1e30
ATT_TILE = 256
HEADS_PER_STEP = 4
LOG2E = math.log2(math.e)
Q_SCALE = QK_DIM ** -0.5 * LOG2E
VMEM_LIMIT = 56 * 1024 * 1024


def _params(sem):
    return pltpu.CompilerParams(dimension_semantics=sem, vmem_limit_bytes=VMEM_LIMIT)


def _full(shape):
    n = len(shape)
    return pl.BlockSpec(shape, lambda *_: (0,) * n)


def _prep_kernel(lr_ref, li_ref, ldt_ref, br_ref, bi_ref, ar_ref, ai_ref, bbr_ref, bbi_ref):
    lr = lr_ref[...]
    li = li_ref[...]
    dt = jnp.exp(ldt_ref[...])
    mag = jnp.exp(lr * dt)
    ar = mag * jnp.cos(li * dt)
    ai = mag * jnp.sin(li * dt)
    den = lr * lr + li * li
    cr = ((ar - 1.0) * lr + ai * li) / den
    ci = (ai * lr - (ar - 1.0) * li) / den
    ar_ref[...] = ar
    ai_ref[...] = ai
    br = br_ref[...]
    bi = bi_ref[...]
    crb = cr[:, None, :]
    cib = ci[:, None, :]
    bbr_ref[...] = crb * br - cib * bi
    bbi_ref[...] = crb * bi + cib * br


def _prep(lam_re, lam_im, log_dt, b_re, b_im):
    g, p, c = b_re.shape
    brt = jnp.swapaxes(b_re, 1, 2)
    bit = jnp.swapaxes(b_im, 1, 2)
    out_shape = (jax.ShapeDtypeStruct((g, p), F32), jax.ShapeDtypeStruct((g, p), F32),
                 jax.ShapeDtypeStruct((g, c, p), F32), jax.ShapeDtypeStruct((g, c, p), F32))
    return pl.pallas_call(_prep_kernel, out_shape=out_shape, name="s5_prep")(
        lam_re, lam_im, log_dt.reshape(g, 1), brt, bit)


def _in_proj_kernel(x_ref, g_ref, w_ref, inv_ref, u_ref, zs_ref, q_ref, kf_ref, kb_ref, vf_ref,
                    vb_ref, za_ref, cl_sc, sl_sc, *, tm, seq_len, pos0, transposed_qv):
    i = pl.program_id(0)
    x = x_ref[...]
    ms = jnp.mean(x * x, axis=-1, keepdims=True)
    hn = (x * lax.rsqrt(ms + EPS) * g_ref[...]).astype(BF16)

    inv = inv_ref[...]

    @pl.when(i == 0)
    def _():
        off = (lax.broadcasted_iota(jnp.int32, (tm, LANES), 0) & (seq_len - 1)).astype(F32) * inv
        cl_sc[...] = jnp.cos(off)
        sl_sc[...] = jnp.sin(off)

    base = (pos0 + ((i * tm) & (seq_len - 1))).astype(F32) * jnp.broadcast_to(inv, (SUBLANES, LANES))
    cb = jnp.cos(base)[:1]
    sb = jnp.sin(base)[:1]
    cl = cl_sc[...]
    sl = sl_sc[...]
    c_m = cb * cl - sb * sl
    sin = sb * cl + cb * sl
    lane = lax.broadcasted_iota(jnp.int32, (tm, LANES), 1) & (QK_DIM - 1)
    half = ROPE_DIM // 2
    s_lo = jnp.where(lane < half, -sin, 0.0)
    s_hi = jnp.where(lane >= half, sin, 0.0)

    def seg(lo, hi):
        return jnp.dot(hn, w_ref[:, lo:hi], preferred_element_type=F32)

    def rope(t):
        outs = []
        for h in range(ATT_HEADS):
            th = t[:, h * LANES:(h + 1) * LANES]
            outs.append(th * c_m + pltpu.roll(th, LANES - half, 1) * s_lo + pltpu.roll(th, half, 1) * s_hi)
        return jnp.concatenate(outs, axis=1)

    def put(ref, t):
        if not transposed_qv:
            ref[...] = t.astype(BF16)
            return
        for h in range(ATT_HEADS):
            tt = t[:, h * LANES:(h + 1) * LANES].T.astype(BF16)
            for c in range(tm // ATT_TILE):
                ref[h, c] = tt[:, c * ATT_TILE:(c + 1) * ATT_TILE]

    u_ref[...] = seg(0, D_SSM).astype(BF16)
    zs_ref[...] = seg(D_SSM, 2 * D_SSM).astype(BF16)
    q = rope(seg(1024, 1536))
    put(q_ref, q * Q_SCALE)
    k = rope(seg(1536, 2048))
    if transposed_qv:
        for h in range(ATT_HEADS):
            kt = k[:, h * LANES:(h + 1) * LANES].T
            kf_ref[h, 0] = kt[:QK_DIM]
            kf_ref[h, 1] = kt[QK_DIM:]
    else:
        kf_ref[...] = k
    kb_ref[...] = k.astype(BF16)
    v = seg(2048, 2560)
    for h in range(ATT_HEADS):
        vf_ref[pl.ds(h, tm, stride=ATT_HEADS), :] = v[:, h * LANES:(h + 1) * LANES]
    put(vb_ref, v)
    za_ref[...] = seg(2560, 3072).astype(BF16)


def _in_proj(x2d, g, w_bf, inv_lane, *, seq_len, pos0, tm, transposed_qv):
    n = x2d.shape[0]
    assert n % tm == 0 and seq_len & (seq_len - 1) == 0 and (tm % seq_len == 0 or seq_len % tm == 0)
    row = lambda i: (i, 0)
    o512 = pl.BlockSpec((tm, 512), row)
    shp = lambda dt: jax.ShapeDtypeStruct((n, 512), dt)
    if transposed_qv:
        assert seq_len % tm == 0 and tm % ATT_TILE == 0
        tps = seq_len // tm
        per = tm // ATT_TILE
        t_spec = pl.BlockSpec((None, ATT_HEADS, per, LANES, ATT_TILE), lambda i: (i // tps, 0, i % tps, 0, 0))
        t_shape = jax.ShapeDtypeStruct((n // seq_len, ATT_HEADS, seq_len // ATT_TILE, LANES, ATT_TILE), BF16)
        kf_spec = pl.BlockSpec((None, ATT_HEADS, 2, QK_DIM, tm), lambda i: (i // tps, 0, 0, 0, i % tps))
        kf_shape = jax.ShapeDtypeStruct((n // seq_len, ATT_HEADS, 2, QK_DIM, seq_len), F32)
    else:
        t_spec, t_shape = o512, shp(BF16)
        kf_spec, kf_shape = o512, shp(F32)
    vf_spec = pl.BlockSpec((tm * ATT_HEADS, V_DIM), row)
    vf_shape = jax.ShapeDtypeStruct((n * ATT_HEADS, V_DIM), F32)
    return pl.pallas_call(
        functools.partial(_in_proj_kernel, tm=tm, seq_len=seq_len, pos0=pos0, transposed_qv=transposed_qv),
        grid=(n // tm,),
        in_specs=[pl.BlockSpec((tm, D_MODEL), row), _full((1, D_MODEL)), _full((D_MODEL, D_IN)),
                  _full((1, LANES))],
        out_specs=[o512, o512, t_spec, kf_spec, o512, vf_spec, t_spec, o512],
        out_shape=[shp(BF16), shp(BF16), t_shape, kf_shape, shp(BF16), vf_shape, t_shape, shp(BF16)],
        scratch_shapes=[pltpu.VMEM((tm, LANES), F32), pltpu.VMEM((tm, LANES), F32)],
        compiler_params=_params(("arbitrary",)),
        name="in_proj",
    )(x2d, g, w_bf, inv_lane)


def _cmul(ar, ai, br, bi):
    return ar * br - ai * bi, ar * bi + ai * br


def _s5_kernel(u_ref, zs_ref, bdb_ref, bdc_ref, d_ref, are_ref, aim_ref, w1_ref, b1_ref, w2_ref, b2_ref,
               h0r_ref, h0i_ref, out_ref, hfr_ref, hfi_ref, hbuf, hb16, car_re, car_im,
               *, R, T, chained, slab_group):
    TP = T + ROW_PAD
    tm = R * T
    half_cols = N_CH // 2

    if chained:
        @pl.when(pl.program_id(1) == 0)
        def _():
            car_re[...] = jnp.zeros_like(car_re)
            car_im[...] = jnp.zeros_like(car_im)

    for b in range(2):
        bu = jnp.dot(u_ref[:, b * 256:(b + 1) * 256], bdb_ref[b], preferred_element_type=F32)
        for part in range(2):
            for k in range(N_SLAB // 2):
                slab = part * N_SLAB + b * (N_SLAB // 2) + k
                col = part * half_cols + k * LANES
                for j in range(R):
                    hbuf[slab, j * TP:j * TP + T, :] = bu[j * T:(j + 1) * T, col:col + LANES]

    finals_re = [None] * N_SLAB
    finals_im = [None] * N_SLAB
    for g0 in range(0, N_SLAB, slab_group):
        slabs = list(range(g0, g0 + slab_group))
        a_r = [jnp.broadcast_to(are_ref[:, c * LANES:(c + 1) * LANES], (R, LANES)) for c in slabs]
        a_i = [jnp.broadcast_to(aim_ref[:, c * LANES:(c + 1) * LANES], (R, LANES)) for c in slabs]

        def body(s, carry, slabs=slabs, a_r=a_r, a_i=a_i):
            new = []
            for n, c in enumerate(slabs):
                hr, hi = carry[n]
                rows = pl.ds(s, R, stride=TP)
                pr, pi = _cmul(a_r[n], a_i[n], hr, hi)
                nr = pr + hbuf[c, rows, :]
                ni = pi + hbuf[N_SLAB + c, rows, :]
                hbuf[c, rows, :] = nr
                hbuf[N_SLAB + c, rows, :] = ni
                new.append((nr, ni))
            return tuple(new)

        z = jnp.zeros((R, LANES), F32)
        fin = lax.fori_loop(0, T, body, tuple((z, z) for _ in slabs), unroll=SCAN_UNROLL)
        for n, c in enumerate(slabs):
            finals_re[c], finals_im[c] = fin[n]

    f_re = jnp.concatenate(finals_re, axis=1)
    f_im = jnp.concatenate(finals_im, axis=1)

    at_r, at_i = are_ref[...], aim_ref[...]
    for _ in range(T.bit_length() - 1):
        at_r, at_i = _cmul(at_r, at_i, at_r, at_i)

    if chained:
        e_r, e_i = car_re[...], car_im[...]
        rows_r, rows_i = [], []
        for j in range(R):
            rows_r.append(e_r)
            rows_i.append(e_i)
            pr, pi = _cmul(at_r, at_i, e_r, e_i)
            e_r = pr + f_re[j:j + 1]
            e_i = pi + f_im[j:j + 1]
        ent_r = jnp.concatenate(rows_r, axis=0)
        ent_i = jnp.concatenate(rows_i, axis=0)
        car_re[...] = e_r
        car_im[...] = e_i
        hfr_ref[0] = e_r
        hfi_ref[0] = e_i
    else:
        ent_r, ent_i = h0r_ref[...], h0i_ref[...]
        pr, pi = _cmul(at_r, at_i, ent_r, ent_i)
        hfr_ref[...] = pr + f_re
        hfi_ref[...] = pi + f_im

    for g0 in range(0, N_SLAB, slab_group):
        slabs = list(range(g0, g0 + slab_group))
        a_r = [jnp.broadcast_to(are_ref[:, c * LANES:(c + 1) * LANES], (R, LANES)) for c in slabs]
        a_i = [jnp.broadcast_to(aim_ref[:, c * LANES:(c + 1) * LANES], (R, LANES)) for c in slabs]

        def fix(s, carry, slabs=slabs, a_r=a_r, a_i=a_i):
            new = []
            for n, c in enumerate(slabs):
                er, ei = carry[n]
                rows = pl.ds(s, R, stride=TP)
                er, ei = _cmul(a_r[n], a_i[n], er, ei)
                hbuf[c, rows, :] = hbuf[c, rows, :] + er
                hbuf[N_SLAB + c, rows, :] = hbuf[N_SLAB + c, rows, :] + ei
                new.append((er, ei))
            return tuple(new)

        init = tuple((ent_r[:, c * LANES:(c + 1) * LANES], ent_i[:, c * LANES:(c + 1) * LANES]) for c in slabs)
        lax.fori_loop(0, T, fix, init, unroll=SCAN_UNROLL)

    for b in range(2):
        for part in range(2):
            for k in range(N_SLAB // 2):
                slab = part * N_SLAB + b * (N_SLAB // 2) + k
                col = b * N_CH + part * half_cols + k * LANES
                for j in range(R):
                    hb16[j * T:(j + 1) * T, col:col + LANES] = hbuf[slab, j * TP:j * TP + T, :].astype(BF16)
    ys = [jnp.dot(hb16[:, b * N_CH:(b + 1) * N_CH], bdc_ref[b], preferred_element_type=F32) for b in range(2)]
    y = jnp.concatenate(ys, axis=1) + u_ref[...].astype(F32) * d_ref[...]

    gb = jax.nn.gelu(y).astype(BF16)
    y1 = jnp.dot(gb, w1_ref[...], preferred_element_type=F32) + b1_ref[...]
    y2 = jnp.dot(gb, w2_ref[...], preferred_element_type=F32) + b2_ref[...]
    out_ref[...] = (y1 * jax.nn.sigmoid(y2) * jax.nn.silu(zs_ref[...].astype(F32))).astype(BF16)


def _s5(u, zs, bdb, bdc, dvec, a_re, a_im, w1, b1, w2, b2, h0r, h0i, *, n_seq, seq_len, chained):
    n = u.shape[0]
    if chained:
        R, T = SUBLANES, 64
        tm = R * T
        tiles = seq_len // tm
        grid = (n_seq, tiles)
        row = lambda b, t: (b * tiles + t, 0)
        hf_shape = jax.ShapeDtypeStruct((n_seq, 1, N_CH), F32)
        hf_spec = pl.BlockSpec((1, 1, N_CH), lambda b, t: (b, 0, 0))
        slab_group = 8
    else:
        R, T = n_seq, seq_len
        tm = R * T
        assert tm == n
        grid = (1, 1)
        row = lambda b, t: (0, 0)
        hf_shape = jax.ShapeDtypeStruct((n_seq, N_CH), F32)
        hf_spec = _full((n_seq, N_CH))
        slab_group = 4
    assert T & (T - 1) == 0 and R % SUBLANES == 0
    blk = pl.BlockSpec((tm, D_SSM), row)
    return pl.pallas_call(
        functools.partial(_s5_kernel, R=R, T=T, chained=chained, slab_group=slab_group),
        grid=grid,
        in_specs=[blk, blk, _full(bdb.shape), _full(bdc.shape), _full((1, D_SSM)), _full((1, N_CH)),
                  _full((1, N_CH)), _full((D_SSM, D_SSM)), _full((1, D_SSM)), _full((D_SSM, D_SSM)),
                  _full((1, D_SSM)), _full(h0r.shape), _full(h0i.shape)],
        out_specs=[blk, hf_spec, hf_spec],
        out_shape=[jax.ShapeDtypeStruct((n, D_SSM), BF16), hf_shape, hf_shape],
        scratch_shapes=[pltpu.VMEM((2 * N_SLAB, R * (T + ROW_PAD), LANES), F32),
                        pltpu.VMEM((tm, 2 * N_CH), BF16),
                        pltpu.VMEM((1, N_CH), F32), pltpu.VMEM((1, N_CH), F32)],
        compiler_params=_params(("arbitrary", "arbitrary")),
        name="s5_chained" if chained else "s5_independent",
    )(u, zs, bdb, bdc, dvec, a_re, a_im, w1, b1, w2, b2, h0r, h0i)


def _lambda(lq1, lk1, lq2, lk2):
    s1 = jnp.sum(lq1[...] * lk1[...], axis=1, keepdims=True)
    s2 = jnp.sum(lq2[...] * lk2[...], axis=1, keepdims=True)
    return jnp.exp(s1) - jnp.exp(s2) + LAMBDA_INIT


def _stack_maps(q):
    lane = lax.broadcasted_iota(jnp.int32, q.shape, 1)
    zero = jnp.zeros_like(q)
    return jnp.concatenate([jnp.where(lane < QK_DIM, q, zero), jnp.where(lane >= QK_DIM, q, zero)], axis=0)


def _subln_gate(o, g, za):
    ms = jnp.mean(o * o, axis=-1, keepdims=True)
    on = (o * lax.rsqrt(ms + EPS) * g) * (1.0 - LAMBDA_INIT)
    return on * jax.nn.silu(za.astype(F32))


def _finish_head(acc, l, lam, g, za, t):
    inv = 1.0 / l
    o = acc[:t] * inv[:t] - lam * (acc[t:] * inv[t:])
    return _subln_gate(o, g, za)


_NT = (((1,), (1,)), ((), ()))


def _attn_kernel(lq1, lk1, lq2, lk2, g_ref, qt_ref, k_ref, vt_ref, za_ref, o_ref, m_sc, l_sc, acc_sc, s_a, s_b):
    tq = ATT_TILE
    qi = pl.program_id(2)
    q2t = []
    for h in range(HEADS_PER_STEP):
        qt = qt_ref[h]
        row = lax.broadcasted_iota(jnp.int32, qt.shape, 0)
        zero = jnp.zeros_like(qt)
        q2t.append(jnp.concatenate([jnp.where(row < QK_DIM, qt, zero), jnp.where(row >= QK_DIM, qt, zero)], axis=1))
    m_sc[...] = jnp.full_like(m_sc, NEG)
    l_sc[...] = jnp.zeros_like(l_sc)
    acc_sc[...] = jnp.zeros_like(acc_sc)
    heads = range(HEADS_PER_STEP)

    def scores_into(s_ref, kt, h):
        start = pl.multiple_of(kt * tq, tq)
        s_ref[h] = jnp.dot(k_ref[pl.ds(start, tq), h * LANES:(h + 1) * LANES], q2t[h],
                           preferred_element_type=F32)

    def softmax_accumulate(s_ref, kt, masked, h):
        s = s_ref[h]
        if masked:
            kc = lax.broadcasted_iota(jnp.int32, s.shape, 0) // CHUNK
            qc = (lax.broadcasted_iota(jnp.int32, s.shape, 1) & (tq - 1)) // CHUNK
            s = jnp.where(kc <= qc, s, NEG)
        m_old = m_sc[h]
        m_new = jnp.maximum(m_old, jnp.max(s, axis=0, keepdims=True))
        alpha = jnp.exp2(m_old - m_new)
        p = jnp.exp2(s - m_new)
        l_sc[h] = alpha * l_sc[h] + jnp.sum(p, axis=0, keepdims=True)
        acc_sc[h] = alpha * acc_sc[h] + jnp.dot(vt_ref[h, kt], p.astype(BF16), preferred_element_type=F32)
        m_sc[h] = m_new

    def stage(kt, s_cur, s_nxt):
        for h in heads:
            scores_into(s_nxt, kt + 1, h)
            softmax_accumulate(s_cur, kt, False, h)

    for h in heads:
        scores_into(s_a, 0, h)

    def body(kt, c):
        even = (kt & 1) == 0

        @pl.when(even)
        def _():
            stage(kt, s_a, s_b)

        @pl.when(jnp.logical_not(even))
        def _():
            stage(kt, s_b, s_a)

        return c

    lax.fori_loop(0, qi, body, 0)

    @pl.when((qi & 1) == 0)
    def _():
        for h in heads:
            softmax_accumulate(s_a, qi, True, h)

    @pl.when((qi & 1) == 1)
    def _():
        for h in heads:
            softmax_accumulate(s_b, qi, True, h)

    lam = _lambda(lq1, lk1, lq2, lk2)
    for h in range(HEADS_PER_STEP):
        acc = acc_sc[h]
        inv = 1.0 / l_sc[h]
        ot = acc[:, :tq] * inv[:, :tq] - lam * (acc[:, tq:] * inv[:, tq:])
        cols = slice(h * LANES, (h + 1) * LANES)
        o_ref[:, cols] = _subln_gate(ot.T, g_ref[...], za_ref[:, cols]).astype(BF16)


def _attn_prompt(lams, g, qt, kb, vt, za, *, n_seq, seq_len):
    tq = ATT_TILE
    nq = seq_len // tq
    hps = HEADS_PER_STEP
    assert tq % CHUNK == 0 and tq & (tq - 1) == 0 and ATT_HEADS % hps == 0
    rowspec = pl.BlockSpec((tq, hps * LANES), lambda b, h, i: (b * nq + i, h))
    qtspec = pl.BlockSpec((None, hps, None, LANES, tq), lambda b, h, i: (b, h, i, 0, 0))
    kspec = pl.BlockSpec((seq_len, hps * LANES), lambda b, h, i: (b, h))
    vtspec = pl.BlockSpec((None, hps, nq, LANES, tq), lambda b, h, i: (b, h, 0, 0, 0))
    vec = _full((1, QK_DIM))
    stat = pltpu.VMEM((hps, 1, 2 * tq), F32)
    return pl.pallas_call(
        _attn_kernel,
        grid=(n_seq, ATT_HEADS // hps, nq),
        in_specs=[vec, vec, vec, vec, _full((1, V_DIM)), qtspec, kspec, vtspec, rowspec],
        out_specs=rowspec,
        out_shape=jax.ShapeDtypeStruct((n_seq * seq_len, D_ATT), BF16),
        scratch_shapes=[stat, stat, pltpu.VMEM((hps, V_DIM, 2 * tq), F32),
                        pltpu.VMEM((hps, tq, 2 * tq), F32), pltpu.VMEM((hps, tq, 2 * tq), F32)],
        compiler_params=_params(("arbitrary", "arbitrary", "arbitrary")),
        name="attn_prompt",
    )(*lams, g, qt, kb, vt, za)


def _attn_sample_kernel(lq1, lk1, lq2, lk2, g_ref, q_ref, kn_ref, vn_ref, ck_ref, cv_ref, za_ref, o_ref, *, t, past):
    lam = _lambda(lq1, lk1, lq2, lk2)
    outs = []
    for h in range(ATT_HEADS):
        cols = slice(h * LANES, (h + 1) * LANES)
        q2 = _stack_maps(q_ref[:, cols])
        kpt = jnp.concatenate([ck_ref[h, 0], ck_ref[h, 1]], axis=0).astype(BF16)
        vp = cv_ref[pl.ds(h, past, stride=ATT_HEADS), :].astype(BF16)
        s_p = jnp.dot(q2, kpt, preferred_element_type=F32)
        s_n = lax.dot_general(q2, kn_ref[:, cols], _NT, preferred_element_type=F32)
        m = jnp.maximum(jnp.max(s_p, axis=1, keepdims=True), jnp.max(s_n, axis=1, keepdims=True))
        p_p = jnp.exp2(s_p - m)
        p_n = jnp.exp2(s_n - m)
        l = jnp.sum(p_p, axis=1, keepdims=True) + jnp.sum(p_n, axis=1, keepdims=True)
        acc = (jnp.dot(p_p.astype(BF16), vp, preferred_element_type=F32)
               + jnp.dot(p_n.astype(BF16), vn_ref[:, cols], preferred_element_type=F32))
        outs.append(_finish_head(acc, l, lam, g_ref[...], za_ref[:, cols], t))
    o_ref[...] = jnp.concatenate(outs, axis=1).astype(BF16)


def _attn_sample(lams, g, q, kb, vb, ck, cv, za, *, n_seq, t, past):
    assert past % CHUNK == 0 and t <= CHUNK
    row = pl.BlockSpec((t, D_ATT), lambda b: (b, 0))
    kcache = pl.BlockSpec((None, ATT_HEADS, 2, QK_DIM, past), lambda b: (b, 0, 0, 0, 0))
    vcache = pl.BlockSpec((None, past * ATT_HEADS, V_DIM), lambda b: (b, 0, 0))
    vec = _full((1, QK_DIM))
    return pl.pallas_call(
        functools.partial(_attn_sample_kernel, t=t, past=past),
        grid=(n_seq,),
        in_specs=[vec, vec, vec, vec, _full((1, V_DIM)), row, row, row, kcache, vcache, row],
        out_specs=row,
        out_shape=jax.ShapeDtypeStruct((n_seq * t, D_ATT), BF16),
        compiler_params=_params(("arbitrary",)),
        name="attn_sample",
    )(*lams, g, q, kb, vb, ck, cv, za)


def _out_proj_kernel(s_ref, a_ref, w_ref, x_ref, g_ref, y_ref):
    mix = (jnp.dot(s_ref[...], w_ref[:D_SSM, :], preferred_element_type=F32)
           + jnp.dot(a_ref[...], w_ref[D_SSM:, :], preferred_element_type=F32))
    ms = jnp.mean(mix * mix, axis=-1, keepdims=True)
    y_ref[...] = x_ref[...] + mix * lax.rsqrt(ms + EPS) * g_ref[...]


def _out_proj(ssm_out, att_out, w_bf, x2d, g, *, tm):
    n = x2d.shape[0]
    row = lambda i: (i, 0)
    half = pl.BlockSpec((tm, 512), row)
    full = pl.BlockSpec((tm, D_MODEL), row)
    return pl.pallas_call(
        _out_proj_kernel,
        grid=(n // tm,),
        in_specs=[half, half, _full((D_MODEL, D_MODEL)), full, _full((1, D_MODEL))],
        out_specs=full,
        out_shape=jax.ShapeDtypeStruct((n, D_MODEL), F32),
        compiler_params=_params(("arbitrary",)),
        name="out_proj",
    )(ssm_out, att_out, w_bf, x2d, g)


def _block_diag_weights(bbr, bbi, c_re, c_im):
    eye = jnp.eye(SSM_GROUPS // 2, dtype=F32)

    def b_side(m):
        m = m.reshape(2, SSM_GROUPS // 2, SSM_GROUP, SSM_STATE)
        return jnp.einsum('bgcp,gh->bgchp', m, eye).reshape(2, 256, N_CH // 2)

    def c_side(m):
        m = m.reshape(2, SSM_GROUPS // 2, SSM_GROUP, SSM_STATE)
        return jnp.einsum('bgcp,gh->bgphc', m, eye).reshape(2, N_CH // 2, 256)

    bdb = jnp.concatenate([b_side(bbr), b_side(bbi)], axis=2).astype(BF16)
    bdc = jnp.concatenate([c_side(c_re), -c_side(c_im)], axis=1).astype(BF16)
    return bdb, bdc


def kernel(x_prompt, x_sample, cache_k, cache_v, state_ssm_re, state_ssm_im, norm_pre_g, w_in, ssm_lambda_re,
           ssm_lambda_im, ssm_log_dt, ssm_b_re, ssm_b_im, ssm_c_re, ssm_c_im, ssm_d, glu_w1, glu_b1, glu_w2,
           glu_b2, lambda_q1, lambda_k1, lambda_q2, lambda_k2, attn_subln_g, w_out, norm_post_g):
    bp, sp, _ = x_prompt.shape
    bs, ss, _ = x_sample.shape
    past = cache_k.shape[2]

    a_re, a_im, bbr, bbi = _prep(ssm_lambda_re[0], ssm_lambda_im[0], ssm_log_dt[0], ssm_b_re[0], ssm_b_im[0])
    bdb, bdc = _block_diag_weights(bbr, bbi, ssm_c_re[0], ssm_c_im[0])
    a_re = a_re.reshape(1, N_CH)
    a_im = a_im.reshape(1, N_CH)
    dvec = ssm_d[0].reshape(1, D_SSM)
    w_in_bf = w_in[0].astype(BF16)
    w_out_bf = w_out[0].astype(BF16)
    w1 = glu_w1[0].astype(BF16)
    w2 = glu_w2[0].astype(BF16)
    b1 = glu_b1[0].reshape(1, D_SSM)
    b2 = glu_b2[0].reshape(1, D_SSM)
    g_pre = norm_pre_g[0].reshape(1, D_MODEL)
    g_post = norm_post_g[0].reshape(1, D_MODEL)
    g_sub = attn_subln_g[0].reshape(1, V_DIM)
    lams = tuple(v[0].reshape(1, QK_DIM) for v in (lambda_q1, lambda_k1, lambda_q2, lambda_k2))
    inv = ROPE_THETA ** (-jnp.arange(ROPE_DIM // 2, dtype=F32) * 2.0 / ROPE_DIM)
    rotary_lane = (jnp.arange(LANES) % QK_DIM) < ROPE_DIM
    inv_lane = jnp.where(rotary_lane, jnp.tile(inv, LANES // (ROPE_DIM // 2)), 0.0).reshape(1, LANES)

    def run(x, n_seq, seq_len, pos0, chained, h0r, h0i, tm):
        x2d = x.reshape(n_seq * seq_len, D_MODEL)
        u, zs, q, kf, kb, vf, vb, za = _in_proj(x2d, g_pre, w_in_bf, inv_lane, seq_len=seq_len, pos0=pos0, tm=tm,
                                                transposed_qv=chained)
        ssm_out, hfr, hfi = _s5(u, zs, bdb, bdc, dvec, a_re, a_im, w1, b1, w2, b2, h0r, h0i,
                                n_seq=n_seq, seq_len=seq_len, chained=chained)
        if chained:
            att = _attn_prompt(lams, g_sub, q, kb, vb, za, n_seq=n_seq, seq_len=seq_len)
            k_out = jnp.transpose(kf, (0, 4, 1, 2, 3))[None]
        else:
            ck = jnp.transpose(cache_k[0], (0, 2, 3, 4, 1))
            cv = cache_v[0].reshape(n_seq, past * ATT_HEADS, V_DIM)
            att = _attn_sample(lams, g_sub, q, kb, vb, ck, cv, za, n_seq=n_seq, t=seq_len, past=past)
            k_out = kf.reshape(1, n_seq, seq_len, ATT_HEADS, 2, QK_DIM)
        y = _out_proj(ssm_out, att, w_out_bf, x2d, g_post, tm=tm)
        return (y.reshape(n_seq, seq_len, D_MODEL),
                k_out,
                vf.reshape(1, n_seq, seq_len, ATT_HEADS, V_DIM),
                hfr.reshape(1, n_seq, SSM_GROUPS, SSM_STATE),
                hfi.reshape(1, n_seq, SSM_GROUPS, SSM_STATE))

    zero = jnp.zeros((1, N_CH), F32)
    yp, kp, vp, hrp, hip = run(x_prompt, bp, sp, 0, True, zero, zero, 512)
    h0r = state_ssm_re[0].reshape(bs, N_CH)
    h0i = state_ssm_im[0].reshape(bs, N_CH)
    ys, ks, vs, hrs, his = run(x_sample, bs, ss, past, False, h0r, h0i, bs * ss)
    return (yp, ys, kp, vp, hrp, hip, ks, vs, hrs, his)
```

```python
import functools
import math

import jax
import jax.numpy as jnp
from jax import lax
from jax.experimental import pallas as pl
from jax.experimental.pallas import tpu as pltpu

F32 = jnp.float32
BF16 = jnp.bfloat16

D_MODEL = 1024
D_SSM = 512
D_ATT = 512
SSM_GROUP = 16
SSM_GROUPS = 32
SSM_STATE = 64
N_CH = SSM_GROUPS * SSM_STATE
ATT_HEADS = 4
QK_DIM = 64
V_DIM = 128
ROPE_DIM = 16
ROPE_THETA = 500000.0
CHUNK = 64
EPS = 1e-6
D_IN = 3072
LAMBDA_INIT = 0.8 - 0.6 * math.exp(-0.3 * 0)

LANES = 128
SUBLANES = 8
N_SLAB = N_CH // LANES
OUT_CHUNK = 256
S5_STEPS = 256
SLAB_PAD = SUBLANES
NEG = -1e30
ATT_TILE = 256
HEADS_PER_STEP = 4
LOG2E = math.log2(math.e)
Q_SCALE = QK_DIM ** -0.5 * LOG2E
VMEM_LIMIT = 56 * 1024 * 1024


def _params(sem):
    return pltpu.CompilerParams(dimension_semantics=sem, vmem_limit_bytes=VMEM_LIMIT)


def _full(shape):
    n = len(shape)
    return pl.BlockSpec(shape, lambda *_: (0,) * n)


def _prep_kernel(lr_ref, li_ref, ldt_ref, br_ref, bi_ref, ar_ref, ai_ref, bbr_ref, bbi_ref):
    lr = lr_ref[...]
    li = li_ref[...]
    dt = jnp.exp(ldt_ref[...])
    mag = jnp.exp(lr * dt)
    ar = mag * jnp.cos(li * dt)
    ai = mag * jnp.sin(li * dt)
    den = lr * lr + li * li
    cr = ((ar - 1.0) * lr + ai * li) / den
    ci = (ai * lr - (ar - 1.0) * li) / den
    ar_ref[...] = ar
    ai_ref[...] = ai
    br = br_ref[...]
    bi = bi_ref[...]
    crb = cr[:, None, :]
    cib = ci[:, None, :]
    bbr_ref[...] = crb * br - cib * bi
    bbi_ref[...] = crb * bi + cib * br


def _prep(lam_re, lam_im, log_dt, b_re, b_im):
    g, p, c = b_re.shape
    brt = jnp.swapaxes(b_re, 1, 2)
    bit = jnp.swapaxes(b_im, 1, 2)
    out_shape = (jax.ShapeDtypeStruct((g, p), F32), jax.ShapeDtypeStruct((g, p), F32),
                 jax.ShapeDtypeStruct((g, c, p), F32), jax.ShapeDtypeStruct((g, c, p), F32))
    return pl.pallas_call(_prep_kernel, out_shape=out_shape, name="s5_prep")(
        lam_re, lam_im, log_dt.reshape(g, 1), brt, bit)


def _in_proj_kernel(x_ref, g_ref, w_ref, inv_ref, u_ref, zs_ref, q_ref, kf_ref, kb_ref, vf_ref,
                    vb_ref, za_ref, cl_sc, sl_sc, *, tm, seq_len, pos0, transposed_qv):
    i = pl.program_id(0)
    x = x_ref[...]
    ms = jnp.mean(x * x, axis=-1, keepdims=True)
    hn = (x * lax.rsqrt(ms + EPS) * g_ref[...]).astype(BF16)

    inv = inv_ref[...]

    @pl.when(i == 0)
    def _():
        off = (lax.broadcasted_iota(jnp.int32, (tm, LANES), 0) & (seq_len - 1)).astype(F32) * inv
        cl_sc[...] = jnp.cos(off)
        sl_sc[...] = jnp.sin(off)

    base = (pos0 + ((i * tm) & (seq_len - 1))).astype(F32) * jnp.broadcast_to(inv, (SUBLANES, LANES))
    cb = jnp.cos(base)[:1]
    sb = jnp.sin(base)[:1]
    cl = cl_sc[...]
    sl = sl_sc[...]
    c_m = cb * cl - sb * sl
    sin = sb * cl + cb * sl
    lane = lax.broadcasted_iota(jnp.int32, (tm, LANES), 1) & (QK_DIM - 1)
    half = ROPE_DIM // 2
    s_lo = jnp.where(lane < half, -sin, 0.0)
    s_hi = jnp.where(lane >= half, sin, 0.0)

    def seg(lo, hi):
        return jnp.dot(hn, w_ref[:, lo:hi], preferred_element_type=F32)

    def rope(t):
        outs = []
        for h in range(ATT_HEADS):
            th = t[:, h * LANES:(h + 1) * LANES]
            outs.append(th * c_m + pltpu.roll(th, LANES - half, 1) * s_lo + pltpu.roll(th, half, 1) * s_hi)
        return jnp.concatenate(outs, axis=1)

    def put(ref, t):
        if not transposed_qv:
            ref[...] = t.astype(BF16)
            return
        for h in range(ATT_HEADS):
            tt = t[:, h * LANES:(h + 1) * LANES].T.astype(BF16)
            for c in range(tm // ATT_TILE):
                ref[h, c] = tt[:, c * ATT_TILE:(c + 1) * ATT_TILE]

    u_ref[...] = seg(0, D_SSM).astype(BF16)
    zs_ref[...] = seg(D_SSM, 2 * D_SSM).astype(BF16)
    q = rope(seg(1024, 1536))
    put(q_ref, q * Q_SCALE)
    k = rope(seg(1536, 2048))
    if transposed_qv:
        for h in range(ATT_HEADS):
            kt = k[:, h * LANES:(h + 1) * LANES].T
            kf_ref[h, 0] = kt[:QK_DIM]
            kf_ref[h, 1] = kt[QK_DIM:]
    else:
        kf_ref[...] = k
    kb_ref[...] = k.astype(BF16)
    v = seg(2048, 2560)
    for h in range(ATT_HEADS):
        vf_ref[pl.ds(h, tm, stride=ATT_HEADS), :] = v[:, h * LANES:(h + 1) * LANES]
    put(vb_ref, v)
    za_ref[...] = seg(2560, 3072).astype(BF16)


def _in_proj(x2d, g, w_bf, inv_lane, *, seq_len, pos0, tm, transposed_qv):
    n = x2d.shape[0]
    assert n % tm == 0 and seq_len & (seq_len - 1) == 0 and (tm % seq_len == 0 or seq_len % tm == 0)
    row = lambda i: (i, 0)
    o512 = pl.BlockSpec((tm, 512), row)
    shp = lambda dt: jax.ShapeDtypeStruct((n, 512), dt)
    if transposed_qv:
        assert seq_len % tm == 0 and tm % ATT_TILE == 0
        tps = seq_len // tm
        per = tm // ATT_TILE
        t_spec = pl.BlockSpec((None, ATT_HEADS, per, LANES, ATT_TILE), lambda i: (i // tps, 0, i % tps, 0, 0))
        t_shape = jax.ShapeDtypeStruct((n // seq_len, ATT_HEADS, seq_len // ATT_TILE, LANES, ATT_TILE), BF16)
        kf_spec = pl.BlockSpec((None, ATT_HEADS, 2, QK_DIM, tm), lambda i: (i // tps, 0, 0, 0, i % tps))
        kf_shape = jax.ShapeDtypeStruct((n // seq_len, ATT_HEADS, 2, QK_DIM, seq_len), F32)
    else:
        t_spec, t_shape = o512, shp(BF16)
        kf_spec, kf_shape = o512, shp(F32)
    vf_spec = pl.BlockSpec((tm * ATT_HEADS, V_DIM), row)
    vf_shape = jax.ShapeDtypeStruct((n * ATT_HEADS, V_DIM), F32)
    return pl.pallas_call(
        functools.partial(_in_proj_kernel, tm=tm, seq_len=seq_len, pos0=pos0, transposed_qv=transposed_qv),
        grid=(n // tm,),
        in_specs=[pl.BlockSpec((tm, D_MODEL), row), _full((1, D_MODEL)), _full((D_MODEL, D_IN)),
                  _full((1, LANES))],
        out_specs=[o512, o512, t_spec, kf_spec, o512, vf_spec, t_spec, o512],
        out_shape=[shp(BF16), shp(BF16), t_shape, kf_shape, shp(BF16), vf_shape, t_shape, shp(BF16)],
        scratch_shapes=[pltpu.VMEM((tm, LANES), F32), pltpu.VMEM((tm, LANES), F32)],
        compiler_params=_params(("arbitrary",)),
        name="in_proj",
    )(x2d, g, w_bf, inv_lane)


def _cmul(ar, ai, br, bi):
    return ar * br - ai * bi, ar * bi + ai * br


def _s5_kernel(u_ref, zs_ref, bdb_ref, bdc_ref, d_ref, a_ref, w1_ref, b1_ref, w2_ref, b2_ref, h0_ref,
               out_ref, hf_ref, hbuf, hout, hb16, car, *, n_chain, T, chained, chain_group, unroll):
    n_rows = n_chain * T
    P = n_rows + SLAB_PAD
    half_cols = N_CH // 2
    u = u_ref[...].reshape(n_rows, D_SSM)

    if chained:
        @pl.when(pl.program_id(0) == 0)
        def _():
            car[...] = jnp.zeros_like(car)

    for b in range(2):
        bu = jnp.dot(u[:, b * 256:(b + 1) * 256], bdb_ref[b], preferred_element_type=F32)
        for part in range(2):
            for k in range(N_SLAB // 2):
                slab = part * N_SLAB + b * (N_SLAB // 2) + k
                col = part * half_cols + k * LANES
                hbuf[slab * P:slab * P + n_rows, :] = bu[:, col:col + LANES]

    src = car if chained else h0_ref
    blocks = range(N_SLAB // SUBLANES)
    a = [(a_ref[SUBLANES * k:SUBLANES * (k + 1), :], a_ref[N_SLAB + SUBLANES * k:N_SLAB + SUBLANES * (k + 1), :])
         for k in blocks]
    for g0 in range(0, n_chain, chain_group):
        chains = list(range(g0, g0 + chain_group))

        def body(t, carry, chains=chains):
            new = []
            for n, c in enumerate(chains):
                st = []
                for k in blocks:
                    hr, hi = carry[n][k]
                    rows_r = pl.ds(SUBLANES * k * P + c * T + t, SUBLANES, stride=P)
                    rows_i = pl.ds((N_SLAB + SUBLANES * k) * P + c * T + t, SUBLANES, stride=P)
                    pr, pi = _cmul(a[k][0], a[k][1], hr, hi)
                    nr = pr + hbuf[rows_r, :]
                    ni = pi + hbuf[rows_i, :]
                    hout[rows_r, :] = nr
                    hout[rows_i, :] = ni
                    st.append((nr, ni))
                new.append(tuple(st))
            return tuple(new)

        init = tuple(tuple((src[c, SUBLANES * k:SUBLANES * (k + 1), :],
                            src[c, N_SLAB + SUBLANES * k:N_SLAB + SUBLANES * (k + 1), :]) for k in blocks)
                     for c in chains)
        fin = lax.fori_loop(0, T, body, init, unroll=unroll)
        for n, c in enumerate(chains):
            for k in blocks:
                hr, hi = fin[n][k]
                hf_ref[c, SUBLANES * k:SUBLANES * (k + 1), :] = hr
                hf_ref[c, N_SLAB + SUBLANES * k:N_SLAB + SUBLANES * (k + 1), :] = hi
                if chained:
                    car[c, SUBLANES * k:SUBLANES * (k + 1), :] = hr
                    car[c, N_SLAB + SUBLANES * k:N_SLAB + SUBLANES * (k + 1), :] = hi

    for b in range(2):
        for part in range(2):
            for k in range(N_SLAB // 2):
                slab = part * N_SLAB + b * (N_SLAB // 2) + k
                col = b * N_CH + part * half_cols + k * LANES
                hb16[:, col:col + LANES] = hout[slab * P:slab * P + n_rows, :].astype(BF16)
    ys = [jnp.dot(hb16[:, b * N_CH:(b + 1) * N_CH], bdc_ref[b], preferred_element_type=F32) for b in range(2)]
    y = jnp.concatenate(ys, axis=1) + u.astype(F32) * d_ref[...]

    gb = jax.nn.gelu(y).astype(BF16)
    y1 = jnp.dot(gb, w1_ref[...], preferred_element_type=F32) + b1_ref[...]
    y2 = jnp.dot(gb, w2_ref[...], preferred_element_type=F32) + b2_ref[...]
    zs = zs_ref[...].reshape(n_rows, D_SSM).astype(F32)
    out_ref[...] = (y1 * jax.nn.sigmoid(y2) * jax.nn.silu(zs)).astype(BF16).reshape(n_chain, T, D_SSM)


def _s5(u, zs, bdb, bdc, dvec, a2, w1, b1, w2, b2, h0, *, n_seq, seq_len, chained):
    if chained:
        T = S5_STEPS
        chain_group, unroll = n_seq, 8
    else:
        T = seq_len
        chain_group, unroll = 4, 4
    assert seq_len % T == 0 and n_seq % chain_group == 0 and T % (2 * SUBLANES) == 0
    n_rows = n_seq * T
    u3 = u.reshape(n_seq, seq_len, D_SSM)
    zs3 = zs.reshape(n_seq, seq_len, D_SSM)
    blk = pl.BlockSpec((n_seq, T, D_SSM), lambda t: (0, t, 0))
    st_shape = (n_seq, 2 * N_SLAB, LANES)
    out, hf = pl.pallas_call(
        functools.partial(_s5_kernel, n_chain=n_seq, T=T, chained=chained, chain_group=chain_group, unroll=unroll),
        grid=(seq_len // T,),
        in_specs=[blk, blk, _full(bdb.shape), _full(bdc.shape), _full((1, D_SSM)), _full((2 * N_SLAB, LANES)),
                  _full((D_SSM, D_SSM)), _full((1, D_SSM)), _full((D_SSM, D_SSM)), _full((1, D_SSM)),
                  _full(st_shape)],
        out_specs=[blk, _full(st_shape)],
        out_shape=[jax.ShapeDtypeStruct((n_seq, seq_len, D_SSM), BF16), jax.ShapeDtypeStruct(st_shape, F32)],
        scratch_shapes=[pltpu.VMEM((2 * N_SLAB * (n_rows + SLAB_PAD), LANES), F32),
                        pltpu.VMEM((2 * N_SLAB * (n_rows + SLAB_PAD), LANES), F32),
                        pltpu.VMEM((n_rows, 2 * N_CH), BF16),
                        pltpu.VMEM(st_shape, F32)],
        compiler_params=_params(("arbitrary",)),
        name="s5_chained" if chained else "s5_independent",
    )(u3, zs3, bdb, bdc, dvec, a2, w1, b1, w2, b2, h0)
    return out.reshape(n_seq * seq_len, D_SSM), hf


def _lambda(lq1, lk1, lq2, lk2):
    s1 = jnp.sum(lq1[...] * lk1[...], axis=1, keepdims=True)
    s2 = jnp.sum(lq2[...] * lk2[...], axis=1, keepdims=True)
    return jnp.exp(s1) - jnp.exp(s2) + LAMBDA_INIT


def _stack_maps(q):
    lane = lax.broadcasted_iota(jnp.int32, q.shape, 1)
    zero = jnp.zeros_like(q)
    return jnp.concatenate([jnp.where(lane < QK_DIM, q, zero), jnp.where(lane >= QK_DIM, q, zero)], axis=0)


def _subln_gate(o, g, za):
    ms = jnp.mean(o * o, axis=-1, keepdims=True)
    on = (o * lax.rsqrt(ms + EPS) * g) * (1.0 - LAMBDA_INIT)
    return on * jax.nn.silu(za.astype(F32))


def _finish_head(acc, l, lam, g, za, t):
    inv = 1.0 / l
    o = acc[:t] * inv[:t] - lam * (acc[t:] * inv[t:])
    return _subln_gate(o, g, za)


_NT = (((1,), (1,)), ((), ()))


def _attn_kernel(lq1, lk1, lq2, lk2, g_ref, qt_ref, k_ref, vt_ref, za_ref, o_ref, m_sc, l_sc, acc_sc, s_a, s_b):
    tq = ATT_TILE
    qi = pl.program_id(2)
    q2t = []
    for h in range(HEADS_PER_STEP):
        qt = qt_ref[h]
        row = lax.broadcasted_iota(jnp.int32, qt.shape, 0)
        zero = jnp.zeros_like(qt)
        q2t.append(jnp.concatenate([jnp.where(row < QK_DIM, qt, zero), jnp.where(row >= QK_DIM, qt, zero)], axis=1))
    m_sc[...] = jnp.full_like(m_sc, NEG)
    l_sc[...] = jnp.zeros_like(l_sc)
    acc_sc[...] = jnp.zeros_like(acc_sc)
    heads = range(HEADS_PER_STEP)
    tk = 2 * tq
    n_full = qi // 2
    odd = (qi & 1) == 1

    def scores_into(s_ref, kt, h):
        start = pl.multiple_of(kt * tk, tk)
        s_ref[h] = jnp.dot(k_ref[pl.ds(start, tk), h * LANES:(h + 1) * LANES], q2t[h],
                           preferred_element_type=F32)

    def softmax_accumulate(s_ref, kt, nkeys, masked, h):
        s = s_ref[h] if nkeys == tk else s_ref[h, :nkeys, :]
        if masked:
            kc = lax.broadcasted_iota(jnp.int32, s.shape, 0) // CHUNK + kt * (tk // CHUNK)
            qc = (lax.broadcasted_iota(jnp.int32, s.shape, 1) & (tq - 1)) // CHUNK + qi * (tq // CHUNK)
            s = jnp.where(kc <= qc, s, NEG)
        m_old = m_sc[h]
        m_new = jnp.maximum(m_old, jnp.max(s, axis=0, keepdims=True))
        alpha = jnp.exp2(m_old - m_new)
        p = jnp.exp2(s - m_new)
        l_sc[h] = alpha * l_sc[h] + jnp.sum(p, axis=0, keepdims=True)
        pb = p.astype(BF16)
        pv = jnp.dot(vt_ref[h, 2 * kt], pb[:tq], preferred_element_type=F32)
        if nkeys == tk:
            pv = pv + jnp.dot(vt_ref[h, 2 * kt + 1], pb[tq:], preferred_element_type=F32)
        acc_sc[h] = alpha * acc_sc[h] + pv
        m_sc[h] = m_new

    def stage(kt, s_cur, s_nxt):
        for h in heads:
            scores_into(s_nxt, kt + 1, h)
            softmax_accumulate(s_cur, kt, tk, False, h)

    for h in heads:
        scores_into(s_a, 0, h)

    def body(kt, c):
        even = (kt & 1) == 0

        @pl.when(even)
        def _():
            stage(kt, s_a, s_b)

        @pl.when(jnp.logical_not(even))
        def _():
            stage(kt, s_b, s_a)

        return c

    lax.fori_loop(0, n_full, body, 0)

    for parity, s_ref in ((0, s_a), (1, s_b)):
        here = (n_full & 1) == parity

        @pl.when(here & odd)
        def _(s_ref=s_ref):
            for h in heads:
                softmax_accumulate(s_ref, n_full, tk, True, h)

        @pl.when(here & jnp.logical_not(odd))
        def _(s_ref=s_ref):
            for h in heads:
                softmax_accumulate(s_ref, n_full, tq, True, h)

    lam = _lambda(lq1, lk1, lq2, lk2)
    for h in range(HEADS_PER_STEP):
        acc = acc_sc[h]
        inv = 1.0 / l_sc[h]
        ot = acc[:, :tq] * inv[:, :tq] - lam * (acc[:, tq:] * inv[:, tq:])
        cols = slice(h * LANES, (h + 1) * LANES)
        o_ref[:, cols] = _subln_gate(ot.T, g_ref[...], za_ref[:, cols]).astype(BF16)


def _attn_prompt(lams, g, qt, kb, vt, za, *, n_seq, seq_len):
    tq = ATT_TILE
    nq = seq_len // tq
    hps = HEADS_PER_STEP
    assert tq % CHUNK == 0 and tq & (tq - 1) == 0 and ATT_HEADS % hps == 0 and seq_len % (2 * tq) == 0
    rowspec = pl.BlockSpec((tq, hps * LANES), lambda b, h, i: (b * nq + i, h))
    qtspec = pl.BlockSpec((None, hps, None, LANES, tq), lambda b, h, i: (b, h, i, 0, 0))
    kspec = pl.BlockSpec((seq_len, hps * LANES), lambda b, h, i: (b, h))
    vtspec = pl.BlockSpec((None, hps, nq, LANES, tq), lambda b, h, i: (b, h, 0, 0, 0))
    vec = _full((1, QK_DIM))
    stat = pltpu.VMEM((hps, 1, 2 * tq), F32)
    return pl.pallas_call(
        _attn_kernel,
        grid=(n_seq, ATT_HEADS // hps, nq),
        in_specs=[vec, vec, vec, vec, _full((1, V_DIM)), qtspec, kspec, vtspec, rowspec],
        out_specs=rowspec,
        out_shape=jax.ShapeDtypeStruct((n_seq * seq_len, D_ATT), BF16),
        scratch_shapes=[stat, stat, pltpu.VMEM((hps, V_DIM, 2 * tq), F32),
                        pltpu.VMEM((hps, 2 * tq, 2 * tq), F32), pltpu.VMEM((hps, 2 * tq, 2 * tq), F32)],
        compiler_params=_params(("arbitrary", "arbitrary", "arbitrary")),
        name="attn_prompt",
    )(*lams, g, qt, kb, vt, za)


def _attn_sample_kernel(lq1, lk1, lq2, lk2, g_ref, q_ref, kn_ref, vn_ref, ck_ref, cv_ref, za_ref, o_ref, *, t, past):
    lam = _lambda(lq1, lk1, lq2, lk2)
    outs = []
    for h in range(ATT_HEADS):
        cols = slice(h * LANES, (h + 1) * LANES)
        q2 = _stack_maps(q_ref[:, cols])
        kpt = jnp.concatenate([ck_ref[h, 0], ck_ref[h, 1]], axis=0).astype(BF16)
        vp = cv_ref[pl.ds(h, past, stride=ATT_HEADS), :].astype(BF16)
        s_p = jnp.dot(q2, kpt, preferred_element_type=F32)
        s_n = lax.dot_general(q2, kn_ref[:, cols], _NT, preferred_element_type=F32)
        m = jnp.maximum(jnp.max(s_p, axis=1, keepdims=True), jnp.max(s_n, axis=1, keepdims=True))
        p_p = jnp.exp2(s_p - m)
        p_n = jnp.exp2(s_n - m)
        l = jnp.sum(p_p, axis=1, keepdims=True) + jnp.sum(p_n, axis=1, keepdims=True)
        acc = (jnp.dot(p_p.astype(BF16), vp, preferred_element_type=F32)
               + jnp.dot(p_n.astype(BF16), vn_ref[:, cols], preferred_element_type=F32))
        outs.append(_finish_head(acc, l, lam, g_ref[...], za_ref[:, cols], t))
    o_ref[...] = jnp.concatenate(outs, axis=1).astype(BF16)


def _attn_sample(lams, g, q, kb, vb, ck, cv, za, *, n_seq, t, past):
    assert past % CHUNK == 0 and t <= CHUNK
    row = pl.BlockSpec((t, D_ATT), lambda b: (b, 0))
    kcache = pl.BlockSpec((None, ATT_HEADS, 2, QK_DIM, past), lambda b: (b, 0, 0, 0, 0))
    vcache = pl.BlockSpec((None, past * ATT_HEADS, V_DIM), lambda b: (b, 0, 0))
    vec = _full((1, QK_DIM))
    return pl.pallas_call(
        functools.partial(_attn_sample_kernel, t=t, past=past),
        grid=(n_seq,),
        in_specs=[vec, vec, vec, vec, _full((1, V_DIM)), row, row, row, kcache, vcache, row],
        out_specs=row,
        out_shape=jax.ShapeDtypeStruct((n_seq * t, D_ATT), BF16),
        compiler_params=_params(("arbitrary",)),
        name="attn_sample",
    )(*lams, g, q, kb, vb, ck, cv, za)


def _out_proj_kernel(s_ref, a_ref, w_ref, x_ref, g_ref, y_ref):
    tm = s_ref.shape[0]
    step = min(tm, OUT_CHUNK)
    for r in range(0, tm, step):
        rows = slice(r, r + step)
        mix = (jnp.dot(s_ref[rows, :], w_ref[:D_SSM, :], preferred_element_type=F32)
               + jnp.dot(a_ref[rows, :], w_ref[D_SSM:, :], preferred_element_type=F32))
        ms = jnp.mean(mix * mix, axis=-1, keepdims=True)
        y_ref[rows, :] = x_ref[rows, :] + mix * lax.rsqrt(ms + EPS) * g_ref[...]


def _out_proj(ssm_out, att_out, w_bf, x2d, g, *, tm):
    n = x2d.shape[0]
    assert n % tm == 0 and tm % min(tm, OUT_CHUNK) == 0
    row = lambda i: (i, 0)
    half = pl.BlockSpec((tm, 512), row)
    full = pl.BlockSpec((tm, D_MODEL), row)
    return pl.pallas_call(
        _out_proj_kernel,
        grid=(n // tm,),
        in_specs=[half, half, _full((D_MODEL, D_MODEL)), full, _full((1, D_MODEL))],
        out_specs=full,
        out_shape=jax.ShapeDtypeStruct((n, D_MODEL), F32),
        compiler_params=_params(("arbitrary",)),
        name="out_proj",
    )(ssm_out, att_out, w_bf, x2d, g)


def _block_diag_weights(bbr, bbi, c_re, c_im):
    eye = jnp.eye(SSM_GROUPS // 2, dtype=F32)

    def b_side(m):
        m = m.reshape(2, SSM_GROUPS // 2, SSM_GROUP, SSM_STATE)
        return jnp.einsum('bgcp,gh->bgchp', m, eye).reshape(2, 256, N_CH // 2)

    def c_side(m):
        m = m.reshape(2, SSM_GROUPS // 2, SSM_GROUP, SSM_STATE)
        return jnp.einsum('bgcp,gh->bgphc', m, eye).reshape(2, N_CH // 2, 256)

    bdb = jnp.concatenate([b_side(bbr), b_side(bbi)], axis=2).astype(BF16)
    bdc = jnp.concatenate([c_side(c_re), -c_side(c_im)], axis=1).astype(BF16)
    return bdb, bdc


def kernel(x_prompt, x_sample, cache_k, cache_v, state_ssm_re, state_ssm_im, norm_pre_g, w_in, ssm_lambda_re,
           ssm_lambda_im, ssm_log_dt, ssm_b_re, ssm_b_im, ssm_c_re, ssm_c_im, ssm_d, glu_w1, glu_b1, glu_w2,
           glu_b2, lambda_q1, lambda_k1, lambda_q2, lambda_k2, attn_subln_g, w_out, norm_post_g):
    bp, sp, _ = x_prompt.shape
    bs, ss, _ = x_sample.shape
    past = cache_k.shape[2]

    a_re, a_im, bbr, bbi = _prep(ssm_lambda_re[0], ssm_lambda_im[0], ssm_log_dt[0], ssm_b_re[0], ssm_b_im[0])
    bdb, bdc = _block_diag_weights(bbr, bbi, ssm_c_re[0], ssm_c_im[0])
    a2 = jnp.concatenate([a_re.reshape(N_SLAB, LANES), a_im.reshape(N_SLAB, LANES)], axis=0)
    dvec = ssm_d[0].reshape(1, D_SSM)
    w_in_bf = w_in[0].astype(BF16)
    w_out_bf = w_out[0].astype(BF16)
    w1 = glu_w1[0].astype(BF16)
    w2 = glu_w2[0].astype(BF16)
    b1 = glu_b1[0].reshape(1, D_SSM)
    b2 = glu_b2[0].reshape(1, D_SSM)
    g_pre = norm_pre_g[0].reshape(1, D_MODEL)
    g_post = norm_post_g[0].reshape(1, D_MODEL)
    g_sub = attn_subln_g[0].reshape(1, V_DIM)
    lams = tuple(v[0].reshape(1, QK_DIM) for v in (lambda_q1, lambda_k1, lambda_q2, lambda_k2))
    inv = ROPE_THETA ** (-jnp.arange(ROPE_DIM // 2, dtype=F32) * 2.0 / ROPE_DIM)
    rotary_lane = (jnp.arange(LANES) % QK_DIM) < ROPE_DIM
    inv_lane = jnp.where(rotary_lane, jnp.tile(inv, LANES // (ROPE_DIM // 2)), 0.0).reshape(1, LANES)

    def run(x, n_seq, seq_len, pos0, chained, h0, tm):
        x2d = x.reshape(n_seq * seq_len, D_MODEL)
        u, zs, q, kf, kb, vf, vb, za = _in_proj(x2d, g_pre, w_in_bf, inv_lane, seq_len=seq_len, pos0=pos0, tm=tm,
                                                transposed_qv=chained)
        ssm_out, hf = _s5(u, zs, bdb, bdc, dvec, a2, w1, b1, w2, b2, h0,
                          n_seq=n_seq, seq_len=seq_len, chained=chained)
        if chained:
            att = _attn_prompt(lams, g_sub, q, kb, vb, za, n_seq=n_seq, seq_len=seq_len)
            k_out = jnp.transpose(kf, (0, 4, 1, 2, 3))[None]
        else:
            ck = jnp.transpose(cache_k[0], (0, 2, 3, 4, 1))
            cv = cache_v[0].reshape(n_seq, past * ATT_HEADS, V_DIM)
            att = _attn_sample(lams, g_sub, q, kb, vb, ck, cv, za, n_seq=n_seq, t=seq_len, past=past)
            k_out = kf.reshape(1, n_seq, seq_len, ATT_HEADS, 2, QK_DIM)
        y = _out_proj(ssm_out, att, w_out_bf, x2d, g_post, tm=tm)
        return (y.reshape(n_seq, seq_len, D_MODEL),
                k_out,
                vf.reshape(1, n_seq, seq_len, ATT_HEADS, V_DIM),
                hf[:, :N_SLAB].reshape(1, n_seq, SSM_GROUPS, SSM_STATE),
                hf[:, N_SLAB:].reshape(1, n_seq, SSM_GROUPS, SSM_STATE))

    yp, kp, vp, hrp, hip = run(x_prompt, bp, sp, 0, True, jnp.zeros((bp, 2 * N_SLAB, LANES), F32), 512)
    h0 = jnp.concatenate([state_ssm_re[0].reshape(bs, N_SLAB, LANES), state_ssm_im[0].reshape(bs, N_SLAB, LANES)],
                         axis=1)
    ys, ks, vs, hrs, his = run(x_sample, bs, ss, past, False, h0, bs * ss)
    return (yp, ys, kp, vp, hrp, hip, ks, vs, hrs, his)
```

```python
import functools
import math

import jax
import jax.numpy as jnp
from jax import lax
from jax.experimental import pallas as pl
from jax.experimental.pallas import tpu as pltpu

F32 = jnp.float32
BF16 = jnp.bfloat16

D_MODEL = 1024
D_SSM = 512
D_ATT = 512
SSM_GROUP = 16
SSM_GROUPS = 32
SSM_STATE = 64
N_CH = SSM_GROUPS * SSM_STATE
ATT_HEADS = 4
QK_DIM = 64
V_DIM = 128
ROPE_DIM = 16
ROPE_THETA = 500000.0
CHUNK = 64
EPS = 1e-6
D_IN = 3072
LAMBDA_INIT = 0.8 - 0.6 * math.exp(-0.3 * 0)

LANES = 128
SUBLANES = 8
N_SLAB = N_CH // LANES
OUT_CHUNK = 256
S5_STEPS = 256
SLAB_PAD = 4
NEG = -1e30
ATT_TILE = 256
HEADS_PER_STEP = 4
LOG2E = math.log2(math.e)
Q_SCALE = QK_DIM ** -0.5 * LOG2E
VMEM_LIMIT = 56 * 1024 * 1024


def _params(sem):
    return pltpu.CompilerParams(dimension_semantics=sem, vmem_limit_bytes=VMEM_LIMIT)


def _full(shape):
    n = len(shape)
    return pl.BlockSpec(shape, lambda *_: (0,) * n)


def _prep_kernel(lr_ref, li_ref, ldt_ref, br_ref, bi_ref, ar_ref, ai_ref, bbr_ref, bbi_ref):
    lr = lr_ref[...]
    li = li_ref[...]
    dt = jnp.exp(ldt_ref[...])
    mag = jnp.exp(lr * dt)
    ar = mag * jnp.cos(li * dt)
    ai = mag * jnp.sin(li * dt)
    den = lr * lr + li * li
    cr = ((ar - 1.0) * lr + ai * li) / den
    ci = (ai * lr - (ar - 1.0) * li) / den
    ar_ref[...] = ar
    ai_ref[...] = ai
    br = br_ref[...]
    bi = bi_ref[...]
    crb = cr[:, None, :]
    cib = ci[:, None, :]
    bbr_ref[...] = crb * br - cib * bi
    bbi_ref[...] = crb * bi + cib * br


def _prep(lam_re, lam_im, log_dt, b_re, b_im):
    g, p, c = b_re.shape
    brt = jnp.swapaxes(b_re, 1, 2)
    bit = jnp.swapaxes(b_im, 1, 2)
    out_shape = (jax.ShapeDtypeStruct((g, p), F32), jax.ShapeDtypeStruct((g, p), F32),
                 jax.ShapeDtypeStruct((g, c, p), F32), jax.ShapeDtypeStruct((g, c, p), F32))
    return pl.pallas_call(_prep_kernel, out_shape=out_shape, name="s5_prep")(
        lam_re, lam_im, log_dt.reshape(g, 1), brt, bit)


def _in_proj_kernel(x_ref, g_ref, w_ref, inv_ref, u_ref, zs_ref, q_ref, kf_ref, kb_ref, vf_ref,
                    vb_ref, za_ref, cl_sc, sl_sc, *, tm, seq_len, pos0, transposed_qv):
    i = pl.program_id(0)
    x = x_ref[...]
    ms = jnp.mean(x * x, axis=-1, keepdims=True)
    hn = (x * lax.rsqrt(ms + EPS) * g_ref[...]).astype(BF16)

    inv = inv_ref[...]

    @pl.when(i == 0)
    def _():
        off = (lax.broadcasted_iota(jnp.int32, (tm, LANES), 0) & (seq_len - 1)).astype(F32) * inv
        cl_sc[...] = jnp.cos(off)
        sl_sc[...] = jnp.sin(off)

    base = (pos0 + ((i * tm) & (seq_len - 1))).astype(F32) * jnp.broadcast_to(inv, (SUBLANES, LANES))
    cb = jnp.cos(base)[:1]
    sb = jnp.sin(base)[:1]
    cl = cl_sc[...]
    sl = sl_sc[...]
    c_m = cb * cl - sb * sl
    sin = sb * cl + cb * sl
    lane = lax.broadcasted_iota(jnp.int32, (tm, LANES), 1) & (QK_DIM - 1)
    half = ROPE_DIM // 2
    s_lo = jnp.where(lane < half, -sin, 0.0)
    s_hi = jnp.where(lane >= half, sin, 0.0)

    def seg(lo, hi):
        return jnp.dot(hn, w_ref[:, lo:hi], preferred_element_type=F32)

    def rope(t):
        outs = []
        for h in range(ATT_HEADS):
            th = t[:, h * LANES:(h + 1) * LANES]
            outs.append(th * c_m + pltpu.roll(th, LANES - half, 1) * s_lo + pltpu.roll(th, half, 1) * s_hi)
        return jnp.concatenate(outs, axis=1)

    def put(ref, t):
        if not transposed_qv:
            ref[...] = t.astype(BF16)
            return
        for h in range(ATT_HEADS):
            tt = t[:, h * LANES:(h + 1) * LANES].T.astype(BF16)
            for c in range(tm // ATT_TILE):
                ref[h, c] = tt[:, c * ATT_TILE:(c + 1) * ATT_TILE]

    u_ref[...] = seg(0, D_SSM).astype(BF16)
    zs_ref[...] = seg(D_SSM, 2 * D_SSM).astype(BF16)
    q = rope(seg(1024, 1536))
    put(q_ref, q * Q_SCALE)
    k = rope(seg(1536, 2048))
    if transposed_qv:
        for h in range(ATT_HEADS):
            kt = k[:, h * LANES:(h + 1) * LANES].T
            kf_ref[h, 0] = kt[:QK_DIM]
            kf_ref[h, 1] = kt[QK_DIM:]
    else:
        kf_ref[...] = k
    kb_ref[...] = k.astype(BF16)
    v = seg(2048, 2560)
    for h in range(ATT_HEADS):
        vf_ref[pl.ds(h, tm, stride=ATT_HEADS), :] = v[:, h * LANES:(h + 1) * LANES]
    put(vb_ref, v)
    za_ref[...] = seg(2560, 3072).astype(BF16)


def _in_proj(x2d, g, w_bf, inv_lane, *, seq_len, pos0, tm, transposed_qv):
    n = x2d.shape[0]
    assert n % tm == 0 and seq_len & (seq_len - 1) == 0 and (tm % seq_len == 0 or seq_len % tm == 0)
    row = lambda i: (i, 0)
    o512 = pl.BlockSpec((tm, 512), row)
    shp = lambda dt: jax.ShapeDtypeStruct((n, 512), dt)
    if transposed_qv:
        assert seq_len % tm == 0 and tm % ATT_TILE == 0
        tps = seq_len // tm
        per = tm // ATT_TILE
        t_spec = pl.BlockSpec((None, ATT_HEADS, per, LANES, ATT_TILE), lambda i: (i // tps, 0, i % tps, 0, 0))
        t_shape = jax.ShapeDtypeStruct((n // seq_len, ATT_HEADS, seq_len // ATT_TILE, LANES, ATT_TILE), BF16)
        kf_spec = pl.BlockSpec((None, ATT_HEADS, 2, QK_DIM, tm), lambda i: (i // tps, 0, 0, 0, i % tps))
        kf_shape = jax.ShapeDtypeStruct((n // seq_len, ATT_HEADS, 2, QK_DIM, seq_len), F32)
    else:
        t_spec, t_shape = o512, shp(BF16)
        kf_spec, kf_shape = o512, shp(F32)
    vf_spec = pl.BlockSpec((tm * ATT_HEADS, V_DIM), row)
    vf_shape = jax.ShapeDtypeStruct((n * ATT_HEADS, V_DIM), F32)
    return pl.pallas_call(
        functools.partial(_in_proj_kernel, tm=tm, seq_len=seq_len, pos0=pos0, transposed_qv=transposed_qv),
        grid=(n // tm,),
        in_specs=[pl.BlockSpec((tm, D_MODEL), row), _full((1, D_MODEL)), _full((D_MODEL, D_IN)),
                  _full((1, LANES))],
        out_specs=[o512, o512, t_spec, kf_spec, o512, vf_spec, t_spec, o512],
        out_shape=[shp(BF16), shp(BF16), t_shape, kf_shape, shp(BF16), vf_shape, t_shape, shp(BF16)],
        scratch_shapes=[pltpu.VMEM((tm, LANES), F32), pltpu.VMEM((tm, LANES), F32)],
        compiler_params=_params(("arbitrary",)),
        name="in_proj",
    )(x2d, g, w_bf, inv_lane)


def _cmul(ar, ai, br, bi):
    return ar * br - ai * bi, ar * bi + ai * br


def _s5_kernel(u_ref, zs_ref, bdb_ref, bdc_ref, d_ref, a_ref, w1_ref, b1_ref, w2_ref, b2_ref, h0_ref,
               out_ref, hf_ref, hbuf, hout, hb16, car, *, n_chain, T, chained, chain_group, unroll):
    n_rows = n_chain * T
    P = n_rows + SLAB_PAD
    half_cols = N_CH // 2
    u = u_ref[...].reshape(n_rows, D_SSM)

    if chained:
        @pl.when(pl.program_id(0) == 0)
        def _():
            car[...] = jnp.zeros_like(car)

    for b in range(2):
        bu = jnp.dot(u[:, b * 256:(b + 1) * 256], bdb_ref[b], preferred_element_type=F32)
        for part in range(2):
            for k in range(N_SLAB // 2):
                slab = part * N_SLAB + b * (N_SLAB // 2) + k
                col = part * half_cols + k * LANES
                hbuf[slab * P:slab * P + n_rows, :] = bu[:, col:col + LANES]

    src = car if chained else h0_ref
    blocks = range(N_SLAB // SUBLANES)
    a = [(a_ref[SUBLANES * k:SUBLANES * (k + 1), :], a_ref[N_SLAB + SUBLANES * k:N_SLAB + SUBLANES * (k + 1), :])
         for k in blocks]
    for g0 in range(0, n_chain, chain_group):
        chains = list(range(g0, g0 + chain_group))

        def body(t, carry, chains=chains):
            new = []
            for n, c in enumerate(chains):
                st = []
                for k in blocks:
                    hr, hi = carry[n][k]
                    rows_r = pl.ds(SUBLANES * k * P + c * T + t, SUBLANES, stride=P)
                    rows_i = pl.ds((N_SLAB + SUBLANES * k) * P + c * T + t, SUBLANES, stride=P)
                    pr, pi = _cmul(a[k][0], a[k][1], hr, hi)
                    nr = pr + hbuf[rows_r, :]
                    ni = pi + hbuf[rows_i, :]
                    hout[rows_r, :] = nr
                    hout[rows_i, :] = ni
                    st.append((nr, ni))
                new.append(tuple(st))
            return tuple(new)

        init = tuple(tuple((src[c, SUBLANES * k:SUBLANES * (k + 1), :],
                            src[c, N_SLAB + SUBLANES * k:N_SLAB + SUBLANES * (k + 1), :]) for k in blocks)
                     for c in chains)
        fin = lax.fori_loop(0, T, body, init, unroll=unroll)
        for n, c in enumerate(chains):
            for k in blocks:
                hr, hi = fin[n][k]
                hf_ref[c, SUBLANES * k:SUBLANES * (k + 1), :] = hr
                hf_ref[c, N_SLAB + SUBLANES * k:N_SLAB + SUBLANES * (k + 1), :] = hi
                if chained:
                    car[c, SUBLANES * k:SUBLANES * (k + 1), :] = hr
                    car[c, N_SLAB + SUBLANES * k:N_SLAB + SUBLANES * (k + 1), :] = hi

    for b in range(2):
        for part in range(2):
            for k in range(N_SLAB // 2):
                slab = part * N_SLAB + b * (N_SLAB // 2) + k
                col = b * N_CH + part * half_cols + k * LANES
                hb16[:, col:col + LANES] = hout[slab * P:slab * P + n_rows, :].astype(BF16)
    ys = [jnp.dot(hb16[:, b * N_CH:(b + 1) * N_CH], bdc_ref[b], preferred_element_type=F32) for b in range(2)]
    y = jnp.concatenate(ys, axis=1) + u.astype(F32) * d_ref[...]

    gb = jax.nn.gelu(y).astype(BF16)
    y1 = jnp.dot(gb, w1_ref[...], preferred_element_type=F32) + b1_ref[...]
    y2 = jnp.dot(gb, w2_ref[...], preferred_element_type=F32) + b2_ref[...]
    zs = zs_ref[...].reshape(n_rows, D_SSM).astype(F32)
    out_ref[...] = (y1 * jax.nn.sigmoid(y2) * jax.nn.silu(zs)).astype(BF16).reshape(n_chain, T, D_SSM)


def _s5(u, zs, bdb, bdc, dvec, a2, w1, b1, w2, b2, h0, *, n_seq, seq_len, chained):
    if chained:
        T = S5_STEPS
        chain_group, unroll = n_seq, 8
    else:
        T = seq_len
        chain_group, unroll = 4, 4
    assert seq_len % T == 0 and n_seq % chain_group == 0 and T % (2 * SUBLANES) == 0
    n_rows = n_seq * T
    u3 = u.reshape(n_seq, seq_len, D_SSM)
    zs3 = zs.reshape(n_seq, seq_len, D_SSM)
    blk = pl.BlockSpec((n_seq, T, D_SSM), lambda t: (0, t, 0))
    st_shape = (n_seq, 2 * N_SLAB, LANES)
    out, hf = pl.pallas_call(
        functools.partial(_s5_kernel, n_chain=n_seq, T=T, chained=chained, chain_group=chain_group, unroll=unroll),
        grid=(seq_len // T,),
        in_specs=[blk, blk, _full(bdb.shape), _full(bdc.shape), _full((1, D_SSM)), _full((2 * N_SLAB, LANES)),
                  _full((D_SSM, D_SSM)), _full((1, D_SSM)), _full((D_SSM, D_SSM)), _full((1, D_SSM)),
                  _full(st_shape)],
        out_specs=[blk, _full(st_shape)],
        out_shape=[jax.ShapeDtypeStruct((n_seq, seq_len, D_SSM), BF16), jax.ShapeDtypeStruct(st_shape, F32)],
        scratch_shapes=[pltpu.VMEM((2 * N_SLAB * (n_rows + SLAB_PAD), LANES), F32),
                        pltpu.VMEM((2 * N_SLAB * (n_rows + SLAB_PAD), LANES), F32),
                        pltpu.VMEM((n_rows, 2 * N_CH), BF16),
                        pltpu.VMEM(st_shape, F32)],
        compiler_params=_params(("arbitrary",)),
        name="s5_chained" if chained else "s5_independent",
    )(u3, zs3, bdb, bdc, dvec, a2, w1, b1, w2, b2, h0)
    return out.reshape(n_seq * seq_len, D_SSM), hf


def _lambda(lq1, lk1, lq2, lk2):
    s1 = jnp.sum(lq1[...] * lk1[...], axis=1, keepdims=True)
    s2 = jnp.sum(lq2[...] * lk2[...], axis=1, keepdims=True)
    return jnp.exp(s1) - jnp.exp(s2) + LAMBDA_INIT


def _stack_maps(q):
    lane = lax.broadcasted_iota(jnp.int32, q.shape, 1)
    zero = jnp.zeros_like(q)
    return jnp.concatenate([jnp.where(lane < QK_DIM, q, zero), jnp.where(lane >= QK_DIM, q, zero)], axis=0)


def _subln_gate(o, g, za):
    ms = jnp.mean(o * o, axis=-1, keepdims=True)
    on = (o * lax.rsqrt(ms + EPS) * g) * (1.0 - LAMBDA_INIT)
    return on * jax.nn.silu(za.astype(F32))


def _finish_head(acc, l, lam, g, za, t):
    inv = 1.0 / l
    o = acc[:t] * inv[:t] - lam * (acc[t:] * inv[t:])
    return _subln_gate(o, g, za)


_NT = (((1,), (1,)), ((), ()))


def _attn_kernel(lq1, lk1, lq2, lk2, g_ref, qt_ref, k_ref, vt_ref, za_ref, o_ref, m_sc, l_sc, acc_sc, s_a, s_b):
    tq = ATT_TILE
    qi = pl.program_id(2)
    q2t = []
    for h in range(HEADS_PER_STEP):
        qt = qt_ref[h]
        row = lax.broadcasted_iota(jnp.int32, qt.shape, 0)
        zero = jnp.zeros_like(qt)
        q2t.append(jnp.concatenate([jnp.where(row < QK_DIM, qt, zero), jnp.where(row >= QK_DIM, qt, zero)], axis=1))
    m_sc[...] = jnp.full_like(m_sc, NEG)
    l_sc[...] = jnp.zeros_like(l_sc)
    acc_sc[...] = jnp.zeros_like(acc_sc)
    heads = range(HEADS_PER_STEP)
    tk = 2 * tq
    n_full = qi // 2
    odd = (qi & 1) == 1

    def scores_into(s_ref, kt, h):
        start = pl.multiple_of(kt * tk, tk)
        s_ref[h] = jnp.dot(k_ref[pl.ds(start, tk), h * LANES:(h + 1) * LANES], q2t[h],
                           preferred_element_type=F32)

    def softmax_accumulate(s_ref, kt, nkeys, masked, h):
        s = s_ref[h] if nkeys == tk else s_ref[h, :nkeys, :]
        if masked:
            kc = lax.broadcasted_iota(jnp.int32, s.shape, 0) // CHUNK + kt * (tk // CHUNK)
            qc = (lax.broadcasted_iota(jnp.int32, s.shape, 1) & (tq - 1)) // CHUNK + qi * (tq // CHUNK)
            s = jnp.where(kc <= qc, s, NEG)
        m_old = m_sc[h]
        m_new = jnp.maximum(m_old, jnp.max(s, axis=0, keepdims=True))
        alpha = jnp.exp2(m_old - m_new)
        p = jnp.exp2(s - m_new)
        l_sc[h] = alpha * l_sc[h] + jnp.sum(p, axis=0, keepdims=True)
        pb = p.astype(BF16)
        pv = jnp.dot(vt_ref[h, 2 * kt], pb[:tq], preferred_element_type=F32)
        if nkeys == tk:
            pv = pv + jnp.dot(vt_ref[h, 2 * kt + 1], pb[tq:], preferred_element_type=F32)
        acc_sc[h] = alpha * acc_sc[h] + pv
        m_sc[h] = m_new

    def stage(kt, s_cur, s_nxt):
        for h in heads:
            scores_into(s_nxt, kt + 1, h)
            softmax_accumulate(s_cur, kt, tk, False, h)

    for h in heads:
        scores_into(s_a, 0, h)

    def body(kt, c):
        even = (kt & 1) == 0

        @pl.when(even)
        def _():
            stage(kt, s_a, s_b)

        @pl.when(jnp.logical_not(even))
        def _():
            stage(kt, s_b, s_a)

        return c

    lax.fori_loop(0, n_full, body, 0)

    for parity, s_ref in ((0, s_a), (1, s_b)):
        here = (n_full & 1) == parity

        @pl.when(here & odd)
        def _(s_ref=s_ref):
            for h in heads:
                softmax_accumulate(s_ref, n_full, tk, True, h)

        @pl.when(here & jnp.logical_not(odd))
        def _(s_ref=s_ref):
            for h in heads:
                softmax_accumulate(s_ref, n_full, tq, True, h)

    lam = _lambda(lq1, lk1, lq2, lk2)
    for h in range(HEADS_PER_STEP):
        acc = acc_sc[h]
        inv = 1.0 / l_sc[h]
        ot = acc[:, :tq] * inv[:, :tq] - lam * (acc[:, tq:] * inv[:, tq:])
        cols = slice(h * LANES, (h + 1) * LANES)
        o_ref[:, cols] = _subln_gate(ot.T, g_ref[...], za_ref[:, cols]).astype(BF16)


def _attn_prompt(lams, g, qt, kb, vt, za, *, n_seq, seq_len):
    tq = ATT_TILE
    nq = seq_len // tq
    hps = HEADS_PER_STEP
    assert tq % CHUNK == 0 and tq & (tq - 1) == 0 and ATT_HEADS % hps == 0 and seq_len % (2 * tq) == 0
    rowspec = pl.BlockSpec((tq, hps * LANES), lambda b, h, i: (b * nq + i, h))
    qtspec = pl.BlockSpec((None, hps, None, LANES, tq), lambda b, h, i: (b, h, i, 0, 0))
    kspec = pl.BlockSpec((seq_len, hps * LANES), lambda b, h, i: (b, h))
    vtspec = pl.BlockSpec((None, hps, nq, LANES, tq), lambda b, h, i: (b, h, 0, 0, 0))
    vec = _full((1, QK_DIM))
    stat = pltpu.VMEM((hps, 1, 2 * tq), F32)
    return pl.pallas_call(
        _attn_kernel,
        grid=(n_seq, ATT_HEADS // hps, nq),
        in_specs=[vec, vec, vec, vec, _full((1, V_DIM)), qtspec, kspec, vtspec, rowspec],
        out_specs=rowspec,
        out_shape=jax.ShapeDtypeStruct((n_seq * seq_len, D_ATT), BF16),
        scratch_shapes=[stat, stat, pltpu.VMEM((hps, V_DIM, 2 * tq), F32),
                        pltpu.VMEM((hps, 2 * tq, 2 * tq), F32), pltpu.VMEM((hps, 2 * tq, 2 * tq), F32)],
        compiler_params=_params(("arbitrary", "arbitrary", "arbitrary")),
        name="attn_prompt",
    )(*lams, g, qt, kb, vt, za)


def _attn_sample_kernel(lq1, lk1, lq2, lk2, g_ref, q_ref, kn_ref, vn_ref, ck_ref, cv_ref, za_ref, o_ref, *, t, past):
    lam = _lambda(lq1, lk1, lq2, lk2)
    outs = []
    for h in range(ATT_HEADS):
        cols = slice(h * LANES, (h + 1) * LANES)
        q2 = _stack_maps(q_ref[:, cols])
        kpt = jnp.concatenate([ck_ref[h, 0], ck_ref[h, 1]], axis=0).astype(BF16)
        vp = cv_ref[pl.ds(h, past, stride=ATT_HEADS), :].astype(BF16)
        s_p = jnp.dot(q2, kpt, preferred_element_type=F32)
        s_n = lax.dot_general(q2, kn_ref[:, cols], _NT, preferred_element_type=F32)
        m = jnp.maximum(jnp.max(s_p, axis=1, keepdims=True), jnp.max(s_n, axis=1, keepdims=True))
        p_p = jnp.exp2(s_p - m)
        p_n = jnp.exp2(s_n - m)
        l = jnp.sum(p_p, axis=1, keepdims=True) + jnp.sum(p_n, axis=1, keepdims=True)
        acc = (jnp.dot(p_p.astype(BF16), vp, preferred_element_type=F32)
               + jnp.dot(p_n.astype(BF16), vn_ref[:, cols], preferred_element_type=F32))
        outs.append(_finish_head(acc, l, lam, g_ref[...], za_ref[:, cols], t))
    o_ref[...] = jnp.concatenate(outs, axis=1).astype(BF16)


def _attn_sample(lams, g, q, kb, vb, ck, cv, za, *, n_seq, t, past):
    assert past % CHUNK == 0 and t <= CHUNK
    row = pl.BlockSpec((t, D_ATT), lambda b: (b, 0))
    kcache = pl.BlockSpec((None, ATT_HEADS, 2, QK_DIM, past), lambda b: (b, 0, 0, 0, 0))
    vcache = pl.BlockSpec((None, past * ATT_HEADS, V_DIM), lambda b: (b, 0, 0))
    vec = _full((1, QK_DIM))
    return pl.pallas_call(
        functools.partial(_attn_sample_kernel, t=t, past=past),
        grid=(n_seq,),
        in_specs=[vec, vec, vec, vec, _full((1, V_DIM)), row, row, row, kcache, vcache, row],
        out_specs=row,
        out_shape=jax.ShapeDtypeStruct((n_seq * t, D_ATT), BF16),
        compiler_params=_params(("arbitrary",)),
        name="attn_sample",
    )(*lams, g, q, kb, vb, ck, cv, za)


def _out_proj_kernel(s_ref, a_ref, w_ref, x_ref, g_ref, y_ref):
    tm = s_ref.shape[0]
    step = min(tm, OUT_CHUNK)
    for r in range(0, tm, step):
        rows = slice(r, r + step)
        mix = (jnp.dot(s_ref[rows, :], w_ref[:D_SSM, :], preferred_element_type=F32)
               + jnp.dot(a_ref[rows, :], w_ref[D_SSM:, :], preferred_element_type=F32))
        ms = jnp.mean(mix * mix, axis=-1, keepdims=True)
        y_ref[rows, :] = x_ref[rows, :] + mix * lax.rsqrt(ms + EPS) * g_ref[...]


def _out_proj(ssm_out, att_out, w_bf, x2d, g, *, tm):
    n = x2d.shape[0]
    assert n % tm == 0 and tm % min(tm, OUT_CHUNK) == 0
    row = lambda i: (i, 0)
    half = pl.BlockSpec((tm, 512), row)
    full = pl.BlockSpec((tm, D_MODEL), row)
    return pl.pallas_call(
        _out_proj_kernel,
        grid=(n // tm,),
        in_specs=[half, half, _full((D_MODEL, D_MODEL)), full, _full((1, D_MODEL))],
        out_specs=full,
        out_shape=jax.ShapeDtypeStruct((n, D_MODEL), F32),
        compiler_params=_params(("arbitrary",)),
        name="out_proj",
    )(ssm_out, att_out, w_bf, x2d, g)


def _block_diag_weights(bbr, bbi, c_re, c_im):
    eye = jnp.eye(SSM_GROUPS // 2, dtype=F32)

    def b_side(m):
        m = m.reshape(2, SSM_GROUPS // 2, SSM_GROUP, SSM_STATE)
        return jnp.einsum('bgcp,gh->bgchp', m, eye).reshape(2, 256, N_CH // 2)

    def c_side(m):
        m = m.reshape(2, SSM_GROUPS // 2, SSM_GROUP, SSM_STATE)
        return jnp.einsum('bgcp,gh->bgphc', m, eye).reshape(2, N_CH // 2, 256)

    bdb = jnp.concatenate([b_side(bbr), b_side(bbi)], axis=2).astype(BF16)
    bdc = jnp.concatenate([c_side(c_re), -c_side(c_im)], axis=1).astype(BF16)
    return bdb, bdc


def kernel(x_prompt, x_sample, cache_k, cache_v, state_ssm_re, state_ssm_im, norm_pre_g, w_in, ssm_lambda_re,
           ssm_lambda_im, ssm_log_dt, ssm_b_re, ssm_b_im, ssm_c_re, ssm_c_im, ssm_d, glu_w1, glu_b1, glu_w2,
           glu_b2, lambda_q1, lambda_k1, lambda_q2, lambda_k2, attn_subln_g, w_out, norm_post_g):
    bp, sp, _ = x_prompt.shape
    bs, ss, _ = x_sample.shape
    past = cache_k.shape[2]

    a_re, a_im, bbr, bbi = _prep(ssm_lambda_re[0], ssm_lambda_im[0], ssm_log_dt[0], ssm_b_re[0], ssm_b_im[0])
    bdb, bdc = _block_diag_weights(bbr, bbi, ssm_c_re[0], ssm_c_im[0])
    a2 = jnp.concatenate([a_re.reshape(N_SLAB, LANES), a_im.reshape(N_SLAB, LANES)], axis=0)
    dvec = ssm_d[0].reshape(1, D_SSM)
    w_in_bf = w_in[0].astype(BF16)
    w_out_bf = w_out[0].astype(BF16)
    w1 = glu_w1[0].astype(BF16)
    w2 = glu_w2[0].astype(BF16)
    b1 = glu_b1[0].reshape(1, D_SSM)
    b2 = glu_b2[0].reshape(1, D_SSM)
    g_pre = norm_pre_g[0].reshape(1, D_MODEL)
    g_post = norm_post_g[0].reshape(1, D_MODEL)
    g_sub = attn_subln_g[0].reshape(1, V_DIM)
    lams = tuple(v[0].reshape(1, QK_DIM) for v in (lambda_q1, lambda_k1, lambda_q2, lambda_k2))
    inv = ROPE_THETA ** (-jnp.arange(ROPE_DIM // 2, dtype=F32) * 2.0 / ROPE_DIM)
    rotary_lane = (jnp.arange(LANES) % QK_DIM) < ROPE_DIM
    inv_lane = jnp.where(rotary_lane, jnp.tile(inv, LANES // (ROPE_DIM // 2)), 0.0).reshape(1, LANES)

    def run(x, n_seq, seq_len, pos0, chained, h0, tm):
        x2d = x.reshape(n_seq * seq_len, D_MODEL)
        u, zs, q, kf, kb, vf, vb, za = _in_proj(x2d, g_pre, w_in_bf, inv_lane, seq_len=seq_len, pos0=pos0, tm=tm,
                                                transposed_qv=chained)
        ssm_out, hf = _s5(u, zs, bdb, bdc, dvec, a2, w1, b1, w2, b2, h0,
                          n_seq=n_seq, seq_len=seq_len, chained=chained)
        if chained:
            att = _attn_prompt(lams, g_sub, q, kb, vb, za, n_seq=n_seq, seq_len=seq_len)
            k_out = jnp.transpose(kf, (0, 4, 1, 2, 3))[None]
        else:
            ck = jnp.transpose(cache_k[0], (0, 2, 3, 4, 1))
            cv = cache_v[0].reshape(n_seq, past * ATT_HEADS, V_DIM)
            att = _attn_sample(lams, g_sub, q, kb, vb, ck, cv, za, n_seq=n_seq, t=seq_len, past=past)
            k_out = kf.reshape(1, n_seq, seq_len, ATT_HEADS, 2, QK_DIM)
        y = _out_proj(ssm_out, att, w_out_bf, x2d, g_post, tm=tm)
        return (y.reshape(n_seq, seq_len, D_MODEL),
                k_out,
                vf.reshape(1, n_seq, seq_len, ATT_HEADS, V_DIM),
                hf[:, :N_SLAB].reshape(1, n_seq, SSM_GROUPS, SSM_STATE),
                hf[:, N_SLAB:].reshape(1, n_seq, SSM_GROUPS, SSM_STATE))

    yp, kp, vp, hrp, hip = run(x_prompt, bp, sp, 0, True, jnp.zeros((bp, 2 * N_SLAB, LANES), F32), 512)
    h0 = jnp.concatenate([state_ssm_re[0].reshape(bs, N_SLAB, LANES), state_ssm_im[0].reshape(bs, N_SLAB, LANES)],
                         axis=1)
    ys, ks, vs, hrs, his = run(x_sample, bs, ss, past, False, h0, bs * ss)
    return (yp, ys, kp, vp, hrp, hip, ks, vs, hrs, his)
```

```python
import functools
import math

import jax
import jax.numpy as jnp
from jax import lax
from jax.experimental import pallas as pl
from jax.experimental.pallas import tpu as pltpu

F32 = jnp.float32
BF16 = jnp.bfloat16

D_MODEL = 1024
D_SSM = 512
D_ATT = 512
SSM_GROUP = 16
SSM_GROUPS = 32
SSM_STATE = 64
N_CH = SSM_GROUPS * SSM_STATE
ATT_HEADS = 4
QK_DIM = 64
V_DIM = 128
ROPE_DIM = 16
ROPE_THETA = 500000.0
CHUNK = 64
EPS = 1e-6
D_IN = 3072
LAMBDA_INIT = 0.8 - 0.6 * math.exp(-0.3 * 0)

LANES = 128
SUBLANES = 8
N_SLAB = N_CH // LANES
OUT_CHUNK = 256
S5_STEPS = 256
SLAB_PAD = 4
NEG = -1e30
ATT_TILE = 256
HEADS_PER_STEP = 4
LOG2E = math.log2(math.e)
Q_SCALE = QK_DIM ** -0.5 * LOG2E
VMEM_LIMIT = 56 * 1024 * 1024


def _params(sem):
    return pltpu.CompilerParams(dimension_semantics=sem, vmem_limit_bytes=VMEM_LIMIT)


def _full(shape):
    n = len(shape)
    return pl.BlockSpec(shape, lambda *_: (0,) * n)


def _prep_kernel(lr_ref, li_ref, ldt_ref, br_ref, bi_ref, ar_ref, ai_ref, bbr_ref, bbi_ref):
    lr = lr_ref[...]
    li = li_ref[...]
    dt = jnp.exp(ldt_ref[...])
    mag = jnp.exp(lr * dt)
    ar = mag * jnp.cos(li * dt)
    ai = mag * jnp.sin(li * dt)
    den = lr * lr + li * li
    cr = ((ar - 1.0) * lr + ai * li) / den
    ci = (ai * lr - (ar - 1.0) * li) / den
    ar_ref[...] = ar
    ai_ref[...] = ai
    br = br_ref[...]
    bi = bi_ref[...]
    crb = cr[:, None, :]
    cib = ci[:, None, :]
    bbr_ref[...] = crb * br - cib * bi
    bbi_ref[...] = crb * bi + cib * br


def _prep(lam_re, lam_im, log_dt, b_re, b_im):
    g, p, c = b_re.shape
    brt = jnp.swapaxes(b_re, 1, 2)
    bit = jnp.swapaxes(b_im, 1, 2)
    out_shape = (jax.ShapeDtypeStruct((g, p), F32), jax.ShapeDtypeStruct((g, p), F32),
                 jax.ShapeDtypeStruct((g, c, p), F32), jax.ShapeDtypeStruct((g, c, p), F32))
    return pl.pallas_call(_prep_kernel, out_shape=out_shape, name="s5_prep")(
        lam_re, lam_im, log_dt.reshape(g, 1), brt, bit)


def _in_proj_kernel(x_ref, g_ref, w_ref, inv_ref, u_ref, zs_ref, q_ref, kf_ref, kb_ref, vf_ref,
                    vb_ref, za_ref, cl_sc, sl_sc, *, tm, seq_len, pos0, transposed_qv):
    i = pl.program_id(0)
    x = x_ref[...]
    ms = jnp.mean(x * x, axis=-1, keepdims=True)
    hn = (x * lax.rsqrt(ms + EPS) * g_ref[...]).astype(BF16)

    inv = inv_ref[...]

    @pl.when(i == 0)
    def _():
        off = (lax.broadcasted_iota(jnp.int32, (tm, LANES), 0) & (seq_len - 1)).astype(F32) * inv
        cl_sc[...] = jnp.cos(off)
        sl_sc[...] = jnp.sin(off)

    base = (pos0 + ((i * tm) & (seq_len - 1))).astype(F32) * jnp.broadcast_to(inv, (SUBLANES, LANES))
    cb = jnp.cos(base)[:1]
    sb = jnp.sin(base)[:1]
    cl = cl_sc[...]
    sl = sl_sc[...]
    c_m = cb * cl - sb * sl
    sin = sb * cl + cb * sl
    lane = lax.broadcasted_iota(jnp.int32, (tm, LANES), 1) & (QK_DIM - 1)
    half = ROPE_DIM // 2
    s_lo = jnp.where(lane < half, -sin, 0.0)
    s_hi = jnp.where(lane >= half, sin, 0.0)

    def seg(lo, hi):
        return jnp.dot(hn, w_ref[:, lo:hi], preferred_element_type=F32)

    def rope(t):
        outs = []
        for h in range(ATT_HEADS):
            th = t[:, h * LANES:(h + 1) * LANES]
            outs.append(th * c_m + pltpu.roll(th, LANES - half, 1) * s_lo + pltpu.roll(th, half, 1) * s_hi)
        return jnp.concatenate(outs, axis=1)

    def put(ref, t):
        if not transposed_qv:
            ref[...] = t.astype(BF16)
            return
        for h in range(ATT_HEADS):
            tt = t[:, h * LANES:(h + 1) * LANES].T.astype(BF16)
            for c in range(tm // ATT_TILE):
                ref[h, c] = tt[:, c * ATT_TILE:(c + 1) * ATT_TILE]

    u_ref[...] = seg(0, D_SSM).astype(BF16)
    zs_ref[...] = seg(D_SSM, 2 * D_SSM).astype(BF16)
    q = rope(seg(1024, 1536))
    put(q_ref, q * Q_SCALE)
    k = rope(seg(1536, 2048))
    if transposed_qv:
        for h in range(ATT_HEADS):
            kt = k[:, h * LANES:(h + 1) * LANES].T
            kf_ref[h, 0] = kt[:QK_DIM]
            kf_ref[h, 1] = kt[QK_DIM:]
    else:
        kf_ref[...] = k
    kb_ref[...] = k.astype(BF16)
    v = seg(2048, 2560)
    for h in range(ATT_HEADS):
        vf_ref[pl.ds(h, tm, stride=ATT_HEADS), :] = v[:, h * LANES:(h + 1) * LANES]
    put(vb_ref, v)
    za_ref[...] = seg(2560, 3072).astype(BF16)


def _in_proj(x2d, g, w_bf, inv_lane, *, seq_len, pos0, tm, transposed_qv):
    n = x2d.shape[0]
    assert n % tm == 0 and seq_len & (seq_len - 1) == 0 and (tm % seq_len == 0 or seq_len % tm == 0)
    row = lambda i: (i, 0)
    o512 = pl.BlockSpec((tm, 512), row)
    shp = lambda dt: jax.ShapeDtypeStruct((n, 512), dt)
    if transposed_qv:
        assert seq_len % tm == 0 and tm % ATT_TILE == 0
        tps = seq_len // tm
        per = tm // ATT_TILE
        t_spec = pl.BlockSpec((None, ATT_HEADS, per, LANES, ATT_TILE), lambda i: (i // tps, 0, i % tps, 0, 0))
        t_shape = jax.ShapeDtypeStruct((n // seq_len, ATT_HEADS, seq_len // ATT_TILE, LANES, ATT_TILE), BF16)
        kf_spec = pl.BlockSpec((None, ATT_HEADS, 2, QK_DIM, tm), lambda i: (i // tps, 0, 0, 0, i % tps))
        kf_shape = jax.ShapeDtypeStruct((n // seq_len, ATT_HEADS, 2, QK_DIM, seq_len), F32)
    else:
        t_spec, t_shape = o512, shp(BF16)
        kf_spec, kf_shape = o512, shp(F32)
    vf_spec = pl.BlockSpec((tm * ATT_HEADS, V_DIM), row)
    vf_shape = jax.ShapeDtypeStruct((n * ATT_HEADS, V_DIM), F32)
    return pl.pallas_call(
        functools.partial(_in_proj_kernel, tm=tm, seq_len=seq_len, pos0=pos0, transposed_qv=transposed_qv),
        grid=(n // tm,),
        in_specs=[pl.BlockSpec((tm, D_MODEL), row), _full((1, D_MODEL)), _full((D_MODEL, D_IN)),
                  _full((1, LANES))],
        out_specs=[o512, o512, t_spec, kf_spec, o512, vf_spec, t_spec, o512],
        out_shape=[shp(BF16), shp(BF16), t_shape, kf_shape, shp(BF16), vf_shape, t_shape, shp(BF16)],
        scratch_shapes=[pltpu.VMEM((tm, LANES), F32), pltpu.VMEM((tm, LANES), F32)],
        compiler_params=_params(("arbitrary",)),
        name="in_proj",
    )(x2d, g, w_bf, inv_lane)


def _cmul(ar, ai, br, bi):
    return ar * br - ai * bi, ar * bi + ai * br


def _s5_kernel(u_ref, up_ref, zsp_ref, bdb_ref, bdc_ref, d_ref, a_ref, w1_ref, b1_ref, w2_ref, b2_ref, h0_ref,
               out_ref, hf_ref, hbuf, hout_a, hout_b, hb16, car, *, n_chain, T, n_steps, chained, chain_group):
    n_rows = n_chain * T
    P = n_rows + SLAB_PAD
    half_cols = N_CH // 2
    i = pl.program_id(0)
    blocks = range(N_SLAB // SUBLANES)

    def project_and_scan(hout):
        u = u_ref[...].reshape(n_rows, D_SSM)
        for b in range(2):
            bu = jnp.dot(u[:, b * 256:(b + 1) * 256], bdb_ref[b], preferred_element_type=F32)
            for part in range(2):
                for k in range(N_SLAB // 2):
                    slab = part * N_SLAB + b * (N_SLAB // 2) + k
                    col = part * half_cols + k * LANES
                    hbuf[slab * P:slab * P + n_rows, :] = bu[:, col:col + LANES]
        a = [(a_ref[SUBLANES * k:SUBLANES * (k + 1), :], a_ref[N_SLAB + SUBLANES * k:N_SLAB + SUBLANES * (k + 1), :])
             for k in blocks]
        for g0 in range(0, n_chain, chain_group):
            chains = list(range(g0, g0 + chain_group))
            st = {(c, k): (car[c, SUBLANES * k:SUBLANES * (k + 1), :],
                           car[c, N_SLAB + SUBLANES * k:N_SLAB + SUBLANES * (k + 1), :])
                  for c in chains for k in blocks}
            for t in range(T):
                for c in chains:
                    for k in blocks:
                        hr, hi = st[(c, k)]
                        rows_r = pl.ds(SUBLANES * k * P + c * T + t, SUBLANES, stride=P)
                        rows_i = pl.ds((N_SLAB + SUBLANES * k) * P + c * T + t, SUBLANES, stride=P)
                        pr, pi = _cmul(a[k][0], a[k][1], hr, hi)
                        nr = pr + hbuf[rows_r, :]
                        ni = pi + hbuf[rows_i, :]
                        hout[rows_r, :] = nr
                        hout[rows_i, :] = ni
                        st[(c, k)] = (nr, ni)
            for c in chains:
                for k in blocks:
                    hr, hi = st[(c, k)]
                    car[c, SUBLANES * k:SUBLANES * (k + 1), :] = hr
                    car[c, N_SLAB + SUBLANES * k:N_SLAB + SUBLANES * (k + 1), :] = hi

    def output_stage(hout):
        up = up_ref[...].reshape(n_rows, D_SSM)
        for b in range(2):
            for part in range(2):
                for k in range(N_SLAB // 2):
                    slab = part * N_SLAB + b * (N_SLAB // 2) + k
                    col = b * N_CH + part * half_cols + k * LANES
                    hb16[:, col:col + LANES] = hout[slab * P:slab * P + n_rows, :].astype(BF16)
        ys = [jnp.dot(hb16[:, b * N_CH:(b + 1) * N_CH], bdc_ref[b], preferred_element_type=F32) for b in range(2)]
        y = jnp.concatenate(ys, axis=1) + up.astype(F32) * d_ref[...]
        gb = jax.nn.gelu(y).astype(BF16)
        y1 = jnp.dot(gb, w1_ref[...], preferred_element_type=F32) + b1_ref[...]
        y2 = jnp.dot(gb, w2_ref[...], preferred_element_type=F32) + b2_ref[...]
        zs = zsp_ref[...].reshape(n_rows, D_SSM).astype(F32)
        out_ref[...] = (y1 * jax.nn.sigmoid(y2) * jax.nn.silu(zs)).astype(BF16).reshape(n_chain, T, D_SSM)

    @pl.when(i == 0)
    def _():
        car[...] = jnp.zeros_like(car) if chained else h0_ref[...]
        project_and_scan(hout_a)

    if n_steps > 1:
        @pl.when((i > 0) & (i < n_steps) & ((i & 1) == 1))
        def _():
            project_and_scan(hout_b)
            output_stage(hout_a)

        @pl.when((i > 0) & (i < n_steps) & ((i & 1) == 0))
        def _():
            project_and_scan(hout_a)
            output_stage(hout_b)

    @pl.when(i == n_steps)
    def _():
        output_stage(hout_a if (n_steps - 1) % 2 == 0 else hout_b)
        hf_ref[...] = car[...]


def _s5(u, zs, bdb, bdc, dvec, a2, w1, b1, w2, b2, h0, *, n_seq, seq_len, chained):
    if chained:
        T = S5_STEPS
        chain_group = n_seq
    else:
        T = seq_len
        chain_group = 4
    assert seq_len % T == 0 and n_seq % chain_group == 0 and T % (2 * SUBLANES) == 0
    n_steps = seq_len // T
    n_rows = n_seq * T
    u3 = u.reshape(n_seq, seq_len, D_SSM)
    zs3 = zs.reshape(n_seq, seq_len, D_SSM)
    cur = pl.BlockSpec((n_seq, T, D_SSM), lambda t: (0, jnp.minimum(t, n_steps - 1), 0))
    prev = pl.BlockSpec((n_seq, T, D_SSM), lambda t: (0, jnp.maximum(t - 1, 0), 0))
    st_shape = (n_seq, 2 * N_SLAB, LANES)
    slabs = pltpu.VMEM((2 * N_SLAB * (n_rows + SLAB_PAD), LANES), F32)
    out, hf = pl.pallas_call(
        functools.partial(_s5_kernel, n_chain=n_seq, T=T, n_steps=n_steps, chained=chained,
                          chain_group=chain_group),
        grid=(n_steps + 1,),
        in_specs=[cur, prev, prev, _full(bdb.shape), _full(bdc.shape), _full((1, D_SSM)),
                  _full((2 * N_SLAB, LANES)), _full((D_SSM, D_SSM)), _full((1, D_SSM)), _full((D_SSM, D_SSM)),
                  _full((1, D_SSM)), _full(st_shape)],
        out_specs=[prev, _full(st_shape)],
        out_shape=[jax.ShapeDtypeStruct((n_seq, seq_len, D_SSM), BF16), jax.ShapeDtypeStruct(st_shape, F32)],
        scratch_shapes=[slabs, slabs, slabs, pltpu.VMEM((n_rows, 2 * N_CH), BF16), pltpu.VMEM(st_shape, F32)],
        compiler_params=_params(("arbitrary",)),
        name="s5_chained" if chained else "s5_independent",
    )(u3, u3, zs3, bdb, bdc, dvec, a2, w1, b1, w2, b2, h0)
    return out.reshape(n_seq * seq_len, D_SSM), hf


def _lambda(lq1, lk1, lq2, lk2):
    s1 = jnp.sum(lq1[...] * lk1[...], axis=1, keepdims=True)
    s2 = jnp.sum(lq2[...] * lk2[...], axis=1, keepdims=True)
    return jnp.exp(s1) - jnp.exp(s2) + LAMBDA_INIT


def _stack_maps(q):
    lane = lax.broadcasted_iota(jnp.int32, q.shape, 1)
    zero = jnp.zeros_like(q)
    return jnp.concatenate([jnp.where(lane < QK_DIM, q, zero), jnp.where(lane >= QK_DIM, q, zero)], axis=0)


def _subln_gate(o, g, za):
    ms = jnp.mean(o * o, axis=-1, keepdims=True)
    on = (o * lax.rsqrt(ms + EPS) * g) * (1.0 - LAMBDA_INIT)
    return on * jax.nn.silu(za.astype(F32))


def _finish_head(acc, l, lam, g, za, t):
    inv = 1.0 / l
    o = acc[:t] * inv[:t] - lam * (acc[t:] * inv[t:])
    return _subln_gate(o, g, za)


_NT = (((1,), (1,)), ((), ()))


def _attn_kernel(lq1, lk1, lq2, lk2, g_ref, qt_ref, k_ref, vt_ref, za_ref, o_ref, m_sc, l_sc, acc_sc, s_a, s_b):
    tq = ATT_TILE
    qi = pl.program_id(2)
    q2t = []
    for h in range(HEADS_PER_STEP):
        qt = qt_ref[h]
        row = lax.broadcasted_iota(jnp.int32, qt.shape, 0)
        zero = jnp.zeros_like(qt)
        q2t.append(jnp.concatenate([jnp.where(row < QK_DIM, qt, zero), jnp.where(row >= QK_DIM, qt, zero)], axis=1))
    m_sc[...] = jnp.full_like(m_sc, NEG)
    l_sc[...] = jnp.zeros_like(l_sc)
    acc_sc[...] = jnp.zeros_like(acc_sc)
    heads = range(HEADS_PER_STEP)
    tk = 2 * tq
    n_full = qi // 2
    odd = (qi & 1) == 1

    def scores_into(s_ref, kt, h):
        start = pl.multiple_of(kt * tk, tk)
        s_ref[h] = jnp.dot(k_ref[pl.ds(start, tk), h * LANES:(h + 1) * LANES], q2t[h],
                           preferred_element_type=F32)

    def softmax_accumulate(s_ref, kt, nkeys, masked, h):
        s = s_ref[h] if nkeys == tk else s_ref[h, :nkeys, :]
        if masked:
            kc = lax.broadcasted_iota(jnp.int32, s.shape, 0) // CHUNK + kt * (tk // CHUNK)
            qc = (lax.broadcasted_iota(jnp.int32, s.shape, 1) & (tq - 1)) // CHUNK + qi * (tq // CHUNK)
            s = jnp.where(kc <= qc, s, NEG)
        m_old = m_sc[h]
        m_new = jnp.maximum(m_old, jnp.max(s, axis=0, keepdims=True))
        alpha = jnp.exp2(m_old - m_new)
        p = jnp.exp2(s - m_new)
        l_sc[h] = alpha * l_sc[h] + jnp.sum(p, axis=0, keepdims=True)
        pb = p.astype(BF16)
        pv = jnp.dot(vt_ref[h, 2 * kt], pb[:tq], preferred_element_type=F32)
        if nkeys == tk:
            pv = pv + jnp.dot(vt_ref[h, 2 * kt + 1], pb[tq:], preferred_element_type=F32)
        acc_sc[h] = alpha * acc_sc[h] + pv
        m_sc[h] = m_new

    def stage(kt, s_cur, s_nxt):
        for h in heads:
            scores_into(s_nxt, kt + 1, h)
            softmax_accumulate(s_cur, kt, tk, False, h)

    for h in heads:
        scores_into(s_a, 0, h)

    def body(kt, c):
        even = (kt & 1) == 0

        @pl.when(even)
        def _():
            stage(kt, s_a, s_b)

        @pl.when(jnp.logical_not(even))
        def _():
            stage(kt, s_b, s_a)

        return c

    lax.fori_loop(0, n_full, body, 0)

    for parity, s_ref in ((0, s_a), (1, s_b)):
        here = (n_full & 1) == parity

        @pl.when(here & odd)
        def _(s_ref=s_ref):
            for h in heads:
                softmax_accumulate(s_ref, n_full, tk, True, h)

        @pl.when(here & jnp.logical_not(odd))
        def _(s_ref=s_ref):
            for h in heads:
                softmax_accumulate(s_ref, n_full, tq, True, h)

    lam = _lambda(lq1, lk1, lq2, lk2)
    for h in range(HEADS_PER_STEP):
        acc = acc_sc[h]
        inv = 1.0 / l_sc[h]
        ot = acc[:, :tq] * inv[:, :tq] - lam * (acc[:, tq:] * inv[:, tq:])
        cols = slice(h * LANES, (h + 1) * LANES)
        o_ref[:, cols] = _subln_gate(ot.T, g_ref[...], za_ref[:, cols]).astype(BF16)


def _attn_prompt(lams, g, qt, kb, vt, za, *, n_seq, seq_len):
    tq = ATT_TILE
    nq = seq_len // tq
    hps = HEADS_PER_STEP
    assert tq % CHUNK == 0 and tq & (tq - 1) == 0 and ATT_HEADS % hps == 0 and seq_len % (2 * tq) == 0
    rowspec = pl.BlockSpec((tq, hps * LANES), lambda b, h, i: (b * nq + i, h))
    qtspec = pl.BlockSpec((None, hps, None, LANES, tq), lambda b, h, i: (b, h, i, 0, 0))
    kspec = pl.BlockSpec((seq_len, hps * LANES), lambda b, h, i: (b, h))
    vtspec = pl.BlockSpec((None, hps, nq, LANES, tq), lambda b, h, i: (b, h, 0, 0, 0))
    vec = _full((1, QK_DIM))
    stat = pltpu.VMEM((hps, 1, 2 * tq), F32)
    return pl.pallas_call(
        _attn_kernel,
        grid=(n_seq, ATT_HEADS // hps, nq),
        in_specs=[vec, vec, vec, vec, _full((1, V_DIM)), qtspec, kspec, vtspec, rowspec],
        out_specs=rowspec,
        out_shape=jax.ShapeDtypeStruct((n_seq * seq_len, D_ATT), BF16),
        scratch_shapes=[stat, stat, pltpu.VMEM((hps, V_DIM, 2 * tq), F32),
                        pltpu.VMEM((hps, 2 * tq, 2 * tq), F32), pltpu.VMEM((hps, 2 * tq, 2 * tq), F32)],
        compiler_params=_params(("arbitrary", "arbitrary", "arbitrary")),
        name="attn_prompt",
    )(*lams, g, qt, kb, vt, za)


def _attn_sample_kernel(lq1, lk1, lq2, lk2, g_ref, q_ref, kn_ref, vn_ref, ck_ref, cv_ref, za_ref, o_ref, *, t, past):
    lam = _lambda(lq1, lk1, lq2, lk2)
    outs = []
    for h in range(ATT_HEADS):
        cols = slice(h * LANES, (h + 1) * LANES)
        q2 = _stack_maps(q_ref[:, cols])
        kpt = jnp.concatenate([ck_ref[h, 0], ck_ref[h, 1]], axis=0).astype(BF16)
        vp = cv_ref[pl.ds(h, past, stride=ATT_HEADS), :].astype(BF16)
        s_p = jnp.dot(q2, kpt, preferred_element_type=F32)
        s_n = lax.dot_general(q2, kn_ref[:, cols], _NT, preferred_element_type=F32)
        m = jnp.maximum(jnp.max(s_p, axis=1, keepdims=True), jnp.max(s_n, axis=1, keepdims=True))
        p_p = jnp.exp2(s_p - m)
        p_n = jnp.exp2(s_n - m)
        l = jnp.sum(p_p, axis=1, keepdims=True) + jnp.sum(p_n, axis=1, keepdims=True)
        acc = (jnp.dot(p_p.astype(BF16), vp, preferred_element_type=F32)
               + jnp.dot(p_n.astype(BF16), vn_ref[:, cols], preferred_element_type=F32))
        outs.append(_finish_head(acc, l, lam, g_ref[...], za_ref[:, cols], t))
    o_ref[...] = jnp.concatenate(outs, axis=1).astype(BF16)


def _attn_sample(lams, g, q, kb, vb, ck, cv, za, *, n_seq, t, past):
    assert past % CHUNK == 0 and t <= CHUNK
    row = pl.BlockSpec((t, D_ATT), lambda b: (b, 0))
    kcache = pl.BlockSpec((None, ATT_HEADS, 2, QK_DIM, past), lambda b: (b, 0, 0, 0, 0))
    vcache = pl.BlockSpec((None, past * ATT_HEADS, V_DIM), lambda b: (b, 0, 0))
    vec = _full((1, QK_DIM))
    return pl.pallas_call(
        functools.partial(_attn_sample_kernel, t=t, past=past),
        grid=(n_seq,),
        in_specs=[vec, vec, vec, vec, _full((1, V_DIM)), row, row, row, kcache, vcache, row],
        out_specs=row,
        out_shape=jax.ShapeDtypeStruct((n_seq * t, D_ATT), BF16),
        compiler_params=_params(("arbitrary",)),
        name="attn_sample",
    )(*lams, g, q, kb, vb, ck, cv, za)


def _out_proj_kernel(s_ref, a_ref, w_ref, x_ref, g_ref, y_ref):
    tm = s_ref.shape[0]
    step = min(tm, OUT_CHUNK)
    for r in range(0, tm, step):
        rows = slice(r, r + step)
        mix = (jnp.dot(s_ref[rows, :], w_ref[:D_SSM, :], preferred_element_type=F32)
               + jnp.dot(a_ref[rows, :], w_ref[D_SSM:, :], preferred_element_type=F32))
        ms = jnp.mean(mix * mix, axis=-1, keepdims=True)
        y_ref[rows, :] = x_ref[rows, :] + mix * lax.rsqrt(ms + EPS) * g_ref[...]


def _out_proj(ssm_out, att_out, w_bf, x2d, g, *, tm):
    n = x2d.shape[0]
    assert n % tm == 0 and tm % min(tm, OUT_CHUNK) == 0
    row = lambda i: (i, 0)
    half = pl.BlockSpec((tm, 512), row)
    full = pl.BlockSpec((tm, D_MODEL), row)
    return pl.pallas_call(
        _out_proj_kernel,
        grid=(n // tm,),
        in_specs=[half, half, _full((D_MODEL, D_MODEL)), full, _full((1, D_MODEL))],
        out_specs=full,
        out_shape=jax.ShapeDtypeStruct((n, D_MODEL), F32),
        compiler_params=_params(("arbitrary",)),
        name="out_proj",
    )(ssm_out, att_out, w_bf, x2d, g)


def _block_diag_weights(bbr, bbi, c_re, c_im):
    eye = jnp.eye(SSM_GROUPS // 2, dtype=F32)

    def b_side(m):
        m = m.reshape(2, SSM_GROUPS // 2, SSM_GROUP, SSM_STATE)
        return jnp.einsum('bgcp,gh->bgchp', m, eye).reshape(2, 256, N_CH // 2)

    def c_side(m):
        m = m.reshape(2, SSM_GROUPS // 2, SSM_GROUP, SSM_STATE)
        return jnp.einsum('bgcp,gh->bgphc', m, eye).reshape(2, N_CH // 2, 256)

    bdb = jnp.concatenate([b_side(bbr), b_side(bbi)], axis=2).astype(BF16)
    bdc = jnp.concatenate([c_side(c_re), -c_side(c_im)], axis=1).astype(BF16)
    return bdb, bdc


def kernel(x_prompt, x_sample, cache_k, cache_v, state_ssm_re, state_ssm_im, norm_pre_g, w_in, ssm_lambda_re,
           ssm_lambda_im, ssm_log_dt, ssm_b_re, ssm_b_im, ssm_c_re, ssm_c_im, ssm_d, glu_w1, glu_b1, glu_w2,
           glu_b2, lambda_q1, lambda_k1, lambda_q2, lambda_k2, attn_subln_g, w_out, norm_post_g):
    bp, sp, _ = x_prompt.shape
    bs, ss, _ = x_sample.shape
    past = cache_k.shape[2]

    a_re, a_im, bbr, bbi = _prep(ssm_lambda_re[0], ssm_lambda_im[0], ssm_log_dt[0], ssm_b_re[0], ssm_b_im[0])
    bdb, bdc = _block_diag_weights(bbr, bbi, ssm_c_re[0], ssm_c_im[0])
    a2 = jnp.concatenate([a_re.reshape(N_SLAB, LANES), a_im.reshape(N_SLAB, LANES)], axis=0)
    dvec = ssm_d[0].reshape(1, D_SSM)
    w_in_bf = w_in[0].astype(BF16)
    w_out_bf = w_out[0].astype(BF16)
    w1 = glu_w1[0].astype(BF16)
    w2 = glu_w2[0].astype(BF16)
    b1 = glu_b1[0].reshape(1, D_SSM)
    b2 = glu_b2[0].reshape(1, D_SSM)
    g_pre = norm_pre_g[0].reshape(1, D_MODEL)
    g_post = norm_post_g[0].reshape(1, D_MODEL)
    g_sub = attn_subln_g[0].reshape(1, V_DIM)
    lams = tuple(v[0].reshape(1, QK_DIM) for v in (lambda_q1, lambda_k1, lambda_q2, lambda_k2))
    inv = ROPE_THETA ** (-jnp.arange(ROPE_DIM // 2, dtype=F32) * 2.0 / ROPE_DIM)
    rotary_lane = (jnp.arange(LANES) % QK_DIM) < ROPE_DIM
    inv_lane = jnp.where(rotary_lane, jnp.tile(inv, LANES // (ROPE_DIM // 2)), 0.0).reshape(1, LANES)

    def run(x, n_seq, seq_len, pos0, chained, h0, tm):
        x2d = x.reshape(n_seq * seq_len, D_MODEL)
        u, zs, q, kf, kb, vf, vb, za = _in_proj(x2d, g_pre, w_in_bf, inv_lane, seq_len=seq_len, pos0=pos0, tm=tm,
                                                transposed_qv=chained)
        ssm_out, hf = _s5(u, zs, bdb, bdc, dvec, a2, w1, b1, w2, b2, h0,
                          n_seq=n_seq, seq_len=seq_len, chained=chained)
        if chained:
            att = _attn_prompt(lams, g_sub, q, kb, vb, za, n_seq=n_seq, seq_len=seq_len)
            k_out = jnp.transpose(kf, (0, 4, 1, 2, 3))[None]
        else:
            ck = jnp.transpose(cache_k[0], (0, 2, 3, 4, 1))
            cv = cache_v[0].reshape(n_seq, past * ATT_HEADS, V_DIM)
            att = _attn_sample(lams, g_sub, q, kb, vb, ck, cv, za, n_seq=n_seq, t=seq_len, past=past)
            k_out = kf.reshape(1, n_seq, seq_len, ATT_HEADS, 2, QK_DIM)
        y = _out_proj(ssm_out, att, w_out_bf, x2d, g_post, tm=tm)
        return (y.reshape(n_seq, seq_len, D_MODEL),
                k_out,
                vf.reshape(1, n_seq, seq_len, ATT_HEADS, V_DIM),
                hf[:, :N_SLAB].reshape(1, n_seq, SSM_GROUPS, SSM_STATE),
                hf[:, N_SLAB:].reshape(1, n_seq, SSM_GROUPS, SSM_STATE))

    yp, kp, vp, hrp, hip = run(x_prompt, bp, sp, 0, True, jnp.zeros((bp, 2 * N_SLAB, LANES), F32), 512)
    h0 = jnp.concatenate([state_ssm_re[0].reshape(bs, N_SLAB, LANES), state_ssm_im[0].reshape(bs, N_SLAB, LANES)],
                         axis=1)
    ys, ks, vs, hrs, his = run(x_sample, bs, ss, past, False, h0, bs * ss)
    return (yp, ys, kp, vp, hrp, hip, ks, vs, hrs, his)
```

```python
import functools
import math

import jax
import jax.numpy as jnp
from jax import lax
from jax.experimental import pallas as pl
from jax.experimental.pallas import tpu as pltpu

F32 = jnp.float32
BF16 = jnp.bfloat16

D_MODEL = 1024
D_SSM = 512
D_ATT = 512
SSM_GROUP = 16
SSM_GROUPS = 32
SSM_STATE = 64
N_CH = SSM_GROUPS * SSM_STATE
ATT_HEADS = 4
QK_DIM = 64
V_DIM = 128
ROPE_DIM = 16
ROPE_THETA = 500000.0
CHUNK = 64
EPS = 1e-6
D_IN = 3072
LAMBDA_INIT = 0.8 - 0.6 * math.exp(-0.3 * 0)

LANES = 128
SUBLANES = 8
N_SLAB = N_CH // LANES
OUT_CHUNK = 256
S5_STEPS = 256
SLAB_PAD = 4
NEG = -1e30
ATT_TILE = 256
HEADS_PER_STEP = 4
SUM_ROWS = 16
LOG2E = math.log2(math.e)
Q_SCALE = QK_DIM ** -0.5 * LOG2E
VMEM_LIMIT = 56 * 1024 * 1024


def _params(sem):
    return pltpu.CompilerParams(dimension_semantics=sem, vmem_limit_bytes=VMEM_LIMIT)


def _full(shape):
    n = len(shape)
    return pl.BlockSpec(shape, lambda *_: (0,) * n)


def _prep_kernel(lr_ref, li_ref, ldt_ref, br_ref, bi_ref, ar_ref, ai_ref, bbr_ref, bbi_ref):
    lr = lr_ref[...]
    li = li_ref[...]
    dt = jnp.exp(ldt_ref[...])
    mag = jnp.exp(lr * dt)
    ar = mag * jnp.cos(li * dt)
    ai = mag * jnp.sin(li * dt)
    den = lr * lr + li * li
    cr = ((ar - 1.0) * lr + ai * li) / den
    ci = (ai * lr - (ar - 1.0) * li) / den
    ar_ref[...] = ar
    ai_ref[...] = ai
    br = br_ref[...]
    bi = bi_ref[...]
    crb = cr[:, None, :]
    cib = ci[:, None, :]
    bbr_ref[...] = crb * br - cib * bi
    bbi_ref[...] = crb * bi + cib * br


def _prep(lam_re, lam_im, log_dt, b_re, b_im):
    g, p, c = b_re.shape
    brt = jnp.swapaxes(b_re, 1, 2)
    bit = jnp.swapaxes(b_im, 1, 2)
    out_shape = (jax.ShapeDtypeStruct((g, p), F32), jax.ShapeDtypeStruct((g, p), F32),
                 jax.ShapeDtypeStruct((g, c, p), F32), jax.ShapeDtypeStruct((g, c, p), F32))
    return pl.pallas_call(_prep_kernel, out_shape=out_shape, name="s5_prep")(
        lam_re, lam_im, log_dt.reshape(g, 1), brt, bit)


def _in_proj_kernel(x_ref, g_ref, w_ref, inv_ref, u_ref, zs_ref, q_ref, kf_ref, kb_ref, vf_ref,
                    vb_ref, za_ref, cl_sc, sl_sc, *, tm, seq_len, pos0, transposed_qv):
    i = pl.program_id(0)
    x = x_ref[...]
    ms = jnp.mean(x * x, axis=-1, keepdims=True)
    hn = (x * lax.rsqrt(ms + EPS) * g_ref[...]).astype(BF16)

    inv = inv_ref[...]

    @pl.when(i == 0)
    def _():
        off = (lax.broadcasted_iota(jnp.int32, (tm, LANES), 0) & (seq_len - 1)).astype(F32) * inv
        cl_sc[...] = jnp.cos(off)
        sl_sc[...] = jnp.sin(off)

    base = (pos0 + ((i * tm) & (seq_len - 1))).astype(F32) * jnp.broadcast_to(inv, (SUBLANES, LANES))
    cb = jnp.cos(base)[:1]
    sb = jnp.sin(base)[:1]
    cl = cl_sc[...]
    sl = sl_sc[...]
    c_m = cb * cl - sb * sl
    sin = sb * cl + cb * sl
    lane = lax.broadcasted_iota(jnp.int32, (tm, LANES), 1) & (QK_DIM - 1)
    half = ROPE_DIM // 2
    s_lo = jnp.where(lane < half, -sin, 0.0)
    s_hi = jnp.where(lane >= half, sin, 0.0)

    def seg(lo, hi):
        return jnp.dot(hn, w_ref[:, lo:hi], preferred_element_type=F32)

    def rope(t):
        outs = []
        for h in range(ATT_HEADS):
            th = t[:, h * LANES:(h + 1) * LANES]
            outs.append(th * c_m + pltpu.roll(th, LANES - half, 1) * s_lo + pltpu.roll(th, half, 1) * s_hi)
        return jnp.concatenate(outs, axis=1)

    def put(ref, t):
        if not transposed_qv:
            ref[...] = t.astype(BF16)
            return
        for h in range(ATT_HEADS):
            tt = t[:, h * LANES:(h + 1) * LANES].T.astype(BF16)
            for c in range(tm // ATT_TILE):
                ref[h, c] = tt[:, c * ATT_TILE:(c + 1) * ATT_TILE]

    u_ref[...] = seg(0, D_SSM).astype(BF16)
    zs_ref[...] = seg(D_SSM, 2 * D_SSM).astype(BF16)
    q = rope(seg(1024, 1536))
    put(q_ref, q * Q_SCALE)
    k = rope(seg(1536, 2048))
    if transposed_qv:
        for h in range(ATT_HEADS):
            kt = k[:, h * LANES:(h + 1) * LANES].T
            kf_ref[h, 0] = kt[:QK_DIM]
            kf_ref[h, 1] = kt[QK_DIM:]
    else:
        kf_ref[...] = k
    kb_ref[...] = k.astype(BF16)
    v = seg(2048, 2560)
    for h in range(ATT_HEADS):
        vf_ref[pl.ds(h, tm, stride=ATT_HEADS), :] = v[:, h * LANES:(h + 1) * LANES]
    put(vb_ref, v)
    za_ref[...] = seg(2560, 3072).astype(BF16)


def _in_proj(x2d, g, w_bf, inv_lane, *, seq_len, pos0, tm, transposed_qv):
    n = x2d.shape[0]
    assert n % tm == 0 and seq_len & (seq_len - 1) == 0 and (tm % seq_len == 0 or seq_len % tm == 0)
    row = lambda i: (i, 0)
    o512 = pl.BlockSpec((tm, 512), row)
    shp = lambda dt: jax.ShapeDtypeStruct((n, 512), dt)
    if transposed_qv:
        assert seq_len % tm == 0 and tm % ATT_TILE == 0
        tps = seq_len // tm
        per = tm // ATT_TILE
        t_spec = pl.BlockSpec((None, ATT_HEADS, per, LANES, ATT_TILE), lambda i: (i // tps, 0, i % tps, 0, 0))
        t_shape = jax.ShapeDtypeStruct((n // seq_len, ATT_HEADS, seq_len // ATT_TILE, LANES, ATT_TILE), BF16)
        kf_spec = pl.BlockSpec((None, ATT_HEADS, 2, QK_DIM, tm), lambda i: (i // tps, 0, 0, 0, i % tps))
        kf_shape = jax.ShapeDtypeStruct((n // seq_len, ATT_HEADS, 2, QK_DIM, seq_len), F32)
    else:
        t_spec, t_shape = o512, shp(BF16)
        kf_spec, kf_shape = o512, shp(F32)
    vf_spec = pl.BlockSpec((tm * ATT_HEADS, V_DIM), row)
    vf_shape = jax.ShapeDtypeStruct((n * ATT_HEADS, V_DIM), F32)
    return pl.pallas_call(
        functools.partial(_in_proj_kernel, tm=tm, seq_len=seq_len, pos0=pos0, transposed_qv=transposed_qv),
        grid=(n // tm,),
        in_specs=[pl.BlockSpec((tm, D_MODEL), row), _full((1, D_MODEL)), _full((D_MODEL, D_IN)),
                  _full((1, LANES))],
        out_specs=[o512, o512, t_spec, kf_spec, o512, vf_spec, t_spec, o512],
        out_shape=[shp(BF16), shp(BF16), t_shape, kf_shape, shp(BF16), vf_shape, t_shape, shp(BF16)],
        scratch_shapes=[pltpu.VMEM((tm, LANES), F32), pltpu.VMEM((tm, LANES), F32)],
        compiler_params=_params(("arbitrary",)),
        name="in_proj",
    )(x2d, g, w_bf, inv_lane)


def _cmul(ar, ai, br, bi):
    return ar * br - ai * bi, ar * bi + ai * br


def _s5_kernel(u_ref, up_ref, zsp_ref, bdb_ref, bdc_ref, d_ref, a_ref, w1_ref, b1_ref, w2_ref, b2_ref, h0_ref,
               out_ref, hf_ref, hbuf, hout_a, hout_b, hb16, car, *, n_chain, T, n_steps, chained, chain_group):
    n_rows = n_chain * T
    P = n_rows + SLAB_PAD
    half_cols = N_CH // 2
    i = pl.program_id(0)
    blocks = range(N_SLAB // SUBLANES)

    def project_and_scan(hout):
        u = u_ref[...].reshape(n_rows, D_SSM)
        for b in range(2):
            bu = jnp.dot(u[:, b * 256:(b + 1) * 256], bdb_ref[b], preferred_element_type=F32)
            for part in range(2):
                for k in range(N_SLAB // 2):
                    slab = part * N_SLAB + b * (N_SLAB // 2) + k
                    col = part * half_cols + k * LANES
                    hbuf[slab * P:slab * P + n_rows, :] = bu[:, col:col + LANES]
        a = [(a_ref[SUBLANES * k:SUBLANES * (k + 1), :], a_ref[N_SLAB + SUBLANES * k:N_SLAB + SUBLANES * (k + 1), :])
             for k in blocks]
        for g0 in range(0, n_chain, chain_group):
            chains = list(range(g0, g0 + chain_group))
            st = {(c, k): (car[c, SUBLANES * k:SUBLANES * (k + 1), :],
                           car[c, N_SLAB + SUBLANES * k:N_SLAB + SUBLANES * (k + 1), :])
                  for c in chains for k in blocks}
            for t in range(T):
                for c in chains:
                    for k in blocks:
                        hr, hi = st[(c, k)]
                        rows_r = pl.ds(SUBLANES * k * P + c * T + t, SUBLANES, stride=P)
                        rows_i = pl.ds((N_SLAB + SUBLANES * k) * P + c * T + t, SUBLANES, stride=P)
                        pr, pi = _cmul(a[k][0], a[k][1], hr, hi)
                        nr = pr + hbuf[rows_r, :]
                        ni = pi + hbuf[rows_i, :]
                        hout[rows_r, :] = nr
                        hout[rows_i, :] = ni
                        st[(c, k)] = (nr, ni)
            for c in chains:
                for k in blocks:
                    hr, hi = st[(c, k)]
                    car[c, SUBLANES * k:SUBLANES * (k + 1), :] = hr
                    car[c, N_SLAB + SUBLANES * k:N_SLAB + SUBLANES * (k + 1), :] = hi

    def output_stage(hout):
        up = up_ref[...].reshape(n_rows, D_SSM)
        for b in range(2):
            for part in range(2):
                for k in range(N_SLAB // 2):
                    slab = part * N_SLAB + b * (N_SLAB // 2) + k
                    col = b * N_CH + part * half_cols + k * LANES
                    hb16[:, col:col + LANES] = hout[slab * P:slab * P + n_rows, :].astype(BF16)
        ys = [jnp.dot(hb16[:, b * N_CH:(b + 1) * N_CH], bdc_ref[b], preferred_element_type=F32) for b in range(2)]
        y = jnp.concatenate(ys, axis=1) + up.astype(F32) * d_ref[...]
        gb = jax.nn.gelu(y).astype(BF16)
        y1 = jnp.dot(gb, w1_ref[...], preferred_element_type=F32) + b1_ref[...]
        y2 = jnp.dot(gb, w2_ref[...], preferred_element_type=F32) + b2_ref[...]
        zs = zsp_ref[...].reshape(n_rows, D_SSM).astype(F32)
        out_ref[...] = (y1 * jax.nn.sigmoid(y2) * jax.nn.silu(zs)).astype(BF16).reshape(n_chain, T, D_SSM)

    @pl.when(i == 0)
    def _():
        car[...] = jnp.zeros_like(car) if chained else h0_ref[...]
        project_and_scan(hout_a)

    if n_steps > 1:
        @pl.when((i > 0) & (i < n_steps) & ((i & 1) == 1))
        def _():
            project_and_scan(hout_b)
            output_stage(hout_a)

        @pl.when((i > 0) & (i < n_steps) & ((i & 1) == 0))
        def _():
            project_and_scan(hout_a)
            output_stage(hout_b)

    @pl.when(i == n_steps)
    def _():
        output_stage(hout_a if (n_steps - 1) % 2 == 0 else hout_b)
        hf_ref[...] = car[...]


def _s5(u, zs, bdb, bdc, dvec, a2, w1, b1, w2, b2, h0, *, n_seq, seq_len, chained):
    if chained:
        T = S5_STEPS
        chain_group = n_seq
    else:
        T = seq_len
        chain_group = 4
    assert seq_len % T == 0 and n_seq % chain_group == 0 and T % (2 * SUBLANES) == 0
    n_steps = seq_len // T
    n_rows = n_seq * T
    u3 = u.reshape(n_seq, seq_len, D_SSM)
    zs3 = zs.reshape(n_seq, seq_len, D_SSM)
    cur = pl.BlockSpec((n_seq, T, D_SSM), lambda t: (0, jnp.minimum(t, n_steps - 1), 0))
    prev = pl.BlockSpec((n_seq, T, D_SSM), lambda t: (0, jnp.maximum(t - 1, 0), 0))
    st_shape = (n_seq, 2 * N_SLAB, LANES)
    slabs = pltpu.VMEM((2 * N_SLAB * (n_rows + SLAB_PAD), LANES), F32)
    out, hf = pl.pallas_call(
        functools.partial(_s5_kernel, n_chain=n_seq, T=T, n_steps=n_steps, chained=chained,
                          chain_group=chain_group),
        grid=(n_steps + 1,),
        in_specs=[cur, prev, prev, _full(bdb.shape), _full(bdc.shape), _full((1, D_SSM)),
                  _full((2 * N_SLAB, LANES)), _full((D_SSM, D_SSM)), _full((1, D_SSM)), _full((D_SSM, D_SSM)),
                  _full((1, D_SSM)), _full(st_shape)],
        out_specs=[prev, _full(st_shape)],
        out_shape=[jax.ShapeDtypeStruct((n_seq, seq_len, D_SSM), BF16), jax.ShapeDtypeStruct(st_shape, F32)],
        scratch_shapes=[slabs, slabs, slabs, pltpu.VMEM((n_rows, 2 * N_CH), BF16), pltpu.VMEM(st_shape, F32)],
        compiler_params=_params(("arbitrary",)),
        name="s5_chained" if chained else "s5_independent",
    )(u3, u3, zs3, bdb, bdc, dvec, a2, w1, b1, w2, b2, h0)
    return out.reshape(n_seq * seq_len, D_SSM), hf


def _lambda(lq1, lk1, lq2, lk2):
    s1 = jnp.sum(lq1[...] * lk1[...], axis=1, keepdims=True)
    s2 = jnp.sum(lq2[...] * lk2[...], axis=1, keepdims=True)
    return jnp.exp(s1) - jnp.exp(s2) + LAMBDA_INIT


def _stack_maps(q):
    lane = lax.broadcasted_iota(jnp.int32, q.shape, 1)
    zero = jnp.zeros_like(q)
    return jnp.concatenate([jnp.where(lane < QK_DIM, q, zero), jnp.where(lane >= QK_DIM, q, zero)], axis=0)


def _subln_gate(o, g, za):
    ms = jnp.mean(o * o, axis=-1, keepdims=True)
    on = (o * lax.rsqrt(ms + EPS) * g) * (1.0 - LAMBDA_INIT)
    return on * jax.nn.silu(za.astype(F32))


def _finish_head(acc, l, lam, g, za, t):
    inv = 1.0 / l
    o = acc[:t] * inv[:t] - lam * (acc[t:] * inv[t:])
    return _subln_gate(o, g, za)


_NT = (((1,), (1,)), ((), ()))


def _attn_kernel(lq1, lk1, lq2, lk2, g_ref, qt_ref, k_ref, vt_ref, za_ref, o_ref, m_sc, acc_sc, s_a, s_b):
    tq = ATT_TILE
    qi = pl.program_id(2)
    q2t = []
    for h in range(HEADS_PER_STEP):
        qt = qt_ref[h]
        row = lax.broadcasted_iota(jnp.int32, qt.shape, 0)
        zero = jnp.zeros_like(qt)
        q2t.append(jnp.concatenate([jnp.where(row < QK_DIM, qt, zero), jnp.where(row >= QK_DIM, qt, zero)], axis=1))
    m_sc[...] = jnp.full_like(m_sc, NEG)
    acc_sc[...] = jnp.zeros_like(acc_sc)
    ones = jnp.ones((SUM_ROWS, tq), BF16)
    heads = range(HEADS_PER_STEP)
    tk = 2 * tq
    n_full = qi // 2
    odd = (qi & 1) == 1

    def scores_into(s_ref, kt, h):
        start = pl.multiple_of(kt * tk, tk)
        s_ref[h] = jnp.dot(k_ref[pl.ds(start, tk), h * LANES:(h + 1) * LANES], q2t[h],
                           preferred_element_type=F32)

    def softmax_accumulate(s_ref, kt, nkeys, masked, h):
        s = s_ref[h] if nkeys == tk else s_ref[h, :nkeys, :]
        if masked:
            kc = lax.broadcasted_iota(jnp.int32, s.shape, 0) // CHUNK + kt * (tk // CHUNK)
            qc = (lax.broadcasted_iota(jnp.int32, s.shape, 1) & (tq - 1)) // CHUNK + qi * (tq // CHUNK)
            s = jnp.where(kc <= qc, s, NEG)
        m_old = m_sc[h]
        m_new = jnp.maximum(m_old, jnp.max(s, axis=0, keepdims=True))
        alpha = jnp.exp2(m_old - m_new)
        pb = jnp.exp2(s - m_new).astype(BF16)
        pv = jnp.dot(jnp.concatenate([vt_ref[h, 2 * kt], ones], axis=0), pb[:tq], preferred_element_type=F32)
        if nkeys == tk:
            pv = pv + jnp.dot(jnp.concatenate([vt_ref[h, 2 * kt + 1], ones], axis=0), pb[tq:],
                              preferred_element_type=F32)
        acc_sc[h] = alpha * acc_sc[h] + pv
        m_sc[h] = m_new

    def stage(kt, s_cur, s_nxt):
        for h in heads:
            scores_into(s_nxt, kt + 1, h)
            softmax_accumulate(s_cur, kt, tk, False, h)

    for h in heads:
        scores_into(s_a, 0, h)

    def body(kt, c):
        even = (kt & 1) == 0

        @pl.when(even)
        def _():
            stage(kt, s_a, s_b)

        @pl.when(jnp.logical_not(even))
        def _():
            stage(kt, s_b, s_a)

        return c

    lax.fori_loop(0, n_full, body, 0)

    for parity, s_ref in ((0, s_a), (1, s_b)):
        here = (n_full & 1) == parity

        @pl.when(here & odd)
        def _(s_ref=s_ref):
            for h in heads:
                softmax_accumulate(s_ref, n_full, tk, True, h)

        @pl.when(here & jnp.logical_not(odd))
        def _(s_ref=s_ref):
            for h in heads:
                softmax_accumulate(s_ref, n_full, tq, True, h)

    lam = _lambda(lq1, lk1, lq2, lk2)
    for h in range(HEADS_PER_STEP):
        acc = acc_sc[h, :V_DIM, :]
        inv = 1.0 / acc_sc[h, V_DIM:V_DIM + 1, :]
        ot = acc[:, :tq] * inv[:, :tq] - lam * (acc[:, tq:] * inv[:, tq:])
        cols = slice(h * LANES, (h + 1) * LANES)
        o_ref[:, cols] = _subln_gate(ot.T, g_ref[...], za_ref[:, cols]).astype(BF16)


def _attn_prompt(lams, g, qt, kb, vt, za, *, n_seq, seq_len):
    tq = ATT_TILE
    nq = seq_len // tq
    hps = HEADS_PER_STEP
    assert tq % CHUNK == 0 and tq & (tq - 1) == 0 and ATT_HEADS % hps == 0 and seq_len % (2 * tq) == 0
    rowspec = pl.BlockSpec((tq, hps * LANES), lambda b, h, i: (b * nq + i, h))
    qtspec = pl.BlockSpec((None, hps, None, LANES, tq), lambda b, h, i: (b, h, i, 0, 0))
    kspec = pl.BlockSpec((seq_len, hps * LANES), lambda b, h, i: (b, h))
    vtspec = pl.BlockSpec((None, hps, nq, LANES, tq), lambda b, h, i: (b, h, 0, 0, 0))
    vec = _full((1, QK_DIM))
    stat = pltpu.VMEM((hps, 1, 2 * tq), F32)
    return pl.pallas_call(
        _attn_kernel,
        grid=(n_seq, ATT_HEADS // hps, nq),
        in_specs=[vec, vec, vec, vec, _full((1, V_DIM)), qtspec, kspec, vtspec, rowspec],
        out_specs=rowspec,
        out_shape=jax.ShapeDtypeStruct((n_seq * seq_len, D_ATT), BF16),
        scratch_shapes=[stat, pltpu.VMEM((hps, V_DIM + SUM_ROWS, 2 * tq), F32),
                        pltpu.VMEM((hps, 2 * tq, 2 * tq), F32), pltpu.VMEM((hps, 2 * tq, 2 * tq), F32)],
        compiler_params=_params(("arbitrary", "arbitrary", "arbitrary")),
        name="attn_prompt",
    )(*lams, g, qt, kb, vt, za)


def _attn_sample_kernel(lq1, lk1, lq2, lk2, g_ref, q_ref, kn_ref, vn_ref, ck_ref, cv_ref, za_ref, o_ref, *, t, past):
    lam = _lambda(lq1, lk1, lq2, lk2)
    outs = []
    for h in range(ATT_HEADS):
        cols = slice(h * LANES, (h + 1) * LANES)
        q2 = _stack_maps(q_ref[:, cols])
        kpt = jnp.concatenate([ck_ref[h, 0], ck_ref[h, 1]], axis=0).astype(BF16)
        vp = cv_ref[pl.ds(h, past, stride=ATT_HEADS), :].astype(BF16)
        s_p = jnp.dot(q2, kpt, preferred_element_type=F32)
        s_n = lax.dot_general(q2, kn_ref[:, cols], _NT, preferred_element_type=F32)
        m = jnp.maximum(jnp.max(s_p, axis=1, keepdims=True), jnp.max(s_n, axis=1, keepdims=True))
        p_p = jnp.exp2(s_p - m)
        p_n = jnp.exp2(s_n - m)
        l = jnp.sum(p_p, axis=1, keepdims=True) + jnp.sum(p_n, axis=1, keepdims=True)
        acc = (jnp.dot(p_p.astype(BF16), vp, preferred_element_type=F32)
               + jnp.dot(p_n.astype(BF16), vn_ref[:, cols], preferred_element_type=F32))
        outs.append(_finish_head(acc, l, lam, g_ref[...], za_ref[:, cols], t))
    o_ref[...] = jnp.concatenate(outs, axis=1).astype(BF16)


def _attn_sample(lams, g, q, kb, vb, ck, cv, za, *, n_seq, t, past):
    assert past % CHUNK == 0 and t <= CHUNK
    row = pl.BlockSpec((t, D_ATT), lambda b: (b, 0))
    kcache = pl.BlockSpec((None, ATT_HEADS, 2, QK_DIM, past), lambda b: (b, 0, 0, 0, 0))
    vcache = pl.BlockSpec((None, past * ATT_HEADS, V_DIM), lambda b: (b, 0, 0))
    vec = _full((1, QK_DIM))
    return pl.pallas_call(
        functools.partial(_attn_sample_kernel, t=t, past=past),
        grid=(n_seq,),
        in_specs=[vec, vec, vec, vec, _full((1, V_DIM)), row, row, row, kcache, vcache, row],
        out_specs=row,
        out_shape=jax.ShapeDtypeStruct((n_seq * t, D_ATT), BF16),
        compiler_params=_params(("arbitrary",)),
        name="attn_sample",
    )(*lams, g, q, kb, vb, ck, cv, za)


def _out_proj_kernel(s_ref, a_ref, w_ref, x_ref, g_ref, y_ref):
    tm = s_ref.shape[0]
    step = min(tm, OUT_CHUNK)
    for r in range(0, tm, step):
        rows = slice(r, r + step)
        mix = (jnp.dot(s_ref[rows, :], w_ref[:D_SSM, :], preferred_element_type=F32)
               + jnp.dot(a_ref[rows, :], w_ref[D_SSM:, :], preferred_element_type=F32))
        ms = jnp.mean(mix * mix, axis=-1, keepdims=True)
        y_ref[rows, :] = x_ref[rows, :] + mix * lax.rsqrt(ms + EPS) * g_ref[...]


def _out_proj(ssm_out, att_out, w_bf, x2d, g, *, tm):
    n = x2d.shape[0]
    assert n % tm == 0 and tm % min(tm, OUT_CHUNK) == 0
    row = lambda i: (i, 0)
    half = pl.BlockSpec((tm, 512), row)
    full = pl.BlockSpec((tm, D_MODEL), row)
    return pl.pallas_call(
        _out_proj_kernel,
        grid=(n // tm,),
        in_specs=[half, half, _full((D_MODEL, D_MODEL)), full, _full((1, D_MODEL))],
        out_specs=full,
        out_shape=jax.ShapeDtypeStruct((n, D_MODEL), F32),
        compiler_params=_params(("arbitrary",)),
        name="out_proj",
    )(ssm_out, att_out, w_bf, x2d, g)


def _block_diag_weights(bbr, bbi, c_re, c_im):
    eye = jnp.eye(SSM_GROUPS // 2, dtype=F32)

    def b_side(m):
        m = m.reshape(2, SSM_GROUPS // 2, SSM_GROUP, SSM_STATE)
        return jnp.einsum('bgcp,gh->bgchp', m, eye).reshape(2, 256, N_CH // 2)

    def c_side(m):
        m = m.reshape(2, SSM_GROUPS // 2, SSM_GROUP, SSM_STATE)
        return jnp.einsum('bgcp,gh->bgphc', m, eye).reshape(2, N_CH // 2, 256)

    bdb = jnp.concatenate([b_side(bbr), b_side(bbi)], axis=2).astype(BF16)
    bdc = jnp.concatenate([c_side(c_re), -c_side(c_im)], axis=1).astype(BF16)
    return bdb, bdc


def kernel(x_prompt, x_sample, cache_k, cache_v, state_ssm_re, state_ssm_im, norm_pre_g, w_in, ssm_lambda_re,
           ssm_lambda_im, ssm_log_dt, ssm_b_re, ssm_b_im, ssm_c_re, ssm_c_im, ssm_d, glu_w1, glu_b1, glu_w2,
           glu_b2, lambda_q1, lambda_k1, lambda_q2, lambda_k2, attn_subln_g, w_out, norm_post_g):
    bp, sp, _ = x_prompt.shape
    bs, ss, _ = x_sample.shape
    past = cache_k.shape[2]

    a_re, a_im, bbr, bbi = _prep(ssm_lambda_re[0], ssm_lambda_im[0], ssm_log_dt[0], ssm_b_re[0], ssm_b_im[0])
    bdb, bdc = _block_diag_weights(bbr, bbi, ssm_c_re[0], ssm_c_im[0])
    a2 = jnp.concatenate([a_re.reshape(N_SLAB, LANES), a_im.reshape(N_SLAB, LANES)], axis=0)
    dvec = ssm_d[0].reshape(1, D_SSM)
    w_in_bf = w_in[0].astype(BF16)
    w_out_bf = w_out[0].astype(BF16)
    w1 = glu_w1[0].astype(BF16)
    w2 = glu_w2[0].astype(BF16)
    b1 = glu_b1[0].reshape(1, D_SSM)
    b2 = glu_b2[0].reshape(1, D_SSM)
    g_pre = norm_pre_g[0].reshape(1, D_MODEL)
    g_post = norm_post_g[0].reshape(1, D_MODEL)
    g_sub = attn_subln_g[0].reshape(1, V_DIM)
    lams = tuple(v[0].reshape(1, QK_DIM) for v in (lambda_q1, lambda_k1, lambda_q2, lambda_k2))
    inv = ROPE_THETA ** (-jnp.arange(ROPE_DIM // 2, dtype=F32) * 2.0 / ROPE_DIM)
    rotary_lane = (jnp.arange(LANES) % QK_DIM) < ROPE_DIM
    inv_lane = jnp.where(rotary_lane, jnp.tile(inv, LANES // (ROPE_DIM // 2)), 0.0).reshape(1, LANES)

    def run(x, n_seq, seq_len, pos0, chained, h0, tm):
        x2d = x.reshape(n_seq * seq_len, D_MODEL)
        u, zs, q, kf, kb, vf, vb, za = _in_proj(x2d, g_pre, w_in_bf, inv_lane, seq_len=seq_len, pos0=pos0, tm=tm,
                                                transposed_qv=chained)
        ssm_out, hf = _s5(u, zs, bdb, bdc, dvec, a2, w1, b1, w2, b2, h0,
                          n_seq=n_seq, seq_len=seq_len, chained=chained)
        if chained:
            att = _attn_prompt(lams, g_sub, q, kb, vb, za, n_seq=n_seq, seq_len=seq_len)
            k_out = jnp.transpose(kf, (0, 4, 1, 2, 3))[None]
        else:
            ck = jnp.transpose(cache_k[0], (0, 2, 3, 4, 1))
            cv = cache_v[0].reshape(n_seq, past * ATT_HEADS, V_DIM)
            att = _attn_sample(lams, g_sub, q, kb, vb, ck, cv, za, n_seq=n_seq, t=seq_len, past=past)
            k_out = kf.reshape(1, n_seq, seq_len, ATT_HEADS, 2, QK_DIM)
        y = _out_proj(ssm_out, att, w_out_bf, x2d, g_post, tm=tm)
        return (y.reshape(n_seq, seq_len, D_MODEL),
                k_out,
                vf.reshape(1, n_seq, seq_len, ATT_HEADS, V_DIM),
                hf[:, :N_SLAB].reshape(1, n_seq, SSM_GROUPS, SSM_STATE),
                hf[:, N_SLAB:].reshape(1, n_seq, SSM_GROUPS, SSM_STATE))

    yp, kp, vp, hrp, hip = run(x_prompt, bp, sp, 0, True, jnp.zeros((bp, 2 * N_SLAB, LANES), F32), 512)
    h0 = jnp.concatenate([state_ssm_re[0].reshape(bs, N_SLAB, LANES), state_ssm_im[0].reshape(bs, N_SLAB, LANES)],
                         axis=1)
    ys, ks, vs, hrs, his = run(x_sample, bs, ss, past, False, h0, bs * ss)
    return (yp, ys, kp, vp, hrp, hip, ks, vs, hrs, his)
```

```python
import functools
import math

import jax
import jax.numpy as jnp
from jax import lax
from jax.experimental import pallas as pl
from jax.experimental.pallas import tpu as pltpu

F32 = jnp.float32
BF16 = jnp.bfloat16

D_MODEL = 1024
D_SSM = 512
D_ATT = 512
SSM_GROUP = 16
SSM_GROUPS = 32
SSM_STATE = 64
N_CH = SSM_GROUPS * SSM_STATE
ATT_HEADS = 4
QK_DIM = 64
V_DIM = 128
ROPE_DIM = 16
ROPE_THETA = 500000.0
CHUNK = 64
EPS = 1e-6
D_IN = 3072
LAMBDA_INIT = 0.8 - 0.6 * math.exp(-0.3 * 0)

LANES = 128
SUBLANES = 8
N_SLAB = N_CH // LANES
OUT_CHUNK = 256
S5_STEPS = 256
SLAB_PAD = 4
NEG = -1e30
ATT_TILE = 256
HEADS_PER_STEP = 4
SUM_ROWS = 16
LOG2E = math.log2(math.e)
Q_SCALE = QK_DIM ** -0.5 * LOG2E
VMEM_LIMIT = 56 * 1024 * 1024


def _params(sem):
    return pltpu.CompilerParams(dimension_semantics=sem, vmem_limit_bytes=VMEM_LIMIT)


def _full(shape):
    n = len(shape)
    return pl.BlockSpec(shape, lambda *_: (0,) * n)


def _prep_kernel(lr_ref, li_ref, ldt_ref, br_ref, bi_ref, ar_ref, ai_ref, bbr_ref, bbi_ref):
    lr = lr_ref[...]
    li = li_ref[...]
    dt = jnp.exp(ldt_ref[...])
    mag = jnp.exp(lr * dt)
    ar = mag * jnp.cos(li * dt)
    ai = mag * jnp.sin(li * dt)
    den = lr * lr + li * li
    cr = ((ar - 1.0) * lr + ai * li) / den
    ci = (ai * lr - (ar - 1.0) * li) / den
    ar_ref[...] = ar
    ai_ref[...] = ai
    br = br_ref[...]
    bi = bi_ref[...]
    crb = cr[:, None, :]
    cib = ci[:, None, :]
    bbr_ref[...] = crb * br - cib * bi
    bbi_ref[...] = crb * bi + cib * br


def _prep(lam_re, lam_im, log_dt, b_re, b_im):
    g, p, c = b_re.shape
    brt = jnp.swapaxes(b_re, 1, 2)
    bit = jnp.swapaxes(b_im, 1, 2)
    out_shape = (jax.ShapeDtypeStruct((g, p), F32), jax.ShapeDtypeStruct((g, p), F32),
                 jax.ShapeDtypeStruct((g, c, p), F32), jax.ShapeDtypeStruct((g, c, p), F32))
    return pl.pallas_call(_prep_kernel, out_shape=out_shape, name="s5_prep")(
        lam_re, lam_im, log_dt.reshape(g, 1), brt, bit)


def _in_proj_kernel(x_ref, g_ref, w_ref, inv_ref, u_ref, zs_ref, q_ref, kf_ref, kb_ref, vf_ref,
                    vb_ref, za_ref, cl_sc, sl_sc, *, tm, seq_len, pos0, transposed_qv):
    i = pl.program_id(0)
    x = x_ref[...]
    ms = jnp.mean(x * x, axis=-1, keepdims=True)
    hn = (x * lax.rsqrt(ms + EPS) * g_ref[...]).astype(BF16)

    inv = inv_ref[...]

    @pl.when(i == 0)
    def _():
        off = (lax.broadcasted_iota(jnp.int32, (tm, LANES), 0) & (seq_len - 1)).astype(F32) * inv
        cl_sc[...] = jnp.cos(off)
        sl_sc[...] = jnp.sin(off)

    base = (pos0 + ((i * tm) & (seq_len - 1))).astype(F32) * jnp.broadcast_to(inv, (SUBLANES, LANES))
    cb = jnp.cos(base)[:1]
    sb = jnp.sin(base)[:1]
    cl = cl_sc[...]
    sl = sl_sc[...]
    c_m = cb * cl - sb * sl
    sin = sb * cl + cb * sl
    lane = lax.broadcasted_iota(jnp.int32, (tm, LANES), 1) & (QK_DIM - 1)
    half = ROPE_DIM // 2
    s_lo = jnp.where(lane < half, -sin, 0.0)
    s_hi = jnp.where(lane >= half, sin, 0.0)

    def seg(lo, hi):
        return jnp.dot(hn, w_ref[:, lo:hi], preferred_element_type=F32)

    def rope(t):
        outs = []
        for h in range(ATT_HEADS):
            th = t[:, h * LANES:(h + 1) * LANES]
            outs.append(th * c_m + pltpu.roll(th, LANES - half, 1) * s_lo + pltpu.roll(th, half, 1) * s_hi)
        return jnp.concatenate(outs, axis=1)

    def put(ref, t):
        if not transposed_qv:
            ref[...] = t.astype(BF16)
            return
        for h in range(ATT_HEADS):
            tt = t[:, h * LANES:(h + 1) * LANES].T.astype(BF16)
            for c in range(tm // ATT_TILE):
                ref[h, c] = tt[:, c * ATT_TILE:(c + 1) * ATT_TILE]

    u_ref[...] = seg(0, D_SSM).astype(BF16)
    zs_ref[...] = seg(D_SSM, 2 * D_SSM).astype(BF16)
    q = rope(seg(1024, 1536))
    put(q_ref, q * Q_SCALE)
    k = rope(seg(1536, 2048))
    if transposed_qv:
        for h in range(ATT_HEADS):
            kt = k[:, h * LANES:(h + 1) * LANES].T
            kf_ref[h, 0] = kt[:QK_DIM]
            kf_ref[h, 1] = kt[QK_DIM:]
    else:
        kf_ref[...] = k
    kb_ref[...] = k.astype(BF16)
    v = seg(2048, 2560)
    for h in range(ATT_HEADS):
        vf_ref[pl.ds(h, tm, stride=ATT_HEADS), :] = v[:, h * LANES:(h + 1) * LANES]
    put(vb_ref, v)
    za_ref[...] = seg(2560, 3072).astype(BF16)


def _in_proj(x2d, g, w_bf, inv_lane, *, seq_len, pos0, tm, transposed_qv):
    n = x2d.shape[0]
    assert n % tm == 0 and seq_len & (seq_len - 1) == 0 and (tm % seq_len == 0 or seq_len % tm == 0)
    row = lambda i: (i, 0)
    o512 = pl.BlockSpec((tm, 512), row)
    shp = lambda dt: jax.ShapeDtypeStruct((n, 512), dt)
    if transposed_qv:
        assert seq_len % tm == 0 and tm % ATT_TILE == 0
        tps = seq_len // tm
        per = tm // ATT_TILE
        t_spec = pl.BlockSpec((None, ATT_HEADS, per, LANES, ATT_TILE), lambda i: (i // tps, 0, i % tps, 0, 0))
        t_shape = jax.ShapeDtypeStruct((n // seq_len, ATT_HEADS, seq_len // ATT_TILE, LANES, ATT_TILE), BF16)
        kf_spec = pl.BlockSpec((None, ATT_HEADS, 2, QK_DIM, tm), lambda i: (i // tps, 0, 0, 0, i % tps))
        kf_shape = jax.ShapeDtypeStruct((n // seq_len, ATT_HEADS, 2, QK_DIM, seq_len), F32)
    else:
        t_spec, t_shape = o512, shp(BF16)
        kf_spec, kf_shape = o512, shp(F32)
    vf_spec = pl.BlockSpec((tm * ATT_HEADS, V_DIM), row)
    vf_shape = jax.ShapeDtypeStruct((n * ATT_HEADS, V_DIM), F32)
    return pl.pallas_call(
        functools.partial(_in_proj_kernel, tm=tm, seq_len=seq_len, pos0=pos0, transposed_qv=transposed_qv),
        grid=(n // tm,),
        in_specs=[pl.BlockSpec((tm, D_MODEL), row), _full((1, D_MODEL)), _full((D_MODEL, D_IN)),
                  _full((1, LANES))],
        out_specs=[o512, o512, t_spec, kf_spec, o512, vf_spec, t_spec, o512],
        out_shape=[shp(BF16), shp(BF16), t_shape, kf_shape, shp(BF16), vf_shape, t_shape, shp(BF16)],
        scratch_shapes=[pltpu.VMEM((tm, LANES), F32), pltpu.VMEM((tm, LANES), F32)],
        compiler_params=_params(("arbitrary",)),
        name="in_proj",
    )(x2d, g, w_bf, inv_lane)


def _cmul(ar, ai, br, bi):
    return ar * br - ai * bi, ar * bi + ai * br


def _s5_kernel(u_ref, up_ref, zsp_ref, bdb_ref, bdc_ref, d_ref, a_ref, w1_ref, b1_ref, w2_ref, b2_ref, h0_ref,
               out_ref, hf_ref, hbuf, hout_a, hout_b, hb16, car, *, n_chain, T, n_steps, chained, chain_group):
    n_rows = n_chain * T
    P = n_rows + SLAB_PAD
    half_cols = N_CH // 2
    i = pl.program_id(0)
    blocks = range(N_SLAB // SUBLANES)

    def project_and_scan(hout):
        u = u_ref[...].reshape(n_rows, D_SSM)
        for b in range(2):
            bu = jnp.dot(u[:, b * 256:(b + 1) * 256], bdb_ref[b], preferred_element_type=F32)
            for part in range(2):
                for k in range(N_SLAB // 2):
                    slab = part * N_SLAB + b * (N_SLAB // 2) + k
                    col = part * half_cols + k * LANES
                    hbuf[slab * P:slab * P + n_rows, :] = bu[:, col:col + LANES]
        a = [(a_ref[SUBLANES * k:SUBLANES * (k + 1), :], a_ref[N_SLAB + SUBLANES * k:N_SLAB + SUBLANES * (k + 1), :])
             for k in blocks]
        for g0 in range(0, n_chain, chain_group):
            chains = list(range(g0, g0 + chain_group))
            st = {(c, k): (car[c, SUBLANES * k:SUBLANES * (k + 1), :],
                           car[c, N_SLAB + SUBLANES * k:N_SLAB + SUBLANES * (k + 1), :])
                  for c in chains for k in blocks}
            for t in range(T):
                for c in chains:
                    for k in blocks:
                        hr, hi = st[(c, k)]
                        rows_r = pl.ds(SUBLANES * k * P + c * T + t, SUBLANES, stride=P)
                        rows_i = pl.ds((N_SLAB + SUBLANES * k) * P + c * T + t, SUBLANES, stride=P)
                        pr, pi = _cmul(a[k][0], a[k][1], hr, hi)
                        nr = pr + hbuf[rows_r, :]
                        ni = pi + hbuf[rows_i, :]
                        hout[rows_r, :] = nr
                        hout[rows_i, :] = ni
                        st[(c, k)] = (nr, ni)
            for c in chains:
                for k in blocks:
                    hr, hi = st[(c, k)]
                    car[c, SUBLANES * k:SUBLANES * (k + 1), :] = hr
                    car[c, N_SLAB + SUBLANES * k:N_SLAB + SUBLANES * (k + 1), :] = hi

    def output_stage(hout):
        up = up_ref[...].reshape(n_rows, D_SSM)
        for b in range(2):
            for part in range(2):
                for k in range(N_SLAB // 2):
                    slab = part * N_SLAB + b * (N_SLAB // 2) + k
                    col = b * N_CH + part * half_cols + k * LANES
                    hb16[:, col:col + LANES] = hout[slab * P:slab * P + n_rows, :].astype(BF16)
        ys = [jnp.dot(hb16[:, b * N_CH:(b + 1) * N_CH], bdc_ref[b], preferred_element_type=F32) for b in range(2)]
        y = jnp.concatenate(ys, axis=1) + up.astype(F32) * d_ref[...]
        gb = jax.nn.gelu(y).astype(BF16)
        y1 = jnp.dot(gb, w1_ref[...], preferred_element_type=F32) + b1_ref[...]
        y2 = jnp.dot(gb, w2_ref[...], preferred_element_type=F32) + b2_ref[...]
        zs = zsp_ref[...].reshape(n_rows, D_SSM).astype(F32)
        out_ref[...] = (y1 * jax.nn.sigmoid(y2) * jax.nn.silu(zs)).astype(BF16).reshape(n_chain, T, D_SSM)

    @pl.when(i == 0)
    def _():
        car[...] = jnp.zeros_like(car) if chained else h0_ref[...]
        project_and_scan(hout_a)

    if n_steps > 1:
        @pl.when((i > 0) & (i < n_steps) & ((i & 1) == 1))
        def _():
            project_and_scan(hout_b)
            output_stage(hout_a)

        @pl.when((i > 0) & (i < n_steps) & ((i & 1) == 0))
        def _():
            project_and_scan(hout_a)
            output_stage(hout_b)

    @pl.when(i == n_steps)
    def _():
        output_stage(hout_a if (n_steps - 1) % 2 == 0 else hout_b)
        hf_ref[...] = car[...]


def _s5(u, zs, bdb, bdc, dvec, a2, w1, b1, w2, b2, h0, *, n_seq, seq_len, chained):
    if chained:
        T = S5_STEPS
        chain_group = n_seq
    else:
        T = seq_len
        chain_group = 4
    assert seq_len % T == 0 and n_seq % chain_group == 0 and T % (2 * SUBLANES) == 0
    n_steps = seq_len // T
    n_rows = n_seq * T
    u3 = u.reshape(n_seq, seq_len, D_SSM)
    zs3 = zs.reshape(n_seq, seq_len, D_SSM)
    cur = pl.BlockSpec((n_seq, T, D_SSM), lambda t: (0, jnp.minimum(t, n_steps - 1), 0))
    prev = pl.BlockSpec((n_seq, T, D_SSM), lambda t: (0, jnp.maximum(t - 1, 0), 0))
    st_shape = (n_seq, 2 * N_SLAB, LANES)
    slabs = pltpu.VMEM((2 * N_SLAB * (n_rows + SLAB_PAD), LANES), F32)
    out, hf = pl.pallas_call(
        functools.partial(_s5_kernel, n_chain=n_seq, T=T, n_steps=n_steps, chained=chained,
                          chain_group=chain_group),
        grid=(n_steps + 1,),
        in_specs=[cur, prev, prev, _full(bdb.shape), _full(bdc.shape), _full((1, D_SSM)),
                  _full((2 * N_SLAB, LANES)), _full((D_SSM, D_SSM)), _full((1, D_SSM)), _full((D_SSM, D_SSM)),
                  _full((1, D_SSM)), _full(st_shape)],
        out_specs=[prev, _full(st_shape)],
        out_shape=[jax.ShapeDtypeStruct((n_seq, seq_len, D_SSM), BF16), jax.ShapeDtypeStruct(st_shape, F32)],
        scratch_shapes=[slabs, slabs, slabs, pltpu.VMEM((n_rows, 2 * N_CH), BF16), pltpu.VMEM(st_shape, F32)],
        compiler_params=_params(("arbitrary",)),
        name="s5_chained" if chained else "s5_independent",
    )(u3, u3, zs3, bdb, bdc, dvec, a2, w1, b1, w2, b2, h0)
    return out.reshape(n_seq * seq_len, D_SSM), hf


def _lambda(lq1, lk1, lq2, lk2):
    s1 = jnp.sum(lq1[...] * lk1[...], axis=1, keepdims=True)
    s2 = jnp.sum(lq2[...] * lk2[...], axis=1, keepdims=True)
    return jnp.exp(s1) - jnp.exp(s2) + LAMBDA_INIT


def _stack_maps(q):
    lane = lax.broadcasted_iota(jnp.int32, q.shape, 1)
    zero = jnp.zeros_like(q)
    return jnp.concatenate([jnp.where(lane < QK_DIM, q, zero), jnp.where(lane >= QK_DIM, q, zero)], axis=0)


def _subln_gate(o, g, za):
    ms = jnp.mean(o * o, axis=-1, keepdims=True)
    on = (o * lax.rsqrt(ms + EPS) * g) * (1.0 - LAMBDA_INIT)
    return on * jax.nn.silu(za.astype(F32))


def _finish_head(acc, l, lam, g, za, t):
    inv = 1.0 / l
    o = acc[:t] * inv[:t] - lam * (acc[t:] * inv[t:])
    return _subln_gate(o, g, za)


_NT = (((1,), (1,)), ((), ()))


def _attn_kernel(lq1, lk1, lq2, lk2, g_ref, qt_ref, k_ref, vt_ref, za_ref, o_ref, m_sc, acc_sc, s_a, s_b, *, nq):
    tq = ATT_TILE
    step = pl.program_id(2)

    def finish():
        lam = _lambda(lq1, lk1, lq2, lk2)
        for h in range(HEADS_PER_STEP):
            acc = acc_sc[h, :V_DIM, :]
            inv = 1.0 / acc_sc[h, V_DIM:V_DIM + 1, :]
            ot = acc[:, :tq] * inv[:, :tq] - lam * (acc[:, tq:] * inv[:, tq:])
            cols = slice(h * LANES, (h + 1) * LANES)
            o_ref[:, cols] = _subln_gate(ot.T, g_ref[...], za_ref[:, cols]).astype(BF16)

    @pl.when((pl.program_id(0) == 0) & (pl.program_id(1) == 0) & (step == 0))
    def _():
        acc_sc[...] = jnp.ones_like(acc_sc)

    @pl.when(step < nq)
    def _():
        finish()
        _attn_tile(step, qt_ref, k_ref, vt_ref, m_sc, acc_sc, s_a, s_b)

    @pl.when(step == nq)
    def _():
        finish()


def _attn_tile(qi, qt_ref, k_ref, vt_ref, m_sc, acc_sc, s_a, s_b):
    tq = ATT_TILE
    q2t = []
    for h in range(HEADS_PER_STEP):
        qt = qt_ref[h]
        row = lax.broadcasted_iota(jnp.int32, qt.shape, 0)
        zero = jnp.zeros_like(qt)
        q2t.append(jnp.concatenate([jnp.where(row < QK_DIM, qt, zero), jnp.where(row >= QK_DIM, qt, zero)], axis=1))
    m_sc[...] = jnp.full_like(m_sc, NEG)
    acc_sc[...] = jnp.zeros_like(acc_sc)
    ones = jnp.ones((SUM_ROWS, tq), BF16)
    heads = range(HEADS_PER_STEP)
    tk = 2 * tq
    n_full = qi // 2
    odd = (qi & 1) == 1

    def scores_into(s_ref, kt, h):
        start = pl.multiple_of(kt * tk, tk)
        s_ref[h] = jnp.dot(k_ref[pl.ds(start, tk), h * LANES:(h + 1) * LANES], q2t[h],
                           preferred_element_type=F32)

    def softmax_accumulate(s_ref, kt, nkeys, masked, h):
        s = s_ref[h] if nkeys == tk else s_ref[h, :nkeys, :]
        if masked:
            kc = lax.broadcasted_iota(jnp.int32, s.shape, 0) // CHUNK + kt * (tk // CHUNK)
            qc = (lax.broadcasted_iota(jnp.int32, s.shape, 1) & (tq - 1)) // CHUNK + qi * (tq // CHUNK)
            s = jnp.where(kc <= qc, s, NEG)
        m_old = m_sc[h]
        m_new = jnp.maximum(m_old, jnp.max(s, axis=0, keepdims=True))
        alpha = jnp.exp2(m_old - m_new)
        pb = jnp.exp2(s - m_new).astype(BF16)
        pv = jnp.dot(jnp.concatenate([vt_ref[h, 2 * kt], ones], axis=0), pb[:tq], preferred_element_type=F32)
        if nkeys == tk:
            pv = pv + jnp.dot(jnp.concatenate([vt_ref[h, 2 * kt + 1], ones], axis=0), pb[tq:],
                              preferred_element_type=F32)
        acc_sc[h] = alpha * acc_sc[h] + pv
        m_sc[h] = m_new

    def stage(kt, s_cur, s_nxt):
        for h in heads:
            scores_into(s_nxt, kt + 1, h)
            softmax_accumulate(s_cur, kt, tk, False, h)

    for h in heads:
        scores_into(s_a, 0, h)

    def body(kt, c):
        even = (kt & 1) == 0

        @pl.when(even)
        def _():
            stage(kt, s_a, s_b)

        @pl.when(jnp.logical_not(even))
        def _():
            stage(kt, s_b, s_a)

        return c

    lax.fori_loop(0, n_full, body, 0)

    for parity, s_ref in ((0, s_a), (1, s_b)):
        here = (n_full & 1) == parity

        @pl.when(here & odd)
        def _(s_ref=s_ref):
            for h in heads:
                softmax_accumulate(s_ref, n_full, tk, True, h)

        @pl.when(here & jnp.logical_not(odd))
        def _(s_ref=s_ref):
            for h in heads:
                softmax_accumulate(s_ref, n_full, tq, True, h)


def _attn_prompt(lams, g, qt, kb, vt, za, *, n_seq, seq_len):
    tq = ATT_TILE
    nq = seq_len // tq
    hps = HEADS_PER_STEP
    assert tq % CHUNK == 0 and tq & (tq - 1) == 0 and ATT_HEADS % hps == 0 and seq_len % (2 * tq) == 0
    rowspec = pl.BlockSpec((tq, hps * LANES), lambda b, h, i: (b * nq + jnp.maximum(i - 1, 0), h))
    qtspec = pl.BlockSpec((None, hps, None, LANES, tq), lambda b, h, i: (b, h, jnp.minimum(i, nq - 1), 0, 0))
    kspec = pl.BlockSpec((seq_len, hps * LANES), lambda b, h, i: (b, h))
    vtspec = pl.BlockSpec((None, hps, nq, LANES, tq), lambda b, h, i: (b, h, 0, 0, 0))
    vec = _full((1, QK_DIM))
    stat = pltpu.VMEM((hps, 1, 2 * tq), F32)
    return pl.pallas_call(
        functools.partial(_attn_kernel, nq=nq),
        grid=(n_seq, ATT_HEADS // hps, nq + 1),
        in_specs=[vec, vec, vec, vec, _full((1, V_DIM)), qtspec, kspec, vtspec, rowspec],
        out_specs=rowspec,
        out_shape=jax.ShapeDtypeStruct((n_seq * seq_len, D_ATT), BF16),
        scratch_shapes=[stat, pltpu.VMEM((hps, V_DIM + SUM_ROWS, 2 * tq), F32),
                        pltpu.VMEM((hps, 2 * tq, 2 * tq), F32), pltpu.VMEM((hps, 2 * tq, 2 * tq), F32)],
        compiler_params=_params(("arbitrary", "arbitrary", "arbitrary")),
        name="attn_prompt",
    )(*lams, g, qt, kb, vt, za)


def _attn_sample_kernel(lq1, lk1, lq2, lk2, g_ref, q_ref, kn_ref, vn_ref, ck_ref, cv_ref, za_ref, o_ref, *, t, past):
    lam = _lambda(lq1, lk1, lq2, lk2)
    outs = []
    for h in range(ATT_HEADS):
        cols = slice(h * LANES, (h + 1) * LANES)
        q2 = _stack_maps(q_ref[:, cols])
        kpt = jnp.concatenate([ck_ref[h, 0], ck_ref[h, 1]], axis=0).astype(BF16)
        vp = cv_ref[pl.ds(h, past, stride=ATT_HEADS), :].astype(BF16)
        s_p = jnp.dot(q2, kpt, preferred_element_type=F32)
        s_n = lax.dot_general(q2, kn_ref[:, cols], _NT, preferred_element_type=F32)
        m = jnp.maximum(jnp.max(s_p, axis=1, keepdims=True), jnp.max(s_n, axis=1, keepdims=True))
        p_p = jnp.exp2(s_p - m)
        p_n = jnp.exp2(s_n - m)
        l = jnp.sum(p_p, axis=1, keepdims=True) + jnp.sum(p_n, axis=1, keepdims=True)
        acc = (jnp.dot(p_p.astype(BF16), vp, preferred_element_type=F32)
               + jnp.dot(p_n.astype(BF16), vn_ref[:, cols], preferred_element_type=F32))
        outs.append(_finish_head(acc, l, lam, g_ref[...], za_ref[:, cols], t))
    o_ref[...] = jnp.concatenate(outs, axis=1).astype(BF16)


def _attn_sample(lams, g, q, kb, vb, ck, cv, za, *, n_seq, t, past):
    assert past % CHUNK == 0 and t <= CHUNK
    row = pl.BlockSpec((t, D_ATT), lambda b: (b, 0))
    kcache = pl.BlockSpec((None, ATT_HEADS, 2, QK_DIM, past), lambda b: (b, 0, 0, 0, 0))
    vcache = pl.BlockSpec((None, past * ATT_HEADS, V_DIM), lambda b: (b, 0, 0))
    vec = _full((1, QK_DIM))
    return pl.pallas_call(
        functools.partial(_attn_sample_kernel, t=t, past=past),
        grid=(n_seq,),
        in_specs=[vec, vec, vec, vec, _full((1, V_DIM)), row, row, row, kcache, vcache, row],
        out_specs=row,
        out_shape=jax.ShapeDtypeStruct((n_seq * t, D_ATT), BF16),
        compiler_params=_params(("arbitrary",)),
        name="attn_sample",
    )(*lams, g, q, kb, vb, ck, cv, za)


def _out_proj_kernel(s_ref, a_ref, w_ref, x_ref, g_ref, y_ref):
    tm = s_ref.shape[0]
    step = min(tm, OUT_CHUNK)
    for r in range(0, tm, step):
        rows = slice(r, r + step)
        mix = (jnp.dot(s_ref[rows, :], w_ref[:D_SSM, :], preferred_element_type=F32)
               + jnp.dot(a_ref[rows, :], w_ref[D_SSM:, :], preferred_element_type=F32))
        ms = jnp.mean(mix * mix, axis=-1, keepdims=True)
        y_ref[rows, :] = x_ref[rows, :] + mix * lax.rsqrt(ms + EPS) * g_ref[...]


def _out_proj(ssm_out, att_out, w_bf, x2d, g, *, tm):
    n = x2d.shape[0]
    assert n % tm == 0 and tm % min(tm, OUT_CHUNK) == 0
    row = lambda i: (i, 0)
    half = pl.BlockSpec((tm, 512), row)
    full = pl.BlockSpec((tm, D_MODEL), row)
    return pl.pallas_call(
        _out_proj_kernel,
        grid=(n // tm,),
        in_specs=[half, half, _full((D_MODEL, D_MODEL)), full, _full((1, D_MODEL))],
        out_specs=full,
        out_shape=jax.ShapeDtypeStruct((n, D_MODEL), F32),
        compiler_params=_params(("arbitrary",)),
        name="out_proj",
    )(ssm_out, att_out, w_bf, x2d, g)


def _block_diag_weights(bbr, bbi, c_re, c_im):
    eye = jnp.eye(SSM_GROUPS // 2, dtype=F32)

    def b_side(m):
        m = m.reshape(2, SSM_GROUPS // 2, SSM_GROUP, SSM_STATE)
        return jnp.einsum('bgcp,gh->bgchp', m, eye).reshape(2, 256, N_CH // 2)

    def c_side(m):
        m = m.reshape(2, SSM_GROUPS // 2, SSM_GROUP, SSM_STATE)
        return jnp.einsum('bgcp,gh->bgphc', m, eye).reshape(2, N_CH // 2, 256)

    bdb = jnp.concatenate([b_side(bbr), b_side(bbi)], axis=2).astype(BF16)
    bdc = jnp.concatenate([c_side(c_re), -c_side(c_im)], axis=1).astype(BF16)
    return bdb, bdc


def kernel(x_prompt, x_sample, cache_k, cache_v, state_ssm_re, state_ssm_im, norm_pre_g, w_in, ssm_lambda_re,
           ssm_lambda_im, ssm_log_dt, ssm_b_re, ssm_b_im, ssm_c_re, ssm_c_im, ssm_d, glu_w1, glu_b1, glu_w2,
           glu_b2, lambda_q1, lambda_k1, lambda_q2, lambda_k2, attn_subln_g, w_out, norm_post_g):
    bp, sp, _ = x_prompt.shape
    bs, ss, _ = x_sample.shape
    past = cache_k.shape[2]

    a_re, a_im, bbr, bbi = _prep(ssm_lambda_re[0], ssm_lambda_im[0], ssm_log_dt[0], ssm_b_re[0], ssm_b_im[0])
    bdb, bdc = _block_diag_weights(bbr, bbi, ssm_c_re[0], ssm_c_im[0])
    a2 = jnp.concatenate([a_re.reshape(N_SLAB, LANES), a_im.reshape(N_SLAB, LANES)], axis=0)
    dvec = ssm_d[0].reshape(1, D_SSM)
    w_in_bf = w_in[0].astype(BF16)
    w_out_bf = w_out[0].astype(BF16)
    w1 = glu_w1[0].astype(BF16)
    w2 = glu_w2[0].astype(BF16)
    b1 = glu_b1[0].reshape(1, D_SSM)
    b2 = glu_b2[0].reshape(1, D_SSM)
    g_pre = norm_pre_g[0].reshape(1, D_MODEL)
    g_post = norm_post_g[0].reshape(1, D_MODEL)
    g_sub = attn_subln_g[0].reshape(1, V_DIM)
    lams = tuple(v[0].reshape(1, QK_DIM) for v in (lambda_q1, lambda_k1, lambda_q2, lambda_k2))
    inv = ROPE_THETA ** (-jnp.arange(ROPE_DIM // 2, dtype=F32) * 2.0 / ROPE_DIM)
    rotary_lane = (jnp.arange(LANES) % QK_DIM) < ROPE_DIM
    inv_lane = jnp.where(rotary_lane, jnp.tile(inv, LANES // (ROPE_DIM // 2)), 0.0).reshape(1, LANES)

    def run(x, n_seq, seq_len, pos0, chained, h0, tm):
        x2d = x.reshape(n_seq * seq_len, D_MODEL)
        u, zs, q, kf, kb, vf, vb, za = _in_proj(x2d, g_pre, w_in_bf, inv_lane, seq_len=seq_len, pos0=pos0, tm=tm,
                                                transposed_qv=chained)
        ssm_out, hf = _s5(u, zs, bdb, bdc, dvec, a2, w1, b1, w2, b2, h0,
                          n_seq=n_seq, seq_len=seq_len, chained=chained)
        if chained:
            att = _attn_prompt(lams, g_sub, q, kb, vb, za, n_seq=n_seq, seq_len=seq_len)
            k_out = jnp.transpose(kf, (0, 4, 1, 2, 3))[None]
        else:
            ck = jnp.transpose(cache_k[0], (0, 2, 3, 4, 1))
            cv = cache_v[0].reshape(n_seq, past * ATT_HEADS, V_DIM)
            att = _attn_sample(lams, g_sub, q, kb, vb, ck, cv, za, n_seq=n_seq, t=seq_len, past=past)
            k_out = kf.reshape(1, n_seq, seq_len, ATT_HEADS, 2, QK_DIM)
        y = _out_proj(ssm_out, att, w_out_bf, x2d, g_post, tm=tm)
        return (y.reshape(n_seq, seq_len, D_MODEL),
                k_out,
                vf.reshape(1, n_seq, seq_len, ATT_HEADS, V_DIM),
                hf[:, :N_SLAB].reshape(1, n_seq, SSM_GROUPS, SSM_STATE),
                hf[:, N_SLAB:].reshape(1, n_seq, SSM_GROUPS, SSM_STATE))

    yp, kp, vp, hrp, hip = run(x_prompt, bp, sp, 0, True, jnp.zeros((bp, 2 * N_SLAB, LANES), F32), 512)
    h0 = jnp.concatenate([state_ssm_re[0].reshape(bs, N_SLAB, LANES), state_ssm_im[0].reshape(bs, N_SLAB, LANES)],
                         axis=1)
    ys, ks, vs, hrs, his = run(x_sample, bs, ss, past, False, h0, bs * ss)
    return (yp, ys, kp, vp, hrp, hip, ks, vs, hrs, his)
```

```python
import functools
import math

import jax
import jax.numpy as jnp
from jax import lax
from jax.experimental import pallas as pl
from jax.experimental.pallas import tpu as pltpu

F32 = jnp.float32
BF16 = jnp.bfloat16

D_MODEL = 1024
D_SSM = 512
D_ATT = 512
SSM_GROUP = 16
SSM_GROUPS = 32
SSM_STATE = 64
N_CH = SSM_GROUPS * SSM_STATE
ATT_HEADS = 4
QK_DIM = 64
V_DIM = 128
ROPE_DIM = 16
ROPE_THETA = 500000.0
CHUNK = 64
EPS = 1e-6
D_IN = 3072
LAMBDA_INIT = 0.8 - 0.6 * math.exp(-0.3 * 0)

LANES = 128
SUBLANES = 8
N_SLAB = N_CH // LANES
OUT_CHUNK = 256
S5_STEPS = 256
SLAB_PAD = 4
NEG = -1e30
ATT_TILE = 256
HEADS_PER_STEP = 4
SUM_ROWS = 16
LOG2E = math.log2(math.e)
Q_SCALE = QK_DIM ** -0.5 * LOG2E
VMEM_LIMIT = 56 * 1024 * 1024


def _params(sem):
    return pltpu.CompilerParams(dimension_semantics=sem, vmem_limit_bytes=VMEM_LIMIT)


def _full(shape):
    n = len(shape)
    return pl.BlockSpec(shape, lambda *_: (0,) * n)


def _prep_kernel(lr_ref, li_ref, ldt_ref, br_ref, bi_ref, cr_ref, ci_ref, ar_ref, ai_ref, bdb_ref, bdc_ref):
    lr = lr_ref[...]
    li = li_ref[...]
    dt = jnp.exp(ldt_ref[...])
    mag = jnp.exp(lr * dt)
    ar = mag * jnp.cos(li * dt)
    ai = mag * jnp.sin(li * dt)
    den = lr * lr + li * li
    cr = ((ar - 1.0) * lr + ai * li) / den
    ci = (ai * lr - (ar - 1.0) * li) / den
    ar_ref[...] = ar
    ai_ref[...] = ai
    br = br_ref[...]
    bi = bi_ref[...]
    crb = cr[:, None, :]
    cib = ci[:, None, :]
    bbar = (crb * br - cib * bi, crb * bi + cib * br)
    cmat = (cr_ref[...], -ci_ref[...])

    gh = SSM_GROUPS // 2
    rows_b, cols_b = gh * SSM_GROUP, gh * SSM_STATE
    spread_b = (lax.broadcasted_iota(jnp.int32, (SSM_STATE, cols_b), 1) & (SSM_STATE - 1)
                == lax.broadcasted_iota(jnp.int32, (SSM_STATE, cols_b), 0)).astype(BF16)
    keep_b = (lax.broadcasted_iota(jnp.int32, (rows_b, cols_b), 0) // SSM_GROUP
              == lax.broadcasted_iota(jnp.int32, (rows_b, cols_b), 1) // SSM_STATE)
    spread_c = (lax.broadcasted_iota(jnp.int32, (SSM_GROUP, rows_b), 1) & (SSM_GROUP - 1)
                == lax.broadcasted_iota(jnp.int32, (SSM_GROUP, rows_b), 0)).astype(BF16)
    keep_c = (lax.broadcasted_iota(jnp.int32, (cols_b, rows_b), 0) // SSM_STATE
              == lax.broadcasted_iota(jnp.int32, (cols_b, rows_b), 1) // SSM_GROUP)
    for b in range(2):
        for part in range(2):
            x = bbar[part][b * gh:(b + 1) * gh].reshape(rows_b, SSM_STATE).astype(BF16)
            t = jnp.dot(x, spread_b, preferred_element_type=F32)
            bdb_ref[b, :, part * cols_b:(part + 1) * cols_b] = jnp.where(keep_b, t, 0.0).astype(BF16)
            x = cmat[part][b * gh:(b + 1) * gh].reshape(cols_b, SSM_GROUP).astype(BF16)
            t = jnp.dot(x, spread_c, preferred_element_type=F32)
            bdc_ref[b, part * cols_b:(part + 1) * cols_b, :] = jnp.where(keep_c, t, 0.0).astype(BF16)


def _prep(lam_re, lam_im, log_dt, b_re, b_im, c_re, c_im):
    g, p, c = b_re.shape
    out_shape = (jax.ShapeDtypeStruct((g, p), F32), jax.ShapeDtypeStruct((g, p), F32),
                 jax.ShapeDtypeStruct((2, g // 2 * c, 2 * g // 2 * p), BF16),
                 jax.ShapeDtypeStruct((2, 2 * g // 2 * p, g // 2 * c), BF16))
    return pl.pallas_call(_prep_kernel, out_shape=out_shape, name="s5_prep")(
        lam_re, lam_im, log_dt.reshape(g, 1), jnp.swapaxes(b_re, 1, 2), jnp.swapaxes(b_im, 1, 2),
        jnp.swapaxes(c_re, 1, 2), jnp.swapaxes(c_im, 1, 2))


def _in_proj_kernel(x_ref, g_ref, w_ref, inv_ref, u_ref, zs_ref, q_ref, kf_ref, kb_ref, vf_ref,
                    vb_ref, za_ref, cl_sc, sl_sc, *, tm, seq_len, pos0, transposed_qv):
    i = pl.program_id(0)
    x = x_ref[...]
    ms = jnp.mean(x * x, axis=-1, keepdims=True)
    hn = (x * lax.rsqrt(ms + EPS) * g_ref[...]).astype(BF16)

    inv = inv_ref[...]

    @pl.when(i == 0)
    def _():
        off = (lax.broadcasted_iota(jnp.int32, (tm, LANES), 0) & (seq_len - 1)).astype(F32) * inv
        cl_sc[...] = jnp.cos(off)
        sl_sc[...] = jnp.sin(off)

    base = (pos0 + ((i * tm) & (seq_len - 1))).astype(F32) * jnp.broadcast_to(inv, (SUBLANES, LANES))
    cb = jnp.cos(base)[:1]
    sb = jnp.sin(base)[:1]
    cl = cl_sc[...]
    sl = sl_sc[...]
    c_m = cb * cl - sb * sl
    sin = sb * cl + cb * sl
    lane = lax.broadcasted_iota(jnp.int32, (tm, LANES), 1) & (QK_DIM - 1)
    half = ROPE_DIM // 2
    s_lo = jnp.where(lane < half, -sin, 0.0)
    s_hi = jnp.where(lane >= half, sin, 0.0)

    def seg(lo, hi):
        return jnp.dot(hn, w_ref[:, lo:hi], preferred_element_type=F32)

    def rope(t):
        outs = []
        for h in range(ATT_HEADS):
            th = t[:, h * LANES:(h + 1) * LANES]
            outs.append(th * c_m + pltpu.roll(th, LANES - half, 1) * s_lo + pltpu.roll(th, half, 1) * s_hi)
        return jnp.concatenate(outs, axis=1)

    def put(ref, t):
        if not transposed_qv:
            ref[...] = t.astype(BF16)
            return
        for h in range(ATT_HEADS):
            tt = t[:, h * LANES:(h + 1) * LANES].T.astype(BF16)
            for c in range(tm // ATT_TILE):
                ref[h, c] = tt[:, c * ATT_TILE:(c + 1) * ATT_TILE]

    u_ref[...] = seg(0, D_SSM).astype(BF16)
    zs_ref[...] = seg(D_SSM, 2 * D_SSM).astype(BF16)
    q = rope(seg(1024, 1536))
    put(q_ref, q * Q_SCALE)
    k = rope(seg(1536, 2048))
    if transposed_qv:
        for h in range(ATT_HEADS):
            kt = k[:, h * LANES:(h + 1) * LANES].T
            kf_ref[h, 0] = kt[:QK_DIM]
            kf_ref[h, 1] = kt[QK_DIM:]
    else:
        kf_ref[...] = k
    kb_ref[...] = k.astype(BF16)
    v = seg(2048, 2560)
    for h in range(ATT_HEADS):
        vf_ref[pl.ds(h, tm, stride=ATT_HEADS), :] = v[:, h * LANES:(h + 1) * LANES]
    put(vb_ref, v)
    za_ref[...] = seg(2560, 3072).astype(BF16)


def _in_proj(x2d, g, w_bf, inv_lane, *, seq_len, pos0, tm, transposed_qv):
    n = x2d.shape[0]
    assert n % tm == 0 and seq_len & (seq_len - 1) == 0 and (tm % seq_len == 0 or seq_len % tm == 0)
    row = lambda i: (i, 0)
    o512 = pl.BlockSpec((tm, 512), row)
    shp = lambda dt: jax.ShapeDtypeStruct((n, 512), dt)
    if transposed_qv:
        assert seq_len % tm == 0 and tm % ATT_TILE == 0
        tps = seq_len // tm
        per = tm // ATT_TILE
        t_spec = pl.BlockSpec((None, ATT_HEADS, per, LANES, ATT_TILE), lambda i: (i // tps, 0, i % tps, 0, 0))
        t_shape = jax.ShapeDtypeStruct((n // seq_len, ATT_HEADS, seq_len // ATT_TILE, LANES, ATT_TILE), BF16)
        kf_spec = pl.BlockSpec((None, ATT_HEADS, 2, QK_DIM, tm), lambda i: (i // tps, 0, 0, 0, i % tps))
        kf_shape = jax.ShapeDtypeStruct((n // seq_len, ATT_HEADS, 2, QK_DIM, seq_len), F32)
    else:
        t_spec, t_shape = o512, shp(BF16)
        kf_spec, kf_shape = o512, shp(F32)
    vf_spec = pl.BlockSpec((tm * ATT_HEADS, V_DIM), row)
    vf_shape = jax.ShapeDtypeStruct((n * ATT_HEADS, V_DIM), F32)
    return pl.pallas_call(
        functools.partial(_in_proj_kernel, tm=tm, seq_len=seq_len, pos0=pos0, transposed_qv=transposed_qv),
        grid=(n // tm,),
        in_specs=[pl.BlockSpec((tm, D_MODEL), row), _full((1, D_MODEL)), _full((D_MODEL, D_IN)),
                  _full((1, LANES))],
        out_specs=[o512, o512, t_spec, kf_spec, o512, vf_spec, t_spec, o512],
        out_shape=[shp(BF16), shp(BF16), t_shape, kf_shape, shp(BF16), vf_shape, t_shape, shp(BF16)],
        scratch_shapes=[pltpu.VMEM((tm, LANES), F32), pltpu.VMEM((tm, LANES), F32)],
        compiler_params=_params(("arbitrary",)),
        name="in_proj",
    )(x2d, g, w_bf, inv_lane)


def _cmul(ar, ai, br, bi):
    return ar * br - ai * bi, ar * bi + ai * br


def _s5_kernel(u_ref, up_ref, zsp_ref, bdb_ref, bdc_ref, d_ref, a_ref, w1_ref, b1_ref, w2_ref, b2_ref, h0_ref,
               out_ref, hf_ref, hbuf, hout_a, hout_b, hb16, car, *, n_chain, T, n_steps, chained, chain_group):
    n_rows = n_chain * T
    P = n_rows + SLAB_PAD
    half_cols = N_CH // 2
    i = pl.program_id(0)
    blocks = range(N_SLAB // SUBLANES)

    def project_and_scan(hout):
        u = u_ref[...].reshape(n_rows, D_SSM)
        for b in range(2):
            bu = jnp.dot(u[:, b * 256:(b + 1) * 256], bdb_ref[b], preferred_element_type=F32)
            for part in range(2):
                for k in range(N_SLAB // 2):
                    slab = part * N_SLAB + b * (N_SLAB // 2) + k
                    col = part * half_cols + k * LANES
                    hbuf[slab * P:slab * P + n_rows, :] = bu[:, col:col + LANES]
        a = [(a_ref[SUBLANES * k:SUBLANES * (k + 1), :], a_ref[N_SLAB + SUBLANES * k:N_SLAB + SUBLANES * (k + 1), :])
             for k in blocks]
        for g0 in range(0, n_chain, chain_group):
            chains = list(range(g0, g0 + chain_group))
            st = {(c, k): (car[c, SUBLANES * k:SUBLANES * (k + 1), :],
                           car[c, N_SLAB + SUBLANES * k:N_SLAB + SUBLANES * (k + 1), :])
                  for c in chains for k in blocks}
            for t in range(T):
                for c in chains:
                    for k in blocks:
                        hr, hi = st[(c, k)]
                        rows_r = pl.ds(SUBLANES * k * P + c * T + t, SUBLANES, stride=P)
                        rows_i = pl.ds((N_SLAB + SUBLANES * k) * P + c * T + t, SUBLANES, stride=P)
                        pr, pi = _cmul(a[k][0], a[k][1], hr, hi)
                        nr = pr + hbuf[rows_r, :]
                        ni = pi + hbuf[rows_i, :]
                        hout[rows_r, :] = nr
                        hout[rows_i, :] = ni
                        st[(c, k)] = (nr, ni)
            for c in chains:
                for k in blocks:
                    hr, hi = st[(c, k)]
                    car[c, SUBLANES * k:SUBLANES * (k + 1), :] = hr
                    car[c, N_SLAB + SUBLANES * k:N_SLAB + SUBLANES * (k + 1), :] = hi

    def output_stage(hout):
        up = up_ref[...].reshape(n_rows, D_SSM)
        for b in range(2):
            for part in range(2):
                for k in range(N_SLAB // 2):
                    slab = part * N_SLAB + b * (N_SLAB // 2) + k
                    col = b * N_CH + part * half_cols + k * LANES
                    hb16[:, col:col + LANES] = hout[slab * P:slab * P + n_rows, :].astype(BF16)
        ys = [jnp.dot(hb16[:, b * N_CH:(b + 1) * N_CH], bdc_ref[b], preferred_element_type=F32) for b in range(2)]
        y = jnp.concatenate(ys, axis=1) + up.astype(F32) * d_ref[...]
        gb = jax.nn.gelu(y).astype(BF16)
        y1 = jnp.dot(gb, w1_ref[...], preferred_element_type=F32) + b1_ref[...]
        y2 = jnp.dot(gb, w2_ref[...], preferred_element_type=F32) + b2_ref[...]
        zs = zsp_ref[...].reshape(n_rows, D_SSM).astype(F32)
        out_ref[...] = (y1 * jax.nn.sigmoid(y2) * jax.nn.silu(zs)).astype(BF16).reshape(n_chain, T, D_SSM)

    @pl.when(i == 0)
    def _():
        car[...] = jnp.zeros_like(car) if chained else h0_ref[...]
        project_and_scan(hout_a)

    if n_steps > 1:
        @pl.when((i > 0) & (i < n_steps) & ((i & 1) == 1))
        def _():
            project_and_scan(hout_b)
            output_stage(hout_a)

        @pl.when((i > 0) & (i < n_steps) & ((i & 1) == 0))
        def _():
            project_and_scan(hout_a)
            output_stage(hout_b)

    @pl.when(i == n_steps)
    def _():
        output_stage(hout_a if (n_steps - 1) % 2 == 0 else hout_b)
        hf_ref[...] = car[...]


def _s5(u, zs, bdb, bdc, dvec, a2, w1, b1, w2, b2, h0, *, n_seq, seq_len, chained):
    if chained:
        T = S5_STEPS
        chain_group = n_seq
    else:
        T = seq_len
        chain_group = 4
    assert seq_len % T == 0 and n_seq % chain_group == 0 and T % (2 * SUBLANES) == 0
    n_steps = seq_len // T
    n_rows = n_seq * T
    u3 = u.reshape(n_seq, seq_len, D_SSM)
    zs3 = zs.reshape(n_seq, seq_len, D_SSM)
    cur = pl.BlockSpec((n_seq, T, D_SSM), lambda t: (0, jnp.minimum(t, n_steps - 1), 0))
    prev = pl.BlockSpec((n_seq, T, D_SSM), lambda t: (0, jnp.maximum(t - 1, 0), 0))
    st_shape = (n_seq, 2 * N_SLAB, LANES)
    slabs = pltpu.VMEM((2 * N_SLAB * (n_rows + SLAB_PAD), LANES), F32)
    out, hf = pl.pallas_call(
        functools.partial(_s5_kernel, n_chain=n_seq, T=T, n_steps=n_steps, chained=chained,
                          chain_group=chain_group),
        grid=(n_steps + 1,),
        in_specs=[cur, prev, prev, _full(bdb.shape), _full(bdc.shape), _full((1, D_SSM)),
                  _full((2 * N_SLAB, LANES)), _full((D_SSM, D_SSM)), _full((1, D_SSM)), _full((D_SSM, D_SSM)),
                  _full((1, D_SSM)), _full(st_shape)],
        out_specs=[prev, _full(st_shape)],
        out_shape=[jax.ShapeDtypeStruct((n_seq, seq_len, D_SSM), BF16), jax.ShapeDtypeStruct(st_shape, F32)],
        scratch_shapes=[slabs, slabs, slabs, pltpu.VMEM((n_rows, 2 * N_CH), BF16), pltpu.VMEM(st_shape, F32)],
        compiler_params=_params(("arbitrary",)),
        name="s5_chained" if chained else "s5_independent",
    )(u3, u3, zs3, bdb, bdc, dvec, a2, w1, b1, w2, b2, h0)
    return out.reshape(n_seq * seq_len, D_SSM), hf


def _lambda(lq1, lk1, lq2, lk2):
    s1 = jnp.sum(lq1[...] * lk1[...], axis=1, keepdims=True)
    s2 = jnp.sum(lq2[...] * lk2[...], axis=1, keepdims=True)
    return jnp.exp(s1) - jnp.exp(s2) + LAMBDA_INIT


def _stack_maps(q):
    lane = lax.broadcasted_iota(jnp.int32, q.shape, 1)
    zero = jnp.zeros_like(q)
    return jnp.concatenate([jnp.where(lane < QK_DIM, q, zero), jnp.where(lane >= QK_DIM, q, zero)], axis=0)


def _subln_gate(o, g, za):
    ms = jnp.mean(o * o, axis=-1, keepdims=True)
    on = (o * lax.rsqrt(ms + EPS) * g) * (1.0 - LAMBDA_INIT)
    return on * jax.nn.silu(za.astype(F32))


def _finish_head(acc, l, lam, g, za, t):
    inv = 1.0 / l
    o = acc[:t] * inv[:t] - lam * (acc[t:] * inv[t:])
    return _subln_gate(o, g, za)


_NT = (((1,), (1,)), ((), ()))


def _attn_kernel(lq1, lk1, lq2, lk2, g_ref, qt_ref, k_ref, vt_ref, za_ref, o_ref, m_sc, acc_sc, s_a, s_b, *, nq):
    tq = ATT_TILE
    step = pl.program_id(2)

    def finish():
        lam = _lambda(lq1, lk1, lq2, lk2)
        for h in range(HEADS_PER_STEP):
            acc = acc_sc[h, :V_DIM, :]
            inv = 1.0 / acc_sc[h, V_DIM:V_DIM + 1, :]
            ot = acc[:, :tq] * inv[:, :tq] - lam * (acc[:, tq:] * inv[:, tq:])
            cols = slice(h * LANES, (h + 1) * LANES)
            o_ref[:, cols] = _subln_gate(ot.T, g_ref[...], za_ref[:, cols]).astype(BF16)

    @pl.when((pl.program_id(0) == 0) & (pl.program_id(1) == 0) & (step == 0))
    def _():
        acc_sc[...] = jnp.ones_like(acc_sc)

    @pl.when(step < nq)
    def _():
        finish()
        _attn_tile(step, qt_ref, k_ref, vt_ref, m_sc, acc_sc, s_a, s_b)

    @pl.when(step == nq)
    def _():
        finish()


def _attn_tile(qi, qt_ref, k_ref, vt_ref, m_sc, acc_sc, s_a, s_b):
    tq = ATT_TILE
    q2t = []
    for h in range(HEADS_PER_STEP):
        qt = qt_ref[h]
        row = lax.broadcasted_iota(jnp.int32, qt.shape, 0)
        zero = jnp.zeros_like(qt)
        q2t.append(jnp.concatenate([jnp.where(row < QK_DIM, qt, zero), jnp.where(row >= QK_DIM, qt, zero)], axis=1))
    m_sc[...] = jnp.full_like(m_sc, NEG)
    acc_sc[...] = jnp.zeros_like(acc_sc)
    ones = jnp.ones((SUM_ROWS, tq), BF16)
    heads = range(HEADS_PER_STEP)
    tk = 2 * tq
    n_full = qi // 2
    odd = (qi & 1) == 1

    def scores_into(s_ref, kt, h):
        start = pl.multiple_of(kt * tk, tk)
        s_ref[h] = jnp.dot(k_ref[pl.ds(start, tk), h * LANES:(h + 1) * LANES], q2t[h],
                           preferred_element_type=F32)

    def softmax_accumulate(s_ref, kt, nkeys, masked, h):
        s = s_ref[h] if nkeys == tk else s_ref[h, :nkeys, :]
        if masked:
            kc = lax.broadcasted_iota(jnp.int32, s.shape, 0) // CHUNK + kt * (tk // CHUNK)
            qc = (lax.broadcasted_iota(jnp.int32, s.shape, 1) & (tq - 1)) // CHUNK + qi * (tq // CHUNK)
            s = jnp.where(kc <= qc, s, NEG)
        m_old = m_sc[h]
        m_new = jnp.maximum(m_old, jnp.max(s, axis=0, keepdims=True))
        alpha = jnp.exp2(m_old - m_new)
        pb = jnp.exp2(s - m_new).astype(BF16)
        pv = jnp.dot(jnp.concatenate([vt_ref[h, 2 * kt], ones], axis=0), pb[:tq], preferred_element_type=F32)
        if nkeys == tk:
            pv = pv + jnp.dot(jnp.concatenate([vt_ref[h, 2 * kt + 1], ones], axis=0), pb[tq:],
                              preferred_element_type=F32)
        acc_sc[h] = alpha * acc_sc[h] + pv
        m_sc[h] = m_new

    def stage(kt, s_cur, s_nxt):
        for h in heads:
            scores_into(s_nxt, kt + 1, h)
            softmax_accumulate(s_cur, kt, tk, False, h)

    for h in heads:
        scores_into(s_a, 0, h)

    def body(kt, c):
        even = (kt & 1) == 0

        @pl.when(even)
        def _():
            stage(kt, s_a, s_b)

        @pl.when(jnp.logical_not(even))
        def _():
            stage(kt, s_b, s_a)

        return c

    lax.fori_loop(0, n_full, body, 0)

    for parity, s_ref in ((0, s_a), (1, s_b)):
        here = (n_full & 1) == parity

        @pl.when(here & odd)
        def _(s_ref=s_ref):
            for h in heads:
                softmax_accumulate(s_ref, n_full, tk, True, h)

        @pl.when(here & jnp.logical_not(odd))
        def _(s_ref=s_ref):
            for h in heads:
                softmax_accumulate(s_ref, n_full, tq, True, h)


def _attn_prompt(lams, g, qt, kb, vt, za, *, n_seq, seq_len):
    tq = ATT_TILE
    nq = seq_len // tq
    hps = HEADS_PER_STEP
    assert tq % CHUNK == 0 and tq & (tq - 1) == 0 and ATT_HEADS % hps == 0 and seq_len % (2 * tq) == 0
    rowspec = pl.BlockSpec((tq, hps * LANES), lambda b, h, i: (b * nq + jnp.maximum(i - 1, 0), h))
    qtspec = pl.BlockSpec((None, hps, None, LANES, tq), lambda b, h, i: (b, h, jnp.minimum(i, nq - 1), 0, 0))
    kspec = pl.BlockSpec((seq_len, hps * LANES), lambda b, h, i: (b, h))
    vtspec = pl.BlockSpec((None, hps, nq, LANES, tq), lambda b, h, i: (b, h, 0, 0, 0))
    vec = _full((1, QK_DIM))
    stat = pltpu.VMEM((hps, 1, 2 * tq), F32)
    return pl.pallas_call(
        functools.partial(_attn_kernel, nq=nq),
        grid=(n_seq, ATT_HEADS // hps, nq + 1),
        in_specs=[vec, vec, vec, vec, _full((1, V_DIM)), qtspec, kspec, vtspec, rowspec],
        out_specs=rowspec,
        out_shape=jax.ShapeDtypeStruct((n_seq * seq_len, D_ATT), BF16),
        scratch_shapes=[stat, pltpu.VMEM((hps, V_DIM + SUM_ROWS, 2 * tq), F32),
                        pltpu.VMEM((hps, 2 * tq, 2 * tq), F32), pltpu.VMEM((hps, 2 * tq, 2 * tq), F32)],
        compiler_params=_params(("arbitrary", "arbitrary", "arbitrary")),
        name="attn_prompt",
    )(*lams, g, qt, kb, vt, za)


def _attn_sample_kernel(lq1, lk1, lq2, lk2, g_ref, q_ref, kn_ref, vn_ref, ck_ref, cv_ref, za_ref, o_ref, *, t, past):
    lam = _lambda(lq1, lk1, lq2, lk2)
    outs = []
    for h in range(ATT_HEADS):
        cols = slice(h * LANES, (h + 1) * LANES)
        q2 = _stack_maps(q_ref[:, cols])
        kpt = jnp.concatenate([ck_ref[h, 0], ck_ref[h, 1]], axis=0).astype(BF16)
        vp = cv_ref[pl.ds(h, past, stride=ATT_HEADS), :].astype(BF16)
        s_p = jnp.dot(q2, kpt, preferred_element_type=F32)
        s_n = lax.dot_general(q2, kn_ref[:, cols], _NT, preferred_element_type=F32)
        m = jnp.maximum(jnp.max(s_p, axis=1, keepdims=True), jnp.max(s_n, axis=1, keepdims=True))
        p_p = jnp.exp2(s_p - m)
        p_n = jnp.exp2(s_n - m)
        l = jnp.sum(p_p, axis=1, keepdims=True) + jnp.sum(p_n, axis=1, keepdims=True)
        acc = (jnp.dot(p_p.astype(BF16), vp, preferred_element_type=F32)
               + jnp.dot(p_n.astype(BF16), vn_ref[:, cols], preferred_element_type=F32))
        outs.append(_finish_head(acc, l, lam, g_ref[...], za_ref[:, cols], t))
    o_ref[...] = jnp.concatenate(outs, axis=1).astype(BF16)


def _attn_sample(lams, g, q, kb, vb, ck, cv, za, *, n_seq, t, past):
    assert past % CHUNK == 0 and t <= CHUNK
    row = pl.BlockSpec((t, D_ATT), lambda b: (b, 0))
    kcache = pl.BlockSpec((None, ATT_HEADS, 2, QK_DIM, past), lambda b: (b, 0, 0, 0, 0))
    vcache = pl.BlockSpec((None, past * ATT_HEADS, V_DIM), lambda b: (b, 0, 0))
    vec = _full((1, QK_DIM))
    return pl.pallas_call(
        functools.partial(_attn_sample_kernel, t=t, past=past),
        grid=(n_seq,),
        in_specs=[vec, vec, vec, vec, _full((1, V_DIM)), row, row, row, kcache, vcache, row],
        out_specs=row,
        out_shape=jax.ShapeDtypeStruct((n_seq * t, D_ATT), BF16),
        compiler_params=_params(("arbitrary",)),
        name="attn_sample",
    )(*lams, g, q, kb, vb, ck, cv, za)


def _out_proj_kernel(s_ref, a_ref, w_ref, x_ref, g_ref, y_ref):
    tm = s_ref.shape[0]
    step = min(tm, OUT_CHUNK)
    for r in range(0, tm, step):
        rows = slice(r, r + step)
        mix = (jnp.dot(s_ref[rows, :], w_ref[:D_SSM, :], preferred_element_type=F32)
               + jnp.dot(a_ref[rows, :], w_ref[D_SSM:, :], preferred_element_type=F32))
        ms = jnp.mean(mix * mix, axis=-1, keepdims=True)
        y_ref[rows, :] = x_ref[rows, :] + mix * lax.rsqrt(ms + EPS) * g_ref[...]


def _out_proj(ssm_out, att_out, w_bf, x2d, g, *, tm):
    n = x2d.shape[0]
    assert n % tm == 0 and tm % min(tm, OUT_CHUNK) == 0
    row = lambda i: (i, 0)
    half = pl.BlockSpec((tm, 512), row)
    full = pl.BlockSpec((tm, D_MODEL), row)
    return pl.pallas_call(
        _out_proj_kernel,
        grid=(n // tm,),
        in_specs=[half, half, _full((D_MODEL, D_MODEL)), full, _full((1, D_MODEL))],
        out_specs=full,
        out_shape=jax.ShapeDtypeStruct((n, D_MODEL), F32),
        compiler_params=_params(("arbitrary",)),
        name="out_proj",
    )(ssm_out, att_out, w_bf, x2d, g)


def kernel(x_prompt, x_sample, cache_k, cache_v, state_ssm_re, state_ssm_im, norm_pre_g, w_in, ssm_lambda_re,
           ssm_lambda_im, ssm_log_dt, ssm_b_re, ssm_b_im, ssm_c_re, ssm_c_im, ssm_d, glu_w1, glu_b1, glu_w2,
           glu_b2, lambda_q1, lambda_k1, lambda_q2, lambda_k2, attn_subln_g, w_out, norm_post_g):
    bp, sp, _ = x_prompt.shape
    bs, ss, _ = x_sample.shape
    past = cache_k.shape[2]

    a_re, a_im, bdb, bdc = _prep(ssm_lambda_re[0], ssm_lambda_im[0], ssm_log_dt[0], ssm_b_re[0], ssm_b_im[0],
                                 ssm_c_re[0], ssm_c_im[0])
    a2 = jnp.concatenate([a_re.reshape(N_SLAB, LANES), a_im.reshape(N_SLAB, LANES)], axis=0)
    dvec = ssm_d[0].reshape(1, D_SSM)
    w_in_bf = w_in[0].astype(BF16)
    w_out_bf = w_out[0].astype(BF16)
    w1 = glu_w1[0].astype(BF16)
    w2 = glu_w2[0].astype(BF16)
    b1 = glu_b1[0].reshape(1, D_SSM)
    b2 = glu_b2[0].reshape(1, D_SSM)
    g_pre = norm_pre_g[0].reshape(1, D_MODEL)
    g_post = norm_post_g[0].reshape(1, D_MODEL)
    g_sub = attn_subln_g[0].reshape(1, V_DIM)
    lams = tuple(v[0].reshape(1, QK_DIM) for v in (lambda_q1, lambda_k1, lambda_q2, lambda_k2))
    inv = ROPE_THETA ** (-jnp.arange(ROPE_DIM // 2, dtype=F32) * 2.0 / ROPE_DIM)
    rotary_lane = (jnp.arange(LANES) % QK_DIM) < ROPE_DIM
    inv_lane = jnp.where(rotary_lane, jnp.tile(inv, LANES // (ROPE_DIM // 2)), 0.0).reshape(1, LANES)

    def run(x, n_seq, seq_len, pos0, chained, h0, tm):
        x2d = x.reshape(n_seq * seq_len, D_MODEL)
        u, zs, q, kf, kb, vf, vb, za = _in_proj(x2d, g_pre, w_in_bf, inv_lane, seq_len=seq_len, pos0=pos0, tm=tm,
                                                transposed_qv=chained)
        ssm_out, hf = _s5(u, zs, bdb, bdc, dvec, a2, w1, b1, w2, b2, h0,
                          n_seq=n_seq, seq_len=seq_len, chained=chained)
        if chained:
            att = _attn_prompt(lams, g_sub, q, kb, vb, za, n_seq=n_seq, seq_len=seq_len)
            k_out = jnp.transpose(kf, (0, 4, 1, 2, 3))[None]
        else:
            ck = jnp.transpose(cache_k[0], (0, 2, 3, 4, 1))
            cv = cache_v[0].reshape(n_seq, past * ATT_HEADS, V_DIM)
            att = _attn_sample(lams, g_sub, q, kb, vb, ck, cv, za, n_seq=n_seq, t=seq_len, past=past)
            k_out = kf.reshape(1, n_seq, seq_len, ATT_HEADS, 2, QK_DIM)
        y = _out_proj(ssm_out, att, w_out_bf, x2d, g_post, tm=min(2 * tm, n_seq * seq_len))
        return (y.reshape(n_seq, seq_len, D_MODEL),
                k_out,
                vf.reshape(1, n_seq, seq_len, ATT_HEADS, V_DIM),
                hf[:, :N_SLAB].reshape(1, n_seq, SSM_GROUPS, SSM_STATE),
                hf[:, N_SLAB:].reshape(1, n_seq, SSM_GROUPS, SSM_STATE))

    yp, kp, vp, hrp, hip = run(x_prompt, bp, sp, 0, True, jnp.zeros((bp, 2 * N_SLAB, LANES), F32), 512)
    h0 = jnp.concatenate([state_ssm_re[0].reshape(bs, N_SLAB, LANES), state_ssm_im[0].reshape(bs, N_SLAB, LANES)],
                         axis=1)
    ys, ks, vs, hrs, his = run(x_sample, bs, ss, past, False, h0, bs * ss)
    return (yp, ys, kp, vp, hrp, hip, ks, vs, hrs, his)
```

```python
import functools
import math

import jax
import jax.numpy as jnp
from jax import lax
from jax.experimental import pallas as pl
from jax.experimental.pallas import tpu as pltpu

F32 = jnp.float32
BF16 = jnp.bfloat16

D_MODEL = 1024
D_SSM = 512
D_ATT = 512
SSM_GROUP = 16
SSM_GROUPS = 32
SSM_STATE = 64
N_CH = SSM_GROUPS * SSM_STATE
ATT_HEADS = 4
QK_DIM = 64
V_DIM = 128
ROPE_DIM = 16
ROPE_THETA = 500000.0
CHUNK = 64
EPS = 1e-6
D_IN = 3072
LAMBDA_INIT = 0.8 - 0.6 * math.exp(-0.3 * 0)

LANES = 128
SUBLANES = 8
N_SLAB = N_CH // LANES
SEQS_PER_STEP = 2
OUT_TILE = 2048
OUT_CHUNK = 256
S5_STEPS = 256
SLAB_PAD = 4
NEG = -1e30
ATT_TILE = 256
HEADS_PER_STEP = 4
SUM_ROWS = 16
LOG2E = math.log2(math.e)
Q_SCALE = QK_DIM ** -0.5 * LOG2E
VMEM_LIMIT = 56 * 1024 * 1024


def _params(sem):
    return pltpu.CompilerParams(dimension_semantics=sem, vmem_limit_bytes=VMEM_LIMIT)


def _full(shape):
    n = len(shape)
    return pl.BlockSpec(shape, lambda *_: (0,) * n)


def _prep_kernel(lr_ref, li_ref, ldt_ref, br_ref, bi_ref, cr_ref, ci_ref, ar_ref, ai_ref, bdb_ref, bdc_ref):
    lr = lr_ref[...]
    li = li_ref[...]
    dt = jnp.exp(ldt_ref[...])
    mag = jnp.exp(lr * dt)
    ar = mag * jnp.cos(li * dt)
    ai = mag * jnp.sin(li * dt)
    den = lr * lr + li * li
    cr = ((ar - 1.0) * lr + ai * li) / den
    ci = (ai * lr - (ar - 1.0) * li) / den
    ar_ref[...] = ar
    ai_ref[...] = ai
    br = br_ref[...]
    bi = bi_ref[...]
    crb = cr[:, None, :]
    cib = ci[:, None, :]
    bbar = (crb * br - cib * bi, crb * bi + cib * br)
    cmat = (cr_ref[...], -ci_ref[...])

    gh = SSM_GROUPS // 2
    rows_b, cols_b = gh * SSM_GROUP, gh * SSM_STATE
    spread_b = (lax.broadcasted_iota(jnp.int32, (SSM_STATE, cols_b), 1) & (SSM_STATE - 1)
                == lax.broadcasted_iota(jnp.int32, (SSM_STATE, cols_b), 0)).astype(BF16)
    keep_b = (lax.broadcasted_iota(jnp.int32, (rows_b, cols_b), 0) // SSM_GROUP
              == lax.broadcasted_iota(jnp.int32, (rows_b, cols_b), 1) // SSM_STATE)
    spread_c = (lax.broadcasted_iota(jnp.int32, (SSM_GROUP, rows_b), 1) & (SSM_GROUP - 1)
                == lax.broadcasted_iota(jnp.int32, (SSM_GROUP, rows_b), 0)).astype(BF16)
    keep_c = (lax.broadcasted_iota(jnp.int32, (cols_b, rows_b), 0) // SSM_STATE
              == lax.broadcasted_iota(jnp.int32, (cols_b, rows_b), 1) // SSM_GROUP)
    for b in range(2):
        for part in range(2):
            x = bbar[part][b * gh:(b + 1) * gh].reshape(rows_b, SSM_STATE).astype(BF16)
            t = jnp.dot(x, spread_b, preferred_element_type=F32)
            bdb_ref[b, :, part * cols_b:(part + 1) * cols_b] = jnp.where(keep_b, t, 0.0).astype(BF16)
            x = cmat[part][b * gh:(b + 1) * gh].reshape(cols_b, SSM_GROUP).astype(BF16)
            t = jnp.dot(x, spread_c, preferred_element_type=F32)
            bdc_ref[b, part * cols_b:(part + 1) * cols_b, :] = jnp.where(keep_c, t, 0.0).astype(BF16)


def _prep(lam_re, lam_im, log_dt, b_re, b_im, c_re, c_im):
    g, p, c = b_re.shape
    out_shape = (jax.ShapeDtypeStruct((g, p), F32), jax.ShapeDtypeStruct((g, p), F32),
                 jax.ShapeDtypeStruct((2, g // 2 * c, 2 * g // 2 * p), BF16),
                 jax.ShapeDtypeStruct((2, 2 * g // 2 * p, g // 2 * c), BF16))
    return pl.pallas_call(_prep_kernel, out_shape=out_shape, name="s5_prep")(
        lam_re, lam_im, log_dt.reshape(g, 1), jnp.swapaxes(b_re, 1, 2), jnp.swapaxes(b_im, 1, 2),
        jnp.swapaxes(c_re, 1, 2), jnp.swapaxes(c_im, 1, 2))


def _in_proj_kernel(x_ref, g_ref, w_ref, inv_ref, u_ref, zs_ref, q_ref, kf_ref, kb_ref, vf_ref,
                    vb_ref, za_ref, cl_sc, sl_sc, *, tm, seq_len, pos0, transposed_qv):
    i = pl.program_id(0)
    x = x_ref[...]
    ms = jnp.mean(x * x, axis=-1, keepdims=True)
    hn = (x * lax.rsqrt(ms + EPS) * g_ref[...]).astype(BF16)

    inv = inv_ref[...]

    @pl.when(i == 0)
    def _():
        off = (lax.broadcasted_iota(jnp.int32, (tm, LANES), 0) & (seq_len - 1)).astype(F32) * inv
        cl_sc[...] = jnp.cos(off)
        sl_sc[...] = jnp.sin(off)

    base = (pos0 + ((i * tm) & (seq_len - 1))).astype(F32) * jnp.broadcast_to(inv, (SUBLANES, LANES))
    cb = jnp.cos(base)[:1]
    sb = jnp.sin(base)[:1]
    cl = cl_sc[...]
    sl = sl_sc[...]
    c_m = cb * cl - sb * sl
    sin = sb * cl + cb * sl
    lane = lax.broadcasted_iota(jnp.int32, (tm, LANES), 1) & (QK_DIM - 1)
    half = ROPE_DIM // 2
    s_lo = jnp.where(lane < half, -sin, 0.0)
    s_hi = jnp.where(lane >= half, sin, 0.0)

    def seg(lo, hi):
        return jnp.dot(hn, w_ref[:, lo:hi], preferred_element_type=F32)

    def rope(t):
        outs = []
        for h in range(ATT_HEADS):
            th = t[:, h * LANES:(h + 1) * LANES]
            outs.append(th * c_m + pltpu.roll(th, LANES - half, 1) * s_lo + pltpu.roll(th, half, 1) * s_hi)
        return jnp.concatenate(outs, axis=1)

    def put(ref, t):
        if not transposed_qv:
            ref[...] = t.astype(BF16)
            return
        for h in range(ATT_HEADS):
            tt = t[:, h * LANES:(h + 1) * LANES].T.astype(BF16)
            for c in range(tm // ATT_TILE):
                ref[h, c] = tt[:, c * ATT_TILE:(c + 1) * ATT_TILE]

    u_ref[...] = seg(0, D_SSM).astype(BF16)
    zs_ref[...] = seg(D_SSM, 2 * D_SSM).astype(BF16)
    q = rope(seg(1024, 1536))
    put(q_ref, q * Q_SCALE)
    k = rope(seg(1536, 2048))
    if transposed_qv:
        for h in range(ATT_HEADS):
            kt = k[:, h * LANES:(h + 1) * LANES].T
            kf_ref[h, 0] = kt[:QK_DIM]
            kf_ref[h, 1] = kt[QK_DIM:]
    else:
        kf_ref[...] = k
    kb_ref[...] = k.astype(BF16)
    v = seg(2048, 2560)
    for h in range(ATT_HEADS):
        vf_ref[pl.ds(h, tm, stride=ATT_HEADS), :] = v[:, h * LANES:(h + 1) * LANES]
    put(vb_ref, v)
    za_ref[...] = seg(2560, 3072).astype(BF16)


def _in_proj(x2d, g, w_bf, inv_lane, *, seq_len, pos0, tm, transposed_qv):
    n = x2d.shape[0]
    assert n % tm == 0 and seq_len & (seq_len - 1) == 0 and (tm % seq_len == 0 or seq_len % tm == 0)
    row = lambda i: (i, 0)
    o512 = pl.BlockSpec((tm, 512), row)
    shp = lambda dt: jax.ShapeDtypeStruct((n, 512), dt)
    if transposed_qv:
        assert seq_len % tm == 0 and tm % ATT_TILE == 0
        tps = seq_len // tm
        per = tm // ATT_TILE
        t_spec = pl.BlockSpec((None, ATT_HEADS, per, LANES, ATT_TILE), lambda i: (i // tps, 0, i % tps, 0, 0))
        t_shape = jax.ShapeDtypeStruct((n // seq_len, ATT_HEADS, seq_len // ATT_TILE, LANES, ATT_TILE), BF16)
        kf_spec = pl.BlockSpec((None, ATT_HEADS, 2, QK_DIM, tm), lambda i: (i // tps, 0, 0, 0, i % tps))
        kf_shape = jax.ShapeDtypeStruct((n // seq_len, ATT_HEADS, 2, QK_DIM, seq_len), F32)
    else:
        t_spec, t_shape = o512, shp(BF16)
        kf_spec, kf_shape = o512, shp(F32)
    vf_spec = pl.BlockSpec((tm * ATT_HEADS, V_DIM), row)
    vf_shape = jax.ShapeDtypeStruct((n * ATT_HEADS, V_DIM), F32)
    return pl.pallas_call(
        functools.partial(_in_proj_kernel, tm=tm, seq_len=seq_len, pos0=pos0, transposed_qv=transposed_qv),
        grid=(n // tm,),
        in_specs=[pl.BlockSpec((tm, D_MODEL), row), _full((1, D_MODEL)), _full((D_MODEL, D_IN)),
                  _full((1, LANES))],
        out_specs=[o512, o512, t_spec, kf_spec, o512, vf_spec, t_spec, o512],
        out_shape=[shp(BF16), shp(BF16), t_shape, kf_shape, shp(BF16), vf_shape, t_shape, shp(BF16)],
        scratch_shapes=[pltpu.VMEM((tm, LANES), F32), pltpu.VMEM((tm, LANES), F32)],
        compiler_params=_params(("arbitrary",)),
        name="in_proj",
    )(x2d, g, w_bf, inv_lane)


def _cmul(ar, ai, br, bi):
    return ar * br - ai * bi, ar * bi + ai * br


def _s5_kernel(u_ref, up_ref, zsp_ref, bdb_ref, bdc_ref, d_ref, a_ref, w1_ref, b1_ref, w2_ref, b2_ref, h0_ref,
               out_ref, hf_ref, hbuf, hout_a, hout_b, hb16, car, *, n_chain, T, n_steps, chained, chain_group):
    n_rows = n_chain * T
    P = n_rows + SLAB_PAD
    half_cols = N_CH // 2
    i = pl.program_id(0)
    blocks = range(N_SLAB // SUBLANES)

    def project_and_scan(hout):
        u = u_ref[...].reshape(n_rows, D_SSM)
        for b in range(2):
            bu = jnp.dot(u[:, b * 256:(b + 1) * 256], bdb_ref[b], preferred_element_type=F32)
            for part in range(2):
                for k in range(N_SLAB // 2):
                    slab = part * N_SLAB + b * (N_SLAB // 2) + k
                    col = part * half_cols + k * LANES
                    hbuf[slab * P:slab * P + n_rows, :] = bu[:, col:col + LANES]
        a = [(a_ref[SUBLANES * k:SUBLANES * (k + 1), :], a_ref[N_SLAB + SUBLANES * k:N_SLAB + SUBLANES * (k + 1), :])
             for k in blocks]
        for g0 in range(0, n_chain, chain_group):
            chains = list(range(g0, g0 + chain_group))
            st = {(c, k): (car[c, SUBLANES * k:SUBLANES * (k + 1), :],
                           car[c, N_SLAB + SUBLANES * k:N_SLAB + SUBLANES * (k + 1), :])
                  for c in chains for k in blocks}
            for t in range(T):
                for c in chains:
                    for k in blocks:
                        hr, hi = st[(c, k)]
                        rows_r = pl.ds(SUBLANES * k * P + c * T + t, SUBLANES, stride=P)
                        rows_i = pl.ds((N_SLAB + SUBLANES * k) * P + c * T + t, SUBLANES, stride=P)
                        pr, pi = _cmul(a[k][0], a[k][1], hr, hi)
                        nr = pr + hbuf[rows_r, :]
                        ni = pi + hbuf[rows_i, :]
                        hout[rows_r, :] = nr
                        hout[rows_i, :] = ni
                        st[(c, k)] = (nr, ni)
            for c in chains:
                for k in blocks:
                    hr, hi = st[(c, k)]
                    car[c, SUBLANES * k:SUBLANES * (k + 1), :] = hr
                    car[c, N_SLAB + SUBLANES * k:N_SLAB + SUBLANES * (k + 1), :] = hi

    def output_stage(hout):
        up = up_ref[...].reshape(n_rows, D_SSM)
        for b in range(2):
            for part in range(2):
                for k in range(N_SLAB // 2):
                    slab = part * N_SLAB + b * (N_SLAB // 2) + k
                    col = b * N_CH + part * half_cols + k * LANES
                    hb16[:, col:col + LANES] = hout[slab * P:slab * P + n_rows, :].astype(BF16)
        ys = [jnp.dot(hb16[:, b * N_CH:(b + 1) * N_CH], bdc_ref[b], preferred_element_type=F32) for b in range(2)]
        y = jnp.concatenate(ys, axis=1) + up.astype(F32) * d_ref[...]
        gb = jax.nn.gelu(y).astype(BF16)
        y1 = jnp.dot(gb, w1_ref[...], preferred_element_type=F32) + b1_ref[...]
        y2 = jnp.dot(gb, w2_ref[...], preferred_element_type=F32) + b2_ref[...]
        zs = zsp_ref[...].reshape(n_rows, D_SSM).astype(F32)
        out_ref[...] = (y1 * jax.nn.sigmoid(y2) * jax.nn.silu(zs)).astype(BF16).reshape(n_chain, T, D_SSM)

    @pl.when(i == 0)
    def _():
        car[...] = jnp.zeros_like(car) if chained else h0_ref[...]
        project_and_scan(hout_a)

    if n_steps > 1:
        @pl.when((i > 0) & (i < n_steps) & ((i & 1) == 1))
        def _():
            project_and_scan(hout_b)
            output_stage(hout_a)

        @pl.when((i > 0) & (i < n_steps) & ((i & 1) == 0))
        def _():
            project_and_scan(hout_a)
            output_stage(hout_b)

    @pl.when(i == n_steps)
    def _():
        output_stage(hout_a if (n_steps - 1) % 2 == 0 else hout_b)
        hf_ref[...] = car[...]


def _s5(u, zs, bdb, bdc, dvec, a2, w1, b1, w2, b2, h0, *, n_seq, seq_len, chained):
    if chained:
        T = S5_STEPS
        chain_group = n_seq
    else:
        T = seq_len
        chain_group = 4
    assert seq_len % T == 0 and n_seq % chain_group == 0 and T % (2 * SUBLANES) == 0
    n_steps = seq_len // T
    n_rows = n_seq * T
    u3 = u.reshape(n_seq, seq_len, D_SSM)
    zs3 = zs.reshape(n_seq, seq_len, D_SSM)
    cur = pl.BlockSpec((n_seq, T, D_SSM), lambda t: (0, jnp.minimum(t, n_steps - 1), 0))
    prev = pl.BlockSpec((n_seq, T, D_SSM), lambda t: (0, jnp.maximum(t - 1, 0), 0))
    st_shape = (n_seq, 2 * N_SLAB, LANES)
    slabs = pltpu.VMEM((2 * N_SLAB * (n_rows + SLAB_PAD), LANES), F32)
    out, hf = pl.pallas_call(
        functools.partial(_s5_kernel, n_chain=n_seq, T=T, n_steps=n_steps, chained=chained,
                          chain_group=chain_group),
        grid=(n_steps + 1,),
        in_specs=[cur, prev, prev, _full(bdb.shape), _full(bdc.shape), _full((1, D_SSM)),
                  _full((2 * N_SLAB, LANES)), _full((D_SSM, D_SSM)), _full((1, D_SSM)), _full((D_SSM, D_SSM)),
                  _full((1, D_SSM)), _full(st_shape)],
        out_specs=[prev, _full(st_shape)],
        out_shape=[jax.ShapeDtypeStruct((n_seq, seq_len, D_SSM), BF16), jax.ShapeDtypeStruct(st_shape, F32)],
        scratch_shapes=[slabs, slabs, slabs, pltpu.VMEM((n_rows, 2 * N_CH), BF16), pltpu.VMEM(st_shape, F32)],
        compiler_params=_params(("arbitrary",)),
        name="s5_chained" if chained else "s5_independent",
    )(u3, u3, zs3, bdb, bdc, dvec, a2, w1, b1, w2, b2, h0)
    return out.reshape(n_seq * seq_len, D_SSM), hf


def _lambda(lq1, lk1, lq2, lk2):
    s1 = jnp.sum(lq1[...] * lk1[...], axis=1, keepdims=True)
    s2 = jnp.sum(lq2[...] * lk2[...], axis=1, keepdims=True)
    return jnp.exp(s1) - jnp.exp(s2) + LAMBDA_INIT


def _stack_maps(q):
    lane = lax.broadcasted_iota(jnp.int32, q.shape, 1)
    zero = jnp.zeros_like(q)
    return jnp.concatenate([jnp.where(lane < QK_DIM, q, zero), jnp.where(lane >= QK_DIM, q, zero)], axis=0)


def _subln_gate(o, g, za):
    ms = jnp.mean(o * o, axis=-1, keepdims=True)
    on = (o * lax.rsqrt(ms + EPS) * g) * (1.0 - LAMBDA_INIT)
    return on * jax.nn.silu(za.astype(F32))


def _finish_head(acc, l, lam, g, za, t):
    inv = 1.0 / l
    o = acc[:t] * inv[:t] - lam * (acc[t:] * inv[t:])
    return _subln_gate(o, g, za)


_NT = (((1,), (1,)), ((), ()))


def _attn_kernel(lq1, lk1, lq2, lk2, g_ref, qt_ref, k_ref, vt_ref, za_ref, o_ref, m_sc, acc_sc, s_a, s_b, *, nq):
    tq = ATT_TILE
    step = pl.program_id(2)

    def finish():
        lam = _lambda(lq1, lk1, lq2, lk2)
        for h in range(HEADS_PER_STEP):
            acc = acc_sc[h, :V_DIM, :]
            inv = 1.0 / acc_sc[h, V_DIM:V_DIM + 1, :]
            ot = acc[:, :tq] * inv[:, :tq] - lam * (acc[:, tq:] * inv[:, tq:])
            cols = slice(h * LANES, (h + 1) * LANES)
            o_ref[:, cols] = _subln_gate(ot.T, g_ref[...], za_ref[:, cols]).astype(BF16)

    @pl.when((pl.program_id(0) == 0) & (pl.program_id(1) == 0) & (step == 0))
    def _():
        acc_sc[...] = jnp.ones_like(acc_sc)

    @pl.when(step < nq)
    def _():
        finish()
        _attn_tile(step, qt_ref, k_ref, vt_ref, m_sc, acc_sc, s_a, s_b)

    @pl.when(step == nq)
    def _():
        finish()


def _attn_tile(qi, qt_ref, k_ref, vt_ref, m_sc, acc_sc, s_a, s_b):
    tq = ATT_TILE
    q2t = []
    for h in range(HEADS_PER_STEP):
        qt = qt_ref[h]
        row = lax.broadcasted_iota(jnp.int32, qt.shape, 0)
        zero = jnp.zeros_like(qt)
        q2t.append(jnp.concatenate([jnp.where(row < QK_DIM, qt, zero), jnp.where(row >= QK_DIM, qt, zero)], axis=1))
    m_sc[...] = jnp.full_like(m_sc, NEG)
    acc_sc[...] = jnp.zeros_like(acc_sc)
    ones = jnp.ones((SUM_ROWS, tq), BF16)
    heads = range(HEADS_PER_STEP)
    tk = 2 * tq
    n_full = qi // 2
    odd = (qi & 1) == 1

    def scores_into(s_ref, kt, h):
        start = pl.multiple_of(kt * tk, tk)
        s_ref[h] = jnp.dot(k_ref[pl.ds(start, tk), h * LANES:(h + 1) * LANES], q2t[h],
                           preferred_element_type=F32)

    def softmax_accumulate(s_ref, kt, nkeys, masked, h):
        s = s_ref[h] if nkeys == tk else s_ref[h, :nkeys, :]
        if masked:
            kc = lax.broadcasted_iota(jnp.int32, s.shape, 0) // CHUNK + kt * (tk // CHUNK)
            qc = (lax.broadcasted_iota(jnp.int32, s.shape, 1) & (tq - 1)) // CHUNK + qi * (tq // CHUNK)
            s = jnp.where(kc <= qc, s, NEG)
        m_old = m_sc[h]
        m_new = jnp.maximum(m_old, jnp.max(s, axis=0, keepdims=True))
        alpha = jnp.exp2(m_old - m_new)
        pb = jnp.exp2(s - m_new).astype(BF16)
        pv = jnp.dot(jnp.concatenate([vt_ref[h, 2 * kt], ones], axis=0), pb[:tq], preferred_element_type=F32)
        if nkeys == tk:
            pv = pv + jnp.dot(jnp.concatenate([vt_ref[h, 2 * kt + 1], ones], axis=0), pb[tq:],
                              preferred_element_type=F32)
        acc_sc[h] = alpha * acc_sc[h] + pv
        m_sc[h] = m_new

    def stage(kt, s_cur, s_nxt):
        for h in heads:
            scores_into(s_nxt, kt + 1, h)
            softmax_accumulate(s_cur, kt, tk, False, h)

    for h in heads:
        scores_into(s_a, 0, h)

    def body(kt, c):
        even = (kt & 1) == 0

        @pl.when(even)
        def _():
            stage(kt, s_a, s_b)

        @pl.when(jnp.logical_not(even))
        def _():
            stage(kt, s_b, s_a)

        return c

    lax.fori_loop(0, n_full, body, 0)

    for parity, s_ref in ((0, s_a), (1, s_b)):
        here = (n_full & 1) == parity

        @pl.when(here & odd)
        def _(s_ref=s_ref):
            for h in heads:
                softmax_accumulate(s_ref, n_full, tk, True, h)

        @pl.when(here & jnp.logical_not(odd))
        def _(s_ref=s_ref):
            for h in heads:
                softmax_accumulate(s_ref, n_full, tq, True, h)


def _attn_prompt(lams, g, qt, kb, vt, za, *, n_seq, seq_len):
    tq = ATT_TILE
    nq = seq_len // tq
    hps = HEADS_PER_STEP
    assert tq % CHUNK == 0 and tq & (tq - 1) == 0 and ATT_HEADS % hps == 0 and seq_len % (2 * tq) == 0
    rowspec = pl.BlockSpec((tq, hps * LANES), lambda b, h, i: (b * nq + jnp.maximum(i - 1, 0), h))
    qtspec = pl.BlockSpec((None, hps, None, LANES, tq), lambda b, h, i: (b, h, jnp.minimum(i, nq - 1), 0, 0))
    kspec = pl.BlockSpec((seq_len, hps * LANES), lambda b, h, i: (b, h))
    vtspec = pl.BlockSpec((None, hps, nq, LANES, tq), lambda b, h, i: (b, h, 0, 0, 0))
    vec = _full((1, QK_DIM))
    stat = pltpu.VMEM((hps, 1, 2 * tq), F32)
    return pl.pallas_call(
        functools.partial(_attn_kernel, nq=nq),
        grid=(n_seq, ATT_HEADS // hps, nq + 1),
        in_specs=[vec, vec, vec, vec, _full((1, V_DIM)), qtspec, kspec, vtspec, rowspec],
        out_specs=rowspec,
        out_shape=jax.ShapeDtypeStruct((n_seq * seq_len, D_ATT), BF16),
        scratch_shapes=[stat, pltpu.VMEM((hps, V_DIM + SUM_ROWS, 2 * tq), F32),
                        pltpu.VMEM((hps, 2 * tq, 2 * tq), F32), pltpu.VMEM((hps, 2 * tq, 2 * tq), F32)],
        compiler_params=_params(("arbitrary", "arbitrary", "arbitrary")),
        name="attn_prompt",
    )(*lams, g, qt, kb, vt, za)


def _attn_sample_kernel(lq1, lk1, lq2, lk2, g_ref, q_ref, kn_ref, vn_ref, ck_ref, cv_ref, za_ref, o_ref, *, t, past):
    lam = _lambda(lq1, lk1, lq2, lk2)
    for j in range(SEQS_PER_STEP):
        rows = slice(j * t, (j + 1) * t)
        outs = []
        for h in range(ATT_HEADS):
            cols = slice(h * LANES, (h + 1) * LANES)
            q2 = _stack_maps(q_ref[rows, cols])
            kpt = jnp.concatenate([ck_ref[j, h, 0], ck_ref[j, h, 1]], axis=0).astype(BF16)
            vp = cv_ref[j, pl.ds(h, past, stride=ATT_HEADS), :].astype(BF16)
            s_p = jnp.dot(q2, kpt, preferred_element_type=F32)
            s_n = lax.dot_general(q2, kn_ref[rows, cols], _NT, preferred_element_type=F32)
            m = jnp.maximum(jnp.max(s_p, axis=1, keepdims=True), jnp.max(s_n, axis=1, keepdims=True))
            p_p = jnp.exp2(s_p - m)
            p_n = jnp.exp2(s_n - m)
            l = jnp.sum(p_p, axis=1, keepdims=True) + jnp.sum(p_n, axis=1, keepdims=True)
            acc = (jnp.dot(p_p.astype(BF16), vp, preferred_element_type=F32)
                   + jnp.dot(p_n.astype(BF16), vn_ref[rows, cols], preferred_element_type=F32))
            outs.append(_finish_head(acc, l, lam, g_ref[...], za_ref[rows, cols], t))
        o_ref[rows, :] = jnp.concatenate(outs, axis=1).astype(BF16)


def _attn_sample(lams, g, q, kb, vb, ck, cv, za, *, n_seq, t, past):
    sps = SEQS_PER_STEP
    assert past % CHUNK == 0 and t <= CHUNK and n_seq % sps == 0
    row = pl.BlockSpec((sps * t, D_ATT), lambda b: (b, 0))
    kcache = pl.BlockSpec((sps, ATT_HEADS, 2, QK_DIM, past), lambda b: (b, 0, 0, 0, 0))
    vcache = pl.BlockSpec((sps, past * ATT_HEADS, V_DIM), lambda b: (b, 0, 0))
    vec = _full((1, QK_DIM))
    return pl.pallas_call(
        functools.partial(_attn_sample_kernel, t=t, past=past),
        grid=(n_seq // sps,),
        in_specs=[vec, vec, vec, vec, _full((1, V_DIM)), row, row, row, kcache, vcache, row],
        out_specs=row,
        out_shape=jax.ShapeDtypeStruct((n_seq * t, D_ATT), BF16),
        compiler_params=_params(("arbitrary",)),
        name="attn_sample",
    )(*lams, g, q, kb, vb, ck, cv, za)


def _out_proj_kernel(s_ref, a_ref, w_ref, x_ref, g_ref, y_ref):
    tm = s_ref.shape[0]
    step = min(tm, OUT_CHUNK)
    for r in range(0, tm, step):
        rows = slice(r, r + step)
        mix = (jnp.dot(s_ref[rows, :], w_ref[:D_SSM, :], preferred_element_type=F32)
               + jnp.dot(a_ref[rows, :], w_ref[D_SSM:, :], preferred_element_type=F32))
        ms = jnp.mean(mix * mix, axis=-1, keepdims=True)
        y_ref[rows, :] = x_ref[rows, :] + mix * lax.rsqrt(ms + EPS) * g_ref[...]


def _out_proj(ssm_out, att_out, w_bf, x2d, g, *, tm):
    n = x2d.shape[0]
    assert n % tm == 0 and tm % min(tm, OUT_CHUNK) == 0
    row = lambda i: (i, 0)
    half = pl.BlockSpec((tm, 512), row)
    full = pl.BlockSpec((tm, D_MODEL), row)
    return pl.pallas_call(
        _out_proj_kernel,
        grid=(n // tm,),
        in_specs=[half, half, _full((D_MODEL, D_MODEL)), full, _full((1, D_MODEL))],
        out_specs=full,
        out_shape=jax.ShapeDtypeStruct((n, D_MODEL), F32),
        compiler_params=_params(("arbitrary",)),
        name="out_proj",
    )(ssm_out, att_out, w_bf, x2d, g)


def kernel(x_prompt, x_sample, cache_k, cache_v, state_ssm_re, state_ssm_im, norm_pre_g, w_in, ssm_lambda_re,
           ssm_lambda_im, ssm_log_dt, ssm_b_re, ssm_b_im, ssm_c_re, ssm_c_im, ssm_d, glu_w1, glu_b1, glu_w2,
           glu_b2, lambda_q1, lambda_k1, lambda_q2, lambda_k2, attn_subln_g, w_out, norm_post_g):
    bp, sp, _ = x_prompt.shape
    bs, ss, _ = x_sample.shape
    past = cache_k.shape[2]

    a_re, a_im, bdb, bdc = _prep(ssm_lambda_re[0], ssm_lambda_im[0], ssm_log_dt[0], ssm_b_re[0], ssm_b_im[0],
                                 ssm_c_re[0], ssm_c_im[0])
    a2 = jnp.concatenate([a_re.reshape(N_SLAB, LANES), a_im.reshape(N_SLAB, LANES)], axis=0)
    dvec = ssm_d[0].reshape(1, D_SSM)
    w_in_bf = w_in[0].astype(BF16)
    w_out_bf = w_out[0].astype(BF16)
    w1 = glu_w1[0].astype(BF16)
    w2 = glu_w2[0].astype(BF16)
    b1 = glu_b1[0].reshape(1, D_SSM)
    b2 = glu_b2[0].reshape(1, D_SSM)
    g_pre = norm_pre_g[0].reshape(1, D_MODEL)
    g_post = norm_post_g[0].reshape(1, D_MODEL)
    g_sub = attn_subln_g[0].reshape(1, V_DIM)
    lams = tuple(v[0].reshape(1, QK_DIM) for v in (lambda_q1, lambda_k1, lambda_q2, lambda_k2))
    inv = ROPE_THETA ** (-jnp.arange(ROPE_DIM // 2, dtype=F32) * 2.0 / ROPE_DIM)
    rotary_lane = (jnp.arange(LANES) % QK_DIM) < ROPE_DIM
    inv_lane = jnp.where(rotary_lane, jnp.tile(inv, LANES // (ROPE_DIM // 2)), 0.0).reshape(1, LANES)

    def run(x, n_seq, seq_len, pos0, chained, h0, tm):
        x2d = x.reshape(n_seq * seq_len, D_MODEL)
        u, zs, q, kf, kb, vf, vb, za = _in_proj(x2d, g_pre, w_in_bf, inv_lane, seq_len=seq_len, pos0=pos0, tm=tm,
                                                transposed_qv=chained)
        ssm_out, hf = _s5(u, zs, bdb, bdc, dvec, a2, w1, b1, w2, b2, h0,
                          n_seq=n_seq, seq_len=seq_len, chained=chained)
        if chained:
            att = _attn_prompt(lams, g_sub, q, kb, vb, za, n_seq=n_seq, seq_len=seq_len)
            k_out = jnp.transpose(kf, (0, 4, 1, 2, 3))[None]
        else:
            ck = jnp.transpose(cache_k[0], (0, 2, 3, 4, 1))
            cv = cache_v[0].reshape(n_seq, past * ATT_HEADS, V_DIM)
            att = _attn_sample(lams, g_sub, q, kb, vb, ck, cv, za, n_seq=n_seq, t=seq_len, past=past)
            k_out = kf.reshape(1, n_seq, seq_len, ATT_HEADS, 2, QK_DIM)
        y = _out_proj(ssm_out, att, w_out_bf, x2d, g_post, tm=min(OUT_TILE, n_seq * seq_len))
        return (y.reshape(n_seq, seq_len, D_MODEL),
                k_out,
                vf.reshape(1, n_seq, seq_len, ATT_HEADS, V_DIM),
                hf[:, :N_SLAB].reshape(1, n_seq, SSM_GROUPS, SSM_STATE),
                hf[:, N_SLAB:].reshape(1, n_seq, SSM_GROUPS, SSM_STATE))

    yp, kp, vp, hrp, hip = run(x_prompt, bp, sp, 0, True, jnp.zeros((bp, 2 * N_SLAB, LANES), F32), 512)
    h0 = jnp.concatenate([state_ssm_re[0].reshape(bs, N_SLAB, LANES), state_ssm_im[0].reshape(bs, N_SLAB, LANES)],
                         axis=1)
    ys, ks, vs, hrs, his = run(x_sample, bs, ss, past, False, h0, bs * ss)
    return (yp, ys, kp, vp, hrp, hip, ks, vs, hrs, his)
```

```python
import functools
import math

import jax
import jax.numpy as jnp
from jax import lax
from jax.experimental import pallas as pl
from jax.experimental.pallas import tpu as pltpu

F32 = jnp.float32
BF16 = jnp.bfloat16

D_MODEL = 1024
D_SSM = 512
D_ATT = 512
SSM_GROUP = 16
SSM_GROUPS = 32
SSM_STATE = 64
N_CH = SSM_GROUPS * SSM_STATE
ATT_HEADS = 4
QK_DIM = 64
V_DIM = 128
ROPE_DIM = 16
ROPE_THETA = 500000.0
CHUNK = 64
EPS = 1e-6
D_IN = 3072
LAMBDA_INIT = 0.8 - 0.6 * math.exp(-0.3 * 0)

LANES = 128
SUBLANES = 8
N_SLAB = N_CH // LANES
SEQS_PER_STEP = 2
OUT_TILE = 2048
OUT_CHUNK = 256
S5_STEPS = 256
SLAB_PAD = 4
NEG = -1e30
ATT_TILE = 256
KEY_TILES = 2
HEADS_PER_STEP = 4
SUM_ROWS = 16
LOG2E = math.log2(math.e)
Q_SCALE = QK_DIM ** -0.5 * LOG2E
VMEM_LIMIT = 56 * 1024 * 1024


def _params(sem):
    return pltpu.CompilerParams(dimension_semantics=sem, vmem_limit_bytes=VMEM_LIMIT)


def _full(shape):
    n = len(shape)
    return pl.BlockSpec(shape, lambda *_: (0,) * n)


def _prep_kernel(lr_ref, li_ref, ldt_ref, br_ref, bi_ref, cr_ref, ci_ref, ar_ref, ai_ref, bdb_ref, bdc_ref):
    lr = lr_ref[...]
    li = li_ref[...]
    dt = jnp.exp(ldt_ref[...])
    mag = jnp.exp(lr * dt)
    ar = mag * jnp.cos(li * dt)
    ai = mag * jnp.sin(li * dt)
    den = lr * lr + li * li
    cr = ((ar - 1.0) * lr + ai * li) / den
    ci = (ai * lr - (ar - 1.0) * li) / den
    ar_ref[...] = ar
    ai_ref[...] = ai
    br = br_ref[...]
    bi = bi_ref[...]
    crb = cr[:, None, :]
    cib = ci[:, None, :]
    bbar = (crb * br - cib * bi, crb * bi + cib * br)
    cmat = (cr_ref[...], -ci_ref[...])

    gh = SSM_GROUPS // 2
    rows_b, cols_b = gh * SSM_GROUP, gh * SSM_STATE
    spread_b = (lax.broadcasted_iota(jnp.int32, (SSM_STATE, cols_b), 1) & (SSM_STATE - 1)
                == lax.broadcasted_iota(jnp.int32, (SSM_STATE, cols_b), 0)).astype(BF16)
    keep_b = (lax.broadcasted_iota(jnp.int32, (rows_b, cols_b), 0) // SSM_GROUP
              == lax.broadcasted_iota(jnp.int32, (rows_b, cols_b), 1) // SSM_STATE)
    spread_c = (lax.broadcasted_iota(jnp.int32, (SSM_GROUP, rows_b), 1) & (SSM_GROUP - 1)
                == lax.broadcasted_iota(jnp.int32, (SSM_GROUP, rows_b), 0)).astype(BF16)
    keep_c = (lax.broadcasted_iota(jnp.int32, (cols_b, rows_b), 0) // SSM_STATE
              == lax.broadcasted_iota(jnp.int32, (cols_b, rows_b), 1) // SSM_GROUP)
    for b in range(2):
        for part in range(2):
            x = bbar[part][b * gh:(b + 1) * gh].reshape(rows_b, SSM_STATE).astype(BF16)
            t = jnp.dot(x, spread_b, preferred_element_type=F32)
            bdb_ref[b, :, part * cols_b:(part + 1) * cols_b] = jnp.where(keep_b, t, 0.0).astype(BF16)
            x = cmat[part][b * gh:(b + 1) * gh].reshape(cols_b, SSM_GROUP).astype(BF16)
            t = jnp.dot(x, spread_c, preferred_element_type=F32)
            bdc_ref[b, part * cols_b:(part + 1) * cols_b, :] = jnp.where(keep_c, t, 0.0).astype(BF16)


def _prep(lam_re, lam_im, log_dt, b_re, b_im, c_re, c_im):
    g, p, c = b_re.shape
    out_shape = (jax.ShapeDtypeStruct((g, p), F32), jax.ShapeDtypeStruct((g, p), F32),
                 jax.ShapeDtypeStruct((2, g // 2 * c, 2 * g // 2 * p), BF16),
                 jax.ShapeDtypeStruct((2, 2 * g // 2 * p, g // 2 * c), BF16))
    return pl.pallas_call(_prep_kernel, out_shape=out_shape, name="s5_prep")(
        lam_re, lam_im, log_dt.reshape(g, 1), jnp.swapaxes(b_re, 1, 2), jnp.swapaxes(b_im, 1, 2),
        jnp.swapaxes(c_re, 1, 2), jnp.swapaxes(c_im, 1, 2))


def _in_proj_kernel(x_ref, g_ref, w_ref, inv_ref, u_ref, zs_ref, q_ref, kf_ref, kb_ref, vf_ref,
                    vb_ref, za_ref, cl_sc, sl_sc, *, tm, seq_len, pos0, transposed_qv):
    i = pl.program_id(0)
    x = x_ref[...]
    ms = jnp.mean(x * x, axis=-1, keepdims=True)
    hn = (x * lax.rsqrt(ms + EPS) * g_ref[...]).astype(BF16)

    inv = inv_ref[...]

    @pl.when(i == 0)
    def _():
        off = (lax.broadcasted_iota(jnp.int32, (tm, LANES), 0) & (seq_len - 1)).astype(F32) * inv
        cl_sc[...] = jnp.cos(off)
        sl_sc[...] = jnp.sin(off)

    base = (pos0 + ((i * tm) & (seq_len - 1))).astype(F32) * jnp.broadcast_to(inv, (SUBLANES, LANES))
    cb = jnp.cos(base)[:1]
    sb = jnp.sin(base)[:1]
    cl = cl_sc[...]
    sl = sl_sc[...]
    c_m = cb * cl - sb * sl
    sin = sb * cl + cb * sl
    lane = lax.broadcasted_iota(jnp.int32, (tm, LANES), 1) & (QK_DIM - 1)
    half = ROPE_DIM // 2
    s_lo = jnp.where(lane < half, -sin, 0.0)
    s_hi = jnp.where(lane >= half, sin, 0.0)

    def seg(lo, hi):
        return jnp.dot(hn, w_ref[:, lo:hi], preferred_element_type=F32)

    def rope(t):
        outs = []
        for h in range(ATT_HEADS):
            th = t[:, h * LANES:(h + 1) * LANES]
            outs.append(th * c_m + pltpu.roll(th, LANES - half, 1) * s_lo + pltpu.roll(th, half, 1) * s_hi)
        return jnp.concatenate(outs, axis=1)

    def put(ref, t):
        if not transposed_qv:
            ref[...] = t.astype(BF16)
            return
        for h in range(ATT_HEADS):
            tt = t[:, h * LANES:(h + 1) * LANES].T.astype(BF16)
            for c in range(tm // ATT_TILE):
                ref[h, c] = tt[:, c * ATT_TILE:(c + 1) * ATT_TILE]

    u_ref[...] = seg(0, D_SSM).astype(BF16)
    zs_ref[...] = seg(D_SSM, 2 * D_SSM).astype(BF16)
    q = rope(seg(1024, 1536))
    put(q_ref, q * Q_SCALE)
    k = rope(seg(1536, 2048))
    if transposed_qv:
        for h in range(ATT_HEADS):
            kt = k[:, h * LANES:(h + 1) * LANES].T
            kf_ref[h, 0] = kt[:QK_DIM]
            kf_ref[h, 1] = kt[QK_DIM:]
    else:
        kf_ref[...] = k
    kb_ref[...] = k.astype(BF16)
    v = seg(2048, 2560)
    for h in range(ATT_HEADS):
        vf_ref[pl.ds(h, tm, stride=ATT_HEADS), :] = v[:, h * LANES:(h + 1) * LANES]
    put(vb_ref, v)
    za_ref[...] = seg(2560, 3072).astype(BF16)


def _in_proj(x2d, g, w_bf, inv_lane, *, seq_len, pos0, tm, transposed_qv):
    n = x2d.shape[0]
    assert n % tm == 0 and seq_len & (seq_len - 1) == 0 and (tm % seq_len == 0 or seq_len % tm == 0)
    row = lambda i: (i, 0)
    o512 = pl.BlockSpec((tm, 512), row)
    shp = lambda dt: jax.ShapeDtypeStruct((n, 512), dt)
    if transposed_qv:
        assert seq_len % tm == 0 and tm % ATT_TILE == 0
        tps = seq_len // tm
        per = tm // ATT_TILE
        t_spec = pl.BlockSpec((None, ATT_HEADS, per, LANES, ATT_TILE), lambda i: (i // tps, 0, i % tps, 0, 0))
        t_shape = jax.ShapeDtypeStruct((n // seq_len, ATT_HEADS, seq_len // ATT_TILE, LANES, ATT_TILE), BF16)
        kf_spec = pl.BlockSpec((None, ATT_HEADS, 2, QK_DIM, tm), lambda i: (i // tps, 0, 0, 0, i % tps))
        kf_shape = jax.ShapeDtypeStruct((n // seq_len, ATT_HEADS, 2, QK_DIM, seq_len), F32)
    else:
        t_spec, t_shape = o512, shp(BF16)
        kf_spec, kf_shape = o512, shp(F32)
    vf_spec = pl.BlockSpec((tm * ATT_HEADS, V_DIM), row)
    vf_shape = jax.ShapeDtypeStruct((n * ATT_HEADS, V_DIM), F32)
    return pl.pallas_call(
        functools.partial(_in_proj_kernel, tm=tm, seq_len=seq_len, pos0=pos0, transposed_qv=transposed_qv),
        grid=(n // tm,),
        in_specs=[pl.BlockSpec((tm, D_MODEL), row), _full((1, D_MODEL)), _full((D_MODEL, D_IN)),
                  _full((1, LANES))],
        out_specs=[o512, o512, t_spec, kf_spec, o512, vf_spec, t_spec, o512],
        out_shape=[shp(BF16), shp(BF16), t_shape, kf_shape, shp(BF16), vf_shape, t_shape, shp(BF16)],
        scratch_shapes=[pltpu.VMEM((tm, LANES), F32), pltpu.VMEM((tm, LANES), F32)],
        compiler_params=_params(("arbitrary",)),
        name="in_proj",
    )(x2d, g, w_bf, inv_lane)


def _cmul(ar, ai, br, bi):
    return ar * br - ai * bi, ar * bi + ai * br


def _s5_kernel(u_ref, up_ref, zsp_ref, bdb_ref, bdc_ref, d_ref, a_ref, w1_ref, b1_ref, w2_ref, b2_ref, h0_ref,
               out_ref, hf_ref, hbuf, hout_a, hout_b, hb16, car, *, n_chain, T, n_steps, chained, chain_group):
    n_rows = n_chain * T
    P = n_rows + SLAB_PAD
    half_cols = N_CH // 2
    i = pl.program_id(0)
    blocks = range(N_SLAB // SUBLANES)

    def project_and_scan(hout):
        u = u_ref[...].reshape(n_rows, D_SSM)
        for b in range(2):
            bu = jnp.dot(u[:, b * 256:(b + 1) * 256], bdb_ref[b], preferred_element_type=F32)
            for part in range(2):
                for k in range(N_SLAB // 2):
                    slab = part * N_SLAB + b * (N_SLAB // 2) + k
                    col = part * half_cols + k * LANES
                    hbuf[slab * P:slab * P + n_rows, :] = bu[:, col:col + LANES]
        a = [(a_ref[SUBLANES * k:SUBLANES * (k + 1), :], a_ref[N_SLAB + SUBLANES * k:N_SLAB + SUBLANES * (k + 1), :])
             for k in blocks]
        for g0 in range(0, n_chain, chain_group):
            chains = list(range(g0, g0 + chain_group))
            st = {(c, k): (car[c, SUBLANES * k:SUBLANES * (k + 1), :],
                           car[c, N_SLAB + SUBLANES * k:N_SLAB + SUBLANES * (k + 1), :])
                  for c in chains for k in blocks}
            for t in range(T):
                for c in chains:
                    for k in blocks:
                        hr, hi = st[(c, k)]
                        rows_r = pl.ds(SUBLANES * k * P + c * T + t, SUBLANES, stride=P)
                        rows_i = pl.ds((N_SLAB + SUBLANES * k) * P + c * T + t, SUBLANES, stride=P)
                        pr, pi = _cmul(a[k][0], a[k][1], hr, hi)
                        nr = pr + hbuf[rows_r, :]
                        ni = pi + hbuf[rows_i, :]
                        hout[rows_r, :] = nr
                        hout[rows_i, :] = ni
                        st[(c, k)] = (nr, ni)
            for c in chains:
                for k in blocks:
                    hr, hi = st[(c, k)]
                    car[c, SUBLANES * k:SUBLANES * (k + 1), :] = hr
                    car[c, N_SLAB + SUBLANES * k:N_SLAB + SUBLANES * (k + 1), :] = hi

    def output_stage(hout):
        up = up_ref[...].reshape(n_rows, D_SSM)
        for b in range(2):
            for part in range(2):
                for k in range(N_SLAB // 2):
                    slab = part * N_SLAB + b * (N_SLAB // 2) + k
                    col = b * N_CH + part * half_cols + k * LANES
                    hb16[:, col:col + LANES] = hout[slab * P:slab * P + n_rows, :].astype(BF16)
        ys = [jnp.dot(hb16[:, b * N_CH:(b + 1) * N_CH], bdc_ref[b], preferred_element_type=F32) for b in range(2)]
        y = jnp.concatenate(ys, axis=1) + up.astype(F32) * d_ref[...]
        gb = jax.nn.gelu(y).astype(BF16)
        y1 = jnp.dot(gb, w1_ref[...], preferred_element_type=F32) + b1_ref[...]
        y2 = jnp.dot(gb, w2_ref[...], preferred_element_type=F32) + b2_ref[...]
        zs = zsp_ref[...].reshape(n_rows, D_SSM).astype(F32)
        out_ref[...] = (y1 * jax.nn.sigmoid(y2) * jax.nn.silu(zs)).astype(BF16).reshape(n_chain, T, D_SSM)

    @pl.when(i == 0)
    def _():
        car[...] = jnp.zeros_like(car) if chained else h0_ref[...]
        project_and_scan(hout_a)

    if n_steps > 1:
        @pl.when((i > 0) & (i < n_steps) & ((i & 1) == 1))
        def _():
            project_and_scan(hout_b)
            output_stage(hout_a)

        @pl.when((i > 0) & (i < n_steps) & ((i & 1) == 0))
        def _():
            project_and_scan(hout_a)
            output_stage(hout_b)

    @pl.when(i == n_steps)
    def _():
        output_stage(hout_a if (n_steps - 1) % 2 == 0 else hout_b)
        hf_ref[...] = car[...]


def _s5(u, zs, bdb, bdc, dvec, a2, w1, b1, w2, b2, h0, *, n_seq, seq_len, chained):
    if chained:
        T = S5_STEPS
        chain_group = n_seq
    else:
        T = seq_len
        chain_group = 4
    assert seq_len % T == 0 and n_seq % chain_group == 0 and T % (2 * SUBLANES) == 0
    n_steps = seq_len // T
    n_rows = n_seq * T
    u3 = u.reshape(n_seq, seq_len, D_SSM)
    zs3 = zs.reshape(n_seq, seq_len, D_SSM)
    cur = pl.BlockSpec((n_seq, T, D_SSM), lambda t: (0, jnp.minimum(t, n_steps - 1), 0))
    prev = pl.BlockSpec((n_seq, T, D_SSM), lambda t: (0, jnp.maximum(t - 1, 0), 0))
    st_shape = (n_seq, 2 * N_SLAB, LANES)
    slabs = pltpu.VMEM((2 * N_SLAB * (n_rows + SLAB_PAD), LANES), F32)
    out, hf = pl.pallas_call(
        functools.partial(_s5_kernel, n_chain=n_seq, T=T, n_steps=n_steps, chained=chained,
                          chain_group=chain_group),
        grid=(n_steps + 1,),
        in_specs=[cur, prev, prev, _full(bdb.shape), _full(bdc.shape), _full((1, D_SSM)),
                  _full((2 * N_SLAB, LANES)), _full((D_SSM, D_SSM)), _full((1, D_SSM)), _full((D_SSM, D_SSM)),
                  _full((1, D_SSM)), _full(st_shape)],
        out_specs=[prev, _full(st_shape)],
        out_shape=[jax.ShapeDtypeStruct((n_seq, seq_len, D_SSM), BF16), jax.ShapeDtypeStruct(st_shape, F32)],
        scratch_shapes=[slabs, slabs, slabs, pltpu.VMEM((n_rows, 2 * N_CH), BF16), pltpu.VMEM(st_shape, F32)],
        compiler_params=_params(("arbitrary",)),
        name="s5_chained" if chained else "s5_independent",
    )(u3, u3, zs3, bdb, bdc, dvec, a2, w1, b1, w2, b2, h0)
    return out.reshape(n_seq * seq_len, D_SSM), hf


def _lambda(lq1, lk1, lq2, lk2):
    s1 = jnp.sum(lq1[...] * lk1[...], axis=1, keepdims=True)
    s2 = jnp.sum(lq2[...] * lk2[...], axis=1, keepdims=True)
    return jnp.exp(s1) - jnp.exp(s2) + LAMBDA_INIT


def _stack_maps(q):
    lane = lax.broadcasted_iota(jnp.int32, q.shape, 1)
    zero = jnp.zeros_like(q)
    return jnp.concatenate([jnp.where(lane < QK_DIM, q, zero), jnp.where(lane >= QK_DIM, q, zero)], axis=0)


def _subln_gate(o, g, za):
    ms = jnp.mean(o * o, axis=-1, keepdims=True)
    on = (o * lax.rsqrt(ms + EPS) * g) * (1.0 - LAMBDA_INIT)
    return on * jax.nn.silu(za.astype(F32))


def _finish_head(acc, l, lam, g, za, t):
    inv = 1.0 / l
    o = acc[:t] * inv[:t] - lam * (acc[t:] * inv[t:])
    return _subln_gate(o, g, za)


_NT = (((1,), (1,)), ((), ()))


def _attn_kernel(lq1, lk1, lq2, lk2, g_ref, qt_ref, k_ref, vt_ref, za_ref, o_ref, m_sc, acc_sc, s_a, s_b, *, nq):
    tq = ATT_TILE
    step = pl.program_id(2)

    def finish():
        lam = _lambda(lq1, lk1, lq2, lk2)
        for h in range(HEADS_PER_STEP):
            acc = acc_sc[h, :V_DIM, :]
            inv = 1.0 / acc_sc[h, V_DIM:V_DIM + 1, :]
            ot = acc[:, :tq] * inv[:, :tq] - lam * (acc[:, tq:] * inv[:, tq:])
            cols = slice(h * LANES, (h + 1) * LANES)
            o_ref[:, cols] = _subln_gate(ot.T, g_ref[...], za_ref[:, cols]).astype(BF16)

    @pl.when((pl.program_id(0) == 0) & (pl.program_id(1) == 0) & (step == 0))
    def _():
        acc_sc[...] = jnp.ones_like(acc_sc)

    @pl.when(step < nq)
    def _():
        finish()
        _attn_tile(step, qt_ref, k_ref, vt_ref, m_sc, acc_sc, s_a, s_b)

    @pl.when(step == nq)
    def _():
        finish()


def _attn_tile(qi, qt_ref, k_ref, vt_ref, m_sc, acc_sc, s_a, s_b):
    tq = ATT_TILE
    q2t = []
    for h in range(HEADS_PER_STEP):
        qt = qt_ref[h]
        row = lax.broadcasted_iota(jnp.int32, qt.shape, 0)
        zero = jnp.zeros_like(qt)
        q2t.append(jnp.concatenate([jnp.where(row < QK_DIM, qt, zero), jnp.where(row >= QK_DIM, qt, zero)], axis=1))
    m_sc[...] = jnp.full_like(m_sc, NEG)
    acc_sc[...] = jnp.zeros_like(acc_sc)
    ones = jnp.ones((SUM_ROWS, tq), BF16)
    heads = range(HEADS_PER_STEP)
    tk = KEY_TILES * tq
    n_full = qi // KEY_TILES
    rem = qi % KEY_TILES

    def scores_into(s_ref, kt, h):
        start = pl.multiple_of(kt * tk, tk)
        s_ref[h] = jnp.dot(k_ref[pl.ds(start, tk), h * LANES:(h + 1) * LANES], q2t[h],
                           preferred_element_type=F32)

    def softmax_accumulate(s_ref, kt, nkeys, masked, h):
        s = s_ref[h] if nkeys == tk else s_ref[h, :nkeys, :]
        if masked:
            kc = lax.broadcasted_iota(jnp.int32, s.shape, 0) // CHUNK + kt * (tk // CHUNK)
            qc = (lax.broadcasted_iota(jnp.int32, s.shape, 1) & (tq - 1)) // CHUNK + qi * (tq // CHUNK)
            s = jnp.where(kc <= qc, s, NEG)
        m_old = m_sc[h]
        m_new = jnp.maximum(m_old, jnp.max(s, axis=0, keepdims=True))
        alpha = jnp.exp2(m_old - m_new)
        pb = jnp.exp2(s - m_new).astype(BF16)
        pv = None
        for j in range(nkeys // tq):
            d = jnp.dot(jnp.concatenate([vt_ref[h, KEY_TILES * kt + j], ones], axis=0), pb[j * tq:(j + 1) * tq],
                        preferred_element_type=F32)
            pv = d if pv is None else pv + d
        acc_sc[h] = alpha * acc_sc[h] + pv
        m_sc[h] = m_new

    def stage(kt, s_cur, s_nxt):
        for h in heads:
            scores_into(s_nxt, kt + 1, h)
            softmax_accumulate(s_cur, kt, tk, False, h)

    for h in heads:
        scores_into(s_a, 0, h)

    def body(kt, c):
        even = (kt & 1) == 0

        @pl.when(even)
        def _():
            stage(kt, s_a, s_b)

        @pl.when(jnp.logical_not(even))
        def _():
            stage(kt, s_b, s_a)

        return c

    lax.fori_loop(0, n_full, body, 0)

    for parity, s_ref in ((0, s_a), (1, s_b)):
        here = (n_full & 1) == parity
        for r in range(KEY_TILES):
            @pl.when(here & (rem == r))
            def _(s_ref=s_ref, r=r):
                for h in heads:
                    softmax_accumulate(s_ref, n_full, (r + 1) * tq, True, h)


def _attn_prompt(lams, g, qt, kb, vt, za, *, n_seq, seq_len):
    tq = ATT_TILE
    nq = seq_len // tq
    hps = HEADS_PER_STEP
    assert tq % CHUNK == 0 and tq & (tq - 1) == 0 and ATT_HEADS % hps == 0 and seq_len % (KEY_TILES * tq) == 0
    rowspec = pl.BlockSpec((tq, hps * LANES), lambda b, h, i: (b * nq + jnp.maximum(i - 1, 0), h))
    qtspec = pl.BlockSpec((None, hps, None, LANES, tq), lambda b, h, i: (b, h, jnp.minimum(i, nq - 1), 0, 0))
    kspec = pl.BlockSpec((seq_len, hps * LANES), lambda b, h, i: (b, h))
    vtspec = pl.BlockSpec((None, hps, nq, LANES, tq), lambda b, h, i: (b, h, 0, 0, 0))
    vec = _full((1, QK_DIM))
    stat = pltpu.VMEM((hps, 1, 2 * tq), F32)
    return pl.pallas_call(
        functools.partial(_attn_kernel, nq=nq),
        grid=(n_seq, ATT_HEADS // hps, nq + 1),
        in_specs=[vec, vec, vec, vec, _full((1, V_DIM)), qtspec, kspec, vtspec, rowspec],
        out_specs=rowspec,
        out_shape=jax.ShapeDtypeStruct((n_seq * seq_len, D_ATT), BF16),
        scratch_shapes=[stat, pltpu.VMEM((hps, V_DIM + SUM_ROWS, 2 * tq), F32),
                        pltpu.VMEM((hps, KEY_TILES * tq, 2 * tq), F32),
                        pltpu.VMEM((hps, KEY_TILES * tq, 2 * tq), F32)],
        compiler_params=_params(("arbitrary", "arbitrary", "arbitrary")),
        name="attn_prompt",
    )(*lams, g, qt, kb, vt, za)


def _attn_sample_kernel(lq1, lk1, lq2, lk2, g_ref, q_ref, kn_ref, vn_ref, ck_ref, cv_ref, za_ref, o_ref, *, t, past):
    lam = _lambda(lq1, lk1, lq2, lk2)
    for j in range(SEQS_PER_STEP):
        rows = slice(j * t, (j + 1) * t)
        outs = []
        for h in range(ATT_HEADS):
            cols = slice(h * LANES, (h + 1) * LANES)
            q2 = _stack_maps(q_ref[rows, cols])
            kpt = jnp.concatenate([ck_ref[j, h, 0], ck_ref[j, h, 1]], axis=0).astype(BF16)
            vp = cv_ref[j, pl.ds(h, past, stride=ATT_HEADS), :].astype(BF16)
            s_p = jnp.dot(q2, kpt, preferred_element_type=F32)
            s_n = lax.dot_general(q2, kn_ref[rows, cols], _NT, preferred_element_type=F32)
            m = jnp.maximum(jnp.max(s_p, axis=1, keepdims=True), jnp.max(s_n, axis=1, keepdims=True))
            p_p = jnp.exp2(s_p - m)
            p_n = jnp.exp2(s_n - m)
            l = jnp.sum(p_p, axis=1, keepdims=True) + jnp.sum(p_n, axis=1, keepdims=True)
            acc = (jnp.dot(p_p.astype(BF16), vp, preferred_element_type=F32)
                   + jnp.dot(p_n.astype(BF16), vn_ref[rows, cols], preferred_element_type=F32))
            outs.append(_finish_head(acc, l, lam, g_ref[...], za_ref[rows, cols], t))
        o_ref[rows, :] = jnp.concatenate(outs, axis=1).astype(BF16)


def _attn_sample(lams, g, q, kb, vb, ck, cv, za, *, n_seq, t, past):
    sps = SEQS_PER_STEP
    assert past % CHUNK == 0 and t <= CHUNK and n_seq % sps == 0
    row = pl.BlockSpec((sps * t, D_ATT), lambda b: (b, 0))
    kcache = pl.BlockSpec((sps, ATT_HEADS, 2, QK_DIM, past), lambda b: (b, 0, 0, 0, 0))
    vcache = pl.BlockSpec((sps, past * ATT_HEADS, V_DIM), lambda b: (b, 0, 0))
    vec = _full((1, QK_DIM))
    return pl.pallas_call(
        functools.partial(_attn_sample_kernel, t=t, past=past),
        grid=(n_seq // sps,),
        in_specs=[vec, vec, vec, vec, _full((1, V_DIM)), row, row, row, kcache, vcache, row],
        out_specs=row,
        out_shape=jax.ShapeDtypeStruct((n_seq * t, D_ATT), BF16),
        compiler_params=_params(("arbitrary",)),
        name="attn_sample",
    )(*lams, g, q, kb, vb, ck, cv, za)


def _out_proj_kernel(s_ref, a_ref, w_ref, x_ref, g_ref, y_ref):
    tm = s_ref.shape[0]
    step = min(tm, OUT_CHUNK)
    for r in range(0, tm, step):
        rows = slice(r, r + step)
        mix = (jnp.dot(s_ref[rows, :], w_ref[:D_SSM, :], preferred_element_type=F32)
               + jnp.dot(a_ref[rows, :], w_ref[D_SSM:, :], preferred_element_type=F32))
        ms = jnp.mean(mix * mix, axis=-1, keepdims=True)
        y_ref[rows, :] = x_ref[rows, :] + mix * lax.rsqrt(ms + EPS) * g_ref[...]


def _out_proj(ssm_out, att_out, w_bf, x2d, g, *, tm):
    n = x2d.shape[0]
    assert n % tm == 0 and tm % min(tm, OUT_CHUNK) == 0
    row = lambda i: (i, 0)
    half = pl.BlockSpec((tm, 512), row)
    full = pl.BlockSpec((tm, D_MODEL), row)
    return pl.pallas_call(
        _out_proj_kernel,
        grid=(n // tm,),
        in_specs=[half, half, _full((D_MODEL, D_MODEL)), full, _full((1, D_MODEL))],
        out_specs=full,
        out_shape=jax.ShapeDtypeStruct((n, D_MODEL), F32),
        compiler_params=_params(("arbitrary",)),
        name="out_proj",
    )(ssm_out, att_out, w_bf, x2d, g)


def kernel(x_prompt, x_sample, cache_k, cache_v, state_ssm_re, state_ssm_im, norm_pre_g, w_in, ssm_lambda_re,
           ssm_lambda_im, ssm_log_dt, ssm_b_re, ssm_b_im, ssm_c_re, ssm_c_im, ssm_d, glu_w1, glu_b1, glu_w2,
           glu_b2, lambda_q1, lambda_k1, lambda_q2, lambda_k2, attn_subln_g, w_out, norm_post_g):
    bp, sp, _ = x_prompt.shape
    bs, ss, _ = x_sample.shape
    past = cache_k.shape[2]

    a_re, a_im, bdb, bdc = _prep(ssm_lambda_re[0], ssm_lambda_im[0], ssm_log_dt[0], ssm_b_re[0], ssm_b_im[0],
                                 ssm_c_re[0], ssm_c_im[0])
    a2 = jnp.concatenate([a_re.reshape(N_SLAB, LANES), a_im.reshape(N_SLAB, LANES)], axis=0)
    dvec = ssm_d[0].reshape(1, D_SSM)
    w_in_bf = w_in[0].astype(BF16)
    w_out_bf = w_out[0].astype(BF16)
    w1 = glu_w1[0].astype(BF16)
    w2 = glu_w2[0].astype(BF16)
    b1 = glu_b1[0].reshape(1, D_SSM)
    b2 = glu_b2[0].reshape(1, D_SSM)
    g_pre = norm_pre_g[0].reshape(1, D_MODEL)
    g_post = norm_post_g[0].reshape(1, D_MODEL)
    g_sub = attn_subln_g[0].reshape(1, V_DIM)
    lams = tuple(v[0].reshape(1, QK_DIM) for v in (lambda_q1, lambda_k1, lambda_q2, lambda_k2))
    inv = ROPE_THETA ** (-jnp.arange(ROPE_DIM // 2, dtype=F32) * 2.0 / ROPE_DIM)
    rotary_lane = (jnp.arange(LANES) % QK_DIM) < ROPE_DIM
    inv_lane = jnp.where(rotary_lane, jnp.tile(inv, LANES // (ROPE_DIM // 2)), 0.0).reshape(1, LANES)

    def run(x, n_seq, seq_len, pos0, chained, h0, tm):
        x2d = x.reshape(n_seq * seq_len, D_MODEL)
        u, zs, q, kf, kb, vf, vb, za = _in_proj(x2d, g_pre, w_in_bf, inv_lane, seq_len=seq_len, pos0=pos0, tm=tm,
                                                transposed_qv=chained)
        ssm_out, hf = _s5(u, zs, bdb, bdc, dvec, a2, w1, b1, w2, b2, h0,
                          n_seq=n_seq, seq_len=seq_len, chained=chained)
        if chained:
            att = _attn_prompt(lams, g_sub, q, kb, vb, za, n_seq=n_seq, seq_len=seq_len)
            k_out = jnp.transpose(kf, (0, 4, 1, 2, 3))[None]
        else:
            ck = jnp.transpose(cache_k[0], (0, 2, 3, 4, 1))
            cv = cache_v[0].reshape(n_seq, past * ATT_HEADS, V_DIM)
            att = _attn_sample(lams, g_sub, q, kb, vb, ck, cv, za, n_seq=n_seq, t=seq_len, past=past)
            k_out = kf.reshape(1, n_seq, seq_len, ATT_HEADS, 2, QK_DIM)
        y = _out_proj(ssm_out, att, w_out_bf, x2d, g_post, tm=min(OUT_TILE, n_seq * seq_len))
        return (y.reshape(n_seq, seq_len, D_MODEL),
                k_out,
                vf.reshape(1, n_seq, seq_len, ATT_HEADS, V_DIM),
                hf[:, :N_SLAB].reshape(1, n_seq, SSM_GROUPS, SSM_STATE),
                hf[:, N_SLAB:].reshape(1, n_seq, SSM_GROUPS, SSM_STATE))

    yp, kp, vp, hrp, hip = run(x_prompt, bp, sp, 0, True, jnp.zeros((bp, 2 * N_SLAB, LANES), F32), 512)
    h0 = jnp.concatenate([state_ssm_re[0].reshape(bs, N_SLAB, LANES), state_ssm_im[0].reshape(bs, N_SLAB, LANES)],
                         axis=1)
    ys, ks, vs, hrs, his = run(x_sample, bs, ss, past, False, h0, bs * ss)
    return (yp, ys, kp, vp, hrp, hip, ks, vs, hrs, his)
```

```python
import functools
import math

import jax
import jax.numpy as jnp
from jax import lax
from jax.experimental import pallas as pl
from jax.experimental.pallas import tpu as pltpu

F32 = jnp.float32
BF16 = jnp.bfloat16

D_MODEL = 1024
D_SSM = 512
D_ATT = 512
SSM_GROUP = 16
SSM_GROUPS = 32
SSM_STATE = 64
N_CH = SSM_GROUPS * SSM_STATE
ATT_HEADS = 4
QK_DIM = 64
V_DIM = 128
ROPE_DIM = 16
ROPE_THETA = 500000.0
CHUNK = 64
EPS = 1e-6
D_IN = 3072
LAMBDA_INIT = 0.8 - 0.6 * math.exp(-0.3 * 0)

HALF_IN = D_SSM // 2
SEG_U, SEG_ZS, SEG_Q, SEG_K, SEG_V, SEG_ZA, SEG_END = 0, 512, 1024, 1536, 2048, 2560, D_IN

LANES = 128
SUBLANES = 8
VMEM_LIMIT = 56 * 1024 * 1024
N_SLAB = N_CH // LANES
PROJ_TILE = 512
OUT_TILE = 2048
OUT_CHUNK = 256
S5_STEPS = 256
SLAB_PAD = 4
ATT_TILE = 256
KEY_TILES = 2
HEADS_PER_STEP = 4
SUM_ROWS = 16
SEQS_PER_STEP = 2
NEG = -1e30
LOG2E = math.log2(math.e)
Q_SCALE = QK_DIM ** -0.5 * LOG2E


def _params(sem):
    return pltpu.CompilerParams(dimension_semantics=sem, vmem_limit_bytes=VMEM_LIMIT)


def _full(shape):
    n = len(shape)
    return pl.BlockSpec(shape, lambda *_: (0,) * n)


def _prep_kernel(lr_ref, li_ref, ldt_ref, br_ref, bi_ref, cr_ref, ci_ref, ar_ref, ai_ref, bdb_ref, bdc_ref):
    lr = lr_ref[...]
    li = li_ref[...]
    dt = jnp.exp(ldt_ref[...])
    mag = jnp.exp(lr * dt)
    ar = mag * jnp.cos(li * dt)
    ai = mag * jnp.sin(li * dt)
    den = lr * lr + li * li
    cr = ((ar - 1.0) * lr + ai * li) / den
    ci = (ai * lr - (ar - 1.0) * li) / den
    ar_ref[...] = ar
    ai_ref[...] = ai
    br = br_ref[...]
    bi = bi_ref[...]
    crb = cr[:, None, :]
    cib = ci[:, None, :]
    bbar = (crb * br - cib * bi, crb * bi + cib * br)
    cmat = (cr_ref[...], -ci_ref[...])

    gh = SSM_GROUPS // 2
    rows_b, cols_b = gh * SSM_GROUP, gh * SSM_STATE
    spread_b = (lax.broadcasted_iota(jnp.int32, (SSM_STATE, cols_b), 1) & (SSM_STATE - 1)
                == lax.broadcasted_iota(jnp.int32, (SSM_STATE, cols_b), 0)).astype(BF16)
    keep_b = (lax.broadcasted_iota(jnp.int32, (rows_b, cols_b), 0) // SSM_GROUP
              == lax.broadcasted_iota(jnp.int32, (rows_b, cols_b), 1) // SSM_STATE)
    spread_c = (lax.broadcasted_iota(jnp.int32, (SSM_GROUP, rows_b), 1) & (SSM_GROUP - 1)
                == lax.broadcasted_iota(jnp.int32, (SSM_GROUP, rows_b), 0)).astype(BF16)
    keep_c = (lax.broadcasted_iota(jnp.int32, (cols_b, rows_b), 0) // SSM_STATE
              == lax.broadcasted_iota(jnp.int32, (cols_b, rows_b), 1) // SSM_GROUP)
    for b in range(2):
        for part in range(2):
            x = bbar[part][b * gh:(b + 1) * gh].reshape(rows_b, SSM_STATE).astype(BF16)
            t = jnp.dot(x, spread_b, preferred_element_type=F32)
            bdb_ref[b, :, part * cols_b:(part + 1) * cols_b] = jnp.where(keep_b, t, 0.0).astype(BF16)
            x = cmat[part][b * gh:(b + 1) * gh].reshape(cols_b, SSM_GROUP).astype(BF16)
            t = jnp.dot(x, spread_c, preferred_element_type=F32)
            bdc_ref[b, part * cols_b:(part + 1) * cols_b, :] = jnp.where(keep_c, t, 0.0).astype(BF16)


def _prep(lam_re, lam_im, log_dt, b_re, b_im, c_re, c_im):
    g, p, c = b_re.shape
    out_shape = (jax.ShapeDtypeStruct((g, p), F32), jax.ShapeDtypeStruct((g, p), F32),
                 jax.ShapeDtypeStruct((2, g // 2 * c, 2 * g // 2 * p), BF16),
                 jax.ShapeDtypeStruct((2, 2 * g // 2 * p, g // 2 * c), BF16))
    return pl.pallas_call(_prep_kernel, out_shape=out_shape, name="s5_prep")(
        lam_re, lam_im, log_dt.reshape(g, 1), jnp.swapaxes(b_re, 1, 2), jnp.swapaxes(b_im, 1, 2),
        jnp.swapaxes(c_re, 1, 2), jnp.swapaxes(c_im, 1, 2))


def _in_proj_kernel(x_ref, g_ref, w_ref, inv_ref, u_ref, zs_ref, q_ref, kf_ref, kb_ref, vf_ref,
                    vb_ref, za_ref, cl_sc, sl_sc, *, tm, seq_len, pos0, transposed_qv):
    i = pl.program_id(0)
    x = x_ref[...]
    ms = jnp.mean(x * x, axis=-1, keepdims=True)
    hn = (x * lax.rsqrt(ms + EPS) * g_ref[...]).astype(BF16)

    inv = inv_ref[...]

    @pl.when(i == 0)
    def _():
        off = (lax.broadcasted_iota(jnp.int32, (tm, LANES), 0) & (seq_len - 1)).astype(F32) * inv
        cl_sc[...] = jnp.cos(off)
        sl_sc[...] = jnp.sin(off)

    base = (pos0 + ((i * tm) & (seq_len - 1))).astype(F32) * jnp.broadcast_to(inv, (SUBLANES, LANES))
    cb = jnp.cos(base)[:1]
    sb = jnp.sin(base)[:1]
    cl = cl_sc[...]
    sl = sl_sc[...]
    c_m = cb * cl - sb * sl
    sin = sb * cl + cb * sl
    lane = lax.broadcasted_iota(jnp.int32, (tm, LANES), 1) & (QK_DIM - 1)
    half = ROPE_DIM // 2
    s_lo = jnp.where(lane < half, -sin, 0.0)
    s_hi = jnp.where(lane >= half, sin, 0.0)

    def seg(lo, hi):
        return jnp.dot(hn, w_ref[:, lo:hi], preferred_element_type=F32)

    def rope(t):
        outs = []
        for h in range(ATT_HEADS):
            th = t[:, h * LANES:(h + 1) * LANES]
            outs.append(th * c_m + pltpu.roll(th, LANES - half, 1) * s_lo + pltpu.roll(th, half, 1) * s_hi)
        return jnp.concatenate(outs, axis=1)

    def put(ref, t):
        if not transposed_qv:
            ref[...] = t.astype(BF16)
            return
        for h in range(ATT_HEADS):
            tt = t[:, h * LANES:(h + 1) * LANES].T.astype(BF16)
            for c in range(tm // ATT_TILE):
                ref[h, c] = tt[:, c * ATT_TILE:(c + 1) * ATT_TILE]

    u_ref[...] = seg(SEG_U, SEG_ZS).astype(BF16)
    zs_ref[...] = seg(SEG_ZS, SEG_Q).astype(BF16)
    q = rope(seg(SEG_Q, SEG_K))
    put(q_ref, q * Q_SCALE)
    k = rope(seg(SEG_K, SEG_V))
    if transposed_qv:
        for h in range(ATT_HEADS):
            kt = k[:, h * LANES:(h + 1) * LANES].T
            kf_ref[h, 0] = kt[:QK_DIM]
            kf_ref[h, 1] = kt[QK_DIM:]
    else:
        kf_ref[...] = k
    kb_ref[...] = k.astype(BF16)
    v = seg(SEG_V, SEG_ZA)
    for h in range(ATT_HEADS):
        vf_ref[pl.ds(h, tm, stride=ATT_HEADS), :] = v[:, h * LANES:(h + 1) * LANES]
    put(vb_ref, v)
    za_ref[...] = seg(SEG_ZA, SEG_END).astype(BF16)


def _in_proj(x2d, g, w_bf, inv_lane, *, seq_len, pos0, tm, transposed_qv):
    n = x2d.shape[0]
    assert n % tm == 0 and seq_len & (seq_len - 1) == 0 and (tm % seq_len == 0 or seq_len % tm == 0)
    row = lambda i: (i, 0)
    o512 = pl.BlockSpec((tm, D_ATT), row)
    shp = lambda dt: jax.ShapeDtypeStruct((n, D_ATT), dt)
    if transposed_qv:
        assert seq_len % tm == 0 and tm % ATT_TILE == 0
        tps = seq_len // tm
        per = tm // ATT_TILE
        t_spec = pl.BlockSpec((None, ATT_HEADS, per, LANES, ATT_TILE), lambda i: (i // tps, 0, i % tps, 0, 0))
        t_shape = jax.ShapeDtypeStruct((n // seq_len, ATT_HEADS, seq_len // ATT_TILE, LANES, ATT_TILE), BF16)
        kf_spec = pl.BlockSpec((None, ATT_HEADS, 2, QK_DIM, tm), lambda i: (i // tps, 0, 0, 0, i % tps))
        kf_shape = jax.ShapeDtypeStruct((n // seq_len, ATT_HEADS, 2, QK_DIM, seq_len), F32)
    else:
        t_spec, t_shape = o512, shp(BF16)
        kf_spec, kf_shape = o512, shp(F32)
    vf_spec = pl.BlockSpec((tm * ATT_HEADS, V_DIM), row)
    vf_shape = jax.ShapeDtypeStruct((n * ATT_HEADS, V_DIM), F32)
    return pl.pallas_call(
        functools.partial(_in_proj_kernel, tm=tm, seq_len=seq_len, pos0=pos0, transposed_qv=transposed_qv),
        grid=(n // tm,),
        in_specs=[pl.BlockSpec((tm, D_MODEL), row), _full((1, D_MODEL)), _full((D_MODEL, D_IN)),
                  _full((1, LANES))],
        out_specs=[o512, o512, t_spec, kf_spec, o512, vf_spec, t_spec, o512],
        out_shape=[shp(BF16), shp(BF16), t_shape, kf_shape, shp(BF16), vf_shape, t_shape, shp(BF16)],
        scratch_shapes=[pltpu.VMEM((tm, LANES), F32), pltpu.VMEM((tm, LANES), F32)],
        compiler_params=_params(("arbitrary",)),
        name="in_proj",
    )(x2d, g, w_bf, inv_lane)


def _cmul(ar, ai, br, bi):
    return ar * br - ai * bi, ar * bi + ai * br


def _s5_kernel(u_ref, up_ref, zsp_ref, bdb_ref, bdc_ref, d_ref, a_ref, w1_ref, b1_ref, w2_ref, b2_ref, h0_ref,
               out_ref, hf_ref, hbuf, hout_a, hout_b, hb16, car, *, n_chain, T, n_steps, chained, chain_group):
    n_rows = n_chain * T
    P = n_rows + SLAB_PAD
    half_cols = N_CH // 2
    i = pl.program_id(0)
    blocks = range(N_SLAB // SUBLANES)

    def project_and_scan(hout):
        u = u_ref[...].reshape(n_rows, D_SSM)
        for b in range(2):
            bu = jnp.dot(u[:, b * HALF_IN:(b + 1) * HALF_IN], bdb_ref[b], preferred_element_type=F32)
            for part in range(2):
                for k in range(N_SLAB // 2):
                    slab = part * N_SLAB + b * (N_SLAB // 2) + k
                    col = part * half_cols + k * LANES
                    hbuf[slab * P:slab * P + n_rows, :] = bu[:, col:col + LANES]
        a = [(a_ref[SUBLANES * k:SUBLANES * (k + 1), :], a_ref[N_SLAB + SUBLANES * k:N_SLAB + SUBLANES * (k + 1), :])
             for k in blocks]
        for g0 in range(0, n_chain, chain_group):
            chains = list(range(g0, g0 + chain_group))
            st = {(c, k): (car[c, SUBLANES * k:SUBLANES * (k + 1), :],
                           car[c, N_SLAB + SUBLANES * k:N_SLAB + SUBLANES * (k + 1), :])
                  for c in chains for k in blocks}
            for t in range(T):
                for c in chains:
                    for k in blocks:
                        hr, hi = st[(c, k)]
                        rows_r = pl.ds(SUBLANES * k * P + c * T + t, SUBLANES, stride=P)
                        rows_i = pl.ds((N_SLAB + SUBLANES * k) * P + c * T + t, SUBLANES, stride=P)
                        pr, pi = _cmul(a[k][0], a[k][1], hr, hi)
                        nr = pr + hbuf[rows_r, :]
                        ni = pi + hbuf[rows_i, :]
                        hout[rows_r, :] = nr
                        hout[rows_i, :] = ni
                        st[(c, k)] = (nr, ni)
            for c in chains:
                for k in blocks:
                    hr, hi = st[(c, k)]
                    car[c, SUBLANES * k:SUBLANES * (k + 1), :] = hr
                    car[c, N_SLAB + SUBLANES * k:N_SLAB + SUBLANES * (k + 1), :] = hi

    def output_stage(hout):
        up = up_ref[...].reshape(n_rows, D_SSM)
        for b in range(2):
            for part in range(2):
                for k in range(N_SLAB // 2):
                    slab = part * N_SLAB + b * (N_SLAB // 2) + k
                    col = b * N_CH + part * half_cols + k * LANES
                    hb16[:, col:col + LANES] = hout[slab * P:slab * P + n_rows, :].astype(BF16)
        ys = [jnp.dot(hb16[:, b * N_CH:(b + 1) * N_CH], bdc_ref[b], preferred_element_type=F32) for b in range(2)]
        y = jnp.concatenate(ys, axis=1) + up.astype(F32) * d_ref[...]
        gb = jax.nn.gelu(y).astype(BF16)
        y1 = jnp.dot(gb, w1_ref[...], preferred_element_type=F32) + b1_ref[...]
        y2 = jnp.dot(gb, w2_ref[...], preferred_element_type=F32) + b2_ref[...]
        zs = zsp_ref[...].reshape(n_rows, D_SSM).astype(F32)
        out_ref[...] = (y1 * jax.nn.sigmoid(y2) * jax.nn.silu(zs)).astype(BF16).reshape(n_chain, T, D_SSM)

    @pl.when(i == 0)
    def _():
        car[...] = jnp.zeros_like(car) if chained else h0_ref[...]
        project_and_scan(hout_a)

    if n_steps > 1:
        @pl.when((i > 0) & (i < n_steps) & ((i & 1) == 1))
        def _():
            project_and_scan(hout_b)
            output_stage(hout_a)

        @pl.when((i > 0) & (i < n_steps) & ((i & 1) == 0))
        def _():
            project_and_scan(hout_a)
            output_stage(hout_b)

    @pl.when(i == n_steps)
    def _():
        output_stage(hout_a if (n_steps - 1) % 2 == 0 else hout_b)
        hf_ref[...] = car[...]


def _s5(u, zs, bdb, bdc, dvec, a2, w1, b1, w2, b2, h0, *, n_seq, seq_len, chained):
    if chained:
        T = S5_STEPS
        chain_group = n_seq
    else:
        T = seq_len
        chain_group = 4
    assert seq_len % T == 0 and n_seq % chain_group == 0 and T % (2 * SUBLANES) == 0
    n_steps = seq_len // T
    n_rows = n_seq * T
    u3 = u.reshape(n_seq, seq_len, D_SSM)
    zs3 = zs.reshape(n_seq, seq_len, D_SSM)
    cur = pl.BlockSpec((n_seq, T, D_SSM), lambda t: (0, jnp.minimum(t, n_steps - 1), 0))
    prev = pl.BlockSpec((n_seq, T, D_SSM), lambda t: (0, jnp.maximum(t - 1, 0), 0))
    st_shape = (n_seq, 2 * N_SLAB, LANES)
    slabs = pltpu.VMEM((2 * N_SLAB * (n_rows + SLAB_PAD), LANES), F32)
    out, hf = pl.pallas_call(
        functools.partial(_s5_kernel, n_chain=n_seq, T=T, n_steps=n_steps, chained=chained,
                          chain_group=chain_group),
        grid=(n_steps + 1,),
        in_specs=[cur, prev, prev, _full(bdb.shape), _full(bdc.shape), _full((1, D_SSM)),
                  _full((2 * N_SLAB, LANES)), _full((D_SSM, D_SSM)), _full((1, D_SSM)), _full((D_SSM, D_SSM)),
                  _full((1, D_SSM)), _full(st_shape)],
        out_specs=[prev, _full(st_shape)],
        out_shape=[jax.ShapeDtypeStruct((n_seq, seq_len, D_SSM), BF16), jax.ShapeDtypeStruct(st_shape, F32)],
        scratch_shapes=[slabs, slabs, slabs, pltpu.VMEM((n_rows, 2 * N_CH), BF16), pltpu.VMEM(st_shape, F32)],
        compiler_params=_params(("arbitrary",)),
        name="s5_chained" if chained else "s5_independent",
    )(u3, u3, zs3, bdb, bdc, dvec, a2, w1, b1, w2, b2, h0)
    return out.reshape(n_seq * seq_len, D_SSM), hf


def _lambda(lq1, lk1, lq2, lk2):
    s1 = jnp.sum(lq1[...] * lk1[...], axis=1, keepdims=True)
    s2 = jnp.sum(lq2[...] * lk2[...], axis=1, keepdims=True)
    return jnp.exp(s1) - jnp.exp(s2) + LAMBDA_INIT


def _stack_maps(q):
    lane = lax.broadcasted_iota(jnp.int32, q.shape, 1)
    zero = jnp.zeros_like(q)
    return jnp.concatenate([jnp.where(lane < QK_DIM, q, zero), jnp.where(lane >= QK_DIM, q, zero)], axis=0)


def _subln_gate(o, g, za):
    ms = jnp.mean(o * o, axis=-1, keepdims=True)
    on = (o * lax.rsqrt(ms + EPS) * g) * (1.0 - LAMBDA_INIT)
    return on * jax.nn.silu(za.astype(F32))


def _finish_head(acc, l, lam, g, za, t):
    inv = 1.0 / l
    o = acc[:t] * inv[:t] - lam * (acc[t:] * inv[t:])
    return _subln_gate(o, g, za)


_NT = (((1,), (1,)), ((), ()))


def _attn_kernel(lq1, lk1, lq2, lk2, g_ref, qt_ref, k_ref, vt_ref, za_ref, o_ref, m_sc, acc_sc, s_a, s_b, *, nq):
    tq = ATT_TILE
    step = pl.program_id(2)

    def finish():
        lam = _lambda(lq1, lk1, lq2, lk2)
        for h in range(HEADS_PER_STEP):
            acc = acc_sc[h, :V_DIM, :]
            inv = 1.0 / acc_sc[h, V_DIM:V_DIM + 1, :]
            ot = acc[:, :tq] * inv[:, :tq] - lam * (acc[:, tq:] * inv[:, tq:])
            cols = slice(h * LANES, (h + 1) * LANES)
            o_ref[:, cols] = _subln_gate(ot.T, g_ref[...], za_ref[:, cols]).astype(BF16)

    @pl.when((pl.program_id(0) == 0) & (pl.program_id(1) == 0) & (step == 0))
    def _():
        acc_sc[...] = jnp.ones_like(acc_sc)

    @pl.when(step < nq)
    def _():
        finish()
        _attn_tile(step, qt_ref, k_ref, vt_ref, m_sc, acc_sc, s_a, s_b)

    @pl.when(step == nq)
    def _():
        finish()


def _attn_tile(qi, qt_ref, k_ref, vt_ref, m_sc, acc_sc, s_a, s_b):
    tq = ATT_TILE
    q2t = []
    for h in range(HEADS_PER_STEP):
        qt = qt_ref[h]
        row = lax.broadcasted_iota(jnp.int32, qt.shape, 0)
        zero = jnp.zeros_like(qt)
        q2t.append(jnp.concatenate([jnp.where(row < QK_DIM, qt, zero), jnp.where(row >= QK_DIM, qt, zero)], axis=1))
    m_sc[...] = jnp.full_like(m_sc, NEG)
    acc_sc[...] = jnp.zeros_like(acc_sc)
    ones = jnp.ones((SUM_ROWS, tq), BF16)
    heads = range(HEADS_PER_STEP)
    tk = KEY_TILES * tq
    n_full = qi // KEY_TILES
    rem = qi % KEY_TILES

    def scores_into(s_ref, kt, h):
        start = pl.multiple_of(kt * tk, tk)
        s_ref[h] = jnp.dot(k_ref[pl.ds(start, tk), h * LANES:(h + 1) * LANES], q2t[h],
                           preferred_element_type=F32)

    def softmax_accumulate(s_ref, kt, nkeys, masked, h):
        s = s_ref[h] if nkeys == tk else s_ref[h, :nkeys, :]
        if masked:
            kc = lax.broadcasted_iota(jnp.int32, s.shape, 0) // CHUNK + kt * (tk // CHUNK)
            qc = (lax.broadcasted_iota(jnp.int32, s.shape, 1) & (tq - 1)) // CHUNK + qi * (tq // CHUNK)
            s = jnp.where(kc <= qc, s, NEG)
        m_old = m_sc[h]
        m_new = jnp.maximum(m_old, jnp.max(s, axis=0, keepdims=True))
        alpha = jnp.exp2(m_old - m_new)
        pb = jnp.exp2(s - m_new).astype(BF16)
        pv = None
        for j in range(nkeys // tq):
            d = jnp.dot(jnp.concatenate([vt_ref[h, KEY_TILES * kt + j], ones], axis=0), pb[j * tq:(j + 1) * tq],
                        preferred_element_type=F32)
            pv = d if pv is None else pv + d
        acc_sc[h] = alpha * acc_sc[h] + pv
        m_sc[h] = m_new

    def stage(kt, s_cur, s_nxt):
        for h in heads:
            scores_into(s_nxt, kt + 1, h)
            softmax_accumulate(s_cur, kt, tk, False, h)

    for h in heads:
        scores_into(s_a, 0, h)

    def body(kt, c):
        even = (kt & 1) == 0

        @pl.when(even)
        def _():
            stage(kt, s_a, s_b)

        @pl.when(jnp.logical_not(even))
        def _():
            stage(kt, s_b, s_a)

        return c

    lax.fori_loop(0, n_full, body, 0)

    for parity, s_ref in ((0, s_a), (1, s_b)):
        here = (n_full & 1) == parity
        for r in range(KEY_TILES):
            @pl.when(here & (rem == r))
            def _(s_ref=s_ref, r=r):
                for h in heads:
                    softmax_accumulate(s_ref, n_full, (r + 1) * tq, True, h)


def _attn_prompt(lams, g, qt, kb, vt, za, *, n_seq, seq_len):
    tq = ATT_TILE
    nq = seq_len // tq
    hps = HEADS_PER_STEP
    assert tq % CHUNK == 0 and tq & (tq - 1) == 0 and ATT_HEADS % hps == 0 and seq_len % (KEY_TILES * tq) == 0
    rowspec = pl.BlockSpec((tq, hps * LANES), lambda b, h, i: (b * nq + jnp.maximum(i - 1, 0), h))
    qtspec = pl.BlockSpec((None, hps, None, LANES, tq), lambda b, h, i: (b, h, jnp.minimum(i, nq - 1), 0, 0))
    kspec = pl.BlockSpec((seq_len, hps * LANES), lambda b, h, i: (b, h))
    vtspec = pl.BlockSpec((None, hps, nq, LANES, tq), lambda b, h, i: (b, h, 0, 0, 0))
    vec = _full((1, QK_DIM))
    stat = pltpu.VMEM((hps, 1, 2 * tq), F32)
    return pl.pallas_call(
        functools.partial(_attn_kernel, nq=nq),
        grid=(n_seq, ATT_HEADS // hps, nq + 1),
        in_specs=[vec, vec, vec, vec, _full((1, V_DIM)), qtspec, kspec, vtspec, rowspec],
        out_specs=rowspec,
        out_shape=jax.ShapeDtypeStruct((n_seq * seq_len, D_ATT), BF16),
        scratch_shapes=[stat, pltpu.VMEM((hps, V_DIM + SUM_ROWS, 2 * tq), F32),
                        pltpu.VMEM((hps, KEY_TILES * tq, 2 * tq), F32),
                        pltpu.VMEM((hps, KEY_TILES * tq, 2 * tq), F32)],
        compiler_params=_params(("arbitrary", "arbitrary", "arbitrary")),
        name="attn_prompt",
    )(*lams, g, qt, kb, vt, za)


def _attn_sample_kernel(lq1, lk1, lq2, lk2, g_ref, q_ref, kn_ref, vn_ref, ck_ref, cv_ref, za_ref, o_ref, *, t, past):
    lam = _lambda(lq1, lk1, lq2, lk2)
    for j in range(SEQS_PER_STEP):
        rows = slice(j * t, (j + 1) * t)
        outs = []
        for h in range(ATT_HEADS):
            cols = slice(h * LANES, (h + 1) * LANES)
            q2 = _stack_maps(q_ref[rows, cols])
            kpt = jnp.concatenate([ck_ref[j, h, 0], ck_ref[j, h, 1]], axis=0).astype(BF16)
            vp = cv_ref[j, pl.ds(h, past, stride=ATT_HEADS), :].astype(BF16)
            s_p = jnp.dot(q2, kpt, preferred_element_type=F32)
            s_n = lax.dot_general(q2, kn_ref[rows, cols], _NT, preferred_element_type=F32)
            m = jnp.maximum(jnp.max(s_p, axis=1, keepdims=True), jnp.max(s_n, axis=1, keepdims=True))
            p_p = jnp.exp2(s_p - m)
            p_n = jnp.exp2(s_n - m)
            l = jnp.sum(p_p, axis=1, keepdims=True) + jnp.sum(p_n, axis=1, keepdims=True)
            acc = (jnp.dot(p_p.astype(BF16), vp, preferred_element_type=F32)
                   + jnp.dot(p_n.astype(BF16), vn_ref[rows, cols], preferred_element_type=F32))
            outs.append(_finish_head(acc, l, lam, g_ref[...], za_ref[rows, cols], t))
        o_ref[rows, :] = jnp.concatenate(outs, axis=1).astype(BF16)


def _attn_sample(lams, g, q, kb, vb, ck, cv, za, *, n_seq, t, past):
    sps = SEQS_PER_STEP
    assert past % CHUNK == 0 and t <= CHUNK and n_seq % sps == 0
    row = pl.BlockSpec((sps * t, D_ATT), lambda b: (b, 0))
    kcache = pl.BlockSpec((sps, ATT_HEADS, 2, QK_DIM, past), lambda b: (b, 0, 0, 0, 0))
    vcache = pl.BlockSpec((sps, past * ATT_HEADS, V_DIM), lambda b: (b, 0, 0))
    vec = _full((1, QK_DIM))
    return pl.pallas_call(
        functools.partial(_attn_sample_kernel, t=t, past=past),
        grid=(n_seq // sps,),
        in_specs=[vec, vec, vec, vec, _full((1, V_DIM)), row, row, row, kcache, vcache, row],
        out_specs=row,
        out_shape=jax.ShapeDtypeStruct((n_seq * t, D_ATT), BF16),
        compiler_params=_params(("arbitrary",)),
        name="attn_sample",
    )(*lams, g, q, kb, vb, ck, cv, za)


def _out_proj_kernel(s_ref, a_ref, w_ref, x_ref, g_ref, y_ref):
    tm = s_ref.shape[0]
    step = min(tm, OUT_CHUNK)
    for r in range(0, tm, step):
        rows = slice(r, r + step)
        mix = (jnp.dot(s_ref[rows, :], w_ref[:D_SSM, :], preferred_element_type=F32)
               + jnp.dot(a_ref[rows, :], w_ref[D_SSM:, :], preferred_element_type=F32))
        ms = jnp.mean(mix * mix, axis=-1, keepdims=True)
        y_ref[rows, :] = x_ref[rows, :] + mix * lax.rsqrt(ms + EPS) * g_ref[...]


def _out_proj(ssm_out, att_out, w_bf, x2d, g, *, tm):
    n = x2d.shape[0]
    assert n % tm == 0 and tm % min(tm, OUT_CHUNK) == 0
    row = lambda i: (i, 0)
    half = pl.BlockSpec((tm, D_SSM), row)
    full = pl.BlockSpec((tm, D_MODEL), row)
    return pl.pallas_call(
        _out_proj_kernel,
        grid=(n // tm,),
        in_specs=[half, half, _full((D_MODEL, D_MODEL)), full, _full((1, D_MODEL))],
        out_specs=full,
        out_shape=jax.ShapeDtypeStruct((n, D_MODEL), F32),
        compiler_params=_params(("arbitrary",)),
        name="out_proj",
    )(ssm_out, att_out, w_bf, x2d, g)


def kernel(x_prompt, x_sample, cache_k, cache_v, state_ssm_re, state_ssm_im, norm_pre_g, w_in, ssm_lambda_re,
           ssm_lambda_im, ssm_log_dt, ssm_b_re, ssm_b_im, ssm_c_re, ssm_c_im, ssm_d, glu_w1, glu_b1, glu_w2,
           glu_b2, lambda_q1, lambda_k1, lambda_q2, lambda_k2, attn_subln_g, w_out, norm_post_g):
    bp, sp, _ = x_prompt.shape
    bs, ss, _ = x_sample.shape
    past = cache_k.shape[2]

    a_re, a_im, bdb, bdc = _prep(ssm_lambda_re[0], ssm_lambda_im[0], ssm_log_dt[0], ssm_b_re[0], ssm_b_im[0],
                                 ssm_c_re[0], ssm_c_im[0])
    a2 = jnp.concatenate([a_re.reshape(N_SLAB, LANES), a_im.reshape(N_SLAB, LANES)], axis=0)
    dvec = ssm_d[0].reshape(1, D_SSM)
    w_in_bf = w_in[0].astype(BF16)
    w_out_bf = w_out[0].astype(BF16)
    w1 = glu_w1[0].astype(BF16)
    w2 = glu_w2[0].astype(BF16)
    b1 = glu_b1[0].reshape(1, D_SSM)
    b2 = glu_b2[0].reshape(1, D_SSM)
    g_pre = norm_pre_g[0].reshape(1, D_MODEL)
    g_post = norm_post_g[0].reshape(1, D_MODEL)
    g_sub = attn_subln_g[0].reshape(1, V_DIM)
    lams = tuple(v[0].reshape(1, QK_DIM) for v in (lambda_q1, lambda_k1, lambda_q2, lambda_k2))
    inv = ROPE_THETA ** (-jnp.arange(ROPE_DIM // 2, dtype=F32) * 2.0 / ROPE_DIM)
    rotary_lane = (jnp.arange(LANES) % QK_DIM) < ROPE_DIM
    inv_lane = jnp.where(rotary_lane, jnp.tile(inv, LANES // (ROPE_DIM // 2)), 0.0).reshape(1, LANES)

    def run(x, n_seq, seq_len, pos0, chained, h0, tm):
        x2d = x.reshape(n_seq * seq_len, D_MODEL)
        u, zs, q, kf, kb, vf, vb, za = _in_proj(x2d, g_pre, w_in_bf, inv_lane, seq_len=seq_len, pos0=pos0, tm=tm,
                                                transposed_qv=chained)
        ssm_out, hf = _s5(u, zs, bdb, bdc, dvec, a2, w1, b1, w2, b2, h0,
                          n_seq=n_seq, seq_len=seq_len, chained=chained)
        if chained:
            att = _attn_prompt(lams, g_sub, q, kb, vb, za, n_seq=n_seq, seq_len=seq_len)
            k_out = jnp.transpose(kf, (0, 4, 1, 2, 3))[None]
        else:
            ck = jnp.transpose(cache_k[0], (0, 2, 3, 4, 1))
            cv = cache_v[0].reshape(n_seq, past * ATT_HEADS, V_DIM)
            att = _attn_sample(lams, g_sub, q, kb, vb, ck, cv, za, n_seq=n_seq, t=seq_len, past=past)
            k_out = kf.reshape(1, n_seq, seq_len, ATT_HEADS, 2, QK_DIM)
        y = _out_proj(ssm_out, att, w_out_bf, x2d, g_post, tm=min(OUT_TILE, n_seq * seq_len))
        return (y.reshape(n_seq, seq_len, D_MODEL),
                k_out,
                vf.reshape(1, n_seq, seq_len, ATT_HEADS, V_DIM),
                hf[:, :N_SLAB].reshape(1, n_seq, SSM_GROUPS, SSM_STATE),
                hf[:, N_SLAB:].reshape(1, n_seq, SSM_GROUPS, SSM_STATE))

    yp, kp, vp, hrp, hip = run(x_prompt, bp, sp, 0, True, jnp.zeros((bp, 2 * N_SLAB, LANES), F32), PROJ_TILE)
    h0 = jnp.concatenate([state_ssm_re[0].reshape(bs, N_SLAB, LANES), state_ssm_im[0].reshape(bs, N_SLAB, LANES)],
                         axis=1)
    ys, ks, vs, hrs, his = run(x_sample, bs, ss, past, False, h0, bs * ss)
    return (yp, ys, kp, vp, hrp, hip, ks, vs, hrs, his)
```

```python
import functools
import math

import jax
import jax.numpy as jnp
from jax import lax
from jax.experimental import pallas as pl
from jax.experimental.pallas import tpu as pltpu

F32 = jnp.float32
BF16 = jnp.bfloat16

D_MODEL = 1024
D_SSM = 512
D_ATT = 512
SSM_GROUP = 16
SSM_GROUPS = 32
SSM_STATE = 64
N_CH = SSM_GROUPS * SSM_STATE
ATT_HEADS = 4
QK_DIM = 64
V_DIM = 128
ROPE_DIM = 16
ROPE_THETA = 500000.0
CHUNK = 64
EPS = 1e-6
D_IN = 3072
LAMBDA_INIT = 0.8 - 0.6 * math.exp(-0.3 * 0)

HALF_IN = D_SSM // 2
SEG_U, SEG_ZS, SEG_Q, SEG_K, SEG_V, SEG_ZA, SEG_END = 0, 512, 1024, 1536, 2048, 2560, D_IN

LANES = 128
SUBLANES = 8
VMEM_LIMIT = 56 * 1024 * 1024
N_SLAB = N_CH // LANES
PROJ_TILE = 512
OUT_TILE = 2048
OUT_CHUNK = 256
S5_STEPS = 256
SLAB_PAD = 4
ATT_TILE = 256
KEY_TILES = 2
HEADS_PER_STEP = 4
SUM_ROWS = 16
SEQS_PER_STEP = 2
NEG = -3.0e38
LOG2E = math.log2(math.e)
Q_SCALE = QK_DIM ** -0.5 * LOG2E


def _params(sem):
    return pltpu.CompilerParams(dimension_semantics=sem, vmem_limit_bytes=VMEM_LIMIT)


def _full(shape):
    n = len(shape)
    return pl.BlockSpec(shape, lambda *_: (0,) * n)


def _prep_kernel(lr_ref, li_ref, ldt_ref, br_ref, bi_ref, cr_ref, ci_ref, ar_ref, ai_ref, bdb_ref, bdc_ref):
    lr = lr_ref[...]
    li = li_ref[...]
    dt = jnp.exp(ldt_ref[...])
    mag = jnp.exp(lr * dt)
    ar = mag * jnp.cos(li * dt)
    ai = mag * jnp.sin(li * dt)
    den = lr * lr + li * li
    cr = ((ar - 1.0) * lr + ai * li) / den
    ci = (ai * lr - (ar - 1.0) * li) / den
    ar_ref[...] = ar
    ai_ref[...] = ai
    br = br_ref[...]
    bi = bi_ref[...]
    crb = cr[:, None, :]
    cib = ci[:, None, :]
    bbar = (crb * br - cib * bi, crb * bi + cib * br)
    cmat = (cr_ref[...], -ci_ref[...])

    gh = SSM_GROUPS // 2
    rows_b, cols_b = gh * SSM_GROUP, gh * SSM_STATE
    spread_b = (lax.broadcasted_iota(jnp.int32, (SSM_STATE, cols_b), 1) & (SSM_STATE - 1)
                == lax.broadcasted_iota(jnp.int32, (SSM_STATE, cols_b), 0)).astype(BF16)
    keep_b = (lax.broadcasted_iota(jnp.int32, (rows_b, cols_b), 0) // SSM_GROUP
              == lax.broadcasted_iota(jnp.int32, (rows_b, cols_b), 1) // SSM_STATE)
    spread_c = (lax.broadcasted_iota(jnp.int32, (SSM_GROUP, rows_b), 1) & (SSM_GROUP - 1)
                == lax.broadcasted_iota(jnp.int32, (SSM_GROUP, rows_b), 0)).astype(BF16)
    keep_c = (lax.broadcasted_iota(jnp.int32, (cols_b, rows_b), 0) // SSM_STATE
              == lax.broadcasted_iota(jnp.int32, (cols_b, rows_b), 1) // SSM_GROUP)
    for b in range(2):
        for part in range(2):
            x = bbar[part][b * gh:(b + 1) * gh].reshape(rows_b, SSM_STATE).astype(BF16)
            t = jnp.dot(x, spread_b, preferred_element_type=F32)
            bdb_ref[b, :, part * cols_b:(part + 1) * cols_b] = jnp.where(keep_b, t, 0.0).astype(BF16)
            x = cmat[part][b * gh:(b + 1) * gh].reshape(cols_b, SSM_GROUP).astype(BF16)
            t = jnp.dot(x, spread_c, preferred_element_type=F32)
            bdc_ref[b, part * cols_b:(part + 1) * cols_b, :] = jnp.where(keep_c, t, 0.0).astype(BF16)


def _prep(lam_re, lam_im, log_dt, b_re, b_im, c_re, c_im):
    g, p, c = b_re.shape
    out_shape = (jax.ShapeDtypeStruct((g, p), F32), jax.ShapeDtypeStruct((g, p), F32),
                 jax.ShapeDtypeStruct((2, g // 2 * c, 2 * g // 2 * p), BF16),
                 jax.ShapeDtypeStruct((2, 2 * g // 2 * p, g // 2 * c), BF16))
    return pl.pallas_call(_prep_kernel, out_shape=out_shape, name="s5_prep")(
        lam_re, lam_im, log_dt.reshape(g, 1), jnp.swapaxes(b_re, 1, 2), jnp.swapaxes(b_im, 1, 2),
        jnp.swapaxes(c_re, 1, 2), jnp.swapaxes(c_im, 1, 2))


def _in_proj_kernel(x_ref, g_ref, w_ref, inv_ref, u_ref, zs_ref, q_ref, kf_ref, kb_ref, vf_ref,
                    vb_ref, za_ref, cl_sc, sl_sc, *, tm, seq_len, pos0, transposed_qv):
    i = pl.program_id(0)
    x = x_ref[...]
    ms = jnp.mean(x * x, axis=-1, keepdims=True)
    hn = (x * lax.rsqrt(ms + EPS) * g_ref[...]).astype(BF16)

    inv = inv_ref[...]

    @pl.when(i == 0)
    def _():
        off = (lax.broadcasted_iota(jnp.int32, (tm, LANES), 0) & (seq_len - 1)).astype(F32) * inv
        cl_sc[...] = jnp.cos(off)
        sl_sc[...] = jnp.sin(off)

    base = (pos0 + ((i * tm) & (seq_len - 1))).astype(F32) * jnp.broadcast_to(inv, (SUBLANES, LANES))
    cb = jnp.cos(base)[:1]
    sb = jnp.sin(base)[:1]
    cl = cl_sc[...]
    sl = sl_sc[...]
    c_m = cb * cl - sb * sl
    sin = sb * cl + cb * sl
    lane = lax.broadcasted_iota(jnp.int32, (tm, LANES), 1) & (QK_DIM - 1)
    half = ROPE_DIM // 2
    s_lo = jnp.where(lane < half, -sin, 0.0)
    s_hi = jnp.where(lane >= half, sin, 0.0)

    def seg(lo, hi):
        return jnp.dot(hn, w_ref[:, lo:hi], preferred_element_type=F32)

    def rope(t):
        outs = []
        for h in range(ATT_HEADS):
            th = t[:, h * LANES:(h + 1) * LANES]
            outs.append(th * c_m + pltpu.roll(th, LANES - half, 1) * s_lo + pltpu.roll(th, half, 1) * s_hi)
        return jnp.concatenate(outs, axis=1)

    def put(ref, t):
        if not transposed_qv:
            ref[...] = t.astype(BF16)
            return
        for h in range(ATT_HEADS):
            tt = t[:, h * LANES:(h + 1) * LANES].T.astype(BF16)
            for c in range(tm // ATT_TILE):
                ref[h, c] = tt[:, c * ATT_TILE:(c + 1) * ATT_TILE]

    u_ref[...] = seg(SEG_U, SEG_ZS).astype(BF16)
    zs_ref[...] = seg(SEG_ZS, SEG_Q).astype(BF16)
    q = rope(seg(SEG_Q, SEG_K))
    put(q_ref, q * Q_SCALE)
    k = rope(seg(SEG_K, SEG_V))
    if transposed_qv:
        for h in range(ATT_HEADS):
            kt = k[:, h * LANES:(h + 1) * LANES].T
            kf_ref[h, 0] = kt[:QK_DIM]
            kf_ref[h, 1] = kt[QK_DIM:]
    else:
        kf_ref[...] = k
    kb_ref[...] = k.astype(BF16)
    v = seg(SEG_V, SEG_ZA)
    for h in range(ATT_HEADS):
        vf_ref[pl.ds(h, tm, stride=ATT_HEADS), :] = v[:, h * LANES:(h + 1) * LANES]
    put(vb_ref, v)
    za_ref[...] = seg(SEG_ZA, SEG_END).astype(BF16)


def _in_proj(x2d, g, w_bf, inv_lane, *, seq_len, pos0, tm, transposed_qv):
    n = x2d.shape[0]
    assert n % tm == 0 and seq_len & (seq_len - 1) == 0 and (tm % seq_len == 0 or seq_len % tm == 0)
    row = lambda i: (i, 0)
    o512 = pl.BlockSpec((tm, D_ATT), row)
    shp = lambda dt: jax.ShapeDtypeStruct((n, D_ATT), dt)
    if transposed_qv:
        assert seq_len % tm == 0 and tm % ATT_TILE == 0
        tps = seq_len // tm
        per = tm // ATT_TILE
        t_spec = pl.BlockSpec((None, ATT_HEADS, per, LANES, ATT_TILE), lambda i: (i // tps, 0, i % tps, 0, 0))
        t_shape = jax.ShapeDtypeStruct((n // seq_len, ATT_HEADS, seq_len // ATT_TILE, LANES, ATT_TILE), BF16)
        kf_spec = pl.BlockSpec((None, ATT_HEADS, 2, QK_DIM, tm), lambda i: (i // tps, 0, 0, 0, i % tps))
        kf_shape = jax.ShapeDtypeStruct((n // seq_len, ATT_HEADS, 2, QK_DIM, seq_len), F32)
    else:
        t_spec, t_shape = o512, shp(BF16)
        kf_spec, kf_shape = o512, shp(F32)
    vf_spec = pl.BlockSpec((tm * ATT_HEADS, V_DIM), row)
    vf_shape = jax.ShapeDtypeStruct((n * ATT_HEADS, V_DIM), F32)
    return pl.pallas_call(
        functools.partial(_in_proj_kernel, tm=tm, seq_len=seq_len, pos0=pos0, transposed_qv=transposed_qv),
        grid=(n // tm,),
        in_specs=[pl.BlockSpec((tm, D_MODEL), row), _full((1, D_MODEL)), _full((D_MODEL, D_IN)),
                  _full((1, LANES))],
        out_specs=[o512, o512, t_spec, kf_spec, o512, vf_spec, t_spec, o512],
        out_shape=[shp(BF16), shp(BF16), t_shape, kf_shape, shp(BF16), vf_shape, t_shape, shp(BF16)],
        scratch_shapes=[pltpu.VMEM((tm, LANES), F32), pltpu.VMEM((tm, LANES), F32)],
        compiler_params=_params(("arbitrary",)),
        name="in_proj",
    )(x2d, g, w_bf, inv_lane)


def _cmul(ar, ai, br, bi):
    return ar * br - ai * bi, ar * bi + ai * br


def _s5_kernel(u_ref, up_ref, zsp_ref, bdb_ref, bdc_ref, d_ref, a_ref, w1_ref, b1_ref, w2_ref, b2_ref, h0_ref,
               out_ref, hf_ref, hbuf, hout_a, hout_b, hb16, car, *, n_chain, T, n_steps, chained, chain_group):
    n_rows = n_chain * T
    P = n_rows + SLAB_PAD
    half_cols = N_CH // 2
    i = pl.program_id(0)
    blocks = range(N_SLAB // SUBLANES)

    def project_and_scan(hout):
        u = u_ref[...].reshape(n_rows, D_SSM)
        for b in range(2):
            bu = jnp.dot(u[:, b * HALF_IN:(b + 1) * HALF_IN], bdb_ref[b], preferred_element_type=F32)
            for part in range(2):
                for k in range(N_SLAB // 2):
                    slab = part * N_SLAB + b * (N_SLAB // 2) + k
                    col = part * half_cols + k * LANES
                    hbuf[slab * P:slab * P + n_rows, :] = bu[:, col:col + LANES]
        a = [(a_ref[SUBLANES * k:SUBLANES * (k + 1), :], a_ref[N_SLAB + SUBLANES * k:N_SLAB + SUBLANES * (k + 1), :])
             for k in blocks]
        for g0 in range(0, n_chain, chain_group):
            chains = list(range(g0, g0 + chain_group))
            st = {(c, k): (car[c, SUBLANES * k:SUBLANES * (k + 1), :],
                           car[c, N_SLAB + SUBLANES * k:N_SLAB + SUBLANES * (k + 1), :])
                  for c in chains for k in blocks}
            for t in range(T):
                for c in chains:
                    for k in blocks:
                        hr, hi = st[(c, k)]
                        rows_r = pl.ds(SUBLANES * k * P + c * T + t, SUBLANES, stride=P)
                        rows_i = pl.ds((N_SLAB + SUBLANES * k) * P + c * T + t, SUBLANES, stride=P)
                        pr, pi = _cmul(a[k][0], a[k][1], hr, hi)
                        nr = pr + hbuf[rows_r, :]
                        ni = pi + hbuf[rows_i, :]
                        hout[rows_r, :] = nr
                        hout[rows_i, :] = ni
                        st[(c, k)] = (nr, ni)
            for c in chains:
                for k in blocks:
                    hr, hi = st[(c, k)]
                    car[c, SUBLANES * k:SUBLANES * (k + 1), :] = hr
                    car[c, N_SLAB + SUBLANES * k:N_SLAB + SUBLANES * (k + 1), :] = hi

    def output_stage(hout):
        up = up_ref[...].reshape(n_rows, D_SSM)
        for b in range(2):
            for part in range(2):
                for k in range(N_SLAB // 2):
                    slab = part * N_SLAB + b * (N_SLAB // 2) + k
                    col = b * N_CH + part * half_cols + k * LANES
                    hb16[:, col:col + LANES] = hout[slab * P:slab * P + n_rows, :].astype(BF16)
        ys = [jnp.dot(hb16[:, b * N_CH:(b + 1) * N_CH], bdc_ref[b], preferred_element_type=F32) for b in range(2)]
        y = jnp.concatenate(ys, axis=1) + up.astype(F32) * d_ref[...]
        gb = jax.nn.gelu(y).astype(BF16)
        y1 = jnp.dot(gb, w1_ref[...], preferred_element_type=F32) + b1_ref[...]
        y2 = jnp.dot(gb, w2_ref[...], preferred_element_type=F32) + b2_ref[...]
        zs = zsp_ref[...].reshape(n_rows, D_SSM).astype(F32)
        out_ref[...] = (y1 * jax.nn.sigmoid(y2) * jax.nn.silu(zs)).astype(BF16).reshape(n_chain, T, D_SSM)

    @pl.when(i == 0)
    def _():
        car[...] = jnp.zeros_like(car) if chained else h0_ref[...]
        project_and_scan(hout_a)

    if n_steps > 1:
        @pl.when((i > 0) & (i < n_steps) & ((i & 1) == 1))
        def _():
            project_and_scan(hout_b)
            output_stage(hout_a)

        @pl.when((i > 0) & (i < n_steps) & ((i & 1) == 0))
        def _():
            project_and_scan(hout_a)
            output_stage(hout_b)

    @pl.when(i == n_steps)
    def _():
        output_stage(hout_a if (n_steps - 1) % 2 == 0 else hout_b)
        hf_ref[...] = car[...]


def _s5(u, zs, bdb, bdc, dvec, a2, w1, b1, w2, b2, h0, *, n_seq, seq_len, chained):
    if chained:
        T = S5_STEPS
        chain_group = n_seq
    else:
        T = seq_len
        chain_group = 4
    assert seq_len % T == 0 and n_seq % chain_group == 0 and T % (2 * SUBLANES) == 0
    n_steps = seq_len // T
    n_rows = n_seq * T
    u3 = u.reshape(n_seq, seq_len, D_SSM)
    zs3 = zs.reshape(n_seq, seq_len, D_SSM)
    cur = pl.BlockSpec((n_seq, T, D_SSM), lambda t: (0, jnp.minimum(t, n_steps - 1), 0))
    prev = pl.BlockSpec((n_seq, T, D_SSM), lambda t: (0, jnp.maximum(t - 1, 0), 0))
    st_shape = (n_seq, 2 * N_SLAB, LANES)
    slabs = pltpu.VMEM((2 * N_SLAB * (n_rows + SLAB_PAD), LANES), F32)
    out, hf = pl.pallas_call(
        functools.partial(_s5_kernel, n_chain=n_seq, T=T, n_steps=n_steps, chained=chained,
                          chain_group=chain_group),
        grid=(n_steps + 1,),
        in_specs=[cur, prev, prev, _full(bdb.shape), _full(bdc.shape), _full((1, D_SSM)),
                  _full((2 * N_SLAB, LANES)), _full((D_SSM, D_SSM)), _full((1, D_SSM)), _full((D_SSM, D_SSM)),
                  _full((1, D_SSM)), _full(st_shape)],
        out_specs=[prev, _full(st_shape)],
        out_shape=[jax.ShapeDtypeStruct((n_seq, seq_len, D_SSM), BF16), jax.ShapeDtypeStruct(st_shape, F32)],
        scratch_shapes=[slabs, slabs, slabs, pltpu.VMEM((n_rows, 2 * N_CH), BF16), pltpu.VMEM(st_shape, F32)],
        compiler_params=_params(("arbitrary",)),
        name="s5_chained" if chained else "s5_independent",
    )(u3, u3, zs3, bdb, bdc, dvec, a2, w1, b1, w2, b2, h0)
    return out.reshape(n_seq * seq_len, D_SSM), hf


def _lambda(lq1, lk1, lq2, lk2):
    s1 = jnp.sum(lq1[...] * lk1[...], axis=1, keepdims=True)
    s2 = jnp.sum(lq2[...] * lk2[...], axis=1, keepdims=True)
    return jnp.exp(s1) - jnp.exp(s2) + LAMBDA_INIT


def _stack_maps(q):
    lane = lax.broadcasted_iota(jnp.int32, q.shape, 1)
    zero = jnp.zeros_like(q)
    return jnp.concatenate([jnp.where(lane < QK_DIM, q, zero), jnp.where(lane >= QK_DIM, q, zero)], axis=0)


def _subln_gate(o, g, za):
    ms = jnp.mean(o * o, axis=-1, keepdims=True)
    on = (o * lax.rsqrt(ms + EPS) * g) * (1.0 - LAMBDA_INIT)
    return on * jax.nn.silu(za.astype(F32))


def _finish_head(acc, l, lam, g, za, t):
    inv = 1.0 / l
    o = acc[:t] * inv[:t] - lam * (acc[t:] * inv[t:])
    return _subln_gate(o, g, za)


_NT = (((1,), (1,)), ((), ()))


def _attn_kernel(lq1, lk1, lq2, lk2, g_ref, qt_ref, k_ref, vt_ref, za_ref, o_ref, m_sc, acc_sc, s_a, s_b, *, nq):
    tq = ATT_TILE
    step = pl.program_id(2)

    def finish():
        lam = _lambda(lq1, lk1, lq2, lk2)
        for h in range(HEADS_PER_STEP):
            acc = acc_sc[h, :V_DIM, :]
            inv = 1.0 / acc_sc[h, V_DIM:V_DIM + 1, :]
            ot = acc[:, :tq] * inv[:, :tq] - lam * (acc[:, tq:] * inv[:, tq:])
            cols = slice(h * LANES, (h + 1) * LANES)
            o_ref[:, cols] = _subln_gate(ot.T, g_ref[...], za_ref[:, cols]).astype(BF16)

    @pl.when((pl.program_id(0) == 0) & (pl.program_id(1) == 0) & (step == 0))
    def _():
        acc_sc[...] = jnp.ones_like(acc_sc)

    @pl.when(step < nq)
    def _():
        finish()
        _attn_tile(step, qt_ref, k_ref, vt_ref, m_sc, acc_sc, s_a, s_b)

    @pl.when(step == nq)
    def _():
        finish()


def _attn_tile(qi, qt_ref, k_ref, vt_ref, m_sc, acc_sc, s_a, s_b):
    tq = ATT_TILE
    q2t = []
    for h in range(HEADS_PER_STEP):
        qt = qt_ref[h]
        row = lax.broadcasted_iota(jnp.int32, qt.shape, 0)
        zero = jnp.zeros_like(qt)
        q2t.append(jnp.concatenate([jnp.where(row < QK_DIM, qt, zero), jnp.where(row >= QK_DIM, qt, zero)], axis=1))
    m_sc[...] = jnp.full_like(m_sc, NEG)
    acc_sc[...] = jnp.zeros_like(acc_sc)
    ones = jnp.ones((SUM_ROWS, tq), BF16)
    heads = range(HEADS_PER_STEP)
    tk = KEY_TILES * tq
    n_full = qi // KEY_TILES
    rem = qi % KEY_TILES

    def scores_into(s_ref, kt, h):
        start = pl.multiple_of(kt * tk, tk)
        s_ref[h] = jnp.dot(k_ref[pl.ds(start, tk), h * LANES:(h + 1) * LANES], q2t[h],
                           preferred_element_type=F32)

    def softmax_accumulate(s_ref, kt, nkeys, masked, h):
        s = s_ref[h] if nkeys == tk else s_ref[h, :nkeys, :]
        if masked:
            kc = lax.broadcasted_iota(jnp.int32, s.shape, 0) // CHUNK + kt * (tk // CHUNK)
            qc = (lax.broadcasted_iota(jnp.int32, s.shape, 1) & (tq - 1)) // CHUNK + qi * (tq // CHUNK)
            s = jnp.where(kc <= qc, s, NEG)
        m_old = m_sc[h]
        m_new = jnp.maximum(m_old, jnp.max(s, axis=0, keepdims=True))
        alpha = jnp.exp2(m_old - m_new)
        pb = jnp.exp2(s - m_new).astype(BF16)
        pv = None
        for j in range(nkeys // tq):
            d = jnp.dot(jnp.concatenate([vt_ref[h, KEY_TILES * kt + j], ones], axis=0), pb[j * tq:(j + 1) * tq],
                        preferred_element_type=F32)
            pv = d if pv is None else pv + d
        acc_sc[h] = alpha * acc_sc[h] + pv
        m_sc[h] = m_new

    def stage(kt, s_cur, s_nxt):
        for h in heads:
            scores_into(s_nxt, kt + 1, h)
            softmax_accumulate(s_cur, kt, tk, False, h)

    for h in heads:
        scores_into(s_a, 0, h)

    def body(kt, c):
        even = (kt & 1) == 0

        @pl.when(even)
        def _():
            stage(kt, s_a, s_b)

        @pl.when(jnp.logical_not(even))
        def _():
            stage(kt, s_b, s_a)

        return c

    lax.fori_loop(0, n_full, body, 0)

    for parity, s_ref in ((0, s_a), (1, s_b)):
        here = (n_full & 1) == parity
        for r in range(KEY_TILES):
            @pl.when(here & (rem == r))
            def _(s_ref=s_ref, r=r):
                for h in heads:
                    softmax_accumulate(s_ref, n_full, (r + 1) * tq, True, h)


def _attn_prompt(lams, g, qt, kb, vt, za, *, n_seq, seq_len):
    tq = ATT_TILE
    nq = seq_len // tq
    hps = HEADS_PER_STEP
    assert tq % CHUNK == 0 and tq & (tq - 1) == 0 and ATT_HEADS % hps == 0 and seq_len % (KEY_TILES * tq) == 0
    rowspec = pl.BlockSpec((tq, hps * LANES), lambda b, h, i: (b * nq + jnp.maximum(i - 1, 0), h))
    qtspec = pl.BlockSpec((None, hps, None, LANES, tq), lambda b, h, i: (b, h, jnp.minimum(i, nq - 1), 0, 0))
    kspec = pl.BlockSpec((seq_len, hps * LANES), lambda b, h, i: (b, h))
    vtspec = pl.BlockSpec((None, hps, nq, LANES, tq), lambda b, h, i: (b, h, 0, 0, 0))
    vec = _full((1, QK_DIM))
    stat = pltpu.VMEM((hps, 1, 2 * tq), F32)
    return pl.pallas_call(
        functools.partial(_attn_kernel, nq=nq),
        grid=(n_seq, ATT_HEADS // hps, nq + 1),
        in_specs=[vec, vec, vec, vec, _full((1, V_DIM)), qtspec, kspec, vtspec, rowspec],
        out_specs=rowspec,
        out_shape=jax.ShapeDtypeStruct((n_seq * seq_len, D_ATT), BF16),
        scratch_shapes=[stat, pltpu.VMEM((hps, V_DIM + SUM_ROWS, 2 * tq), F32),
                        pltpu.VMEM((hps, KEY_TILES * tq, 2 * tq), F32),
                        pltpu.VMEM((hps, KEY_TILES * tq, 2 * tq), F32)],
        compiler_params=_params(("arbitrary", "arbitrary", "arbitrary")),
        name="attn_prompt",
    )(*lams, g, qt, kb, vt, za)


def _attn_sample_kernel(lq1, lk1, lq2, lk2, g_ref, q_ref, kn_ref, vn_ref, ck_ref, cv_ref, za_ref, o_ref, *, t, past):
    lam = _lambda(lq1, lk1, lq2, lk2)
    for j in range(SEQS_PER_STEP):
        rows = slice(j * t, (j + 1) * t)
        outs = []
        for h in range(ATT_HEADS):
            cols = slice(h * LANES, (h + 1) * LANES)
            q2 = _stack_maps(q_ref[rows, cols])
            kpt = jnp.concatenate([ck_ref[j, h, 0], ck_ref[j, h, 1]], axis=0).astype(BF16)
            vp = cv_ref[j, pl.ds(h, past, stride=ATT_HEADS), :].astype(BF16)
            s_p = jnp.dot(q2, kpt, preferred_element_type=F32)
            s_n = lax.dot_general(q2, kn_ref[rows, cols], _NT, preferred_element_type=F32)
            m = jnp.maximum(jnp.max(s_p, axis=1, keepdims=True), jnp.max(s_n, axis=1, keepdims=True))
            p_p = jnp.exp2(s_p - m)
            p_n = jnp.exp2(s_n - m)
            l = jnp.sum(p_p, axis=1, keepdims=True) + jnp.sum(p_n, axis=1, keepdims=True)
            acc = (jnp.dot(p_p.astype(BF16), vp, preferred_element_type=F32)
                   + jnp.dot(p_n.astype(BF16), vn_ref[rows, cols], preferred_element_type=F32))
            outs.append(_finish_head(acc, l, lam, g_ref[...], za_ref[rows, cols], t))
        o_ref[rows, :] = jnp.concatenate(outs, axis=1).astype(BF16)


def _attn_sample(lams, g, q, kb, vb, ck, cv, za, *, n_seq, t, past):
    sps = SEQS_PER_STEP
    assert past % CHUNK == 0 and t <= CHUNK and n_seq % sps == 0
    row = pl.BlockSpec((sps * t, D_ATT), lambda b: (b, 0))
    kcache = pl.BlockSpec((sps, ATT_HEADS, 2, QK_DIM, past), lambda b: (b, 0, 0, 0, 0))
    vcache = pl.BlockSpec((sps, past * ATT_HEADS, V_DIM), lambda b: (b, 0, 0))
    vec = _full((1, QK_DIM))
    return pl.pallas_call(
        functools.partial(_attn_sample_kernel, t=t, past=past),
        grid=(n_seq // sps,),
        in_specs=[vec, vec, vec, vec, _full((1, V_DIM)), row, row, row, kcache, vcache, row],
        out_specs=row,
        out_shape=jax.ShapeDtypeStruct((n_seq * t, D_ATT), BF16),
        compiler_params=_params(("arbitrary",)),
        name="attn_sample",
    )(*lams, g, q, kb, vb, ck, cv, za)


def _out_proj_kernel(s_ref, a_ref, w_ref, x_ref, g_ref, y_ref):
    tm = s_ref.shape[0]
    step = min(tm, OUT_CHUNK)
    for r in range(0, tm, step):
        rows = slice(r, r + step)
        mix = (jnp.dot(s_ref[rows, :], w_ref[:D_SSM, :], preferred_element_type=F32)
               + jnp.dot(a_ref[rows, :], w_ref[D_SSM:, :], preferred_element_type=F32))
        ms = jnp.mean(mix * mix, axis=-1, keepdims=True)
        y_ref[rows, :] = x_ref[rows, :] + mix * lax.rsqrt(ms + EPS) * g_ref[...]


def _out_proj(ssm_out, att_out, w_bf, x2d, g, *, tm):
    n = x2d.shape[0]
    assert n % tm == 0 and tm % min(tm, OUT_CHUNK) == 0
    row = lambda i: (i, 0)
    half = pl.BlockSpec((tm, D_SSM), row)
    full = pl.BlockSpec((tm, D_MODEL), row)
    return pl.pallas_call(
        _out_proj_kernel,
        grid=(n // tm,),
        in_specs=[half, half, _full((D_MODEL, D_MODEL)), full, _full((1, D_MODEL))],
        out_specs=full,
        out_shape=jax.ShapeDtypeStruct((n, D_MODEL), F32),
        compiler_params=_params(("arbitrary",)),
        name="out_proj",
    )(ssm_out, att_out, w_bf, x2d, g)


def kernel(x_prompt, x_sample, cache_k, cache_v, state_ssm_re, state_ssm_im, norm_pre_g, w_in, ssm_lambda_re,
           ssm_lambda_im, ssm_log_dt, ssm_b_re, ssm_b_im, ssm_c_re, ssm_c_im, ssm_d, glu_w1, glu_b1, glu_w2,
           glu_b2, lambda_q1, lambda_k1, lambda_q2, lambda_k2, attn_subln_g, w_out, norm_post_g):
    bp, sp, _ = x_prompt.shape
    bs, ss, _ = x_sample.shape
    past = cache_k.shape[2]

    a_re, a_im, bdb, bdc = _prep(ssm_lambda_re[0], ssm_lambda_im[0], ssm_log_dt[0], ssm_b_re[0], ssm_b_im[0],
                                 ssm_c_re[0], ssm_c_im[0])
    a2 = jnp.concatenate([a_re.reshape(N_SLAB, LANES), a_im.reshape(N_SLAB, LANES)], axis=0)
    dvec = ssm_d[0].reshape(1, D_SSM)
    w_in_bf = w_in[0].astype(BF16)
    w_out_bf = w_out[0].astype(BF16)
    w1 = glu_w1[0].astype(BF16)
    w2 = glu_w2[0].astype(BF16)
    b1 = glu_b1[0].reshape(1, D_SSM)
    b2 = glu_b2[0].reshape(1, D_SSM)
    g_pre = norm_pre_g[0].reshape(1, D_MODEL)
    g_post = norm_post_g[0].reshape(1, D_MODEL)
    g_sub = attn_subln_g[0].reshape(1, V_DIM)
    lams = tuple(v[0].reshape(1, QK_DIM) for v in (lambda_q1, lambda_k1, lambda_q2, lambda_k2))
    inv = ROPE_THETA ** (-jnp.arange(ROPE_DIM // 2, dtype=F32) * 2.0 / ROPE_DIM)
    rotary_lane = (jnp.arange(LANES) % QK_DIM) < ROPE_DIM
    inv_lane = jnp.where(rotary_lane, jnp.tile(inv, LANES // (ROPE_DIM // 2)), 0.0).reshape(1, LANES)

    def run(x, n_seq, seq_len, pos0, chained, h0, tm):
        x2d = x.reshape(n_seq * seq_len, D_MODEL)
        u, zs, q, kf, kb, vf, vb, za = _in_proj(x2d, g_pre, w_in_bf, inv_lane, seq_len=seq_len, pos0=pos0, tm=tm,
                                                transposed_qv=chained)
        ssm_out, hf = _s5(u, zs, bdb, bdc, dvec, a2, w1, b1, w2, b2, h0,
                          n_seq=n_seq, seq_len=seq_len, chained=chained)
        if chained:
            att = _attn_prompt(lams, g_sub, q, kb, vb, za, n_seq=n_seq, seq_len=seq_len)
            k_out = jnp.transpose(kf, (0, 4, 1, 2, 3))[None]
        else:
            ck = jnp.transpose(cache_k[0], (0, 2, 3, 4, 1))
            cv = cache_v[0].reshape(n_seq, past * ATT_HEADS, V_DIM)
            att = _attn_sample(lams, g_sub, q, kb, vb, ck, cv, za, n_seq=n_seq, t=seq_len, past=past)
            k_out = kf.reshape(1, n_seq, seq_len, ATT_HEADS, 2, QK_DIM)
        y = _out_proj(ssm_out, att, w_out_bf, x2d, g_post, tm=min(OUT_TILE, n_seq * seq_len))
        return (y.reshape(n_seq, seq_len, D_MODEL),
                k_out,
                vf.reshape(1, n_seq, seq_len, ATT_HEADS, V_DIM),
                hf[:, :N_SLAB].reshape(1, n_seq, SSM_GROUPS, SSM_STATE),
                hf[:, N_SLAB:].reshape(1, n_seq, SSM_GROUPS, SSM_STATE))

    yp, kp, vp, hrp, hip = run(x_prompt, bp, sp, 0, True, jnp.zeros((bp, 2 * N_SLAB, LANES), F32), PROJ_TILE)
    h0 = jnp.concatenate([state_ssm_re[0].reshape(bs, N_SLAB, LANES), state_ssm_im[0].reshape(bs, N_SLAB, LANES)],
                         axis=1)
    ys, ks, vs, hrs, his = run(x_sample, bs, ss, past, False, h0, bs * ss)
    return (yp, ys, kp, vp, hrp, hip, ks, vs, hrs, his)
```

```python
import functools
import math

import jax
import jax.numpy as jnp
from jax import lax
from jax.experimental import pallas as pl
from jax.experimental.pallas import tpu as pltpu

F32 = jnp.float32
BF16 = jnp.bfloat16

D_MODEL = 1024
D_SSM = 512
D_ATT = 512
SSM_GROUP = 16
SSM_GROUPS = 32
SSM_STATE = 64
N_CH = SSM_GROUPS * SSM_STATE
ATT_HEADS = 4
QK_DIM = 64
V_DIM = 128
ROPE_DIM = 16
ROPE_THETA = 500000.0
CHUNK = 64
EPS = 1e-6
D_IN = 3072
LAMBDA_INIT = 0.8 - 0.6 * math.exp(-0.3 * 0)

HALF_IN = D_SSM // 2
SEG_U, SEG_ZS, SEG_Q, SEG_K, SEG_V, SEG_ZA, SEG_END = 0, 512, 1024, 1536, 2048, 2560, D_IN

LANES = 128
SUBLANES = 8
VMEM_LIMIT = 56 * 1024 * 1024
N_SLAB = N_CH // LANES
PROJ_TILE = 512
OUT_TILE = 2048
OUT_CHUNK = 256
S5_STEPS = 256
SLAB_PAD = 4
ATT_TILE = 256
KEY_TILES = 2
HEADS_PER_STEP = 4
PARAM_LANES = 512
SUM_ROWS = 16
SEQS_PER_STEP = 2
NEG = -3.0e38
LOG2E = math.log2(math.e)
Q_SCALE = QK_DIM ** -0.5 * LOG2E


def _params(sem):
    return pltpu.CompilerParams(dimension_semantics=sem, vmem_limit_bytes=VMEM_LIMIT)


def _full(shape):
    n = len(shape)
    return pl.BlockSpec(shape, lambda *_: (0,) * n)


def _prep_kernel(lr_ref, li_ref, ldt_ref, br_ref, bi_ref, cr_ref, ci_ref, ar_ref, ai_ref, bdb_ref, bdc_ref):
    lr = lr_ref[...]
    li = li_ref[...]
    dt = jnp.exp(ldt_ref[...])
    mag = jnp.exp(lr * dt)
    ar = mag * jnp.cos(li * dt)
    ai = mag * jnp.sin(li * dt)
    den = lr * lr + li * li
    cr = ((ar - 1.0) * lr + ai * li) / den
    ci = (ai * lr - (ar - 1.0) * li) / den
    ar_ref[...] = ar
    ai_ref[...] = ai
    br = br_ref[...]
    bi = bi_ref[...]
    crb = cr[:, None, :]
    cib = ci[:, None, :]
    bbar = (crb * br - cib * bi, crb * bi + cib * br)
    cmat = (cr_ref[...], -ci_ref[...])

    gh = SSM_GROUPS // 2
    rows_b, cols_b = gh * SSM_GROUP, gh * SSM_STATE
    spread_b = (lax.broadcasted_iota(jnp.int32, (SSM_STATE, cols_b), 1) & (SSM_STATE - 1)
                == lax.broadcasted_iota(jnp.int32, (SSM_STATE, cols_b), 0)).astype(BF16)
    keep_b = (lax.broadcasted_iota(jnp.int32, (rows_b, cols_b), 0) // SSM_GROUP
              == lax.broadcasted_iota(jnp.int32, (rows_b, cols_b), 1) // SSM_STATE)
    spread_c = (lax.broadcasted_iota(jnp.int32, (SSM_GROUP, rows_b), 1) & (SSM_GROUP - 1)
                == lax.broadcasted_iota(jnp.int32, (SSM_GROUP, rows_b), 0)).astype(BF16)
    keep_c = (lax.broadcasted_iota(jnp.int32, (cols_b, rows_b), 0) // SSM_STATE
              == lax.broadcasted_iota(jnp.int32, (cols_b, rows_b), 1) // SSM_GROUP)
    for b in range(2):
        for part in range(2):
            x = bbar[part][b * gh:(b + 1) * gh].reshape(rows_b, SSM_STATE).astype(BF16)
            t = jnp.dot(x, spread_b, preferred_element_type=F32)
            bdb_ref[b, :, part * cols_b:(part + 1) * cols_b] = jnp.where(keep_b, t, 0.0).astype(BF16)
            x = cmat[part][b * gh:(b + 1) * gh].reshape(cols_b, SSM_GROUP).astype(BF16)
            t = jnp.dot(x, spread_c, preferred_element_type=F32)
            bdc_ref[b, part * cols_b:(part + 1) * cols_b, :] = jnp.where(keep_c, t, 0.0).astype(BF16)


def _prep(lam_re, lam_im, log_dt, b_re, b_im, c_re, c_im):
    g, p, c = b_re.shape
    out_shape = (jax.ShapeDtypeStruct((g, p), F32), jax.ShapeDtypeStruct((g, p), F32),
                 jax.ShapeDtypeStruct((2, g // 2 * c, 2 * g // 2 * p), BF16),
                 jax.ShapeDtypeStruct((2, 2 * g // 2 * p, g // 2 * c), BF16))
    return pl.pallas_call(_prep_kernel, out_shape=out_shape, name="s5_prep")(
        lam_re, lam_im, log_dt.reshape(g, 1), jnp.swapaxes(b_re, 1, 2), jnp.swapaxes(b_im, 1, 2),
        jnp.swapaxes(c_re, 1, 2), jnp.swapaxes(c_im, 1, 2))


def _in_proj_kernel(x_ref, g_ref, w_ref, inv_ref, u_ref, zs_ref, q_ref, kf_ref, kb_ref, vf_ref,
                    vb_ref, za_ref, cl_sc, sl_sc, *, tm, seq_len, pos0, transposed_qv):
    i = pl.program_id(0)
    x = x_ref[...]
    ms = jnp.mean(x * x, axis=-1, keepdims=True)
    hn = (x * lax.rsqrt(ms + EPS) * g_ref[...]).astype(BF16)

    inv = inv_ref[...]

    @pl.when(i == 0)
    def _():
        off = (lax.broadcasted_iota(jnp.int32, (tm, LANES), 0) & (seq_len - 1)).astype(F32) * inv
        cl_sc[...] = jnp.cos(off)
        sl_sc[...] = jnp.sin(off)

    base = (pos0 + ((i * tm) & (seq_len - 1))).astype(F32) * jnp.broadcast_to(inv, (SUBLANES, LANES))
    cb = jnp.cos(base)[:1]
    sb = jnp.sin(base)[:1]
    cl = cl_sc[...]
    sl = sl_sc[...]
    c_m = cb * cl - sb * sl
    sin = sb * cl + cb * sl
    lane = lax.broadcasted_iota(jnp.int32, (tm, LANES), 1) & (QK_DIM - 1)
    half = ROPE_DIM // 2
    s_lo = jnp.where(lane < half, -sin, 0.0)
    s_hi = jnp.where(lane >= half, sin, 0.0)

    def seg(lo, hi):
        return jnp.dot(hn, w_ref[:, lo:hi], preferred_element_type=F32)

    def rope(t):
        outs = []
        for h in range(ATT_HEADS):
            th = t[:, h * LANES:(h + 1) * LANES]
            outs.append(th * c_m + pltpu.roll(th, LANES - half, 1) * s_lo + pltpu.roll(th, half, 1) * s_hi)
        return jnp.concatenate(outs, axis=1)

    def put(ref, t):
        if not transposed_qv:
            ref[...] = t.astype(BF16)
            return
        for h in range(ATT_HEADS):
            tt = t[:, h * LANES:(h + 1) * LANES].T.astype(BF16)
            for c in range(tm // ATT_TILE):
                ref[h, c] = tt[:, c * ATT_TILE:(c + 1) * ATT_TILE]

    u_ref[...] = seg(SEG_U, SEG_ZS).astype(BF16)
    zs_ref[...] = seg(SEG_ZS, SEG_Q).astype(BF16)
    q = rope(seg(SEG_Q, SEG_K))
    put(q_ref, q * Q_SCALE)
    k = rope(seg(SEG_K, SEG_V))
    if transposed_qv:
        for h in range(ATT_HEADS):
            kt = k[:, h * LANES:(h + 1) * LANES].T
            kf_ref[h, 0] = kt[:QK_DIM]
            kf_ref[h, 1] = kt[QK_DIM:]
    else:
        kf_ref[...] = k
    kb_ref[...] = k.astype(BF16)
    v = seg(SEG_V, SEG_ZA)
    for h in range(ATT_HEADS):
        vf_ref[pl.ds(h, tm, stride=ATT_HEADS), :] = v[:, h * LANES:(h + 1) * LANES]
    put(vb_ref, v)
    za_ref[...] = seg(SEG_ZA, SEG_END).astype(BF16)


def _in_proj(x2d, g, w_bf, inv_lane, *, seq_len, pos0, tm, transposed_qv):
    n = x2d.shape[0]
    assert n % tm == 0 and seq_len & (seq_len - 1) == 0 and (tm % seq_len == 0 or seq_len % tm == 0)
    row = lambda i: (i, 0)
    o512 = pl.BlockSpec((tm, D_ATT), row)
    shp = lambda dt: jax.ShapeDtypeStruct((n, D_ATT), dt)
    if transposed_qv:
        assert seq_len % tm == 0 and tm % ATT_TILE == 0
        tps = seq_len // tm
        per = tm // ATT_TILE
        t_spec = pl.BlockSpec((None, ATT_HEADS, per, LANES, ATT_TILE), lambda i: (i // tps, 0, i % tps, 0, 0))
        t_shape = jax.ShapeDtypeStruct((n // seq_len, ATT_HEADS, seq_len // ATT_TILE, LANES, ATT_TILE), BF16)
        kf_spec = pl.BlockSpec((None, ATT_HEADS, 2, QK_DIM, tm), lambda i: (i // tps, 0, 0, 0, i % tps))
        kf_shape = jax.ShapeDtypeStruct((n // seq_len, ATT_HEADS, 2, QK_DIM, seq_len), F32)
    else:
        t_spec, t_shape = o512, shp(BF16)
        kf_spec, kf_shape = o512, shp(F32)
    vf_spec = pl.BlockSpec((tm * ATT_HEADS, V_DIM), row)
    vf_shape = jax.ShapeDtypeStruct((n * ATT_HEADS, V_DIM), F32)
    return pl.pallas_call(
        functools.partial(_in_proj_kernel, tm=tm, seq_len=seq_len, pos0=pos0, transposed_qv=transposed_qv),
        grid=(n // tm,),
        in_specs=[pl.BlockSpec((tm, D_MODEL), row), _full((1, D_MODEL)), _full((D_MODEL, D_IN)),
                  _full((1, LANES))],
        out_specs=[o512, o512, t_spec, kf_spec, o512, vf_spec, t_spec, o512],
        out_shape=[shp(BF16), shp(BF16), t_shape, kf_shape, shp(BF16), vf_shape, t_shape, shp(BF16)],
        scratch_shapes=[pltpu.VMEM((tm, LANES), F32), pltpu.VMEM((tm, LANES), F32)],
        compiler_params=_params(("arbitrary",)),
        name="in_proj",
    )(x2d, g, w_bf, inv_lane)


def _cmul(ar, ai, br, bi):
    return ar * br - ai * bi, ar * bi + ai * br


def _s5_kernel(u_ref, up_ref, zsp_ref, bdb_ref, bdc_ref, d_ref, a_ref, w1_ref, b1_ref, w2_ref, b2_ref, h0_ref,
               out_ref, hf_ref, hbuf, hout_a, hout_b, hb16, car, *, n_chain, T, n_steps, chained, chain_group):
    n_rows = n_chain * T
    P = n_rows + SLAB_PAD
    half_cols = N_CH // 2
    i = pl.program_id(0)
    blocks = range(N_SLAB // SUBLANES)

    def project_and_scan(hout):
        u = u_ref[...].reshape(n_rows, D_SSM)
        for b in range(2):
            bu = jnp.dot(u[:, b * HALF_IN:(b + 1) * HALF_IN], bdb_ref[b], preferred_element_type=F32)
            for part in range(2):
                for k in range(N_SLAB // 2):
                    slab = part * N_SLAB + b * (N_SLAB // 2) + k
                    col = part * half_cols + k * LANES
                    hbuf[slab * P:slab * P + n_rows, :] = bu[:, col:col + LANES]
        a = [(a_ref[SUBLANES * k:SUBLANES * (k + 1), :], a_ref[N_SLAB + SUBLANES * k:N_SLAB + SUBLANES * (k + 1), :])
             for k in blocks]
        for g0 in range(0, n_chain, chain_group):
            chains = list(range(g0, g0 + chain_group))
            st = {(c, k): (car[c, SUBLANES * k:SUBLANES * (k + 1), :],
                           car[c, N_SLAB + SUBLANES * k:N_SLAB + SUBLANES * (k + 1), :])
                  for c in chains for k in blocks}
            for t in range(T):
                for c in chains:
                    for k in blocks:
                        hr, hi = st[(c, k)]
                        rows_r = pl.ds(SUBLANES * k * P + c * T + t, SUBLANES, stride=P)
                        rows_i = pl.ds((N_SLAB + SUBLANES * k) * P + c * T + t, SUBLANES, stride=P)
                        pr, pi = _cmul(a[k][0], a[k][1], hr, hi)
                        nr = pr + hbuf[rows_r, :]
                        ni = pi + hbuf[rows_i, :]
                        hout[rows_r, :] = nr
                        hout[rows_i, :] = ni
                        st[(c, k)] = (nr, ni)
            for c in chains:
                for k in blocks:
                    hr, hi = st[(c, k)]
                    car[c, SUBLANES * k:SUBLANES * (k + 1), :] = hr
                    car[c, N_SLAB + SUBLANES * k:N_SLAB + SUBLANES * (k + 1), :] = hi

    def output_stage(hout):
        up = up_ref[...].reshape(n_rows, D_SSM)
        for b in range(2):
            for part in range(2):
                for k in range(N_SLAB // 2):
                    slab = part * N_SLAB + b * (N_SLAB // 2) + k
                    col = b * N_CH + part * half_cols + k * LANES
                    hb16[:, col:col + LANES] = hout[slab * P:slab * P + n_rows, :].astype(BF16)
        ys = [jnp.dot(hb16[:, b * N_CH:(b + 1) * N_CH], bdc_ref[b], preferred_element_type=F32) for b in range(2)]
        y = jnp.concatenate(ys, axis=1) + up.astype(F32) * d_ref[...]
        gb = jax.nn.gelu(y).astype(BF16)
        y1 = jnp.dot(gb, w1_ref[...], preferred_element_type=F32) + b1_ref[...]
        y2 = jnp.dot(gb, w2_ref[...], preferred_element_type=F32) + b2_ref[...]
        zs = zsp_ref[...].reshape(n_rows, D_SSM).astype(F32)
        out_ref[...] = (y1 * jax.nn.sigmoid(y2) * jax.nn.silu(zs)).astype(BF16).reshape(n_chain, T, D_SSM)

    @pl.when(i == 0)
    def _():
        car[...] = jnp.zeros_like(car) if chained else h0_ref[...]
        project_and_scan(hout_a)

    if n_steps > 1:
        @pl.when((i > 0) & (i < n_steps) & ((i & 1) == 1))
        def _():
            project_and_scan(hout_b)
            output_stage(hout_a)

        @pl.when((i > 0) & (i < n_steps) & ((i & 1) == 0))
        def _():
            project_and_scan(hout_a)
            output_stage(hout_b)

    @pl.when(i == n_steps)
    def _():
        output_stage(hout_a if (n_steps - 1) % 2 == 0 else hout_b)
        hf_ref[...] = car[...]


def _s5(u, zs, bdb, bdc, dvec, a2, w1, b1, w2, b2, h0, *, n_seq, seq_len, chained):
    if chained:
        T = S5_STEPS
        chain_group = n_seq
    else:
        T = seq_len
        chain_group = 4
    assert seq_len % T == 0 and n_seq % chain_group == 0 and T % (2 * SUBLANES) == 0
    n_steps = seq_len // T
    n_rows = n_seq * T
    u3 = u.reshape(n_seq, seq_len, D_SSM)
    zs3 = zs.reshape(n_seq, seq_len, D_SSM)
    cur = pl.BlockSpec((n_seq, T, D_SSM), lambda t: (0, jnp.minimum(t, n_steps - 1), 0))
    prev = pl.BlockSpec((n_seq, T, D_SSM), lambda t: (0, jnp.maximum(t - 1, 0), 0))
    st_shape = (n_seq, 2 * N_SLAB, LANES)
    slabs = pltpu.VMEM((2 * N_SLAB * (n_rows + SLAB_PAD), LANES), F32)
    out, hf = pl.pallas_call(
        functools.partial(_s5_kernel, n_chain=n_seq, T=T, n_steps=n_steps, chained=chained,
                          chain_group=chain_group),
        grid=(n_steps + 1,),
        in_specs=[cur, prev, prev, _full(bdb.shape), _full(bdc.shape), _full((1, D_SSM)),
                  _full((2 * N_SLAB, LANES)), _full((D_SSM, D_SSM)), _full((1, D_SSM)), _full((D_SSM, D_SSM)),
                  _full((1, D_SSM)), _full(st_shape)],
        out_specs=[prev, _full(st_shape)],
        out_shape=[jax.ShapeDtypeStruct((n_seq, seq_len, D_SSM), BF16), jax.ShapeDtypeStruct(st_shape, F32)],
        scratch_shapes=[slabs, slabs, slabs, pltpu.VMEM((n_rows, 2 * N_CH), BF16), pltpu.VMEM(st_shape, F32)],
        compiler_params=_params(("arbitrary",)),
        name="s5_chained" if chained else "s5_independent",
    )(u3, u3, zs3, bdb, bdc, dvec, a2, w1, b1, w2, b2, h0)
    return out.reshape(n_seq * seq_len, D_SSM), hf


def _lambda(lq1, lk1, lq2, lk2):
    s1 = jnp.sum(lq1[...] * lk1[...], axis=1, keepdims=True)
    s2 = jnp.sum(lq2[...] * lk2[...], axis=1, keepdims=True)
    return jnp.exp(s1) - jnp.exp(s2) + LAMBDA_INIT


def _stack_maps(q):
    lane = lax.broadcasted_iota(jnp.int32, q.shape, 1)
    zero = jnp.zeros_like(q)
    return jnp.concatenate([jnp.where(lane < QK_DIM, q, zero), jnp.where(lane >= QK_DIM, q, zero)], axis=0)


def _subln_gate(o, g, za):
    ms = jnp.mean(o * o, axis=-1, keepdims=True)
    on = (o * lax.rsqrt(ms + EPS) * g) * (1.0 - LAMBDA_INIT)
    return on * jax.nn.silu(za.astype(F32))


def _finish_head(acc, l, lam, g, za, t):
    inv = 1.0 / l
    o = acc[:t] * inv[:t] - lam * (acc[t:] * inv[t:])
    return _subln_gate(o, g, za)


_NT = (((1,), (1,)), ((), ()))


def _attn_kernel(par_ref, qt_ref, k_ref, vt_ref, za_ref, o_ref, m_sc, acc_sc, s_a, s_b, *, nq):
    tq = ATT_TILE
    step = pl.program_id(2)

    def finish():
        lam = _lambda(*(par_ref[r:r + 1, :] for r in range(4)))
        g = par_ref[4:5, :V_DIM]
        for h in range(HEADS_PER_STEP):
            acc = acc_sc[h, :V_DIM, :]
            inv = 1.0 / acc_sc[h, V_DIM:V_DIM + 1, :]
            ot = acc[:, :tq] * inv[:, :tq] - lam * (acc[:, tq:] * inv[:, tq:])
            cols = slice(h * LANES, (h + 1) * LANES)
            o_ref[:, cols] = _subln_gate(ot.T, g, za_ref[:, cols]).astype(BF16)

    @pl.when((pl.program_id(0) == 0) & (pl.program_id(1) == 0) & (step == 0))
    def _():
        acc_sc[...] = jnp.ones_like(acc_sc)

    @pl.when(step < nq)
    def _():
        finish()
        _attn_tile(step, qt_ref, k_ref, vt_ref, m_sc, acc_sc, s_a, s_b)

    @pl.when(step == nq)
    def _():
        finish()


def _attn_tile(qi, qt_ref, k_ref, vt_ref, m_sc, acc_sc, s_a, s_b):
    tq = ATT_TILE
    q2t = []
    for h in range(HEADS_PER_STEP):
        qt = qt_ref[h]
        row = lax.broadcasted_iota(jnp.int32, qt.shape, 0)
        zero = jnp.zeros_like(qt)
        q2t.append(jnp.concatenate([jnp.where(row < QK_DIM, qt, zero), jnp.where(row >= QK_DIM, qt, zero)], axis=1))
    m_sc[...] = jnp.full_like(m_sc, NEG)
    acc_sc[...] = jnp.zeros_like(acc_sc)
    ones = jnp.ones((SUM_ROWS, tq), BF16)
    heads = range(HEADS_PER_STEP)
    tk = KEY_TILES * tq
    n_full = qi // KEY_TILES
    rem = qi % KEY_TILES

    def scores_into(s_ref, kt, h):
        start = pl.multiple_of(kt * tk, tk)
        s_ref[h] = jnp.dot(k_ref[pl.ds(start, tk), h * LANES:(h + 1) * LANES], q2t[h],
                           preferred_element_type=F32)

    def softmax_accumulate(s_ref, kt, nkeys, masked, h):
        s = s_ref[h] if nkeys == tk else s_ref[h, :nkeys, :]
        if masked:
            kc = lax.broadcasted_iota(jnp.int32, s.shape, 0) // CHUNK + kt * (tk // CHUNK)
            qc = (lax.broadcasted_iota(jnp.int32, s.shape, 1) & (tq - 1)) // CHUNK + qi * (tq // CHUNK)
            s = jnp.where(kc <= qc, s, NEG)
        m_old = m_sc[h, 0:1, :]
        m_new = jnp.maximum(m_old, jnp.max(s, axis=0, keepdims=True))
        alpha = jnp.exp2(m_old - m_new)
        pb = jnp.exp2(s - m_new).astype(BF16)
        pv = None
        for j in range(nkeys // tq):
            d = jnp.dot(jnp.concatenate([vt_ref[h, KEY_TILES * kt + j], ones], axis=0), pb[j * tq:(j + 1) * tq],
                        preferred_element_type=F32)
            pv = d if pv is None else pv + d
        acc_sc[h] = alpha * acc_sc[h] + pv
        m_sc[h, 0:1, :] = m_new

    def stage(kt, s_cur, s_nxt):
        for h in heads:
            scores_into(s_nxt, kt + 1, h)
            softmax_accumulate(s_cur, kt, tk, False, h)

    for h in heads:
        scores_into(s_a, 0, h)

    def body(kt, c):
        even = (kt & 1) == 0

        @pl.when(even)
        def _():
            stage(kt, s_a, s_b)

        @pl.when(jnp.logical_not(even))
        def _():
            stage(kt, s_b, s_a)

        return c

    lax.fori_loop(0, n_full, body, 0)

    for parity, s_ref in ((0, s_a), (1, s_b)):
        here = (n_full & 1) == parity
        for r in range(KEY_TILES):
            @pl.when(here & (rem == r))
            def _(s_ref=s_ref, r=r):
                for h in heads:
                    softmax_accumulate(s_ref, n_full, (r + 1) * tq, True, h)


def _attn_prompt(lams, g, qt, kb, vt, za, *, n_seq, seq_len):
    tq = ATT_TILE
    nq = seq_len // tq
    hps = HEADS_PER_STEP
    assert tq % CHUNK == 0 and tq & (tq - 1) == 0 and ATT_HEADS % hps == 0 and seq_len % (KEY_TILES * tq) == 0
    rowspec = pl.BlockSpec((tq, hps * LANES), lambda b, h, i: (b * nq + jnp.maximum(i - 1, 0), h))
    qtspec = pl.BlockSpec((None, hps, None, LANES, tq), lambda b, h, i: (b, h, jnp.minimum(i, nq - 1), 0, 0))
    kspec = pl.BlockSpec((seq_len, hps * LANES), lambda b, h, i: (b, h))
    vtspec = pl.BlockSpec((None, hps, nq, LANES, tq), lambda b, h, i: (b, h, 0, 0, 0))
    small = jnp.zeros((SUBLANES, PARAM_LANES), F32)
    for r, vec in enumerate(lams + (g,)):
        small = small.at[r, :vec.shape[1]].set(vec[0])
    stat = pltpu.VMEM((hps, SUBLANES, 2 * tq), F32)
    return pl.pallas_call(
        functools.partial(_attn_kernel, nq=nq),
        grid=(n_seq, ATT_HEADS // hps, nq + 1),
        in_specs=[_full((SUBLANES, PARAM_LANES)), qtspec, kspec, vtspec, rowspec],
        out_specs=rowspec,
        out_shape=jax.ShapeDtypeStruct((n_seq * seq_len, D_ATT), BF16),
        scratch_shapes=[stat, pltpu.VMEM((hps, V_DIM + SUM_ROWS, 2 * tq), F32),
                        pltpu.VMEM((hps, KEY_TILES * tq, 2 * tq), F32),
                        pltpu.VMEM((hps, KEY_TILES * tq, 2 * tq), F32)],
        compiler_params=_params(("arbitrary", "arbitrary", "arbitrary")),
        name="attn_prompt",
    )(small, qt, kb, vt, za)


def _attn_sample_kernel(lq1, lk1, lq2, lk2, g_ref, q_ref, kn_ref, vn_ref, ck_ref, cv_ref, za_ref, o_ref, *, t, past):
    lam = _lambda(lq1, lk1, lq2, lk2)
    for j in range(SEQS_PER_STEP):
        rows = slice(j * t, (j + 1) * t)
        outs = []
        for h in range(ATT_HEADS):
            cols = slice(h * LANES, (h + 1) * LANES)
            q2 = _stack_maps(q_ref[rows, cols])
            kpt = jnp.concatenate([ck_ref[j, h, 0], ck_ref[j, h, 1]], axis=0).astype(BF16)
            vp = cv_ref[j, pl.ds(h, past, stride=ATT_HEADS), :].astype(BF16)
            s_p = jnp.dot(q2, kpt, preferred_element_type=F32)
            s_n = lax.dot_general(q2, kn_ref[rows, cols], _NT, preferred_element_type=F32)
            m = jnp.maximum(jnp.max(s_p, axis=1, keepdims=True), jnp.max(s_n, axis=1, keepdims=True))
            p_p = jnp.exp2(s_p - m)
            p_n = jnp.exp2(s_n - m)
            l = jnp.sum(p_p, axis=1, keepdims=True) + jnp.sum(p_n, axis=1, keepdims=True)
            acc = (jnp.dot(p_p.astype(BF16), vp, preferred_element_type=F32)
                   + jnp.dot(p_n.astype(BF16), vn_ref[rows, cols], preferred_element_type=F32))
            outs.append(_finish_head(acc, l, lam, g_ref[...], za_ref[rows, cols], t))
        o_ref[rows, :] = jnp.concatenate(outs, axis=1).astype(BF16)


def _attn_sample(lams, g, q, kb, vb, ck, cv, za, *, n_seq, t, past):
    sps = SEQS_PER_STEP
    assert past % CHUNK == 0 and t <= CHUNK and n_seq % sps == 0
    row = pl.BlockSpec((sps * t, D_ATT), lambda b: (b, 0))
    kcache = pl.BlockSpec((sps, ATT_HEADS, 2, QK_DIM, past), lambda b: (b, 0, 0, 0, 0))
    vcache = pl.BlockSpec((sps, past * ATT_HEADS, V_DIM), lambda b: (b, 0, 0))
    vec = _full((1, QK_DIM))
    return pl.pallas_call(
        functools.partial(_attn_sample_kernel, t=t, past=past),
        grid=(n_seq // sps,),
        in_specs=[vec, vec, vec, vec, _full((1, V_DIM)), row, row, row, kcache, vcache, row],
        out_specs=row,
        out_shape=jax.ShapeDtypeStruct((n_seq * t, D_ATT), BF16),
        compiler_params=_params(("arbitrary",)),
        name="attn_sample",
    )(*lams, g, q, kb, vb, ck, cv, za)


def _out_proj_kernel(s_ref, a_ref, w_ref, x_ref, g_ref, y_ref):
    tm = s_ref.shape[0]
    step = min(tm, OUT_CHUNK)
    for r in range(0, tm, step):
        rows = slice(r, r + step)
        mix = (jnp.dot(s_ref[rows, :], w_ref[:D_SSM, :], preferred_element_type=F32)
               + jnp.dot(a_ref[rows, :], w_ref[D_SSM:, :], preferred_element_type=F32))
        ms = jnp.mean(mix * mix, axis=-1, keepdims=True)
        y_ref[rows, :] = x_ref[rows, :] + mix * lax.rsqrt(ms + EPS) * g_ref[...]


def _out_proj(ssm_out, att_out, w_bf, x2d, g, *, tm):
    n = x2d.shape[0]
    assert n % tm == 0 and tm % min(tm, OUT_CHUNK) == 0
    row = lambda i: (i, 0)
    half = pl.BlockSpec((tm, D_SSM), row)
    full = pl.BlockSpec((tm, D_MODEL), row)
    return pl.pallas_call(
        _out_proj_kernel,
        grid=(n // tm,),
        in_specs=[half, half, _full((D_MODEL, D_MODEL)), full, _full((1, D_MODEL))],
        out_specs=full,
        out_shape=jax.ShapeDtypeStruct((n, D_MODEL), F32),
        compiler_params=_params(("arbitrary",)),
        name="out_proj",
    )(ssm_out, att_out, w_bf, x2d, g)


def kernel(x_prompt, x_sample, cache_k, cache_v, state_ssm_re, state_ssm_im, norm_pre_g, w_in, ssm_lambda_re,
           ssm_lambda_im, ssm_log_dt, ssm_b_re, ssm_b_im, ssm_c_re, ssm_c_im, ssm_d, glu_w1, glu_b1, glu_w2,
           glu_b2, lambda_q1, lambda_k1, lambda_q2, lambda_k2, attn_subln_g, w_out, norm_post_g):
    bp, sp, _ = x_prompt.shape
    bs, ss, _ = x_sample.shape
    past = cache_k.shape[2]

    a_re, a_im, bdb, bdc = _prep(ssm_lambda_re[0], ssm_lambda_im[0], ssm_log_dt[0], ssm_b_re[0], ssm_b_im[0],
                                 ssm_c_re[0], ssm_c_im[0])
    a2 = jnp.concatenate([a_re.reshape(N_SLAB, LANES), a_im.reshape(N_SLAB, LANES)], axis=0)
    dvec = ssm_d[0].reshape(1, D_SSM)
    w_in_bf = w_in[0].astype(BF16)
    w_out_bf = w_out[0].astype(BF16)
    w1 = glu_w1[0].astype(BF16)
    w2 = glu_w2[0].astype(BF16)
    b1 = glu_b1[0].reshape(1, D_SSM)
    b2 = glu_b2[0].reshape(1, D_SSM)
    g_pre = norm_pre_g[0].reshape(1, D_MODEL)
    g_post = norm_post_g[0].reshape(1, D_MODEL)
    g_sub = attn_subln_g[0].reshape(1, V_DIM)
    lams = tuple(v[0].reshape(1, QK_DIM) for v in (lambda_q1, lambda_k1, lambda_q2, lambda_k2))
    inv = ROPE_THETA ** (-jnp.arange(ROPE_DIM // 2, dtype=F32) * 2.0 / ROPE_DIM)
    rotary_lane = (jnp.arange(LANES) % QK_DIM) < ROPE_DIM
    inv_lane = jnp.where(rotary_lane, jnp.tile(inv, LANES // (ROPE_DIM // 2)), 0.0).reshape(1, LANES)

    def run(x, n_seq, seq_len, pos0, chained, h0, tm):
        x2d = x.reshape(n_seq * seq_len, D_MODEL)
        u, zs, q, kf, kb, vf, vb, za = _in_proj(x2d, g_pre, w_in_bf, inv_lane, seq_len=seq_len, pos0=pos0, tm=tm,
                                                transposed_qv=chained)
        ssm_out, hf = _s5(u, zs, bdb, bdc, dvec, a2, w1, b1, w2, b2, h0,
                          n_seq=n_seq, seq_len=seq_len, chained=chained)
        if chained:
            att = _attn_prompt(lams, g_sub, q, kb, vb, za, n_seq=n_seq, seq_len=seq_len)
            k_out = jnp.transpose(kf, (0, 4, 1, 2, 3))[None]
        else:
            ck = jnp.transpose(cache_k[0], (0, 2, 3, 4, 1))
            cv = cache_v[0].reshape(n_seq, past * ATT_HEADS, V_DIM)
            att = _attn_sample(lams, g_sub, q, kb, vb, ck, cv, za, n_seq=n_seq, t=seq_len, past=past)
            k_out = kf.reshape(1, n_seq, seq_len, ATT_HEADS, 2, QK_DIM)
        y = _out_proj(ssm_out, att, w_out_bf, x2d, g_post, tm=min(OUT_TILE, n_seq * seq_len))
        return (y.reshape(n_seq, seq_len, D_MODEL),
                k_out,
                vf.reshape(1, n_seq, seq_len, ATT_HEADS, V_DIM),
                hf[:, :N_SLAB].reshape(1, n_seq, SSM_GROUPS, SSM_STATE),
                hf[:, N_SLAB:].reshape(1, n_seq, SSM_GROUPS, SSM_STATE))

    yp, kp, vp, hrp, hip = run(x_prompt, bp, sp, 0, True, jnp.zeros((bp, 2 * N_SLAB, LANES), F32), PROJ_TILE)
    h0 = jnp.concatenate([state_ssm_re[0].reshape(bs, N_SLAB, LANES), state_ssm_im[0].reshape(bs, N_SLAB, LANES)],
                         axis=1)
    ys, ks, vs, hrs, his = run(x_sample, bs, ss, past, False, h0, bs * ss)
    return (yp, ys, kp, vp, hrp, hip, ks, vs, hrs, his)
```

```python
import functools
import math

import jax
import jax.numpy as jnp
from jax import lax
from jax.experimental import pallas as pl
from jax.experimental.pallas import tpu as pltpu

F32 = jnp.float32
BF16 = jnp.bfloat16

D_MODEL = 1024
D_SSM = 512
D_ATT = 512
SSM_GROUP = 16
SSM_GROUPS = 32
SSM_STATE = 64
N_CH = SSM_GROUPS * SSM_STATE
ATT_HEADS = 4
QK_DIM = 64
V_DIM = 128
ROPE_DIM = 16
ROPE_THETA = 500000.0
CHUNK = 64
EPS = 1e-6
D_IN = 3072
LAMBDA_INIT = 0.8 - 0.6 * math.exp(-0.3 * 0)

HALF_IN = D_SSM // 2
SEG_U, SEG_ZS, SEG_Q, SEG_K, SEG_V, SEG_ZA, SEG_END = 0, 512, 1024, 1536, 2048, 2560, D_IN

LANES = 128
SUBLANES = 8
VMEM_LIMIT = 56 * 1024 * 1024
N_SLAB = N_CH // LANES
PROJ_TILE = 512
OUT_TILE = 2048
OUT_CHUNK = 256
S5_STEPS = 256
SLAB_PAD = 4
ATT_TILE = 256
KEY_TILES = 2
HEADS_PER_STEP = 4
PARAM_LANES = 512
SUM_ROWS = 16
SEQS_PER_STEP = 2
NEG = -3.0e38
LOG2E = math.log2(math.e)
Q_SCALE = QK_DIM ** -0.5 * LOG2E


def _params(sem):
    return pltpu.CompilerParams(dimension_semantics=sem, vmem_limit_bytes=VMEM_LIMIT)


def _full(shape):
    n = len(shape)
    return pl.BlockSpec(shape, lambda *_: (0,) * n)


def _prep_kernel(lr_ref, li_ref, ldt_ref, br_ref, bi_ref, cr_ref, ci_ref, ar_ref, ai_ref, bdb_ref, bdc_ref):
    lr = lr_ref[...]
    li = li_ref[...]
    dt = jnp.exp(ldt_ref[...])
    mag = jnp.exp(lr * dt)
    ar = mag * jnp.cos(li * dt)
    ai = mag * jnp.sin(li * dt)
    den = lr * lr + li * li
    cr = ((ar - 1.0) * lr + ai * li) / den
    ci = (ai * lr - (ar - 1.0) * li) / den
    ar_ref[...] = ar
    ai_ref[...] = ai
    br = br_ref[...]
    bi = bi_ref[...]
    crb = cr[:, None, :]
    cib = ci[:, None, :]
    bbar = (crb * br - cib * bi, crb * bi + cib * br)
    cmat = (cr_ref[...], -ci_ref[...])

    gh = SSM_GROUPS // 2
    rows_b, cols_b = gh * SSM_GROUP, gh * SSM_STATE
    spread_b = (lax.broadcasted_iota(jnp.int32, (SSM_STATE, cols_b), 1) & (SSM_STATE - 1)
                == lax.broadcasted_iota(jnp.int32, (SSM_STATE, cols_b), 0)).astype(BF16)
    keep_b = (lax.broadcasted_iota(jnp.int32, (rows_b, cols_b), 0) // SSM_GROUP
              == lax.broadcasted_iota(jnp.int32, (rows_b, cols_b), 1) // SSM_STATE)
    spread_c = (lax.broadcasted_iota(jnp.int32, (SSM_GROUP, rows_b), 1) & (SSM_GROUP - 1)
                == lax.broadcasted_iota(jnp.int32, (SSM_GROUP, rows_b), 0)).astype(BF16)
    keep_c = (lax.broadcasted_iota(jnp.int32, (cols_b, rows_b), 0) // SSM_STATE
              == lax.broadcasted_iota(jnp.int32, (cols_b, rows_b), 1) // SSM_GROUP)
    for b in range(2):
        for part in range(2):
            x = bbar[part][b * gh:(b + 1) * gh].reshape(rows_b, SSM_STATE).astype(BF16)
            t = jnp.dot(x, spread_b, preferred_element_type=F32)
            bdb_ref[b, :, part * cols_b:(part + 1) * cols_b] = jnp.where(keep_b, t, 0.0).astype(BF16)
            x = cmat[part][b * gh:(b + 1) * gh].reshape(cols_b, SSM_GROUP).astype(BF16)
            t = jnp.dot(x, spread_c, preferred_element_type=F32)
            bdc_ref[b, part * cols_b:(part + 1) * cols_b, :] = jnp.where(keep_c, t, 0.0).astype(BF16)


def _prep(lam_re, lam_im, log_dt, b_re, b_im, c_re, c_im):
    g, p, c = b_re.shape
    out_shape = (jax.ShapeDtypeStruct((g, p), F32), jax.ShapeDtypeStruct((g, p), F32),
                 jax.ShapeDtypeStruct((2, g // 2 * c, 2 * g // 2 * p), BF16),
                 jax.ShapeDtypeStruct((2, 2 * g // 2 * p, g // 2 * c), BF16))
    return pl.pallas_call(_prep_kernel, out_shape=out_shape, name="s5_prep")(
        lam_re, lam_im, log_dt.reshape(g, 1), jnp.swapaxes(b_re, 1, 2), jnp.swapaxes(b_im, 1, 2),
        jnp.swapaxes(c_re, 1, 2), jnp.swapaxes(c_im, 1, 2))


def _in_proj_kernel(x_ref, g_ref, w_ref, inv_ref, u_ref, zs_ref, q_ref, kf_ref, kb_ref, vf_ref,
                    vb_ref, za_ref, cl_sc, sl_sc, *, tm, seq_len, pos0, transposed_qv):
    i = pl.program_id(0)
    x = x_ref[...]
    ms = jnp.mean(x * x, axis=-1, keepdims=True)
    hn = (x * lax.rsqrt(ms + EPS) * g_ref[0:1, :]).astype(BF16)

    inv = inv_ref[0:1, :LANES]

    @pl.when(i == 0)
    def _():
        off = (lax.broadcasted_iota(jnp.int32, (tm, LANES), 0) & (seq_len - 1)).astype(F32) * inv
        cl_sc[...] = jnp.cos(off)
        sl_sc[...] = jnp.sin(off)

    base = (pos0 + ((i * tm) & (seq_len - 1))).astype(F32) * jnp.broadcast_to(inv, (SUBLANES, LANES))
    cb = jnp.cos(base)[:1]
    sb = jnp.sin(base)[:1]
    cl = cl_sc[...]
    sl = sl_sc[...]
    c_m = cb * cl - sb * sl
    sin = sb * cl + cb * sl
    lane = lax.broadcasted_iota(jnp.int32, (tm, LANES), 1) & (QK_DIM - 1)
    half = ROPE_DIM // 2
    s_lo = jnp.where(lane < half, -sin, 0.0)
    s_hi = jnp.where(lane >= half, sin, 0.0)

    def seg(lo, hi):
        return jnp.dot(hn, w_ref[:, lo:hi], preferred_element_type=F32)

    def rope(t):
        outs = []
        for h in range(ATT_HEADS):
            th = t[:, h * LANES:(h + 1) * LANES]
            outs.append(th * c_m + pltpu.roll(th, LANES - half, 1) * s_lo + pltpu.roll(th, half, 1) * s_hi)
        return jnp.concatenate(outs, axis=1)

    def put(ref, t):
        if not transposed_qv:
            ref[...] = t.astype(BF16)
            return
        for h in range(ATT_HEADS):
            tt = t[:, h * LANES:(h + 1) * LANES].T.astype(BF16)
            for c in range(tm // ATT_TILE):
                ref[h, c] = tt[:, c * ATT_TILE:(c + 1) * ATT_TILE]

    u_ref[...] = seg(SEG_U, SEG_ZS).astype(BF16)
    zs_ref[...] = seg(SEG_ZS, SEG_Q).astype(BF16)
    q = rope(seg(SEG_Q, SEG_K))
    put(q_ref, q * Q_SCALE)
    k = rope(seg(SEG_K, SEG_V))
    if transposed_qv:
        for h in range(ATT_HEADS):
            kt = k[:, h * LANES:(h + 1) * LANES].T
            kf_ref[h, 0] = kt[:QK_DIM]
            kf_ref[h, 1] = kt[QK_DIM:]
    else:
        kf_ref[...] = k
    kb_ref[...] = k.astype(BF16)
    v = seg(SEG_V, SEG_ZA)
    for h in range(ATT_HEADS):
        vf_ref[pl.ds(h, tm, stride=ATT_HEADS), :] = v[:, h * LANES:(h + 1) * LANES]
    put(vb_ref, v)
    za_ref[...] = seg(SEG_ZA, SEG_END).astype(BF16)


def _in_proj(x2d, g, w_bf, inv_lane, *, seq_len, pos0, tm, transposed_qv):
    n = x2d.shape[0]
    assert n % tm == 0 and seq_len & (seq_len - 1) == 0 and (tm % seq_len == 0 or seq_len % tm == 0)
    row = lambda i: (i, 0)
    o512 = pl.BlockSpec((tm, D_ATT), row)
    shp = lambda dt: jax.ShapeDtypeStruct((n, D_ATT), dt)
    if transposed_qv:
        assert seq_len % tm == 0 and tm % ATT_TILE == 0
        tps = seq_len // tm
        per = tm // ATT_TILE
        t_spec = pl.BlockSpec((None, ATT_HEADS, per, LANES, ATT_TILE), lambda i: (i // tps, 0, i % tps, 0, 0))
        t_shape = jax.ShapeDtypeStruct((n // seq_len, ATT_HEADS, seq_len // ATT_TILE, LANES, ATT_TILE), BF16)
        kf_spec = pl.BlockSpec((None, ATT_HEADS, 2, QK_DIM, tm), lambda i: (i // tps, 0, 0, 0, i % tps))
        kf_shape = jax.ShapeDtypeStruct((n // seq_len, ATT_HEADS, 2, QK_DIM, seq_len), F32)
    else:
        t_spec, t_shape = o512, shp(BF16)
        kf_spec, kf_shape = o512, shp(F32)
    vf_spec = pl.BlockSpec((tm * ATT_HEADS, V_DIM), row)
    vf_shape = jax.ShapeDtypeStruct((n * ATT_HEADS, V_DIM), F32)
    return pl.pallas_call(
        functools.partial(_in_proj_kernel, tm=tm, seq_len=seq_len, pos0=pos0, transposed_qv=transposed_qv),
        grid=(n // tm,),
        in_specs=[pl.BlockSpec((tm, D_MODEL), row), _full((SUBLANES, D_MODEL)), _full((D_MODEL, D_IN)),
                  _full((SUBLANES, PARAM_LANES))],
        out_specs=[o512, o512, t_spec, kf_spec, o512, vf_spec, t_spec, o512],
        out_shape=[shp(BF16), shp(BF16), t_shape, kf_shape, shp(BF16), vf_shape, t_shape, shp(BF16)],
        scratch_shapes=[pltpu.VMEM((tm, LANES), F32), pltpu.VMEM((tm, LANES), F32)],
        compiler_params=_params(("arbitrary",)),
        name="in_proj",
    )(x2d, g, w_bf, inv_lane)


def _cmul(ar, ai, br, bi):
    return ar * br - ai * bi, ar * bi + ai * br


def _s5_kernel(u_ref, up_ref, zsp_ref, bdb_ref, bdc_ref, d_ref, a_ref, w1_ref, b1_ref, w2_ref, b2_ref, h0_ref,
               out_ref, hf_ref, hbuf, hout_a, hout_b, hb16, car, *, n_chain, T, n_steps, chained, chain_group):
    n_rows = n_chain * T
    P = n_rows + SLAB_PAD
    half_cols = N_CH // 2
    i = pl.program_id(0)
    blocks = range(N_SLAB // SUBLANES)

    def project_and_scan(hout):
        u = u_ref[...].reshape(n_rows, D_SSM)
        for b in range(2):
            bu = jnp.dot(u[:, b * HALF_IN:(b + 1) * HALF_IN], bdb_ref[b], preferred_element_type=F32)
            for part in range(2):
                for k in range(N_SLAB // 2):
                    slab = part * N_SLAB + b * (N_SLAB // 2) + k
                    col = part * half_cols + k * LANES
                    hbuf[slab * P:slab * P + n_rows, :] = bu[:, col:col + LANES]
        a = [(a_ref[SUBLANES * k:SUBLANES * (k + 1), :], a_ref[N_SLAB + SUBLANES * k:N_SLAB + SUBLANES * (k + 1), :])
             for k in blocks]
        for g0 in range(0, n_chain, chain_group):
            chains = list(range(g0, g0 + chain_group))
            st = {(c, k): (car[c, SUBLANES * k:SUBLANES * (k + 1), :],
                           car[c, N_SLAB + SUBLANES * k:N_SLAB + SUBLANES * (k + 1), :])
                  for c in chains for k in blocks}
            for t in range(T):
                for c in chains:
                    for k in blocks:
                        hr, hi = st[(c, k)]
                        rows_r = pl.ds(SUBLANES * k * P + c * T + t, SUBLANES, stride=P)
                        rows_i = pl.ds((N_SLAB + SUBLANES * k) * P + c * T + t, SUBLANES, stride=P)
                        pr, pi = _cmul(a[k][0], a[k][1], hr, hi)
                        nr = pr + hbuf[rows_r, :]
                        ni = pi + hbuf[rows_i, :]
                        hout[rows_r, :] = nr
                        hout[rows_i, :] = ni
                        st[(c, k)] = (nr, ni)
            for c in chains:
                for k in blocks:
                    hr, hi = st[(c, k)]
                    car[c, SUBLANES * k:SUBLANES * (k + 1), :] = hr
                    car[c, N_SLAB + SUBLANES * k:N_SLAB + SUBLANES * (k + 1), :] = hi

    def output_stage(hout):
        up = up_ref[...].reshape(n_rows, D_SSM)
        for b in range(2):
            for part in range(2):
                for k in range(N_SLAB // 2):
                    slab = part * N_SLAB + b * (N_SLAB // 2) + k
                    col = b * N_CH + part * half_cols + k * LANES
                    hb16[:, col:col + LANES] = hout[slab * P:slab * P + n_rows, :].astype(BF16)
        ys = [jnp.dot(hb16[:, b * N_CH:(b + 1) * N_CH], bdc_ref[b], preferred_element_type=F32) for b in range(2)]
        y = jnp.concatenate(ys, axis=1) + up.astype(F32) * d_ref[0:1, :]
        gb = jax.nn.gelu(y).astype(BF16)
        y1 = jnp.dot(gb, w1_ref[...], preferred_element_type=F32) + b1_ref[0:1, :]
        y2 = jnp.dot(gb, w2_ref[...], preferred_element_type=F32) + b2_ref[0:1, :]
        zs = zsp_ref[...].reshape(n_rows, D_SSM).astype(F32)
        out_ref[...] = (y1 * jax.nn.sigmoid(y2) * jax.nn.silu(zs)).astype(BF16).reshape(n_chain, T, D_SSM)

    @pl.when(i == 0)
    def _():
        car[...] = jnp.zeros_like(car) if chained else h0_ref[...]
        project_and_scan(hout_a)

    if n_steps > 1:
        @pl.when((i > 0) & (i < n_steps) & ((i & 1) == 1))
        def _():
            project_and_scan(hout_b)
            output_stage(hout_a)

        @pl.when((i > 0) & (i < n_steps) & ((i & 1) == 0))
        def _():
            project_and_scan(hout_a)
            output_stage(hout_b)

    @pl.when(i == n_steps)
    def _():
        output_stage(hout_a if (n_steps - 1) % 2 == 0 else hout_b)
        hf_ref[...] = car[...]


def _s5(u, zs, bdb, bdc, dvec, a2, w1, b1, w2, b2, h0, *, n_seq, seq_len, chained):
    if chained:
        T = S5_STEPS
        chain_group = n_seq
    else:
        T = seq_len
        chain_group = 4
    assert seq_len % T == 0 and n_seq % chain_group == 0 and T % (2 * SUBLANES) == 0
    n_steps = seq_len // T
    n_rows = n_seq * T
    u3 = u.reshape(n_seq, seq_len, D_SSM)
    zs3 = zs.reshape(n_seq, seq_len, D_SSM)
    cur = pl.BlockSpec((n_seq, T, D_SSM), lambda t: (0, jnp.minimum(t, n_steps - 1), 0))
    prev = pl.BlockSpec((n_seq, T, D_SSM), lambda t: (0, jnp.maximum(t - 1, 0), 0))
    st_shape = (n_seq, 2 * N_SLAB, LANES)
    slabs = pltpu.VMEM((2 * N_SLAB * (n_rows + SLAB_PAD), LANES), F32)
    out, hf = pl.pallas_call(
        functools.partial(_s5_kernel, n_chain=n_seq, T=T, n_steps=n_steps, chained=chained,
                          chain_group=chain_group),
        grid=(n_steps + 1,),
        in_specs=[cur, prev, prev, _full(bdb.shape), _full(bdc.shape), _full((SUBLANES, D_SSM)),
                  _full((2 * N_SLAB, LANES)), _full((D_SSM, D_SSM)), _full((SUBLANES, D_SSM)),
                  _full((D_SSM, D_SSM)), _full((SUBLANES, D_SSM)), _full(st_shape)],
        out_specs=[prev, _full(st_shape)],
        out_shape=[jax.ShapeDtypeStruct((n_seq, seq_len, D_SSM), BF16), jax.ShapeDtypeStruct(st_shape, F32)],
        scratch_shapes=[slabs, slabs, slabs, pltpu.VMEM((n_rows, 2 * N_CH), BF16), pltpu.VMEM(st_shape, F32)],
        compiler_params=_params(("arbitrary",)),
        name="s5_chained" if chained else "s5_independent",
    )(u3, u3, zs3, bdb, bdc, dvec, a2, w1, b1, w2, b2, h0)
    return out.reshape(n_seq * seq_len, D_SSM), hf


def _lambda(lq1, lk1, lq2, lk2):
    s1 = jnp.sum(lq1[...] * lk1[...], axis=1, keepdims=True)
    s2 = jnp.sum(lq2[...] * lk2[...], axis=1, keepdims=True)
    return jnp.exp(s1) - jnp.exp(s2) + LAMBDA_INIT


def _stack_maps(q):
    lane = lax.broadcasted_iota(jnp.int32, q.shape, 1)
    zero = jnp.zeros_like(q)
    return jnp.concatenate([jnp.where(lane < QK_DIM, q, zero), jnp.where(lane >= QK_DIM, q, zero)], axis=0)


def _subln_gate(o, g, za):
    ms = jnp.mean(o * o, axis=-1, keepdims=True)
    on = (o * lax.rsqrt(ms + EPS) * g) * (1.0 - LAMBDA_INIT)
    return on * jax.nn.silu(za.astype(F32))


def _finish_head(acc, l, lam, g, za, t):
    inv = 1.0 / l
    o = acc[:t] * inv[:t] - lam * (acc[t:] * inv[t:])
    return _subln_gate(o, g, za)


_NT = (((1,), (1,)), ((), ()))


def _attn_kernel(par_ref, qt_ref, k_ref, vt_ref, za_ref, o_ref, m_sc, acc_sc, s_a, s_b, *, nq):
    tq = ATT_TILE
    step = pl.program_id(2)

    def finish():
        lam = _lambda(*(par_ref[r:r + 1, :] for r in range(4)))
        g = par_ref[4:5, :V_DIM]
        for h in range(HEADS_PER_STEP):
            acc = acc_sc[h, :V_DIM, :]
            inv = 1.0 / acc_sc[h, V_DIM:V_DIM + 1, :]
            ot = acc[:, :tq] * inv[:, :tq] - lam * (acc[:, tq:] * inv[:, tq:])
            cols = slice(h * LANES, (h + 1) * LANES)
            o_ref[:, cols] = _subln_gate(ot.T, g, za_ref[:, cols]).astype(BF16)

    @pl.when((pl.program_id(0) == 0) & (pl.program_id(1) == 0) & (step == 0))
    def _():
        acc_sc[...] = jnp.ones_like(acc_sc)

    @pl.when(step < nq)
    def _():
        finish()
        _attn_tile(step, qt_ref, k_ref, vt_ref, m_sc, acc_sc, s_a, s_b)

    @pl.when(step == nq)
    def _():
        finish()


def _attn_tile(qi, qt_ref, k_ref, vt_ref, m_sc, acc_sc, s_a, s_b):
    tq = ATT_TILE
    q2t = []
    for h in range(HEADS_PER_STEP):
        qt = qt_ref[h]
        row = lax.broadcasted_iota(jnp.int32, qt.shape, 0)
        zero = jnp.zeros_like(qt)
        q2t.append(jnp.concatenate([jnp.where(row < QK_DIM, qt, zero), jnp.where(row >= QK_DIM, qt, zero)], axis=1))
    m_sc[...] = jnp.full_like(m_sc, NEG)
    acc_sc[...] = jnp.zeros_like(acc_sc)
    ones = jnp.ones((SUM_ROWS, tq), BF16)
    heads = range(HEADS_PER_STEP)
    tk = KEY_TILES * tq
    n_full = qi // KEY_TILES
    rem = qi % KEY_TILES

    def scores_into(s_ref, kt, h):
        start = pl.multiple_of(kt * tk, tk)
        s_ref[h] = jnp.dot(k_ref[pl.ds(start, tk), h * LANES:(h + 1) * LANES], q2t[h],
                           preferred_element_type=F32)

    def softmax_accumulate(s_ref, kt, nkeys, masked, h):
        s = s_ref[h] if nkeys == tk else s_ref[h, :nkeys, :]
        if masked:
            kc = lax.broadcasted_iota(jnp.int32, s.shape, 0) // CHUNK + kt * (tk // CHUNK)
            qc = (lax.broadcasted_iota(jnp.int32, s.shape, 1) & (tq - 1)) // CHUNK + qi * (tq // CHUNK)
            s = jnp.where(kc <= qc, s, NEG)
        m_old = m_sc[h, 0:1, :]
        m_new = jnp.maximum(m_old, jnp.max(s, axis=0, keepdims=True))
        alpha = jnp.exp2(m_old - m_new)
        pb = jnp.exp2(s - m_new).astype(BF16)
        pv = None
        for j in range(nkeys // tq):
            d = jnp.dot(jnp.concatenate([vt_ref[h, KEY_TILES * kt + j], ones], axis=0), pb[j * tq:(j + 1) * tq],
                        preferred_element_type=F32)
            pv = d if pv is None else pv + d
        acc_sc[h] = alpha * acc_sc[h] + pv
        m_sc[h, 0:1, :] = m_new

    def stage(kt, s_cur, s_nxt):
        for h in heads:
            scores_into(s_nxt, kt + 1, h)
            softmax_accumulate(s_cur, kt, tk, False, h)

    for h in heads:
        scores_into(s_a, 0, h)

    def body(kt, c):
        even = (kt & 1) == 0

        @pl.when(even)
        def _():
            stage(kt, s_a, s_b)

        @pl.when(jnp.logical_not(even))
        def _():
            stage(kt, s_b, s_a)

        return c

    lax.fori_loop(0, n_full, body, 0)

    for parity, s_ref in ((0, s_a), (1, s_b)):
        here = (n_full & 1) == parity
        for r in range(KEY_TILES):
            @pl.when(here & (rem == r))
            def _(s_ref=s_ref, r=r):
                for h in heads:
                    softmax_accumulate(s_ref, n_full, (r + 1) * tq, True, h)


def _pack_rows(*vecs):
    block = jnp.zeros((SUBLANES, PARAM_LANES), F32)
    for r, vec in enumerate(vecs):
        block = block.at[r, :vec.shape[1]].set(vec[0])
    return block


def _attn_prompt(lams, g, qt, kb, vt, za, *, n_seq, seq_len):
    tq = ATT_TILE
    nq = seq_len // tq
    hps = HEADS_PER_STEP
    assert tq % CHUNK == 0 and tq & (tq - 1) == 0 and ATT_HEADS % hps == 0 and seq_len % (KEY_TILES * tq) == 0
    rowspec = pl.BlockSpec((tq, hps * LANES), lambda b, h, i: (b * nq + jnp.maximum(i - 1, 0), h))
    qtspec = pl.BlockSpec((None, hps, None, LANES, tq), lambda b, h, i: (b, h, jnp.minimum(i, nq - 1), 0, 0))
    kspec = pl.BlockSpec((seq_len, hps * LANES), lambda b, h, i: (b, h))
    vtspec = pl.BlockSpec((None, hps, nq, LANES, tq), lambda b, h, i: (b, h, 0, 0, 0))
    small = _pack_rows(*lams, g)
    stat = pltpu.VMEM((hps, SUBLANES, 2 * tq), F32)
    return pl.pallas_call(
        functools.partial(_attn_kernel, nq=nq),
        grid=(n_seq, ATT_HEADS // hps, nq + 1),
        in_specs=[_full((SUBLANES, PARAM_LANES)), qtspec, kspec, vtspec, rowspec],
        out_specs=rowspec,
        out_shape=jax.ShapeDtypeStruct((n_seq * seq_len, D_ATT), BF16),
        scratch_shapes=[stat, pltpu.VMEM((hps, V_DIM + SUM_ROWS, 2 * tq), F32),
                        pltpu.VMEM((hps, KEY_TILES * tq, 2 * tq), F32),
                        pltpu.VMEM((hps, KEY_TILES * tq, 2 * tq), F32)],
        compiler_params=_params(("arbitrary", "arbitrary", "arbitrary")),
        name="attn_prompt",
    )(small, qt, kb, vt, za)


def _attn_sample_kernel(par_ref, q_ref, kn_ref, vn_ref, ck_ref, cv_ref, za_ref, o_ref, *, t, past):
    lam = _lambda(*(par_ref[r:r + 1, :] for r in range(4)))
    g = par_ref[4:5, :V_DIM]
    for j in range(SEQS_PER_STEP):
        rows = slice(j * t, (j + 1) * t)
        outs = []
        for h in range(ATT_HEADS):
            cols = slice(h * LANES, (h + 1) * LANES)
            q2 = _stack_maps(q_ref[rows, cols])
            kpt = jnp.concatenate([ck_ref[j, h, 0], ck_ref[j, h, 1]], axis=0).astype(BF16)
            vp = cv_ref[j, pl.ds(h, past, stride=ATT_HEADS), :].astype(BF16)
            s_p = jnp.dot(q2, kpt, preferred_element_type=F32)
            s_n = lax.dot_general(q2, kn_ref[rows, cols], _NT, preferred_element_type=F32)
            m = jnp.maximum(jnp.max(s_p, axis=1, keepdims=True), jnp.max(s_n, axis=1, keepdims=True))
            p_p = jnp.exp2(s_p - m)
            p_n = jnp.exp2(s_n - m)
            l = jnp.sum(p_p, axis=1, keepdims=True) + jnp.sum(p_n, axis=1, keepdims=True)
            acc = (jnp.dot(p_p.astype(BF16), vp, preferred_element_type=F32)
                   + jnp.dot(p_n.astype(BF16), vn_ref[rows, cols], preferred_element_type=F32))
            outs.append(_finish_head(acc, l, lam, g, za_ref[rows, cols], t))
        o_ref[rows, :] = jnp.concatenate(outs, axis=1).astype(BF16)


def _attn_sample(lams, g, q, kb, vb, ck, cv, za, *, n_seq, t, past):
    sps = SEQS_PER_STEP
    assert past % CHUNK == 0 and t <= CHUNK and n_seq % sps == 0
    row = pl.BlockSpec((sps * t, D_ATT), lambda b: (b, 0))
    kcache = pl.BlockSpec((sps, ATT_HEADS, 2, QK_DIM, past), lambda b: (b, 0, 0, 0, 0))
    vcache = pl.BlockSpec((sps, past * ATT_HEADS, V_DIM), lambda b: (b, 0, 0))
    return pl.pallas_call(
        functools.partial(_attn_sample_kernel, t=t, past=past),
        grid=(n_seq // sps,),
        in_specs=[_full((SUBLANES, PARAM_LANES)), row, row, row, kcache, vcache, row],
        out_specs=row,
        out_shape=jax.ShapeDtypeStruct((n_seq * t, D_ATT), BF16),
        compiler_params=_params(("arbitrary",)),
        name="attn_sample",
    )(_pack_rows(*lams, g), q, kb, vb, ck, cv, za)


def _out_proj_kernel(s_ref, a_ref, w_ref, x_ref, g_ref, y_ref):
    tm = s_ref.shape[0]
    step = min(tm, OUT_CHUNK)
    for r in range(0, tm, step):
        rows = slice(r, r + step)
        mix = (jnp.dot(s_ref[rows, :], w_ref[:D_SSM, :], preferred_element_type=F32)
               + jnp.dot(a_ref[rows, :], w_ref[D_SSM:, :], preferred_element_type=F32))
        ms = jnp.mean(mix * mix, axis=-1, keepdims=True)
        y_ref[rows, :] = x_ref[rows, :] + mix * lax.rsqrt(ms + EPS) * g_ref[0:1, :]


def _out_proj(ssm_out, att_out, w_bf, x2d, g, *, tm):
    n = x2d.shape[0]
    assert n % tm == 0 and tm % min(tm, OUT_CHUNK) == 0
    row = lambda i: (i, 0)
    half = pl.BlockSpec((tm, D_SSM), row)
    full = pl.BlockSpec((tm, D_MODEL), row)
    return pl.pallas_call(
        _out_proj_kernel,
        grid=(n // tm,),
        in_specs=[half, half, _full((D_MODEL, D_MODEL)), full, _full((SUBLANES, D_MODEL))],
        out_specs=full,
        out_shape=jax.ShapeDtypeStruct((n, D_MODEL), F32),
        compiler_params=_params(("arbitrary",)),
        name="out_proj",
    )(ssm_out, att_out, w_bf, x2d, g)


def kernel(x_prompt, x_sample, cache_k, cache_v, state_ssm_re, state_ssm_im, norm_pre_g, w_in, ssm_lambda_re,
           ssm_lambda_im, ssm_log_dt, ssm_b_re, ssm_b_im, ssm_c_re, ssm_c_im, ssm_d, glu_w1, glu_b1, glu_w2,
           glu_b2, lambda_q1, lambda_k1, lambda_q2, lambda_k2, attn_subln_g, w_out, norm_post_g):
    bp, sp, _ = x_prompt.shape
    bs, ss, _ = x_sample.shape
    past = cache_k.shape[2]

    a_re, a_im, bdb, bdc = _prep(ssm_lambda_re[0], ssm_lambda_im[0], ssm_log_dt[0], ssm_b_re[0], ssm_b_im[0],
                                 ssm_c_re[0], ssm_c_im[0])
    a2 = jnp.concatenate([a_re.reshape(N_SLAB, LANES), a_im.reshape(N_SLAB, LANES)], axis=0)
    rows8 = lambda v: jnp.broadcast_to(v.reshape(1, -1), (SUBLANES, v.size))
    dvec = rows8(ssm_d[0])
    w_in_bf = w_in[0].astype(BF16)
    w_out_bf = w_out[0].astype(BF16)
    w1 = glu_w1[0].astype(BF16)
    w2 = glu_w2[0].astype(BF16)
    b1 = rows8(glu_b1[0])
    b2 = rows8(glu_b2[0])
    g_pre = rows8(norm_pre_g[0])
    g_post = rows8(norm_post_g[0])
    g_sub = attn_subln_g[0].reshape(1, V_DIM)
    lams = tuple(v[0].reshape(1, QK_DIM) for v in (lambda_q1, lambda_k1, lambda_q2, lambda_k2))
    inv = ROPE_THETA ** (-jnp.arange(ROPE_DIM // 2, dtype=F32) * 2.0 / ROPE_DIM)
    rotary_lane = (jnp.arange(LANES) % QK_DIM) < ROPE_DIM
    inv_lane = _pack_rows(jnp.where(rotary_lane, jnp.tile(inv, LANES // (ROPE_DIM // 2)), 0.0).reshape(1, LANES))

    def run(x, n_seq, seq_len, pos0, chained, h0, tm):
        x2d = x.reshape(n_seq * seq_len, D_MODEL)
        u, zs, q, kf, kb, vf, vb, za = _in_proj(x2d, g_pre, w_in_bf, inv_lane, seq_len=seq_len, pos0=pos0, tm=tm,
                                                transposed_qv=chained)
        ssm_out, hf = _s5(u, zs, bdb, bdc, dvec, a2, w1, b1, w2, b2, h0,
                          n_seq=n_seq, seq_len=seq_len, chained=chained)
        if chained:
            att = _attn_prompt(lams, g_sub, q, kb, vb, za, n_seq=n_seq, seq_len=seq_len)
            k_out = jnp.transpose(kf, (0, 4, 1, 2, 3))[None]
        else:
            ck = jnp.transpose(cache_k[0], (0, 2, 3, 4, 1))
            cv = cache_v[0].reshape(n_seq, past * ATT_HEADS, V_DIM)
            att = _attn_sample(lams, g_sub, q, kb, vb, ck, cv, za, n_seq=n_seq, t=seq_len, past=past)
            k_out = kf.reshape(1, n_seq, seq_len, ATT_HEADS, 2, QK_DIM)
        y = _out_proj(ssm_out, att, w_out_bf, x2d, g_post, tm=min(OUT_TILE, n_seq * seq_len))
        return (y.reshape(n_seq, seq_len, D_MODEL),
                k_out,
                vf.reshape(1, n_seq, seq_len, ATT_HEADS, V_DIM),
                hf[:, :N_SLAB].reshape(1, n_seq, SSM_GROUPS, SSM_STATE),
                hf[:, N_SLAB:].reshape(1, n_seq, SSM_GROUPS, SSM_STATE))

    yp, kp, vp, hrp, hip = run(x_prompt, bp, sp, 0, True, jnp.zeros((bp, 2 * N_SLAB, LANES), F32), PROJ_TILE)
    h0 = jnp.concatenate([state_ssm_re[0].reshape(bs, N_SLAB, LANES), state_ssm_im[0].reshape(bs, N_SLAB, LANES)],
                         axis=1)
    ys, ks, vs, hrs, his = run(x_sample, bs, ss, past, False, h0, bs * ss)
    return (yp, ys, kp, vp, hrp, hip, ks, vs, hrs, his)
```

```python
import functools
import math

import jax
import jax.numpy as jnp
from jax import lax
from jax.experimental import pallas as pl
from jax.experimental.pallas import tpu as pltpu

F32 = jnp.float32
BF16 = jnp.bfloat16

D_MODEL = 1024
D_SSM = 512
D_ATT = 512
SSM_GROUP = 16
SSM_GROUPS = 32
SSM_STATE = 64
N_CH = SSM_GROUPS * SSM_STATE
ATT_HEADS = 4
QK_DIM = 64
V_DIM = 128
ROPE_DIM = 16
ROPE_THETA = 500000.0
CHUNK = 64
EPS = 1e-6
D_IN = 3072
LAMBDA_INIT = 0.8 - 0.6 * math.exp(-0.3 * 0)

HALF_IN = D_SSM // 2
SEG_U, SEG_ZS, SEG_Q, SEG_K, SEG_V, SEG_ZA, SEG_END = 0, 512, 1024, 1536, 2048, 2560, D_IN

LANES = 128
SUBLANES = 8
VMEM_LIMIT = 56 * 1024 * 1024
N_SLAB = N_CH // LANES
PROJ_TILE = 1024
OUT_TILE = 2048
OUT_CHUNK = 256
S5_STEPS = 256
SLAB_PAD = 4
ATT_TILE = 256
KEY_TILES = 2
HEADS_PER_STEP = 4
PARAM_LANES = 512
SUM_ROWS = 16
SEQS_PER_STEP = 2
NEG = -3.0e38
LOG2E = math.log2(math.e)
Q_SCALE = QK_DIM ** -0.5 * LOG2E


def _params(sem):
    return pltpu.CompilerParams(dimension_semantics=sem, vmem_limit_bytes=VMEM_LIMIT)


def _full(shape):
    n = len(shape)
    return pl.BlockSpec(shape, lambda *_: (0,) * n)


def _prep_kernel(lr_ref, li_ref, ldt_ref, br_ref, bi_ref, cr_ref, ci_ref, ar_ref, ai_ref, bdb_ref, bdc_ref):
    lr = lr_ref[...]
    li = li_ref[...]
    dt = jnp.exp(ldt_ref[...])
    mag = jnp.exp(lr * dt)
    ar = mag * jnp.cos(li * dt)
    ai = mag * jnp.sin(li * dt)
    den = lr * lr + li * li
    cr = ((ar - 1.0) * lr + ai * li) / den
    ci = (ai * lr - (ar - 1.0) * li) / den
    ar_ref[...] = ar
    ai_ref[...] = ai
    br = br_ref[...]
    bi = bi_ref[...]
    crb = cr[:, None, :]
    cib = ci[:, None, :]
    bbar = (crb * br - cib * bi, crb * bi + cib * br)
    cmat = (cr_ref[...], -ci_ref[...])

    gh = SSM_GROUPS // 2
    rows_b, cols_b = gh * SSM_GROUP, gh * SSM_STATE
    spread_b = (lax.broadcasted_iota(jnp.int32, (SSM_STATE, cols_b), 1) & (SSM_STATE - 1)
                == lax.broadcasted_iota(jnp.int32, (SSM_STATE, cols_b), 0)).astype(BF16)
    keep_b = (lax.broadcasted_iota(jnp.int32, (rows_b, cols_b), 0) // SSM_GROUP
              == lax.broadcasted_iota(jnp.int32, (rows_b, cols_b), 1) // SSM_STATE)
    spread_c = (lax.broadcasted_iota(jnp.int32, (SSM_GROUP, rows_b), 1) & (SSM_GROUP - 1)
                == lax.broadcasted_iota(jnp.int32, (SSM_GROUP, rows_b), 0)).astype(BF16)
    keep_c = (lax.broadcasted_iota(jnp.int32, (cols_b, rows_b), 0) // SSM_STATE
              == lax.broadcasted_iota(jnp.int32, (cols_b, rows_b), 1) // SSM_GROUP)
    for b in range(2):
        for part in range(2):
            x = bbar[part][b * gh:(b + 1) * gh].reshape(rows_b, SSM_STATE).astype(BF16)
            t = jnp.dot(x, spread_b, preferred_element_type=F32)
            bdb_ref[b, :, part * cols_b:(part + 1) * cols_b] = jnp.where(keep_b, t, 0.0).astype(BF16)
            x = cmat[part][b * gh:(b + 1) * gh].reshape(cols_b, SSM_GROUP).astype(BF16)
            t = jnp.dot(x, spread_c, preferred_element_type=F32)
            bdc_ref[b, part * cols_b:(part + 1) * cols_b, :] = jnp.where(keep_c, t, 0.0).astype(BF16)


def _prep(lam_re, lam_im, log_dt, b_re, b_im, c_re, c_im):
    g, p, c = b_re.shape
    out_shape = (jax.ShapeDtypeStruct((g, p), F32), jax.ShapeDtypeStruct((g, p), F32),
                 jax.ShapeDtypeStruct((2, g // 2 * c, 2 * g // 2 * p), BF16),
                 jax.ShapeDtypeStruct((2, 2 * g // 2 * p, g // 2 * c), BF16))
    return pl.pallas_call(_prep_kernel, out_shape=out_shape, name="s5_prep")(
        lam_re, lam_im, log_dt.reshape(g, 1), jnp.swapaxes(b_re, 1, 2), jnp.swapaxes(b_im, 1, 2),
        jnp.swapaxes(c_re, 1, 2), jnp.swapaxes(c_im, 1, 2))


def _in_proj_kernel(x_ref, g_ref, w_ref, inv_ref, u_ref, zs_ref, q_ref, kf_ref, kb_ref, vf_ref,
                    vb_ref, za_ref, cl_sc, sl_sc, *, tm, seq_len, pos0, transposed_qv):
    i = pl.program_id(0)
    x = x_ref[...]
    ms = jnp.mean(x * x, axis=-1, keepdims=True)
    hn = (x * lax.rsqrt(ms + EPS) * g_ref[0:1, :]).astype(BF16)

    inv = inv_ref[0:1, :LANES]

    @pl.when(i == 0)
    def _():
        off = (lax.broadcasted_iota(jnp.int32, (tm, LANES), 0) & (seq_len - 1)).astype(F32) * inv
        cl_sc[...] = jnp.cos(off)
        sl_sc[...] = jnp.sin(off)

    base = (pos0 + ((i * tm) & (seq_len - 1))).astype(F32) * jnp.broadcast_to(inv, (SUBLANES, LANES))
    cb = jnp.cos(base)[:1]
    sb = jnp.sin(base)[:1]
    cl = cl_sc[...]
    sl = sl_sc[...]
    c_m = cb * cl - sb * sl
    sin = sb * cl + cb * sl
    lane = lax.broadcasted_iota(jnp.int32, (tm, LANES), 1) & (QK_DIM - 1)
    half = ROPE_DIM // 2
    s_lo = jnp.where(lane < half, -sin, 0.0)
    s_hi = jnp.where(lane >= half, sin, 0.0)

    def seg(lo, hi):
        return jnp.dot(hn, w_ref[:, lo:hi], preferred_element_type=F32)

    def rope(t):
        outs = []
        for h in range(ATT_HEADS):
            th = t[:, h * LANES:(h + 1) * LANES]
            outs.append(th * c_m + pltpu.roll(th, LANES - half, 1) * s_lo + pltpu.roll(th, half, 1) * s_hi)
        return jnp.concatenate(outs, axis=1)

    def put(ref, t):
        if not transposed_qv:
            ref[...] = t.astype(BF16)
            return
        for h in range(ATT_HEADS):
            tt = t[:, h * LANES:(h + 1) * LANES].T.astype(BF16)
            for c in range(tm // ATT_TILE):
                ref[h, c] = tt[:, c * ATT_TILE:(c + 1) * ATT_TILE]

    u_ref[...] = seg(SEG_U, SEG_ZS).astype(BF16)
    zs_ref[...] = seg(SEG_ZS, SEG_Q).astype(BF16)
    q = rope(seg(SEG_Q, SEG_K))
    put(q_ref, q * Q_SCALE)
    k = rope(seg(SEG_K, SEG_V))
    if transposed_qv:
        for h in range(ATT_HEADS):
            kt = k[:, h * LANES:(h + 1) * LANES].T
            kf_ref[h, 0] = kt[:QK_DIM]
            kf_ref[h, 1] = kt[QK_DIM:]
    else:
        kf_ref[...] = k
    kb_ref[...] = k.astype(BF16)
    v = seg(SEG_V, SEG_ZA)
    for h in range(ATT_HEADS):
        vf_ref[pl.ds(h, tm, stride=ATT_HEADS), :] = v[:, h * LANES:(h + 1) * LANES]
    put(vb_ref, v)
    za_ref[...] = seg(SEG_ZA, SEG_END).astype(BF16)


def _in_proj(x2d, g, w_bf, inv_lane, *, seq_len, pos0, tm, transposed_qv):
    n = x2d.shape[0]
    assert n % tm == 0 and seq_len & (seq_len - 1) == 0 and (tm % seq_len == 0 or seq_len % tm == 0)
    row = lambda i: (i, 0)
    o512 = pl.BlockSpec((tm, D_ATT), row)
    shp = lambda dt: jax.ShapeDtypeStruct((n, D_ATT), dt)
    if transposed_qv:
        assert seq_len % tm == 0 and tm % ATT_TILE == 0
        tps = seq_len // tm
        per = tm // ATT_TILE
        t_spec = pl.BlockSpec((None, ATT_HEADS, per, LANES, ATT_TILE), lambda i: (i // tps, 0, i % tps, 0, 0))
        t_shape = jax.ShapeDtypeStruct((n // seq_len, ATT_HEADS, seq_len // ATT_TILE, LANES, ATT_TILE), BF16)
        kf_spec = pl.BlockSpec((None, ATT_HEADS, 2, QK_DIM, tm), lambda i: (i // tps, 0, 0, 0, i % tps))
        kf_shape = jax.ShapeDtypeStruct((n // seq_len, ATT_HEADS, 2, QK_DIM, seq_len), F32)
    else:
        t_spec, t_shape = o512, shp(BF16)
        kf_spec, kf_shape = o512, shp(F32)
    vf_spec = pl.BlockSpec((tm * ATT_HEADS, V_DIM), row)
    vf_shape = jax.ShapeDtypeStruct((n * ATT_HEADS, V_DIM), F32)
    return pl.pallas_call(
        functools.partial(_in_proj_kernel, tm=tm, seq_len=seq_len, pos0=pos0, transposed_qv=transposed_qv),
        grid=(n // tm,),
        in_specs=[pl.BlockSpec((tm, D_MODEL), row), _full((SUBLANES, D_MODEL)), _full((D_MODEL, D_IN)),
                  _full((SUBLANES, PARAM_LANES))],
        out_specs=[o512, o512, t_spec, kf_spec, o512, vf_spec, t_spec, o512],
        out_shape=[shp(BF16), shp(BF16), t_shape, kf_shape, shp(BF16), vf_shape, t_shape, shp(BF16)],
        scratch_shapes=[pltpu.VMEM((tm, LANES), F32), pltpu.VMEM((tm, LANES), F32)],
        compiler_params=_params(("arbitrary",)),
        name="in_proj",
    )(x2d, g, w_bf, inv_lane)


def _cmul(ar, ai, br, bi):
    return ar * br - ai * bi, ar * bi + ai * br


def _s5_kernel(u_ref, up_ref, zsp_ref, bdb_ref, bdc_ref, d_ref, a_ref, w1_ref, b1_ref, w2_ref, b2_ref, h0_ref,
               out_ref, hf_ref, hbuf, hout_a, hout_b, hb16, car, *, n_chain, T, n_steps, chained, chain_group):
    n_rows = n_chain * T
    P = n_rows + SLAB_PAD
    half_cols = N_CH // 2
    i = pl.program_id(0)
    blocks = range(N_SLAB // SUBLANES)

    def project_and_scan(hout):
        u = u_ref[...].reshape(n_rows, D_SSM)
        for b in range(2):
            bu = jnp.dot(u[:, b * HALF_IN:(b + 1) * HALF_IN], bdb_ref[b], preferred_element_type=F32)
            for part in range(2):
                for k in range(N_SLAB // 2):
                    slab = part * N_SLAB + b * (N_SLAB // 2) + k
                    col = part * half_cols + k * LANES
                    hbuf[slab * P:slab * P + n_rows, :] = bu[:, col:col + LANES]
        a = [(a_ref[SUBLANES * k:SUBLANES * (k + 1), :], a_ref[N_SLAB + SUBLANES * k:N_SLAB + SUBLANES * (k + 1), :])
             for k in blocks]
        for g0 in range(0, n_chain, chain_group):
            chains = list(range(g0, g0 + chain_group))
            st = {(c, k): (car[c, SUBLANES * k:SUBLANES * (k + 1), :],
                           car[c, N_SLAB + SUBLANES * k:N_SLAB + SUBLANES * (k + 1), :])
                  for c in chains for k in blocks}
            for t in range(T):
                for c in chains:
                    for k in blocks:
                        hr, hi = st[(c, k)]
                        rows_r = pl.ds(SUBLANES * k * P + c * T + t, SUBLANES, stride=P)
                        rows_i = pl.ds((N_SLAB + SUBLANES * k) * P + c * T + t, SUBLANES, stride=P)
                        pr, pi = _cmul(a[k][0], a[k][1], hr, hi)
                        nr = pr + hbuf[rows_r, :]
                        ni = pi + hbuf[rows_i, :]
                        hout[rows_r, :] = nr
                        hout[rows_i, :] = ni
                        st[(c, k)] = (nr, ni)
            for c in chains:
                for k in blocks:
                    hr, hi = st[(c, k)]
                    car[c, SUBLANES * k:SUBLANES * (k + 1), :] = hr
                    car[c, N_SLAB + SUBLANES * k:N_SLAB + SUBLANES * (k + 1), :] = hi

    def output_stage(hout):
        up = up_ref[...].reshape(n_rows, D_SSM)
        for b in range(2):
            for part in range(2):
                for k in range(N_SLAB // 2):
                    slab = part * N_SLAB + b * (N_SLAB // 2) + k
                    col = b * N_CH + part * half_cols + k * LANES
                    hb16[:, col:col + LANES] = hout[slab * P:slab * P + n_rows, :].astype(BF16)
        ys = [jnp.dot(hb16[:, b * N_CH:(b + 1) * N_CH], bdc_ref[b], preferred_element_type=F32) for b in range(2)]
        y = jnp.concatenate(ys, axis=1) + up.astype(F32) * d_ref[0:1, :]
        gb = jax.nn.gelu(y).astype(BF16)
        y1 = jnp.dot(gb, w1_ref[...], preferred_element_type=F32) + b1_ref[0:1, :]
        y2 = jnp.dot(gb, w2_ref[...], preferred_element_type=F32) + b2_ref[0:1, :]
        zs = zsp_ref[...].reshape(n_rows, D_SSM).astype(F32)
        out_ref[...] = (y1 * jax.nn.sigmoid(y2) * jax.nn.silu(zs)).astype(BF16).reshape(n_chain, T, D_SSM)

    @pl.when(i == 0)
    def _():
        car[...] = jnp.zeros_like(car) if chained else h0_ref[...]
        project_and_scan(hout_a)

    if n_steps > 1:
        @pl.when((i > 0) & (i < n_steps) & ((i & 1) == 1))
        def _():
            project_and_scan(hout_b)
            output_stage(hout_a)

        @pl.when((i > 0) & (i < n_steps) & ((i & 1) == 0))
        def _():
            project_and_scan(hout_a)
            output_stage(hout_b)

    @pl.when(i == n_steps)
    def _():
        output_stage(hout_a if (n_steps - 1) % 2 == 0 else hout_b)
        hf_ref[...] = car[...]


def _s5(u, zs, bdb, bdc, dvec, a2, w1, b1, w2, b2, h0, *, n_seq, seq_len, chained):
    if chained:
        T = S5_STEPS
        chain_group = n_seq
    else:
        T = seq_len
        chain_group = 4
    assert seq_len % T == 0 and n_seq % chain_group == 0 and T % (2 * SUBLANES) == 0
    n_steps = seq_len // T
    n_rows = n_seq * T
    u3 = u.reshape(n_seq, seq_len, D_SSM)
    zs3 = zs.reshape(n_seq, seq_len, D_SSM)
    cur = pl.BlockSpec((n_seq, T, D_SSM), lambda t: (0, jnp.minimum(t, n_steps - 1), 0))
    prev = pl.BlockSpec((n_seq, T, D_SSM), lambda t: (0, jnp.maximum(t - 1, 0), 0))
    st_shape = (n_seq, 2 * N_SLAB, LANES)
    slabs = pltpu.VMEM((2 * N_SLAB * (n_rows + SLAB_PAD), LANES), F32)
    out, hf = pl.pallas_call(
        functools.partial(_s5_kernel, n_chain=n_seq, T=T, n_steps=n_steps, chained=chained,
                          chain_group=chain_group),
        grid=(n_steps + 1,),
        in_specs=[cur, prev, prev, _full(bdb.shape), _full(bdc.shape), _full((SUBLANES, D_SSM)),
                  _full((2 * N_SLAB, LANES)), _full((D_SSM, D_SSM)), _full((SUBLANES, D_SSM)),
                  _full((D_SSM, D_SSM)), _full((SUBLANES, D_SSM)), _full(st_shape)],
        out_specs=[prev, _full(st_shape)],
        out_shape=[jax.ShapeDtypeStruct((n_seq, seq_len, D_SSM), BF16), jax.ShapeDtypeStruct(st_shape, F32)],
        scratch_shapes=[slabs, slabs, slabs, pltpu.VMEM((n_rows, 2 * N_CH), BF16), pltpu.VMEM(st_shape, F32)],
        compiler_params=_params(("arbitrary",)),
        name="s5_chained" if chained else "s5_independent",
    )(u3, u3, zs3, bdb, bdc, dvec, a2, w1, b1, w2, b2, h0)
    return out.reshape(n_seq * seq_len, D_SSM), hf


def _lambda(lq1, lk1, lq2, lk2):
    s1 = jnp.sum(lq1[...] * lk1[...], axis=1, keepdims=True)
    s2 = jnp.sum(lq2[...] * lk2[...], axis=1, keepdims=True)
    return jnp.exp(s1) - jnp.exp(s2) + LAMBDA_INIT


def _stack_maps(q):
    lane = lax.broadcasted_iota(jnp.int32, q.shape, 1)
    zero = jnp.zeros_like(q)
    return jnp.concatenate([jnp.where(lane < QK_DIM, q, zero), jnp.where(lane >= QK_DIM, q, zero)], axis=0)


def _subln_gate(o, g, za):
    ms = jnp.mean(o * o, axis=-1, keepdims=True)
    on = (o * lax.rsqrt(ms + EPS) * g) * (1.0 - LAMBDA_INIT)
    return on * jax.nn.silu(za.astype(F32))


def _finish_head(acc, l, lam, g, za, t):
    inv = 1.0 / l
    o = acc[:t] * inv[:t] - lam * (acc[t:] * inv[t:])
    return _subln_gate(o, g, za)


_NT = (((1,), (1,)), ((), ()))


def _attn_kernel(par_ref, qt_ref, k_ref, vt_ref, za_ref, o_ref, m_sc, acc_sc, s_a, s_b, *, nq):
    tq = ATT_TILE
    step = pl.program_id(2)

    def finish():
        lam = _lambda(*(par_ref[r:r + 1, :] for r in range(4)))
        g = par_ref[4:5, :V_DIM]
        for h in range(HEADS_PER_STEP):
            acc = acc_sc[h, :V_DIM, :]
            inv = 1.0 / acc_sc[h, V_DIM:V_DIM + 1, :]
            ot = acc[:, :tq] * inv[:, :tq] - lam * (acc[:, tq:] * inv[:, tq:])
            cols = slice(h * LANES, (h + 1) * LANES)
            o_ref[:, cols] = _subln_gate(ot.T, g, za_ref[:, cols]).astype(BF16)

    @pl.when((pl.program_id(0) == 0) & (pl.program_id(1) == 0) & (step == 0))
    def _():
        acc_sc[...] = jnp.ones_like(acc_sc)

    @pl.when(step < nq)
    def _():
        finish()
        _attn_tile(step, qt_ref, k_ref, vt_ref, m_sc, acc_sc, s_a, s_b)

    @pl.when(step == nq)
    def _():
        finish()


def _attn_tile(qi, qt_ref, k_ref, vt_ref, m_sc, acc_sc, s_a, s_b):
    tq = ATT_TILE
    q2t = []
    for h in range(HEADS_PER_STEP):
        qt = qt_ref[h]
        row = lax.broadcasted_iota(jnp.int32, qt.shape, 0)
        zero = jnp.zeros_like(qt)
        q2t.append(jnp.concatenate([jnp.where(row < QK_DIM, qt, zero), jnp.where(row >= QK_DIM, qt, zero)], axis=1))
    m_sc[...] = jnp.full_like(m_sc, NEG)
    acc_sc[...] = jnp.zeros_like(acc_sc)
    ones = jnp.ones((SUM_ROWS, tq), BF16)
    heads = range(HEADS_PER_STEP)
    tk = KEY_TILES * tq
    n_full = qi // KEY_TILES
    rem = qi % KEY_TILES

    def scores_into(s_ref, kt, h):
        start = pl.multiple_of(kt * tk, tk)
        s_ref[h] = jnp.dot(k_ref[pl.ds(start, tk), h * LANES:(h + 1) * LANES], q2t[h],
                           preferred_element_type=F32)

    def softmax_accumulate(s_ref, kt, nkeys, masked, h):
        s = s_ref[h] if nkeys == tk else s_ref[h, :nkeys, :]
        if masked:
            kc = lax.broadcasted_iota(jnp.int32, s.shape, 0) // CHUNK + kt * (tk // CHUNK)
            qc = (lax.broadcasted_iota(jnp.int32, s.shape, 1) & (tq - 1)) // CHUNK + qi * (tq // CHUNK)
            s = jnp.where(kc <= qc, s, NEG)
        m_old = m_sc[h, 0:1, :]
        m_new = jnp.maximum(m_old, jnp.max(s, axis=0, keepdims=True))
        alpha = jnp.exp2(m_old - m_new)
        pb = jnp.exp2(s - m_new).astype(BF16)
        pv = None
        for j in range(nkeys // tq):
            d = jnp.dot(jnp.concatenate([vt_ref[h, KEY_TILES * kt + j], ones], axis=0), pb[j * tq:(j + 1) * tq],
                        preferred_element_type=F32)
            pv = d if pv is None else pv + d
        acc_sc[h] = alpha * acc_sc[h] + pv
        m_sc[h, 0:1, :] = m_new

    def stage(kt, s_cur, s_nxt):
        for h in heads:
            scores_into(s_nxt, kt + 1, h)
            softmax_accumulate(s_cur, kt, tk, False, h)

    for h in heads:
        scores_into(s_a, 0, h)

    def body(kt, c):
        even = (kt & 1) == 0

        @pl.when(even)
        def _():
            stage(kt, s_a, s_b)

        @pl.when(jnp.logical_not(even))
        def _():
            stage(kt, s_b, s_a)

        return c

    lax.fori_loop(0, n_full, body, 0)

    for parity, s_ref in ((0, s_a), (1, s_b)):
        here = (n_full & 1) == parity
        for r in range(KEY_TILES):
            @pl.when(here & (rem == r))
            def _(s_ref=s_ref, r=r):
                for h in heads:
                    softmax_accumulate(s_ref, n_full, (r + 1) * tq, True, h)


def _pack_rows(*vecs):
    block = jnp.zeros((SUBLANES, PARAM_LANES), F32)
    for r, vec in enumerate(vecs):
        block = block.at[r, :vec.shape[1]].set(vec[0])
    return block


def _attn_prompt(lams, g, qt, kb, vt, za, *, n_seq, seq_len):
    tq = ATT_TILE
    nq = seq_len // tq
    hps = HEADS_PER_STEP
    assert tq % CHUNK == 0 and tq & (tq - 1) == 0 and ATT_HEADS % hps == 0 and seq_len % (KEY_TILES * tq) == 0
    rowspec = pl.BlockSpec((tq, hps * LANES), lambda b, h, i: (b * nq + jnp.maximum(i - 1, 0), h))
    qtspec = pl.BlockSpec((None, hps, None, LANES, tq), lambda b, h, i: (b, h, jnp.minimum(i, nq - 1), 0, 0))
    kspec = pl.BlockSpec((seq_len, hps * LANES), lambda b, h, i: (b, h))
    vtspec = pl.BlockSpec((None, hps, nq, LANES, tq), lambda b, h, i: (b, h, 0, 0, 0))
    small = _pack_rows(*lams, g)
    stat = pltpu.VMEM((hps, SUBLANES, 2 * tq), F32)
    return pl.pallas_call(
        functools.partial(_attn_kernel, nq=nq),
        grid=(n_seq, ATT_HEADS // hps, nq + 1),
        in_specs=[_full((SUBLANES, PARAM_LANES)), qtspec, kspec, vtspec, rowspec],
        out_specs=rowspec,
        out_shape=jax.ShapeDtypeStruct((n_seq * seq_len, D_ATT), BF16),
        scratch_shapes=[stat, pltpu.VMEM((hps, V_DIM + SUM_ROWS, 2 * tq), F32),
                        pltpu.VMEM((hps, KEY_TILES * tq, 2 * tq), F32),
                        pltpu.VMEM((hps, KEY_TILES * tq, 2 * tq), F32)],
        compiler_params=_params(("arbitrary", "arbitrary", "arbitrary")),
        name="attn_prompt",
    )(small, qt, kb, vt, za)


def _attn_sample_kernel(par_ref, q_ref, kn_ref, vn_ref, ck_ref, cv_ref, za_ref, o_ref, *, t, past):
    lam = _lambda(*(par_ref[r:r + 1, :] for r in range(4)))
    g = par_ref[4:5, :V_DIM]
    for j in range(SEQS_PER_STEP):
        rows = slice(j * t, (j + 1) * t)
        outs = []
        for h in range(ATT_HEADS):
            cols = slice(h * LANES, (h + 1) * LANES)
            q2 = _stack_maps(q_ref[rows, cols])
            kpt = jnp.concatenate([ck_ref[j, h, 0], ck_ref[j, h, 1]], axis=0).astype(BF16)
            vp = cv_ref[j, pl.ds(h, past, stride=ATT_HEADS), :].astype(BF16)
            s_p = jnp.dot(q2, kpt, preferred_element_type=F32)
            s_n = lax.dot_general(q2, kn_ref[rows, cols], _NT, preferred_element_type=F32)
            m = jnp.maximum(jnp.max(s_p, axis=1, keepdims=True), jnp.max(s_n, axis=1, keepdims=True))
            p_p = jnp.exp2(s_p - m)
            p_n = jnp.exp2(s_n - m)
            l = jnp.sum(p_p, axis=1, keepdims=True) + jnp.sum(p_n, axis=1, keepdims=True)
            acc = (jnp.dot(p_p.astype(BF16), vp, preferred_element_type=F32)
                   + jnp.dot(p_n.astype(BF16), vn_ref[rows, cols], preferred_element_type=F32))
            outs.append(_finish_head(acc, l, lam, g, za_ref[rows, cols], t))
        o_ref[rows, :] = jnp.concatenate(outs, axis=1).astype(BF16)


def _attn_sample(lams, g, q, kb, vb, ck, cv, za, *, n_seq, t, past):
    sps = SEQS_PER_STEP
    assert past % CHUNK == 0 and t <= CHUNK and n_seq % sps == 0
    row = pl.BlockSpec((sps * t, D_ATT), lambda b: (b, 0))
    kcache = pl.BlockSpec((sps, ATT_HEADS, 2, QK_DIM, past), lambda b: (b, 0, 0, 0, 0))
    vcache = pl.BlockSpec((sps, past * ATT_HEADS, V_DIM), lambda b: (b, 0, 0))
    return pl.pallas_call(
        functools.partial(_attn_sample_kernel, t=t, past=past),
        grid=(n_seq // sps,),
        in_specs=[_full((SUBLANES, PARAM_LANES)), row, row, row, kcache, vcache, row],
        out_specs=row,
        out_shape=jax.ShapeDtypeStruct((n_seq * t, D_ATT), BF16),
        compiler_params=_params(("arbitrary",)),
        name="attn_sample",
    )(_pack_rows(*lams, g), q, kb, vb, ck, cv, za)


def _out_proj_kernel(s_ref, a_ref, w_ref, x_ref, g_ref, y_ref):
    tm = s_ref.shape[0]
    step = min(tm, OUT_CHUNK)
    for r in range(0, tm, step):
        rows = slice(r, r + step)
        mix = (jnp.dot(s_ref[rows, :], w_ref[:D_SSM, :], preferred_element_type=F32)
               + jnp.dot(a_ref[rows, :], w_ref[D_SSM:, :], preferred_element_type=F32))
        ms = jnp.mean(mix * mix, axis=-1, keepdims=True)
        y_ref[rows, :] = x_ref[rows, :] + mix * lax.rsqrt(ms + EPS) * g_ref[0:1, :]


def _out_proj(ssm_out, att_out, w_bf, x2d, g, *, tm):
    n = x2d.shape[0]
    assert n % tm == 0 and tm % min(tm, OUT_CHUNK) == 0
    row = lambda i: (i, 0)
    half = pl.BlockSpec((tm, D_SSM), row)
    full = pl.BlockSpec((tm, D_MODEL), row)
    return pl.pallas_call(
        _out_proj_kernel,
        grid=(n // tm,),
        in_specs=[half, half, _full((D_MODEL, D_MODEL)), full, _full((SUBLANES, D_MODEL))],
        out_specs=full,
        out_shape=jax.ShapeDtypeStruct((n, D_MODEL), F32),
        compiler_params=_params(("arbitrary",)),
        name="out_proj",
    )(ssm_out, att_out, w_bf, x2d, g)


def kernel(x_prompt, x_sample, cache_k, cache_v, state_ssm_re, state_ssm_im, norm_pre_g, w_in, ssm_lambda_re,
           ssm_lambda_im, ssm_log_dt, ssm_b_re, ssm_b_im, ssm_c_re, ssm_c_im, ssm_d, glu_w1, glu_b1, glu_w2,
           glu_b2, lambda_q1, lambda_k1, lambda_q2, lambda_k2, attn_subln_g, w_out, norm_post_g):
    bp, sp, _ = x_prompt.shape
    bs, ss, _ = x_sample.shape
    past = cache_k.shape[2]

    a_re, a_im, bdb, bdc = _prep(ssm_lambda_re[0], ssm_lambda_im[0], ssm_log_dt[0], ssm_b_re[0], ssm_b_im[0],
                                 ssm_c_re[0], ssm_c_im[0])
    a2 = jnp.concatenate([a_re.reshape(N_SLAB, LANES), a_im.reshape(N_SLAB, LANES)], axis=0)
    rows8 = lambda v: jnp.broadcast_to(v.reshape(1, -1), (SUBLANES, v.size))
    dvec = rows8(ssm_d[0])
    w_in_bf = w_in[0].astype(BF16)
    w_out_bf = w_out[0].astype(BF16)
    w1 = glu_w1[0].astype(BF16)
    w2 = glu_w2[0].astype(BF16)
    b1 = rows8(glu_b1[0])
    b2 = rows8(glu_b2[0])
    g_pre = rows8(norm_pre_g[0])
    g_post = rows8(norm_post_g[0])
    g_sub = attn_subln_g[0].reshape(1, V_DIM)
    lams = tuple(v[0].reshape(1, QK_DIM) for v in (lambda_q1, lambda_k1, lambda_q2, lambda_k2))
    inv = ROPE_THETA ** (-jnp.arange(ROPE_DIM // 2, dtype=F32) * 2.0 / ROPE_DIM)
    rotary_lane = (jnp.arange(LANES) % QK_DIM) < ROPE_DIM
    inv_lane = _pack_rows(jnp.where(rotary_lane, jnp.tile(inv, LANES // (ROPE_DIM // 2)), 0.0).reshape(1, LANES))

    def run(x, n_seq, seq_len, pos0, chained, h0, tm):
        x2d = x.reshape(n_seq * seq_len, D_MODEL)
        u, zs, q, kf, kb, vf, vb, za = _in_proj(x2d, g_pre, w_in_bf, inv_lane, seq_len=seq_len, pos0=pos0, tm=tm,
                                                transposed_qv=chained)
        ssm_out, hf = _s5(u, zs, bdb, bdc, dvec, a2, w1, b1, w2, b2, h0,
                          n_seq=n_seq, seq_len=seq_len, chained=chained)
        if chained:
            att = _attn_prompt(lams, g_sub, q, kb, vb, za, n_seq=n_seq, seq_len=seq_len)
            k_out = jnp.transpose(kf, (0, 4, 1, 2, 3))[None]
        else:
            ck = jnp.transpose(cache_k[0], (0, 2, 3, 4, 1))
            cv = cache_v[0].reshape(n_seq, past * ATT_HEADS, V_DIM)
            att = _attn_sample(lams, g_sub, q, kb, vb, ck, cv, za, n_seq=n_seq, t=seq_len, past=past)
            k_out = kf.reshape(1, n_seq, seq_len, ATT_HEADS, 2, QK_DIM)
        y = _out_proj(ssm_out, att, w_out_bf, x2d, g_post, tm=min(OUT_TILE, n_seq * seq_len))
        return (y.reshape(n_seq, seq_len, D_MODEL),
                k_out,
                vf.reshape(1, n_seq, seq_len, ATT_HEADS, V_DIM),
                hf[:, :N_SLAB].reshape(1, n_seq, SSM_GROUPS, SSM_STATE),
                hf[:, N_SLAB:].reshape(1, n_seq, SSM_GROUPS, SSM_STATE))

    yp, kp, vp, hrp, hip = run(x_prompt, bp, sp, 0, True, jnp.zeros((bp, 2 * N_SLAB, LANES), F32), PROJ_TILE)
    h0 = jnp.concatenate([state_ssm_re[0].reshape(bs, N_SLAB, LANES), state_ssm_im[0].reshape(bs, N_SLAB, LANES)],
                         axis=1)
    ys, ks, vs, hrs, his = run(x_sample, bs, ss, past, False, h0, bs * ss)
    return (yp, ys, kp, vp, hrp, hip, ks, vs, hrs, his)
```
